```python
import math
import jax, jax.numpy as jnp
from jax import lax
import numpy as np

D_MODEL = 1024
BATCH = 4
SEQ = 4096
DEPTH = 2

CTX_LEN = 256
GRID_W = 64
EPS = 1e-6
N_MOD = 6

N_MIXERS = 4
GROUP_WIDTH = D_MODEL // N_MIXERS
MIX_WIDTH = N_MIXERS * GROUP_WIDTH

SSD_HEAD_DIM = 64
SSD_HEADS = GROUP_WIDTH // SSD_HEAD_DIM
SSD_GROUPS = 2
SSD_STATE = 128
SSD_CONV = 4
SSD_CHUNK = 128
SSD_CONV_CH = GROUP_WIDTH + 2 * SSD_GROUPS * SSD_STATE
SSD_IN = GROUP_WIDTH + SSD_CONV_CH + 2 * SSD_HEADS

LRU_WIDTH = GROUP_WIDTH
LRU_BLOCKS = 4
LRU_BLOCK_W = LRU_WIDTH // LRU_BLOCKS
LRU_CONV = 4
LRU_C = 8.0
LRU_IN = 2 * LRU_WIDTH

ATTN_HEAD_DIM = 64
ATTN_Q_HEADS = GROUP_WIDTH // ATTN_HEAD_DIM
ATTN_KV_HEADS = 2
ATTN_GQA = ATTN_Q_HEADS // ATTN_KV_HEADS
ATTN_IN = (ATTN_Q_HEADS + 2 * ATTN_KV_HEADS) * ATTN_HEAD_DIM
Q_BLOCK = 128
WINDOW = 128
ROPE_THETA = 10000.0
ROPE_AXIS_DIM = ATTN_HEAD_DIM // 2

IN_COLS = SSD_IN + LRU_IN + 2 * ATTN_IN
IN_SPLITS = (SSD_IN, SSD_IN + LRU_IN, SSD_IN + LRU_IN + ATTN_IN)
FFN_HIDDEN = 4 * D_MODEL

kernel_name = 'hybrid_ssd_rglru_gqa_swa_dit_block'

F32 = jnp.float32


def rms_norm(x, g):
    xf = x.astype(F32)
    y = xf * lax.rsqrt(jnp.mean(xf * xf, axis=-1, keepdims=True) + EPS)
    return (y * g.astype(F32)).astype(x.dtype)


def modulate(h, shift, scale):
    return h * (1.0 + scale) + shift


def conv_centred(x, w, b):
    k = w.shape[0]
    y = lax.conv_general_dilated(x, w[:, None, :].astype(x.dtype), window_strides=(1,),
                                 padding=[(k // 2, k - 1 - k // 2)],
                                 dimension_numbers=('NWC', 'WIO', 'NWC'),
                                 feature_group_count=x.shape[-1])
    return y + b.astype(x.dtype)


def rope_2d_tables(length):
    rows = length // GRID_W
    row = jnp.repeat(jnp.arange(rows, dtype=F32), GRID_W)
    col = jnp.tile(jnp.arange(GRID_W, dtype=F32), rows)
    inv = ROPE_THETA ** (-jnp.arange(0, ROPE_AXIS_DIM, 2, dtype=F32) / ROPE_AXIS_DIM)
    ang = jnp.stack([row, col], axis=-1)[:, :, None] * inv
    return jnp.cos(ang), jnp.sin(ang)


def apply_rope_2d(x, cos, sin):
    bsz, length, h, d = x.shape
    xr = x.astype(F32).reshape(bsz, length, h, 2, 2, ROPE_AXIS_DIM // 2)
    x1, x2 = xr[..., 0, :], xr[..., 1, :]
    cs, sn = cos[None, :, None], sin[None, :, None]
    out = jnp.stack([x1 * cs - x2 * sn, x2 * cs + x1 * sn], axis=-2)
    return out.reshape(bsz, length, h, d).astype(x.dtype)


def ssd_chunked(x, dt, a, bm, cm, h0):
    bsz, length, h, p = x.shape
    nc = length // SSD_CHUNK
    shp = (bsz, nc, SSD_CHUNK)
    xc = x.astype(F32).reshape(shp + (h, p))
    bc = bm.astype(F32).reshape(shp + (h, SSD_STATE))
    cc = cm.astype(F32).reshape(shp + (h, SSD_STATE))
    dtc = dt.reshape(shp + (h,))
    a_cum = jnp.cumsum(dtc * a, axis=2)
    seg = a_cum[:, :, :, None, :] - a_cum[:, :, None, :, :]
    tri = jnp.tril(jnp.ones((SSD_CHUNK, SSD_CHUNK), dtype=bool))
    decay = jnp.exp(jnp.where(tri[None, None, :, :, None], seg, -jnp.inf))
    scores = jnp.einsum('bcihn,bcjhn->bcijh', cc, bc) * decay
    y_diag = jnp.einsum('bcijh,bcjhp->bcihp', scores * dtc[:, :, None, :, :], xc)
    w_end = jnp.exp(a_cum[:, :, -1:, :] - a_cum) * dtc
    states = jnp.einsum('bcjh,bcjhn,bcjhp->bchpn', w_end, bc, xc)
    chunk_decay = jnp.exp(a_cum[:, :, -1, :])

    def step(hs, inp):
        s, dcy = inp
        return hs * dcy[:, :, None, None] + s, hs

    h_last, h_start = lax.scan(step, h0, (jnp.moveaxis(states, 1, 0), jnp.moveaxis(chunk_decay, 1, 0)))
    h_start = jnp.moveaxis(h_start, 0, 1)
    y_off = jnp.einsum('bcihn,bchpn->bcihp', cc, h_start) * jnp.exp(a_cum)[..., None]
    return (y_diag + y_off).reshape(bsz, length, h, p), h_last


def _ssd_inputs(p, conv_w, conv_b, dt_bias):
    bsz, length, _ = p.shape
    z, xbc, dt = jnp.split(p, [GROUP_WIDTH, GROUP_WIDTH + SSD_CONV_CH], axis=-1)
    xbc = jax.nn.silu(conv_centred(xbc, conv_w, conv_b))
    xs, bm, cm = jnp.split(xbc, [GROUP_WIDTH, GROUP_WIDTH + SSD_GROUPS * SSD_STATE], axis=-1)
    rep = SSD_HEADS // SSD_GROUPS
    xs = xs.reshape(bsz, length, SSD_HEADS, SSD_HEAD_DIM)
    bm = jnp.repeat(bm.reshape(bsz, length, SSD_GROUPS, SSD_STATE), rep, axis=2)
    cm = jnp.repeat(cm.reshape(bsz, length, SSD_GROUPS, SSD_STATE), rep, axis=2)
    dt = jax.nn.softplus(dt.astype(F32).reshape(bsz, length, 2, SSD_HEADS) + dt_bias.astype(F32))
    return z, xs, bm, cm, dt


def _ssd_bidir(xs, bm, cm, dt, a, h0_f, h0_b):
    y_f, s_f = ssd_chunked(xs, dt[:, :, 0], a[0], bm, cm, h0_f)
    y_b, s_b = ssd_chunked(xs[:, ::-1], dt[:, ::-1, 1], a[1], bm[:, ::-1], cm[:, ::-1], h0_b)
    return y_f + y_b[:, ::-1], s_f, s_b


def _ssd_out(y, xs, z, d_skip, norm_g):
    y = y + d_skip.astype(F32)[:, None] * xs.astype(F32)
    y = y.reshape(z.shape).astype(z.dtype)
    return rms_norm(y * jax.nn.silu(z), norm_g)


def ssd_mixer(pl, pc, conv_w, conv_b, a_log, dt_bias, d_skip, norm_g, need_ctx):
    a = -jnp.exp(a_log.astype(F32))
    zl, xl, bl, cl, dtl = _ssd_inputs(pl, conv_w, conv_b, dt_bias)
    zc, xc, bc, cc, dtc = _ssd_inputs(pc, conv_w, conv_b, dt_bias)
    h0 = jnp.zeros((pc.shape[0], SSD_HEADS, SSD_HEAD_DIM, SSD_STATE), F32)
    yc, s_f, s_b = _ssd_bidir(xc, bc, cc, dtc, a, h0, h0)
    yl, _, _ = _ssd_bidir(xl, bl, cl, dtl, a, s_f, s_b)
    out_l = _ssd_out(yl, xl, zl, d_skip, norm_g)
    out_c = _ssd_out(yc, xc, zc, d_skip, norm_g) if need_ctx else None
    return out_l, out_c


def _linear_combine(left, right):
    a_l, b_l = left
    a_r, b_r = right
    return a_l * a_r, a_r * b_l + b_r


def _rglru_dir(x, lam, w_a, b_a, w_i, b_i, h0):
    bsz, length, _ = x.shape
    xf = x.astype(F32)
    xb = xf.reshape(bsz, length, LRU_BLOCKS, LRU_BLOCK_W)
    r = jax.nn.sigmoid(jnp.einsum('blki,kij->blkj', xb, w_a.astype(F32)).reshape(bsz, length, LRU_WIDTH) + b_a.astype(F32))
    ig = jax.nn.sigmoid(jnp.einsum('blki,kij->blkj', xb, w_i.astype(F32)).reshape(bsz, length, LRU_WIDTH) + b_i.astype(F32))
    log_a = -LRU_C * r * jax.nn.softplus(-lam.astype(F32))
    a = jnp.exp(log_a)
    b = jnp.sqrt(-jnp.expm1(2.0 * log_a)) * (ig * xf)
    b = b.at[:, 0].add(a[:, 0] * h0)
    _, h = lax.associative_scan(_linear_combine, (a, b), axis=1)
    return h, h[:, -1]


def rglru_mixer(pl, pc, conv_w, conv_b, lam, w_a, b_a, w_i, b_i, need_ctx):
    gl, xl = jnp.split(pl, 2, axis=-1)
    gc, xc = jnp.split(pc, 2, axis=-1)
    xl = conv_centred(xl, conv_w, conv_b)
    xc = conv_centred(xc, conv_w, conv_b)
    h0 = jnp.zeros((pc.shape[0], LRU_WIDTH), F32)
    hc_f, s_f = _rglru_dir(xc, lam[0], w_a[0], b_a[0], w_i[0], b_i[0], h0)
    hc_b, s_b = _rglru_dir(xc[:, ::-1], lam[1], w_a[1], b_a[1], w_i[1], b_i[1], h0)
    hl_f, _ = _rglru_dir(xl, lam[0], w_a[0], b_a[0], w_i[0], b_i[0], s_f)
    hl_b, _ = _rglru_dir(xl[:, ::-1], lam[1], w_a[1], b_a[1], w_i[1], b_i[1], s_b)
    out_l = (hl_f + hl_b[:, ::-1]).astype(pl.dtype) * jax.nn.gelu(gl)
    out_c = (hc_f + hc_b[:, ::-1]).astype(pc.dtype) * jax.nn.gelu(gc) if need_ctx else None
    return out_l, out_c


def attn_heads(p, q_norm, k_norm, rope):
    bsz, length, _ = p.shape
    q, k, v = jnp.split(p, [ATTN_Q_HEADS * ATTN_HEAD_DIM, (ATTN_Q_HEADS + ATTN_KV_HEADS) * ATTN_HEAD_DIM], axis=-1)
    q = rms_norm(q.reshape(bsz, length, ATTN_Q_HEADS, ATTN_HEAD_DIM), q_norm)
    k = rms_norm(k.reshape(bsz, length, ATTN_KV_HEADS, ATTN_HEAD_DIM), k_norm)
    v = v.reshape(bsz, length, ATTN_KV_HEADS, ATTN_HEAD_DIM)
    if rope is not None:
        q = apply_rope_2d(q, rope[0], rope[1])
        k = apply_rope_2d(k, rope[0], rope[1])
    return q.reshape(bsz, length, ATTN_KV_HEADS, ATTN_GQA, ATTN_HEAD_DIM), k, v


def global_attention(q, k, v, kc, vc):
    bsz, length = q.shape[:2]
    nb = length // Q_BLOCK
    scale = ATTN_HEAD_DIM ** -0.5
    k_all = jnp.concatenate([k, kc], axis=1)
    v_all = jnp.concatenate([v, vc], axis=1)
    qb = jnp.moveaxis(q.reshape(bsz, nb, Q_BLOCK, ATTN_KV_HEADS, ATTN_GQA, ATTN_HEAD_DIM), 1, 0)

    def block(qblk):
        s = jnp.einsum('bqhgd,bkhd->bhgqk', qblk, k_all).astype(F32) * scale
        pr = jax.nn.softmax(s, axis=-1).astype(v_all.dtype)
        return jnp.einsum('bhgqk,bkhd->bqhgd', pr, v_all)

    o = lax.map(block, qb)
    return jnp.moveaxis(o, 0, 1).reshape(bsz, length, ATTN_Q_HEADS * ATTN_HEAD_DIM)


def context_attention(qc, kc, vc, sink):
    bsz, lc = qc.shape[:2]
    s = jnp.einsum('bqhgd,bkhd->bhgqk', qc, kc).astype(F32) * ATTN_HEAD_DIM ** -0.5
    if sink is not None:
        s_sink = jnp.broadcast_to(sink.astype(F32).reshape(1, ATTN_KV_HEADS, ATTN_GQA, 1, 1), s.shape[:-1] + (1,))
        s = jnp.concatenate([s, s_sink], axis=-1)
    pr = jax.nn.softmax(s, axis=-1)[..., :lc].astype(vc.dtype)
    o = jnp.einsum('bhgqk,bkhd->bqhgd', pr, vc)
    return o.reshape(bsz, lc, ATTN_Q_HEADS * ATTN_HEAD_DIM)


def window_attention(q, k, v, kc, vc, sink):
    bsz, length = q.shape[:2]
    nb = length // WINDOW
    scale = ATTN_HEAD_DIM ** -0.5
    pad = ((0, 0), (WINDOW, WINDOW), (0, 0), (0, 0))
    kp = jnp.pad(k, pad).reshape(bsz, nb + 2, WINDOW, ATTN_KV_HEADS, ATTN_HEAD_DIM)
    vp = jnp.pad(v, pad).reshape(bsz, nb + 2, WINDOW, ATTN_KV_HEADS, ATTN_HEAD_DIM)
    kw = jnp.concatenate([kp[:, :-2], kp[:, 1:-1], kp[:, 2:]], axis=2)
    vw = jnp.concatenate([vp[:, :-2], vp[:, 1:-1], vp[:, 2:]], axis=2)
    qb = q.reshape(bsz, nb, WINDOW, ATTN_KV_HEADS, ATTN_GQA, ATTN_HEAD_DIM)
    s_band = jnp.einsum('bnqhgd,bnkhd->bnhgqk', qb, kw).astype(F32) * scale
    qi = jnp.arange(WINDOW)[:, None]
    kj = jnp.arange(3 * WINDOW)[None, :]
    kpos = jnp.arange(nb)[:, None, None] * WINDOW + kj[None] - WINDOW
    mask = (jnp.abs(kj - WINDOW - qi) <= WINDOW)[None] & (kpos >= 0) & (kpos < length)
    s_band = jnp.where(mask[None, :, None, None], s_band, -jnp.inf)
    s_ctx = jnp.einsum('bnqhgd,bkhd->bnhgqk', qb, kc).astype(F32) * scale
    s_sink = jnp.broadcast_to(sink.astype(F32).reshape(1, 1, ATTN_KV_HEADS, ATTN_GQA, 1, 1), s_band.shape[:-1] + (1,))
    pr = jax.nn.softmax(jnp.concatenate([s_band, s_ctx, s_sink], axis=-1), axis=-1).astype(v.dtype)
    nk = 3 * WINDOW
    o = (jnp.einsum('bnhgqk,bnkhd->bnqhgd', pr[..., :nk], vw)
         + jnp.einsum('bnhgqk,bkhd->bnqhgd', pr[..., nk:nk + kc.shape[1]], vc))
    return o.reshape(bsz, length, ATTN_Q_HEADS * ATTN_HEAD_DIM)


def squared_relu_mlp(h, w1, w2):
    return jnp.square(jax.nn.relu(h @ w1)) @ w2


def setup_inputs(seed: int = 0) -> dict:
    key = jax.random.key(seed)
    ks = iter(jax.random.split(key, 40))

    def nrm(shape, scale):
        return jax.random.normal(next(ks), shape, F32) * scale

    L = DEPTH
    x = nrm((BATCH, SEQ, D_MODEL), 1.0)
    c = nrm((BATCH, D_MODEL), 1.0)
    ctx = nrm((BATCH, CTX_LEN, D_MODEL), 1.0)
    c_ctx = nrm((D_MODEL,), 1.0)
    w_mod = nrm((L, D_MODEL, N_MOD * D_MODEL), 0.02)
    b_mod = nrm((L, N_MOD * D_MODEL), 0.02)
    g_mix = 1.0 + nrm((L, D_MODEL), 0.02)
    w_in = nrm((L, D_MODEL, IN_COLS), D_MODEL ** -0.5)
    ssd_conv_w = nrm((L, SSD_CONV, SSD_CONV_CH), SSD_CONV ** -0.5)
    ssd_conv_b = nrm((L, SSD_CONV_CH), 0.02)
    ssd_a_log = jnp.log(jax.random.uniform(next(ks), (L, 2, SSD_HEADS), F32, 1.0, 16.0))
    dt0 = jnp.exp(jax.random.uniform(next(ks), (L, 2, SSD_HEADS), F32, math.log(1e-3), math.log(1e-1)))
    ssd_dt_bias = dt0 + jnp.log(-jnp.expm1(-dt0))
    ssd_d = 1.0 + nrm((L, SSD_HEADS), 0.02)
    ssd_norm_g = 1.0 + nrm((L, GROUP_WIDTH), 0.02)
    lru_conv_w = nrm((L, LRU_CONV, LRU_WIDTH), LRU_CONV ** -0.5)
    lru_conv_b = nrm((L, LRU_WIDTH), 0.02)
    a0 = jax.random.uniform(next(ks), (L, 2, LRU_WIDTH), F32, 0.9, 0.999)
    s0 = a0 ** (1.0 / LRU_C)
    lru_lambda = jnp.log(s0) - jnp.log1p(-s0)
    lru_w_a = nrm((L, 2, LRU_BLOCKS, LRU_BLOCK_W, LRU_BLOCK_W), LRU_BLOCK_W ** -0.5)
    lru_b_a = nrm((L, 2, LRU_WIDTH), 0.02)
    lru_w_i = nrm((L, 2, LRU_BLOCKS, LRU_BLOCK_W, LRU_BLOCK_W), LRU_BLOCK_W ** -0.5)
    lru_b_i = nrm((L, 2, LRU_WIDTH), 0.02)
    gqa_q_norm = 1.0 + nrm((L, ATTN_HEAD_DIM), 0.02)
    gqa_k_norm = 1.0 + nrm((L, ATTN_HEAD_DIM), 0.02)
    swa_q_norm = 1.0 + nrm((L, ATTN_HEAD_DIM), 0.02)
    swa_k_norm = 1.0 + nrm((L, ATTN_HEAD_DIM), 0.02)
    swa_sink = nrm((L, ATTN_Q_HEADS), 0.5)
    w_out = nrm((L, MIX_WIDTH, D_MODEL), MIX_WIDTH ** -0.5)
    g_ffn = 1.0 + nrm((L, D_MODEL), 0.02)
    w_ffn1 = nrm((L, D_MODEL, FFN_HIDDEN), D_MODEL ** -0.5)
    w_ffn2 = nrm((L, FFN_HIDDEN, D_MODEL), FFN_HIDDEN ** -0.5)
    return {'x': x, 'c': c, 'ctx': ctx, 'c_ctx': c_ctx, 'w_mod': w_mod, 'b_mod': b_mod,
            'g_mix': g_mix, 'w_in': w_in, 'ssd_conv_w': ssd_conv_w, 'ssd_conv_b': ssd_conv_b,
            'ssd_a_log': ssd_a_log, 'ssd_dt_bias': ssd_dt_bias, 'ssd_d': ssd_d, 'ssd_norm_g': ssd_norm_g,
            'lru_conv_w': lru_conv_w, 'lru_conv_b': lru_conv_b, 'lru_lambda': lru_lambda,
            'lru_w_a': lru_w_a, 'lru_b_a': lru_b_a, 'lru_w_i': lru_w_i, 'lru_b_i': lru_b_i,
            'gqa_q_norm': gqa_q_norm, 'gqa_k_norm': gqa_k_norm, 'swa_q_norm': swa_q_norm,
            'swa_k_norm': swa_k_norm, 'swa_sink': swa_sink, 'w_out': w_out, 'g_ffn': g_ffn,
            'w_ffn1': w_ffn1, 'w_ffn2': w_ffn2}


def reference(x, c, ctx, c_ctx, w_mod, b_mod, g_mix, w_in, ssd_conv_w, ssd_conv_b, ssd_a_log,
              ssd_dt_bias, ssd_d, ssd_norm_g, lru_conv_w, lru_conv_b, lru_lambda, lru_w_a, lru_b_a,
              lru_w_i, lru_b_i, gqa_q_norm, gqa_k_norm, swa_q_norm, swa_k_norm, swa_sink, w_out,
              g_ffn, w_ffn1, w_ffn2):
    rope = rope_2d_tables(x.shape[1])
    for l in range(DEPTH):
        need_ctx = l < DEPTH - 1
        m_lat = (jax.nn.silu(c) @ w_mod[l] + b_mod[l])[:, None, :]
        m_ctx = (jax.nn.silu(c_ctx) @ w_mod[l] + b_mod[l])[None, None, :]
        sh1, sc1, ga1, sh2, sc2, ga2 = jnp.split(m_lat, N_MOD, axis=-1)
        csh1, csc1, cga1, csh2, csc2, cga2 = jnp.split(m_ctx, N_MOD, axis=-1)

        pl = modulate(rms_norm(x, g_mix[l]), sh1, sc1) @ w_in[l]
        pc = modulate(rms_norm(ctx, g_mix[l]), csh1, csc1) @ w_in[l]
        pl_a, pl_b, pl_c, pl_d = jnp.split(pl, IN_SPLITS, axis=-1)
        pc_a, pc_b, pc_c, pc_d = jnp.split(pc, IN_SPLITS, axis=-1)

        ya_l, ya_c = ssd_mixer(pl_a, pc_a, ssd_conv_w[l], ssd_conv_b[l], ssd_a_log[l],
                               ssd_dt_bias[l], ssd_d[l], ssd_norm_g[l], need_ctx)
        yb_l, yb_c = rglru_mixer(pl_b, pc_b, lru_conv_w[l], lru_conv_b[l], lru_lambda[l],
                                 lru_w_a[l], lru_b_a[l], lru_w_i[l], lru_b_i[l], need_ctx)
        qcl, kcl, vcl = attn_heads(pl_c, gqa_q_norm[l], gqa_k_norm[l], rope)
        qcc, kcc, vcc = attn_heads(pc_c, gqa_q_norm[l], gqa_k_norm[l], None)
        yc_l = global_attention(qcl, kcl, vcl, kcc, vcc)
        qdl, kdl, vdl = attn_heads(pl_d, swa_q_norm[l], swa_k_norm[l], rope)
        qdc, kdc, vdc = attn_heads(pc_d, swa_q_norm[l], swa_k_norm[l], None)
        yd_l = window_attention(qdl, kdl, vdl, kdc, vdc, swa_sink[l])

        x = x + ga1 * (jnp.concatenate([ya_l, yb_l, yc_l, yd_l], axis=-1) @ w_out[l])
        x = x + ga2 * squared_relu_mlp(modulate(rms_norm(x, g_ffn[l]), sh2, sc2), w_ffn1[l], w_ffn2[l])

        if need_ctx:
            yc_c = context_attention(qcc, kcc, vcc, None)
            yd_c = context_attention(qdc, kdc, vdc, swa_sink[l])
            ctx = ctx + cga1 * (jnp.concatenate([ya_c, yb_c, yc_c, yd_c], axis=-1) @ w_out[l])
            ctx = ctx + cga2 * squared_relu_mlp(modulate(rms_norm(ctx, g_ffn[l]), csh2, csc2), w_ffn1[l], w_ffn2[l])
    return x
```

```python
import functools
import math

import numpy as np
import jax
import jax.numpy as jnp
from jax import lax
from jax.experimental import pallas as pl
from jax.experimental.pallas import tpu as pltpu

F32 = jnp.float32
BF16 = jnp.bfloat16

EPS = 1e-6
GRID_W = 64
N_MOD = 6
GROUP_WIDTH = 256
SSD_HEADS = 4
SSD_STATE = 128
SSD_CONV = 4
SSD_CHUNK = 128
SSD_CONV_CH = 768
LRU_WIDTH = 256
LRU_BLOCKS = 4
LRU_C = 8.0
HEAD_DIM = 64
Q_HEADS = 4
WINDOW = 128
ROPE_THETA = 10000.0
ROPE_AXIS_DIM = 32

V7X_VMEM_BYTES = 64 * 1024 * 1024
LANES = 128
BF16_ROWS = 16

COL_SSD, COL_LRU, COL_C, COL_D, COL_DT, COL_END = 0, 1024, 1536, 2048, 2560, 2688
Q_HEAD_ORDER = (0, 2, 1, 3)

CONV_WIN = SSD_CHUNK + 2 * BF16_ROWS


def _cparams(sem, vmem_mib):
    return pltpu.CompilerParams(dimension_semantics=sem, vmem_limit_bytes=vmem_mib * 1024 * 1024)


def _dot(a, b):
    return jnp.dot(a, b, preferred_element_type=F32)


def _dot_nt(a, b):
    return lax.dot_general(a, b, (((1,), (1,)), ((), ())), preferred_element_type=F32)


def _split_bf16(a, parts):
    out = []
    for _ in range(parts - 1):
        hi = a.astype(BF16)
        out.append(hi)
        a = a - hi.astype(F32)
    out.append(a.astype(BF16))
    return out


def _dot_exact_rhs(a, b_bf16, parts=3):
    acc = None
    for p in _split_bf16(a, parts):
        t = _dot(p, b_bf16)
        acc = t if acc is None else acc + t
    return acc


def _dot_exact_lhs(a_bf16, b, parts=3):
    acc = None
    for p in _split_bf16(b, parts):
        t = _dot(a_bf16, p)
        acc = t if acc is None else acc + t
    return acc


def _sigmoid(x):
    return 1.0 / (1.0 + jnp.exp(-x))


def _silu(x):
    return x * _sigmoid(x)


def _softplus(x):
    return jnp.maximum(x, 0.0) + jnp.log1p(jnp.exp(-jnp.abs(x)))


def _gelu_tanh(x):
    return 0.5 * x * (1.0 + jnp.tanh(math.sqrt(2.0 / math.pi) * (x + 0.044715 * (x * x * x))))


def _lane_iota(shape):
    return lax.broadcasted_iota(jnp.int32, shape, len(shape) - 1)


def _row_iota(shape):
    return lax.broadcasted_iota(jnp.int32, shape, len(shape) - 2)


def _mod_kernel(cv_ref, w_ref, b_ref, o_ref):
    cv = cv_ref[...]
    s = _silu(cv)
    w = w_ref[0]
    s_hi, s_lo = _split_bf16(s, 2)
    w_hi, w_lo = _split_bf16(w, 2)
    acc = _dot(s_hi, w_hi) + _dot(s_lo, w_hi) + _dot(s_hi, w_lo)
    o_ref[0] = acc + b_ref[0]


def _modulation(cvecs, w_mod, b_mod):
    depth, d, n = w_mod.shape
    tn = 1024
    return pl.pallas_call(
        _mod_kernel,
        out_shape=jax.ShapeDtypeStruct((depth, 8, n), F32),
        grid=(depth, n // tn),
        in_specs=[pl.BlockSpec((8, d), lambda l, j: (0, 0)),
                  pl.BlockSpec((1, d, tn), lambda l, j: (l, 0, j)),
                  pl.BlockSpec((1, 1, tn), lambda l, j: (l, 0, j))],
        out_specs=pl.BlockSpec((1, 8, tn), lambda l, j: (l, 0, j)),
        compiler_params=_cparams(("parallel", "parallel"), 32),
        name="modulation",
    )(cvecs, w_mod, b_mod.reshape(depth, 1, n))


def _head_norm_rope(p, g, bd, cos, sin, scale):
    ms = _dot_exact_rhs(p * p, bd, parts=2)
    y = p * lax.rsqrt(ms + EPS) * g
    if cos is not None:
        w = y.shape[-1]
        first = (_lane_iota(y.shape) % ROPE_AXIS_DIM) < (ROPE_AXIS_DIM // 2)
        partner = jnp.where(first, pltpu.roll(y, w - ROPE_AXIS_DIM // 2, 1), pltpu.roll(y, ROPE_AXIS_DIM // 2, 1))
        y = y * cos + partner * sin
    if scale != 1.0:
        y = y * scale
    return y


def _inproj_kernel(*refs, use_rope):
    if use_rope:
        x_ref, mod_ref, g_ref, w_ref, nrm_ref, bd_ref, cos_ref, sin_ref, ssd_ref, lru_ref, c_ref, d_ref, dt_ref = refs
    else:
        x_ref, mod_ref, g_ref, w_ref, nrm_ref, bd_ref, ssd_ref, lru_ref, c_ref, d_ref, dt_ref = refs
    x = x_ref[...]
    ms = jnp.mean(x * x, axis=-1, keepdims=True)
    m = mod_ref[0]
    h = (x * lax.rsqrt(ms + EPS) * g_ref[...]) * (1.0 + m[1:2]) + m[0:1]
    hb = h.astype(BF16)
    ssd_ref[...] = _dot(hb, w_ref[:, COL_SSD:COL_LRU]).astype(BF16)
    lru_ref[...] = _dot(hb, w_ref[:, COL_LRU:COL_C]).astype(BF16)
    dt_ref[...] = _dot(hb, w_ref[:, COL_DT:COL_END])
    bd = bd_ref[...]
    if use_rope:
        cos1, sin1 = cos_ref[...], sin_ref[...]
        cos2 = jnp.concatenate([cos1, cos1], axis=1)
        sin2 = jnp.concatenate([sin1, sin1], axis=1)
    else:
        cos1 = sin1 = cos2 = sin2 = None
    for o_ref, c0, row in ((c_ref, COL_C, 0), (d_ref, COL_D, 2)):
        p = _dot(hb, w_ref[:, c0:c0 + 512])
        q = _head_norm_rope(p[:, 0:256], nrm_ref[row:row + 1, :], bd, cos2, sin2, HEAD_DIM ** -0.5)
        k = _head_norm_rope(p[:, 256:384], nrm_ref[row + 1:row + 2, 0:128], bd[0:128, 0:128], cos1, sin1, 1.0)
        o_ref[:, 0:256] = q.astype(BF16)
        o_ref[:, 256:384] = k.astype(BF16)
        o_ref[:, 384:512] = p[:, 384:512].astype(BF16)


def _in_projection(x2d, mod, mod_row_of_tile, g, w, nrm, bd, rope, tm, rope_tiles):
    rows, d = x2d.shape
    use_rope = rope is not None
    in_specs = [pl.BlockSpec((tm, d), lambda i: (i, 0)),
                pl.BlockSpec((1, N_MOD, d), lambda i: (mod_row_of_tile(i), 0, 0)),
                pl.BlockSpec((1, d), lambda i: (0, 0)),
                pl.BlockSpec((d, COL_END), lambda i: (0, 0), pipeline_mode=pl.Buffered(1)),
                pl.BlockSpec((4, 256), lambda i: (0, 0)),
                pl.BlockSpec((256, 256), lambda i: (0, 0))]
    args = [x2d, mod, g, w, nrm, bd]
    if use_rope:
        in_specs += [pl.BlockSpec((tm, LANES), lambda i: (i % rope_tiles, 0))] * 2
        args += list(rope)
    widths = (1024, 512, 512, 512, LANES)
    dtypes = (BF16, BF16, BF16, BF16, F32)
    return pl.pallas_call(
        functools.partial(_inproj_kernel, use_rope=use_rope),
        out_shape=tuple(jax.ShapeDtypeStruct((rows, wd), dt) for wd, dt in zip(widths, dtypes)),
        grid=(rows // tm,),
        in_specs=in_specs,
        out_specs=tuple(pl.BlockSpec((tm, wd), lambda i: (i, 0)) for wd in widths),
        compiler_params=_cparams(("parallel",), 48),
        name="in_projection",
    )(*args)


def _conv_chunk(seq_ref, col0, width, length, r0, w_ref, b_ref):
    ws = jnp.clip(r0 - BF16_ROWS, 0, length - CONV_WIN)
    ws = pl.multiple_of(ws, BF16_ROWS)
    delta = r0 - ws
    xw = seq_ref[pl.ds(ws, CONV_WIN), col0:col0 + width]
    shp = (SSD_CONV * SSD_CHUNK, CONV_WIN)
    row = _row_iota(shp)
    want = (row % SSD_CHUNK) + (row // SSD_CHUNK) - SSD_CONV // 2 + delta
    sel = jnp.where(_lane_iota(shp) == want, 1.0, 0.0).astype(BF16)
    sh = _dot(sel, xw)
    w = w_ref[...]
    acc = b_ref[...] + sh[0:SSD_CHUNK] * w[0:1]
    for k in range(1, SSD_CONV):
        acc = acc + sh[k * SSD_CHUNK:(k + 1) * SSD_CHUNK] * w[k:k + 1]
    return acc


def _ssd_kernel(pl_ref, pc_ref, dtl_ref, dtc_ref, cw_ref, cb_ref, hp_ref, sel_ref, dsk_ref, ng_ref,
                yl_ref, yc_ref, accl_ref, accc_ref):
    q = SSD_CHUNK
    lane8 = _lane_iota((1, LANES)) < 2 * SSD_HEADS
    dt_bias = hp_ref[0:1, :]
    a_row = jnp.where(lane8, -jnp.exp(hp_ref[1:2, :]), 0.0)
    ii = _row_iota((q, q))
    jj = _lane_iota((q, q))
    lo = _lane_iota((q, LANES)) < HEAD_DIM

    def chunk(seq_ref, dt_ref, length, r0, d, state):
        xbc = _silu(_conv_chunk(seq_ref, GROUP_WIDTH, SSD_CONV_CH, length, r0, cw_ref, cb_ref))
        xs, bm, cm = xbc[:, 0:256], xbc[:, 256:512], xbc[:, 512:768]
        dtv = _softplus(dt_ref[pl.ds(r0, q), :] + dt_bias)
        da = dtv * a_row
        mask = (jj <= ii) if d == 0 else (jj >= ii)
        tri = jnp.where(mask, 1.0, 0.0).astype(BF16)
        acum = _dot_exact_lhs(tri, da)
        ex = _dot_exact_rhs(jnp.concatenate([dtv, acum], axis=0), sel_ref[d])
        dtb, acb = ex[0:q], ex[q:2 * q]
        act = acum.T
        edge = q - 1 if d == 0 else 0
        ys, new_state = [], []
        for g in range(2):
            per_head = []
            for hh in range(2):
                h = 2 * g + hh
                ac = acb[:, h * LANES:(h + 1) * LANES]
                ar = act[SSD_HEADS * d + h:SSD_HEADS * d + h + 1, :]
                lmat = jnp.where(mask, jnp.exp(ac - ar), 0.0)
                alast = ac[edge:edge + 1, :]
                dth = dtb[:, h * LANES:(h + 1) * LANES]
                per_head.append((lmat, dth, jnp.exp(alast - ac) * dth, jnp.exp(ac), jnp.exp(alast)))
            (l0, dt0, we0, ei0, cd0), (l1, dt1, we1, ei1, cd1) = per_head
            bt = bm[:, g * LANES:(g + 1) * LANES].T.astype(BF16)
            cg = cm[:, g * LANES:(g + 1) * LANES].astype(BF16)
            xg = xs[:, g * LANES:(g + 1) * LANES]
            cb = _dot(cg, bt)
            xdt = xg * jnp.where(lo, dt0, dt1)
            y = (_dot((cb * l0).astype(BF16), jnp.where(lo, xdt, 0.0).astype(BF16))
                 + _dot((cb * l1).astype(BF16), jnp.where(lo, 0.0, xdt).astype(BF16)))
            y = y + _dot(cg, state[g].astype(BF16)) * jnp.where(lo, ei0, ei1)
            s_new = _dot(bt, (xg * jnp.where(lo, we0, we1)).astype(BF16))
            new_state.append(state[g] * jnp.where(lo, cd0, cd1) + s_new)
            ys.append(y)
        return jnp.concatenate(ys, axis=1), xs, tuple(new_state)

    def sweep(seq_ref, dt_ref, acc_ref, out_ref, d, state):
        length = seq_ref.shape[0]
        nc = length // q

        def body(i, st):
            c = i if d == 0 else nc - 1 - i
            r0 = pl.multiple_of(c * q, q)
            y, xs, st = chunk(seq_ref, dt_ref, length, r0, d, st)
            if d == 0:
                acc_ref[pl.ds(r0, q), :] = y
            else:
                y = acc_ref[pl.ds(r0, q), :] + y + dsk_ref[...] * xs
                z = seq_ref[pl.ds(r0, q), 0:GROUP_WIDTH].astype(F32)
                t = y * _silu(z)
                ms = jnp.mean(t * t, axis=-1, keepdims=True)
                out_ref[pl.ds(r0, q), :] = (t * lax.rsqrt(ms + EPS) * ng_ref[...]).astype(out_ref.dtype)
            return st

        return lax.fori_loop(0, nc, body, state)

    zero = (jnp.zeros((SSD_STATE, LANES), F32), jnp.zeros((SSD_STATE, LANES), F32))
    for d in range(2):
        st = sweep(pc_ref, dtc_ref, accc_ref, yc_ref, d, zero)
        sweep(pl_ref, dtl_ref, accl_ref, yl_ref, d, st)


def _ssd_mixer(p_lat, p_ctx, dt_lat, dt_ctx, conv_w, conv_b, head_params, sel, d_skip, norm_g, batch):
    seq = p_lat.shape[0] // batch
    lc = p_ctx.shape[0] // batch
    const = lambda *shape: pl.BlockSpec(shape, lambda b: (0,) * len(shape))
    return pl.pallas_call(
        _ssd_kernel,
        out_shape=(jax.ShapeDtypeStruct((batch * seq, GROUP_WIDTH), BF16),
                   jax.ShapeDtypeStruct((batch * lc, GROUP_WIDTH), BF16)),
        grid=(batch,),
        in_specs=[pl.BlockSpec((seq, 1024), lambda b: (b, 0)),
                  pl.BlockSpec((lc, 1024), lambda b: (b, 0)),
                  pl.BlockSpec((seq, LANES), lambda b: (b, 0)),
                  pl.BlockSpec((lc, LANES), lambda b: (b, 0)),
                  const(SSD_CONV, SSD_CONV_CH), const(1, SSD_CONV_CH), const(8, LANES),
                  const(2, LANES, SSD_HEADS * LANES), const(1, GROUP_WIDTH), const(1, GROUP_WIDTH)],
        out_specs=(pl.BlockSpec((seq, GROUP_WIDTH), lambda b: (b, 0)),
                   pl.BlockSpec((lc, GROUP_WIDTH), lambda b: (b, 0))),
        scratch_shapes=[pltpu.VMEM((seq, GROUP_WIDTH), F32), pltpu.VMEM((lc, GROUP_WIDTH), F32)],
        compiler_params=_cparams(("parallel",), 48),
        name="ssd_mixer",
    )(p_lat, p_ctx, dt_lat, dt_ctx, conv_w, conv_b, head_params, sel, d_skip, norm_g)


LRU_SEGMENTS = 8


def _lru_kernel(pl_ref, pc_ref, cw_ref, cb_ref, wg_ref, bg_ref, lam_ref, ol_ref, oc_ref, a_scr, b_scr, cin_scr):
    q = SSD_CHUNK
    clam = -LRU_C * _softplus(-lam_ref[...])

    def run(seq_ref, out_ref, h0):
        length = seq_ref.shape[0]
        seglen = length // LRU_SEGMENTS
        nc = length // q

        def gates(c, _):
            r0 = pl.multiple_of(c * q, q)
            xc = _conv_chunk(seq_ref, LRU_WIDTH, LRU_WIDTH, length, r0, cw_ref, cb_ref)
            gt = _dot(xc.astype(BF16), wg_ref[...]) + bg_ref[...]
            for d in range(2):
                r = _sigmoid(gt[:, d * 512:d * 512 + 256])
                ig = _sigmoid(gt[:, d * 512 + 256:(d + 1) * 512])
                la = clam[:, d * 256:(d + 1) * 256] * r
                a = jnp.exp(la)
                bv = jnp.sqrt(-jnp.tanh(la) * (a * a + 1.0)) * (ig * xc)
                for hf in range(2):
                    a_scr[2 * d + hf, pl.ds(r0, q), :] = a[:, hf * LANES:(hf + 1) * LANES]
                    b_scr[2 * d + hf, pl.ds(r0, q), :] = bv[:, hf * LANES:(hf + 1) * LANES]
            return 0

        lax.fori_loop(0, nc, gates, 0)

        def scan(k, carry):
            hs, ps = carry
            new_h, new_p = [], []
            for qd in range(4):
                s = k if qd < 2 else seglen - 1 - k
                idx = (qd, pl.ds(s, LRU_SEGMENTS, stride=seglen), slice(None))
                a = a_scr[idx]
                h = a * hs[qd] + b_scr[idx]
                p = ps[qd] * a
                b_scr[idx] = h
                a_scr[idx] = p
                new_h.append(h)
                new_p.append(p)
            return tuple(new_h), tuple(new_p)

        z8 = jnp.zeros((LRU_SEGMENTS, LANES), F32)
        o8 = jnp.ones((LRU_SEGMENTS, LANES), F32)
        hend, pend = lax.fori_loop(0, seglen, scan, ((z8,) * 4, (o8,) * 4))

        h_out = []
        for qd in range(4):
            c = h0[qd]
            order = range(LRU_SEGMENTS) if qd < 2 else range(LRU_SEGMENTS - 1, -1, -1)
            for k in order:
                cin_scr[qd, k:k + 1, :] = c
                c = hend[qd][k:k + 1, :] + pend[qd][k:k + 1, :] * c
            h_out.append(c)

        oc_rows = min(q, seglen)

        def emit(c, _):
            r0 = pl.multiple_of(c * oc_rows, oc_rows)
            seg = r0 // seglen
            hq = [b_scr[qd, pl.ds(r0, oc_rows), :] + a_scr[qd, pl.ds(r0, oc_rows), :] * cin_scr[qd, pl.ds(seg, 1), :]
                  for qd in range(4)]
            hsum = jnp.concatenate([hq[0] + hq[2], hq[1] + hq[3]], axis=1)
            gate = seq_ref[pl.ds(r0, oc_rows), 0:LRU_WIDTH].astype(F32)
            out_ref[pl.ds(r0, oc_rows), :] = (hsum * _gelu_tanh(gate)).astype(out_ref.dtype)
            return 0

        lax.fori_loop(0, length // oc_rows, emit, 0)
        return h_out

    z1 = jnp.zeros((1, LANES), F32)
    states = run(pc_ref, oc_ref, [z1] * 4)
    run(pl_ref, ol_ref, states)


def _lru_mixer(p_lat, p_ctx, conv_w, conv_b, w_gates, b_gates, lam, batch):
    seq = p_lat.shape[0] // batch
    lc = p_ctx.shape[0] // batch
    const = lambda *shape: pl.BlockSpec(shape, lambda b: (0,) * len(shape))
    return pl.pallas_call(
        _lru_kernel,
        out_shape=(jax.ShapeDtypeStruct((batch * seq, LRU_WIDTH), BF16),
                   jax.ShapeDtypeStruct((batch * lc, LRU_WIDTH), BF16)),
        grid=(batch,),
        in_specs=[pl.BlockSpec((seq, 512), lambda b: (b, 0)),
                  pl.BlockSpec((lc, 512), lambda b: (b, 0)),
                  const(SSD_CONV, LRU_WIDTH), const(1, LRU_WIDTH), const(LRU_WIDTH, 1024), const(1, 1024),
                  const(1, 512)],
        out_specs=(pl.BlockSpec((seq, LRU_WIDTH), lambda b: (b, 0)),
                   pl.BlockSpec((lc, LRU_WIDTH), lambda b: (b, 0))),
        scratch_shapes=[pltpu.VMEM((4, seq, LANES), F32), pltpu.VMEM((4, seq, LANES), F32),
                        pltpu.VMEM((4, LRU_SEGMENTS, LANES), F32)],
        compiler_params=_cparams(("parallel",), 48),
        name="lru_mixer",
    )(p_lat, p_ctx, conv_w, conv_b, w_gates, b_gates, lam)


def _stack_heads(qblk):
    lo = _lane_iota((qblk.shape[0], LANES)) < HEAD_DIM
    zero = jnp.zeros((), qblk.dtype)
    parts = []
    for slab in range(2):
        qs = qblk[:, slab * LANES:(slab + 1) * LANES]
        parts += [jnp.where(lo, qs, zero), jnp.where(lo, zero, qs)]
    return jnp.concatenate(parts, axis=0)


def _unstack_heads(acc, t):
    lo = _lane_iota((t, LANES)) < HEAD_DIM
    return jnp.concatenate([jnp.where(lo, acc[0:t], acc[t:2 * t]),
                            jnp.where(lo, acc[2 * t:3 * t], acc[3 * t:4 * t])], axis=1)


def _gattn_kernel(q_ref, kl_ref, vl_ref, kc_ref, vc_ref, o_ref, *, tk):
    tq = q_ref.shape[0]
    qs = _stack_heads(q_ref[...])
    rows = 4 * tq

    def step(k, v, carry):
        m, l, acc = carry
        s = _dot_nt(qs, k)
        m_new = jnp.maximum(m, jnp.max(s, axis=-1, keepdims=True))
        alpha = jnp.exp(m - m_new)
        p = jnp.exp(s - m_new)
        l = alpha * l + jnp.sum(p, axis=-1, keepdims=True)
        acc = alpha * acc + _dot(p.astype(BF16), v)
        return m_new, l, acc

    def body(j, carry):
        r0 = pl.multiple_of(j * tk, tk)
        return step(kl_ref[pl.ds(r0, tk), :], vl_ref[pl.ds(r0, tk), :], carry)

    init = (jnp.full((rows, 1), -jnp.inf, F32), jnp.zeros((rows, 1), F32), jnp.zeros((rows, LANES), F32))
    carry = lax.fori_loop(0, kl_ref.shape[0] // tk, body, init)
    m, l, acc = step(kc_ref[...], vc_ref[...], carry)
    o_ref[...] = _unstack_heads(acc / l, tq).astype(o_ref.dtype)


def _global_attention(qkv_lat, qkv_ctx, batch, tq, tk):
    seq = qkv_lat.shape[0] // batch
    lc = qkv_ctx.shape[0] // batch
    nq = seq // tq
    return pl.pallas_call(
        functools.partial(_gattn_kernel, tk=tk),
        out_shape=jax.ShapeDtypeStruct((batch * seq, 256), BF16),
        grid=(batch, nq),
        in_specs=[pl.BlockSpec((tq, 256), lambda b, i: (b * nq + i, 0)),
                  pl.BlockSpec((seq, LANES), lambda b, i: (b, 2)),
                  pl.BlockSpec((seq, LANES), lambda b, i: (b, 3)),
                  pl.BlockSpec((lc, LANES), lambda b, i: (b, 2)),
                  pl.BlockSpec((lc, LANES), lambda b, i: (b, 3))],
        out_specs=pl.BlockSpec((tq, 256), lambda b, i: (b * nq + i, 0)),
        compiler_params=_cparams(("parallel", "parallel"), 48),
        name="global_attention",
    )(qkv_lat, qkv_lat, qkv_lat, qkv_ctx, qkv_ctx)


def _sink_rows(sink_ref, t):
    head_of_row = _row_iota((4 * t, 1)) // t
    out = jnp.zeros((4 * t, 1), F32)
    for pos, h in enumerate(Q_HEAD_ORDER):
        out = jnp.where(head_of_row == pos, sink_ref[h], out)
    return out


def _wattn_kernel(sink_ref, q_ref, kl_ref, vl_ref, kc_ref, vc_ref, o_ref):
    w = WINDOW
    seq = kl_ref.shape[0]
    nblk = q_ref.shape[0] // w
    band = 3 * w
    sink = _sink_rows(sink_ref, w)
    kc, vc = kc_ref[...], vc_ref[...]
    qi = _row_iota((4 * w, band)) % w
    kj = _lane_iota((4 * w, band))

    def body(n, _):
        r0 = pl.multiple_of(n * w, w)
        gq = pl.program_id(1) * q_ref.shape[0] + r0
        ks = pl.multiple_of(jnp.clip(gq - w, 0, seq - band), w)
        qs = _stack_heads(q_ref[pl.ds(r0, w), :])
        s_band = _dot_nt(qs, kl_ref[pl.ds(ks, band), :])
        s_band = jnp.where(jnp.abs((ks + kj) - (gq + qi)) <= w, s_band, -jnp.inf)
        s_ctx = _dot_nt(qs, kc)
        m = jnp.maximum(jnp.maximum(jnp.max(s_band, axis=-1, keepdims=True),
                                    jnp.max(s_ctx, axis=-1, keepdims=True)), sink)
        p_band = jnp.exp(s_band - m)
        p_ctx = jnp.exp(s_ctx - m)
        den = (jnp.sum(p_band, axis=-1, keepdims=True) + jnp.sum(p_ctx, axis=-1, keepdims=True)
               + jnp.exp(sink - m))
        acc = _dot(p_band.astype(BF16), vl_ref[pl.ds(ks, band), :]) + _dot(p_ctx.astype(BF16), vc)
        o_ref[pl.ds(r0, w), :] = _unstack_heads(acc / den, w).astype(o_ref.dtype)
        return 0

    lax.fori_loop(0, nblk, body, 0)


def _window_attention(sink, qkv_lat, qkv_ctx, batch, tq):
    seq = qkv_lat.shape[0] // batch
    lc = qkv_ctx.shape[0] // batch
    nq = seq // tq
    return pl.pallas_call(
        _wattn_kernel,
        out_shape=jax.ShapeDtypeStruct((batch * seq, 256), BF16),
        grid=(batch, nq),
        in_specs=[pl.BlockSpec(memory_space=pltpu.SMEM),
                  pl.BlockSpec((tq, 256), lambda b, i: (b * nq + i, 0)),
                  pl.BlockSpec((seq, LANES), lambda b, i: (b, 2)),
                  pl.BlockSpec((seq, LANES), lambda b, i: (b, 3)),
                  pl.BlockSpec((lc, LANES), lambda b, i: (b, 2)),
                  pl.BlockSpec((lc, LANES), lambda b, i: (b, 3))],
        out_specs=pl.BlockSpec((tq, 256), lambda b, i: (b * nq + i, 0)),
        compiler_params=_cparams(("parallel", "parallel"), 32),
        name="window_attention",
    )(sink, qkv_lat, qkv_lat, qkv_lat, qkv_ctx, qkv_ctx)


def _cattn_kernel(sink_ref, c_ref, d_ref, oc_ref, od_ref):
    t = c_ref.shape[0]
    for ref, out, has_sink in ((c_ref, oc_ref, False), (d_ref, od_ref, True)):
        qs = _stack_heads(ref[:, 0:256])
        s = _dot_nt(qs, ref[:, 256:384])
        m = jnp.max(s, axis=-1, keepdims=True)
        if has_sink:
            sink = _sink_rows(sink_ref, t)
            m = jnp.maximum(m, sink)
        p = jnp.exp(s - m)
        den = jnp.sum(p, axis=-1, keepdims=True)
        if has_sink:
            den = den + jnp.exp(sink - m)
        acc = _dot(p.astype(BF16), ref[:, 384:512])
        out[...] = _unstack_heads(acc / den, t).astype(out.dtype)


def _context_attention(sink, qkv_c, qkv_d, batch):
    lc = qkv_c.shape[0] // batch
    blk = lambda width: pl.BlockSpec((lc, width), lambda b: (b, 0))
    return pl.pallas_call(
        _cattn_kernel,
        out_shape=(jax.ShapeDtypeStruct((batch * lc, 256), BF16),) * 2,
        grid=(batch,),
        in_specs=[pl.BlockSpec(memory_space=pltpu.SMEM), blk(512), blk(512)],
        out_specs=(blk(256), blk(256)),
        compiler_params=_cparams(("parallel",), 32),
        name="context_attention",
    )(sink, qkv_c, qkv_d)


def _out_ffn_kernel(x_ref, ya_ref, yb_ref, yc_ref, yd_ref, mod_ref, wo_ref, g_ref, w1_ref, w2_ref, o_ref, *, hchunk):
    m = mod_ref[0]
    gw = GROUP_WIDTH
    mix = _dot(ya_ref[...], wo_ref[0:gw, :])
    for i, y_ref in enumerate((yb_ref, yc_ref, yd_ref), start=1):
        mix = mix + _dot(y_ref[...], wo_ref[i * gw:(i + 1) * gw, :])
    x1 = x_ref[...] + m[2:3] * mix
    ms = jnp.mean(x1 * x1, axis=-1, keepdims=True)
    h = ((x1 * lax.rsqrt(ms + EPS) * g_ref[...]) * (1.0 + m[4:5]) + m[3:4]).astype(BF16)
    acc = None
    for c in range(w1_ref.shape[1] // hchunk):
        u = jnp.maximum(_dot(h, w1_ref[:, c * hchunk:(c + 1) * hchunk]), 0.0)
        t = _dot((u * u).astype(BF16), w2_ref[c * hchunk:(c + 1) * hchunk, :])
        acc = t if acc is None else acc + t
    o_ref[...] = x1 + m[5:6] * acc


def _out_ffn(x2d, ys, mod, mod_row_of_tile, w_out, g, w1, w2, tm):
    rows, d = x2d.shape
    hidden = w1.shape[1]
    row_blk = lambda width: pl.BlockSpec((tm, width), lambda i: (i, 0))
    const = lambda *shape: pl.BlockSpec(shape, lambda i: (0,) * len(shape), pipeline_mode=pl.Buffered(1))
    return pl.pallas_call(
        functools.partial(_out_ffn_kernel, hchunk=1024),
        out_shape=jax.ShapeDtypeStruct((rows, d), F32),
        grid=(rows // tm,),
        in_specs=[row_blk(d), row_blk(256), row_blk(256), row_blk(256), row_blk(256),
                  pl.BlockSpec((1, N_MOD, d), lambda i: (mod_row_of_tile(i), 0, 0)),
                  const(d, d), pl.BlockSpec((1, d), lambda i: (0, 0)), const(d, hidden), const(hidden, d)],
        out_specs=row_blk(d),
        compiler_params=_cparams(("parallel",), 56),
        name="out_ffn",
    )(x2d, *ys, mod, w_out, g, w1, w2)


def _permute_q_heads(cols):
    q = cols[..., 0:256].reshape(cols.shape[:-1] + (Q_HEADS, HEAD_DIM))
    q = q[..., Q_HEAD_ORDER, :].reshape(cols.shape[:-1] + (256,))
    return jnp.concatenate([q, cols[..., 256:512]], axis=-1)


def _relayout_w_in(w):
    d = w.shape[0]
    ssd_in = GROUP_WIDTH + SSD_CONV_CH + 2 * SSD_HEADS
    a0, b0, c0, d0 = 0, ssd_in, ssd_in + 512, ssd_in + 1024
    dt = jnp.pad(w[:, GROUP_WIDTH + SSD_CONV_CH:ssd_in], ((0, 0), (0, LANES - 2 * SSD_HEADS)))
    return jnp.concatenate([w[:, a0:GROUP_WIDTH + SSD_CONV_CH], w[:, b0:c0], _permute_q_heads(w[:, c0:d0]),
                            _permute_q_heads(w[:, d0:d0 + 512]), dt], axis=1).astype(BF16)


def _relayout_w_out(w):
    rows = []
    for base in (512, 768):
        blk = w[base:base + 256].reshape(Q_HEADS, HEAD_DIM, -1)
        rows.append(blk[jnp.array(Q_HEAD_ORDER)].reshape(256, -1))
    return jnp.concatenate([w[0:512]] + rows, axis=0).astype(BF16)


def _block_diag(w):
    out = jnp.zeros((LRU_WIDTH, LRU_WIDTH), w.dtype)
    bw = LRU_WIDTH // LRU_BLOCKS
    for k in range(LRU_BLOCKS):
        out = out.at[k * bw:(k + 1) * bw, k * bw:(k + 1) * bw].set(w[k])
    return out


def _rope_tables(length):
    rows = length // GRID_W
    row = jnp.repeat(jnp.arange(rows, dtype=F32), GRID_W)
    col = jnp.tile(jnp.arange(GRID_W, dtype=F32), rows)
    inv = ROPE_THETA ** (-jnp.arange(0, ROPE_AXIS_DIM, 2, dtype=F32) / ROPE_AXIS_DIM)
    half = ROPE_AXIS_DIM // 2
    ang = jnp.concatenate([row[:, None] * inv] * 2 + [col[:, None] * inv] * 2, axis=1)
    sign = jnp.tile(jnp.concatenate([-jnp.ones((half,), F32), jnp.ones((half,), F32)]), 2)
    cos = jnp.tile(jnp.cos(ang), (1, LANES // HEAD_DIM))
    sin = jnp.tile(jnp.sin(ang) * sign, (1, LANES // HEAD_DIM))
    return cos, sin


def _head_selectors():
    sel = np.zeros((2, LANES, SSD_HEADS * LANES), np.float32)
    for d in range(2):
        for h in range(SSD_HEADS):
            sel[d, SSD_HEADS * d + h, h * LANES:(h + 1) * LANES] = 1.0
    return jnp.asarray(sel, BF16)


def _head_mean_matrix():
    idx = np.arange(256) // HEAD_DIM
    return jnp.asarray((idx[:, None] == idx[None, :]).astype(np.float32) / HEAD_DIM, BF16)


def _pad_lanes(v):
    v = v.reshape(1, -1)
    return jnp.pad(v, ((0, 0), (0, LANES - v.shape[1])))


def kernel(x, c, ctx, c_ctx, w_mod, b_mod, g_mix, w_in, ssd_conv_w, ssd_conv_b, ssd_a_log, ssd_dt_bias, ssd_d, ssd_norm_g, lru_conv_w, lru_conv_b, lru_lambda, lru_w_a, lru_b_a, lru_w_i, lru_b_i, gqa_q_norm, gqa_k_norm, swa_q_norm, swa_k_norm, swa_sink, w_out, g_ffn, w_ffn1, w_ffn2):
    batch, seq, d = x.shape
    lc = ctx.shape[1]
    depth = w_mod.shape[0]
    assert batch + 1 <= 8 and seq % 1024 == 0 and lc % 256 == 0 and seq % GRID_W == 0

    cvecs = jnp.concatenate([c, c_ctx[None, :], jnp.zeros((8 - batch - 1, d), F32)], axis=0)
    mod = _modulation(cvecs, w_mod, b_mod).reshape(depth * 8, N_MOD, d)

    rope = _rope_tables(seq)
    sel = _head_selectors()
    bd = _head_mean_matrix()
    tm = 512
    tiles_per_batch = seq // tm

    xl = x.reshape(batch * seq, d)
    xc = ctx.reshape(batch * lc, d)
    for l in range(depth):
        need_ctx = l < depth - 1
        lat_row = lambda i, l=l: l * 8 + i // tiles_per_batch
        ctx_row = lambda i, l=l: l * 8 + batch
        w_in_l = _relayout_w_in(w_in[l])
        nrm = jnp.stack([jnp.tile(v, 256 // HEAD_DIM) for v in (gqa_q_norm[l], gqa_k_norm[l], swa_q_norm[l], swa_k_norm[l])])
        g1 = g_mix[l].reshape(1, d)
        ssd_l, lru_l, qc_l, qd_l, dt_l = _in_projection(xl, mod, lat_row, g1, w_in_l, nrm, bd, rope, tm, tiles_per_batch)
        ssd_c, lru_c, qc_c, qd_c, dt_c = _in_projection(xc, mod, ctx_row, g1, w_in_l, nrm, bd, None, 256, 1)

        head_params = jnp.concatenate([_pad_lanes(ssd_dt_bias[l]), _pad_lanes(ssd_a_log[l]), jnp.zeros((6, LANES), F32)], axis=0)
        ya_l, ya_c = _ssd_mixer(ssd_l, ssd_c, dt_l, dt_c, ssd_conv_w[l], ssd_conv_b[l].reshape(1, -1), head_params, sel,
                                jnp.repeat(ssd_d[l], HEAD_DIM).reshape(1, -1), ssd_norm_g[l].reshape(1, -1), batch)

        w_gates = jnp.concatenate([_block_diag(lru_w_a[l, 0]), _block_diag(lru_w_i[l, 0]),
                                   _block_diag(lru_w_a[l, 1]), _block_diag(lru_w_i[l, 1])], axis=1).astype(BF16)
        b_gates = jnp.concatenate([lru_b_a[l, 0], lru_b_i[l, 0], lru_b_a[l, 1], lru_b_i[l, 1]]).reshape(1, -1)
        yb_l, yb_c = _lru_mixer(lru_l, lru_c, lru_conv_w[l], lru_conv_b[l].reshape(1, -1), w_gates, b_gates,
                                lru_lambda[l].reshape(1, -1), batch)

        yc_l = _global_attention(qc_l, qc_c, batch, 256, 512)
        yd_l = _window_attention(swa_sink[l], qd_l, qd_c, batch, 1024)

        w_out_l = _relayout_w_out(w_out[l])
        g2 = g_ffn[l].reshape(1, d)
        w1 = w_ffn1[l].astype(BF16)
        w2 = w_ffn2[l].astype(BF16)
        xl = _out_ffn(xl, (ya_l, yb_l, yc_l, yd_l), mod, lat_row, w_out_l, g2, w1, w2, tm)
        if need_ctx:
            yc_c, yd_c = _context_attention(swa_sink[l], qc_c, qd_c, batch)
            xc = _out_ffn(xc, (ya_c, yb_c, yc_c, yd_c), mod, ctx_row, w_out_l, g2, w1, w2, 256)
    return xl.reshape(batch, seq, d)
```

```python
import functools
import math

import numpy as np
import jax
import jax.numpy as jnp
from jax import lax
from jax.experimental import pallas as pl
from jax.experimental.pallas import tpu as pltpu

F32 = jnp.float32
BF16 = jnp.bfloat16

EPS = 1e-6
GRID_W = 64
N_MOD = 6
GROUP_WIDTH = 256
SSD_HEADS = 4
SSD_STATE = 128
SSD_CONV = 4
SSD_CHUNK = 128
SSD_CONV_CH = 768
LRU_WIDTH = 256
LRU_BLOCKS = 4
LRU_C = 8.0
HEAD_DIM = 64
Q_HEADS = 4
WINDOW = 128
ROPE_THETA = 10000.0
ROPE_AXIS_DIM = 32
LOG2_E = math.log2(math.e)
Q_SCALE = HEAD_DIM ** -0.5 * LOG2_E

V7X_VMEM_BYTES = 64 * 1024 * 1024
LANES = 128
BF16_ROWS = 16

COL_SSD, COL_LRU, COL_C, COL_D, COL_DT, COL_END = 0, 1024, 1536, 2048, 2560, 2688
Q_HEAD_ORDER = (0, 2, 1, 3)

CONV_WIN = SSD_CHUNK + 2 * BF16_ROWS


def _cparams(sem, vmem_mib):
    return pltpu.CompilerParams(dimension_semantics=sem, vmem_limit_bytes=vmem_mib * 1024 * 1024)


def _dot(a, b):
    return jnp.dot(a, b, preferred_element_type=F32)


def _dot_nt(a, b):
    return lax.dot_general(a, b, (((1,), (1,)), ((), ())), preferred_element_type=F32)


def _split_bf16(a, parts):
    out = []
    for _ in range(parts - 1):
        hi = a.astype(BF16)
        out.append(hi)
        a = a - hi.astype(F32)
    out.append(a.astype(BF16))
    return out


def _dot_exact_rhs(a, b_bf16, parts=3):
    acc = None
    for p in _split_bf16(a, parts):
        t = _dot(p, b_bf16)
        acc = t if acc is None else acc + t
    return acc


def _dot_exact_lhs(a_bf16, b, parts=3):
    acc = None
    for p in _split_bf16(b, parts):
        t = _dot(a_bf16, p)
        acc = t if acc is None else acc + t
    return acc


def _sigmoid(x):
    return 0.5 * jnp.tanh(0.5 * x) + 0.5


def _silu(x):
    return x * _sigmoid(x)


def _softplus(x):
    return jnp.maximum(x, 0.0) + jnp.log1p(jnp.exp(-jnp.abs(x)))


def _gelu_tanh(x):
    return 0.5 * x * (1.0 + jnp.tanh(math.sqrt(2.0 / math.pi) * (x + 0.044715 * (x * x * x))))


def _lane_iota(shape):
    return lax.broadcasted_iota(jnp.int32, shape, len(shape) - 1)


def _row_iota(shape):
    return lax.broadcasted_iota(jnp.int32, shape, len(shape) - 2)


def _mod_kernel(cv_ref, w_ref, b_ref, o_ref):
    cv = cv_ref[...]
    s = _silu(cv)
    w = w_ref[0]
    s_hi, s_lo = _split_bf16(s, 2)
    w_hi, w_lo = _split_bf16(w, 2)
    acc = _dot(s_hi, w_hi) + _dot(s_lo, w_hi) + _dot(s_hi, w_lo)
    o_ref[0] = acc + b_ref[0]


def _modulation(cvecs, w_mod, b_mod):
    depth, d, n = w_mod.shape
    tn = 1024
    return pl.pallas_call(
        _mod_kernel,
        out_shape=jax.ShapeDtypeStruct((depth, 8, n), F32),
        grid=(depth, n // tn),
        in_specs=[pl.BlockSpec((8, d), lambda l, j: (0, 0)),
                  pl.BlockSpec((1, d, tn), lambda l, j: (l, 0, j)),
                  pl.BlockSpec((1, 1, tn), lambda l, j: (l, 0, j))],
        out_specs=pl.BlockSpec((1, 8, tn), lambda l, j: (l, 0, j)),
        compiler_params=_cparams(("parallel", "parallel"), 32),
        name="modulation",
    )(cvecs, w_mod, b_mod.reshape(depth, 1, n))


def _head_norm_rope(p, g, bd, cos, sin, scale):
    ms = _dot_exact_rhs(p * p, bd, parts=2)
    y = p * lax.rsqrt(ms + EPS) * g
    if cos is not None:
        w = y.shape[-1]
        first = (_lane_iota(y.shape) % ROPE_AXIS_DIM) < (ROPE_AXIS_DIM // 2)
        partner = jnp.where(first, pltpu.roll(y, w - ROPE_AXIS_DIM // 2, 1), pltpu.roll(y, ROPE_AXIS_DIM // 2, 1))
        y = y * cos + partner * sin
    if scale != 1.0:
        y = y * scale
    return y


def _inproj_kernel(*refs, use_rope):
    if use_rope:
        x_ref, mod_ref, g_ref, w_ref, nrm_ref, bd_ref, cos_ref, sin_ref, ssd_ref, lru_ref, c_ref, d_ref, dt_ref = refs
    else:
        x_ref, mod_ref, g_ref, w_ref, nrm_ref, bd_ref, ssd_ref, lru_ref, c_ref, d_ref, dt_ref = refs
    x = x_ref[...]
    ms = jnp.mean(x * x, axis=-1, keepdims=True)
    m = mod_ref[0]
    h = (x * lax.rsqrt(ms + EPS) * g_ref[...]) * (1.0 + m[1:2]) + m[0:1]
    hb = h.astype(BF16)
    ssd_ref[...] = _dot(hb, w_ref[:, COL_SSD:COL_LRU]).astype(BF16)
    lru_ref[...] = _dot(hb, w_ref[:, COL_LRU:COL_C]).astype(BF16)
    dt_ref[...] = _dot(hb, w_ref[:, COL_DT:COL_END])
    bd = bd_ref[...]
    if use_rope:
        cos1, sin1 = cos_ref[...], sin_ref[...]
        cos2 = jnp.concatenate([cos1, cos1], axis=1)
        sin2 = jnp.concatenate([sin1, sin1], axis=1)
    else:
        cos1 = sin1 = cos2 = sin2 = None
    for o_ref, c0, row in ((c_ref, COL_C, 0), (d_ref, COL_D, 2)):
        p = _dot(hb, w_ref[:, c0:c0 + 512])
        q = _head_norm_rope(p[:, 0:256], nrm_ref[row:row + 1, :], bd, cos2, sin2, Q_SCALE)
        k = _head_norm_rope(p[:, 256:384], nrm_ref[row + 1:row + 2, 0:128], bd[0:128, 0:128], cos1, sin1, 1.0)
        o_ref[:, 0:256] = q.astype(BF16)
        o_ref[:, 256:384] = k.astype(BF16)
        o_ref[:, 384:512] = p[:, 384:512].astype(BF16)


def _in_projection(x2d, mod, mod_row_of_tile, g, w, nrm, bd, rope, tm, rope_tiles):
    rows, d = x2d.shape
    use_rope = rope is not None
    in_specs = [pl.BlockSpec((tm, d), lambda i: (i, 0)),
                pl.BlockSpec((1, N_MOD, d), lambda i: (mod_row_of_tile(i), 0, 0)),
                pl.BlockSpec((1, d), lambda i: (0, 0)),
                pl.BlockSpec((d, COL_END), lambda i: (0, 0), pipeline_mode=pl.Buffered(1)),
                pl.BlockSpec((4, 256), lambda i: (0, 0)),
                pl.BlockSpec((256, 256), lambda i: (0, 0))]
    args = [x2d, mod, g, w, nrm, bd]
    if use_rope:
        in_specs += [pl.BlockSpec((tm, LANES), lambda i: (i % rope_tiles, 0))] * 2
        args += list(rope)
    widths = (1024, 512, 512, 512, LANES)
    dtypes = (BF16, BF16, BF16, BF16, F32)
    return pl.pallas_call(
        functools.partial(_inproj_kernel, use_rope=use_rope),
        out_shape=tuple(jax.ShapeDtypeStruct((rows, wd), dt) for wd, dt in zip(widths, dtypes)),
        grid=(rows // tm,),
        in_specs=in_specs,
        out_specs=tuple(pl.BlockSpec((tm, wd), lambda i: (i, 0)) for wd in widths),
        compiler_params=_cparams(("parallel",), 48),
        name="in_projection",
    )(*args)


def _conv_selectors():
    row = np.arange(SSD_CONV * SSD_CHUNK)
    want = (row % SSD_CHUNK) + (row // SSD_CHUNK) - SSD_CONV // 2
    col = np.arange(CONV_WIN)
    sel = np.stack([(col[None, :] == want[:, None] + BF16_ROWS * case) for case in range(3)])
    return jnp.asarray(sel.astype(np.float32), BF16)


def _conv_chunk(seq_ref, col0, width, length, r0, sel_ref, w_ref, b_ref):
    ws = jnp.clip(r0 - BF16_ROWS, 0, length - CONV_WIN)
    ws = pl.multiple_of(ws, BF16_ROWS)
    xw = seq_ref[pl.ds(ws, CONV_WIN), col0:col0 + width]
    sh = _dot(sel_ref[(r0 - ws) // BF16_ROWS], xw)
    w = w_ref[...]
    acc = b_ref[...] + sh[0:SSD_CHUNK] * w[0:1]
    for k in range(1, SSD_CONV):
        acc = acc + sh[k * SSD_CHUNK:(k + 1) * SSD_CHUNK] * w[k:k + 1]
    return acc


def _ssd_kernel(pl_ref, pc_ref, dtl_ref, dtc_ref, csel_ref, cw_ref, cb_ref, hp_ref, sel_ref, dsk_ref, ng_ref,
                yl_ref, yc_ref, accl_ref, accc_ref, xbcl_ref, xbcc_ref):
    q = SSD_CHUNK
    lane8 = _lane_iota((1, LANES)) < 2 * SSD_HEADS
    dt_bias = hp_ref[0:1, :]
    a_row = jnp.where(lane8, -jnp.exp(hp_ref[1:2, :]), 0.0)
    ii = _row_iota((q, q))
    jj = _lane_iota((q, q))
    lo = _lane_iota((q, LANES)) < HEAD_DIM

    def chunk(seq_ref, xbc_ref, dt_ref, length, r0, d, state):
        if d == 0:
            conv = _conv_chunk(seq_ref, GROUP_WIDTH, SSD_CONV_CH, length, r0, csel_ref, cw_ref, cb_ref)
            xbc_ref[pl.ds(r0, q), :] = _silu(conv).astype(BF16)
        xbc = xbc_ref[pl.ds(r0, q), :].astype(F32)
        xs, bm, cm = xbc[:, 0:256], xbc[:, 256:512], xbc[:, 512:768]
        dtv = _softplus(dt_ref[pl.ds(r0, q), :] + dt_bias)
        da = dtv * a_row
        mask = (jj <= ii) if d == 0 else (jj >= ii)
        tri = jnp.where(mask, 1.0, 0.0).astype(BF16)
        acum = _dot_exact_lhs(tri, da, parts=2)
        ex = _dot_exact_rhs(jnp.concatenate([dtv, acum], axis=0), sel_ref[d], parts=2)
        dtb, acb = ex[0:q], ex[q:2 * q]
        act = acum.T
        edge = q - 1 if d == 0 else 0
        ys, new_state = [], []
        for g in range(2):
            per_head = []
            for hh in range(2):
                h = 2 * g + hh
                ac = acb[:, h * LANES:(h + 1) * LANES]
                ar = act[SSD_HEADS * d + h:SSD_HEADS * d + h + 1, :]
                lmat = jnp.where(mask, jnp.exp(ac - ar), 0.0)
                alast = ac[edge:edge + 1, :]
                dth = dtb[:, h * LANES:(h + 1) * LANES]
                per_head.append((lmat, dth, jnp.exp(alast - ac) * dth, jnp.exp(ac), jnp.exp(alast)))
            (l0, dt0, we0, ei0, cd0), (l1, dt1, we1, ei1, cd1) = per_head
            bt = bm[:, g * LANES:(g + 1) * LANES].T.astype(BF16)
            cg = cm[:, g * LANES:(g + 1) * LANES].astype(BF16)
            xg = xs[:, g * LANES:(g + 1) * LANES]
            cb = _dot(cg, bt)
            xdt = xg * jnp.where(lo, dt0, dt1)
            y = (_dot((cb * l0).astype(BF16), jnp.where(lo, xdt, 0.0).astype(BF16))
                 + _dot((cb * l1).astype(BF16), jnp.where(lo, 0.0, xdt).astype(BF16)))
            y = y + _dot(cg, state[g].astype(BF16)) * jnp.where(lo, ei0, ei1)
            s_new = _dot(bt, (xg * jnp.where(lo, we0, we1)).astype(BF16))
            new_state.append(state[g] * jnp.where(lo, cd0, cd1) + s_new)
            ys.append(y)
        return jnp.concatenate(ys, axis=1), xs, tuple(new_state)

    def sweep(seq_ref, xbc_ref, dt_ref, acc_ref, out_ref, d, state):
        length = seq_ref.shape[0]
        nc = length // q

        def body(i, st):
            c = i if d == 0 else nc - 1 - i
            r0 = pl.multiple_of(c * q, q)
            y, xs, st = chunk(seq_ref, xbc_ref, dt_ref, length, r0, d, st)
            if d == 0:
                acc_ref[pl.ds(r0, q), :] = y
            else:
                y = acc_ref[pl.ds(r0, q), :] + y + dsk_ref[...] * xs
                z = seq_ref[pl.ds(r0, q), 0:GROUP_WIDTH].astype(F32)
                t = y * _silu(z)
                ms = jnp.mean(t * t, axis=-1, keepdims=True)
                out_ref[pl.ds(r0, q), :] = (t * lax.rsqrt(ms + EPS) * ng_ref[...]).astype(out_ref.dtype)
            return st

        return lax.fori_loop(0, nc, body, state, unroll=2)

    zero = (jnp.zeros((SSD_STATE, LANES), F32), jnp.zeros((SSD_STATE, LANES), F32))
    for d in range(2):
        st = sweep(pc_ref, xbcc_ref, dtc_ref, accc_ref, yc_ref, d, zero)
        sweep(pl_ref, xbcl_ref, dtl_ref, accl_ref, yl_ref, d, st)


def _ssd_mixer(p_lat, p_ctx, dt_lat, dt_ctx, conv_sel, conv_w, conv_b, head_params, sel, d_skip, norm_g, batch):
    seq = p_lat.shape[0] // batch
    lc = p_ctx.shape[0] // batch
    const = lambda *shape: pl.BlockSpec(shape, lambda b: (0,) * len(shape))
    return pl.pallas_call(
        _ssd_kernel,
        out_shape=(jax.ShapeDtypeStruct((batch * seq, GROUP_WIDTH), BF16),
                   jax.ShapeDtypeStruct((batch * lc, GROUP_WIDTH), BF16)),
        grid=(batch,),
        in_specs=[pl.BlockSpec((seq, 1024), lambda b: (b, 0)),
                  pl.BlockSpec((lc, 1024), lambda b: (b, 0)),
                  pl.BlockSpec((seq, LANES), lambda b: (b, 0)),
                  pl.BlockSpec((lc, LANES), lambda b: (b, 0)),
                  const(3, SSD_CONV * SSD_CHUNK, CONV_WIN),
                  const(SSD_CONV, SSD_CONV_CH), const(1, SSD_CONV_CH), const(8, LANES),
                  const(2, LANES, SSD_HEADS * LANES), const(1, GROUP_WIDTH), const(1, GROUP_WIDTH)],
        out_specs=(pl.BlockSpec((seq, GROUP_WIDTH), lambda b: (b, 0)),
                   pl.BlockSpec((lc, GROUP_WIDTH), lambda b: (b, 0))),
        scratch_shapes=[pltpu.VMEM((seq, GROUP_WIDTH), F32), pltpu.VMEM((lc, GROUP_WIDTH), F32),
                        pltpu.VMEM((seq, SSD_CONV_CH), BF16), pltpu.VMEM((lc, SSD_CONV_CH), BF16)],
        compiler_params=_cparams(("parallel",), 48),
        name="ssd_mixer",
    )(p_lat, p_ctx, dt_lat, dt_ctx, conv_sel, conv_w, conv_b, head_params, sel, d_skip, norm_g)


LRU_SEGMENTS = 8


def _lru_kernel(pl_ref, pc_ref, csel_ref, cw_ref, cb_ref, wg_ref, bg_ref, lam_ref, ol_ref, oc_ref, a_scr, b_scr, cin_scr):
    q = SSD_CHUNK
    nseg = LRU_SEGMENTS
    clam = -LRU_C * _softplus(-lam_ref[...])

    def run(seq_ref, out_ref, h0):
        length = seq_ref.shape[0]
        seglen = length // nseg
        sub = min(q, seglen)

        def seg_rows(t0):
            return pl.ds((t0 % seglen) * nseg + t0 // seglen, sub, stride=nseg)

        def gates(c, _):
            r0 = pl.multiple_of(c * q, q)
            xc = _conv_chunk(seq_ref, LRU_WIDTH, LRU_WIDTH, length, r0, csel_ref, cw_ref, cb_ref)
            gt = _dot(xc.astype(BF16), wg_ref[...]) + bg_ref[...]
            for d in range(2):
                r = _sigmoid(gt[:, d * 512:d * 512 + 256])
                ig = _sigmoid(gt[:, d * 512 + 256:(d + 1) * 512])
                la = clam[:, d * 256:(d + 1) * 256] * r
                a = jnp.exp(la)
                bv = jnp.sqrt(-jnp.tanh(la) * (a * a + 1.0)) * (ig * xc)
                for hf in range(2):
                    for j in range(q // sub):
                        rows = seg_rows(r0 + j * sub)
                        a_scr[2 * d + hf, rows, :] = a[j * sub:(j + 1) * sub, hf * LANES:(hf + 1) * LANES]
                        b_scr[2 * d + hf, rows, :] = bv[j * sub:(j + 1) * sub, hf * LANES:(hf + 1) * LANES]
            return 0

        lax.fori_loop(0, length // q, gates, 0, unroll=2)

        def scan(k, carry):
            hs, ps = carry
            new_h, new_p = [], []
            for qd in range(4):
                s = k if qd < 2 else seglen - 1 - k
                idx = (qd, pl.ds(pl.multiple_of(s * nseg, nseg), nseg), slice(None))
                a = a_scr[idx]
                h = a * hs[qd] + b_scr[idx]
                p = ps[qd] * a
                b_scr[idx] = h
                a_scr[idx] = p
                new_h.append(h)
                new_p.append(p)
            return tuple(new_h), tuple(new_p)

        z8 = jnp.zeros((nseg, LANES), F32)
        o8 = jnp.ones((nseg, LANES), F32)
        hend, pend = lax.fori_loop(0, seglen, scan, ((z8,) * 4, (o8,) * 4), unroll=4)

        h_out = []
        for qd in range(4):
            c = h0[qd]
            order = range(nseg) if qd < 2 else range(nseg - 1, -1, -1)
            for k in order:
                cin_scr[qd, k:k + 1, :] = c
                c = hend[qd][k:k + 1, :] + pend[qd][k:k + 1, :] * c
            h_out.append(c)

        def emit(c, _):
            r0 = pl.multiple_of(c * sub, sub)
            rows = seg_rows(r0)
            hq = [b_scr[qd, rows, :] + a_scr[qd, rows, :] * cin_scr[qd, pl.ds(r0 // seglen, 1), :] for qd in range(4)]
            hsum = jnp.concatenate([hq[0] + hq[2], hq[1] + hq[3]], axis=1)
            gate = seq_ref[pl.ds(r0, sub), 0:LRU_WIDTH].astype(F32)
            out_ref[pl.ds(r0, sub), :] = (hsum * _gelu_tanh(gate)).astype(out_ref.dtype)
            return 0

        lax.fori_loop(0, length // sub, emit, 0)
        return h_out

    z1 = jnp.zeros((1, LANES), F32)
    states = run(pc_ref, oc_ref, [z1] * 4)
    run(pl_ref, ol_ref, states)


def _lru_mixer(p_lat, p_ctx, conv_sel, conv_w, conv_b, w_gates, b_gates, lam, batch):
    seq = p_lat.shape[0] // batch
    lc = p_ctx.shape[0] // batch
    const = lambda *shape: pl.BlockSpec(shape, lambda b: (0,) * len(shape))
    return pl.pallas_call(
        _lru_kernel,
        out_shape=(jax.ShapeDtypeStruct((batch * seq, LRU_WIDTH), BF16),
                   jax.ShapeDtypeStruct((batch * lc, LRU_WIDTH), BF16)),
        grid=(batch,),
        in_specs=[pl.BlockSpec((seq, 512), lambda b: (b, 0)),
                  pl.BlockSpec((lc, 512), lambda b: (b, 0)),
                  const(3, SSD_CONV * SSD_CHUNK, CONV_WIN),
                  const(SSD_CONV, LRU_WIDTH), const(1, LRU_WIDTH), const(LRU_WIDTH, 1024), const(1, 1024),
                  const(1, 512)],
        out_specs=(pl.BlockSpec((seq, LRU_WIDTH), lambda b: (b, 0)),
                   pl.BlockSpec((lc, LRU_WIDTH), lambda b: (b, 0))),
        scratch_shapes=[pltpu.VMEM((4, seq, LANES), F32), pltpu.VMEM((4, seq, LANES), F32),
                        pltpu.VMEM((4, LRU_SEGMENTS, LANES), F32)],
        compiler_params=_cparams(("parallel",), 48),
        name="lru_mixer",
    )(p_lat, p_ctx, conv_sel, conv_w, conv_b, w_gates, b_gates, lam)


def _stack_heads(qblk):
    lo = _lane_iota((qblk.shape[0], LANES)) < HEAD_DIM
    zero = jnp.zeros((), qblk.dtype)
    s0, s1 = qblk[:, 0:LANES], qblk[:, LANES:2 * LANES]
    return jnp.concatenate([jnp.where(lo, s0, zero), jnp.where(lo, s1, zero),
                            jnp.where(lo, zero, s0), jnp.where(lo, zero, s1)], axis=0)


def _unstack_heads(acc, t):
    lo = _lane_iota((t, LANES)) < HEAD_DIM
    return jnp.concatenate([jnp.where(lo, acc[0:t], acc[2 * t:3 * t]),
                            jnp.where(lo, acc[t:2 * t], acc[3 * t:4 * t])], axis=1)


MAX_SHIFT_BOUND = 50.0


def _gattn_kernel(bound_ref, q_ref, kl_ref, vl_ref, kc_ref, vc_ref, o_ref, qs_scr, m_scr, acc_scr, *, tk, online):
    tq = q_ref.shape[0]
    half = 2 * tq
    qs_scr[...] = _stack_heads(q_ref[...])
    acc_scr[...] = jnp.zeros(acc_scr.shape, F32)
    if online:
        m_scr[...] = jnp.full(m_scr.shape, -jnp.inf, F32)

    def step(k, v):
        lo = _lane_iota(v.shape) < HEAD_DIM
        one = jnp.ones((), v.dtype)
        s = _dot_nt(qs_scr[...], k)
        if online:
            m_old = m_scr[...]
            shift = jnp.maximum(m_old, jnp.max(s, axis=-1, keepdims=True))
            m_scr[...] = shift
            acc_scr[...] = acc_scr[...] * jnp.exp2(m_old - shift)
        else:
            shift = bound_ref[0]
        p = jnp.exp2(s - shift).astype(BF16)
        acc_scr[0:half, :] += _dot(p[0:half], jnp.where(lo, v, one))
        acc_scr[half:, :] += _dot(p[half:], jnp.where(lo, one, v))

    def body(j, _):
        r0 = pl.multiple_of(j * tk, tk)
        step(kl_ref[pl.ds(r0, tk), :], vl_ref[pl.ds(r0, tk), :])
        return 0

    lax.fori_loop(0, kl_ref.shape[0] // tk, body, 0)
    step(kc_ref[...], vc_ref[...])
    acc = acc_scr[...]
    o_ref[...] = _unstack_heads(acc / pltpu.roll(acc, HEAD_DIM, 1), tq).astype(o_ref.dtype)


def _global_attention(qkv_lat, qkv_ctx, q_gain, k_gain, batch, tq, tk):
    seq = qkv_lat.shape[0] // batch
    lc = qkv_ctx.shape[0] // batch
    nq = seq // tq
    bound = (HEAD_DIM * Q_SCALE * 1.02 * jnp.max(jnp.abs(q_gain)) * jnp.max(jnp.abs(k_gain))).reshape(1)

    def call(online):
        return pl.pallas_call(
            functools.partial(_gattn_kernel, tk=tk, online=online),
            out_shape=jax.ShapeDtypeStruct((batch * seq, 256), BF16),
            grid=(batch, nq),
            in_specs=[pl.BlockSpec(memory_space=pltpu.SMEM),
                      pl.BlockSpec((tq, 256), lambda b, i: (b * nq + i, 0)),
                      pl.BlockSpec((seq, LANES), lambda b, i: (b, 2)),
                      pl.BlockSpec((seq, LANES), lambda b, i: (b, 3)),
                      pl.BlockSpec((lc, LANES), lambda b, i: (b, 2)),
                      pl.BlockSpec((lc, LANES), lambda b, i: (b, 3))],
            out_specs=pl.BlockSpec((tq, 256), lambda b, i: (b * nq + i, 0)),
            scratch_shapes=[pltpu.VMEM((4 * tq, LANES), BF16), pltpu.VMEM((4 * tq, 1), F32),
                            pltpu.VMEM((4 * tq, LANES), F32)],
            compiler_params=_cparams(("parallel", "parallel"), 48),
            name="global_attention_online" if online else "global_attention",
        )(bound, qkv_lat, qkv_lat, qkv_lat, qkv_ctx, qkv_ctx)

    return lax.cond(bound[0] <= MAX_SHIFT_BOUND, lambda: call(False), lambda: call(True))


def _sink_rows(sink_ref, t):
    head_of_row = _row_iota((4 * t, 1)) // t
    out = jnp.zeros((4 * t, 1), F32)
    for h in range(Q_HEADS):
        out = jnp.where(head_of_row == h, sink_ref[h] * LOG2_E, out)
    return out


def _wattn_kernel(sink_ref, q_ref, kl_ref, vl_ref, kc_ref, vc_ref, o_ref, bias_scr):
    w = WINDOW
    seq = kl_ref.shape[0]
    band = 3 * w
    dmat = _lane_iota((w, band)) - _row_iota((w, band))
    for case in range(3):
        bias_scr[case] = jnp.where(jnp.abs(dmat - case * w) <= w, 0.0, -jnp.inf)
    one = jnp.ones((), BF16)
    kc, vc = kc_ref[...], vc_ref[...]
    lo_c = _lane_iota(vc.shape) < HEAD_DIM
    vc_sum = (jnp.where(lo_c, vc, one), jnp.where(lo_c, one, vc))
    lo_b = _lane_iota((band, LANES)) < HEAD_DIM
    lo = _lane_iota((w, LANES)) < HEAD_DIM

    def body(n, _):
        r0 = pl.multiple_of(n * w, w)
        gq = pl.program_id(1) * q_ref.shape[0] + r0
        ks = pl.multiple_of(jnp.clip(gq - w, 0, seq - band), w)
        bias = bias_scr[(gq - ks) // w]
        kb, vb = kl_ref[pl.ds(ks, band), :], vl_ref[pl.ds(ks, band), :]
        vb_sum = (jnp.where(lo_b, vb, one), jnp.where(lo_b, one, vb))
        qs = _stack_heads(q_ref[pl.ds(r0, w), :])
        s_band = _dot_nt(qs, kb)
        s_ctx = _dot_nt(qs, kc)
        outs = []
        for g in range(2):
            pb, pc, sk = [], [], []
            for h in (2 * g, 2 * g + 1):
                sb = s_band[h * w:(h + 1) * w] + bias
                sc = s_ctx[h * w:(h + 1) * w]
                sink = sink_ref[h] * LOG2_E
                m = jnp.maximum(jnp.maximum(jnp.max(sb, axis=-1, keepdims=True),
                                            jnp.max(sc, axis=-1, keepdims=True)), sink)
                pb.append(jnp.exp2((sb - m).astype(BF16)))
                pc.append(jnp.exp2((sc - m).astype(BF16)))
                sk.append(jnp.exp2(sink - m))
            acc = _dot(jnp.concatenate(pb, axis=0), vb_sum[g]) + _dot(jnp.concatenate(pc, axis=0), vc_sum[g])
            den = pltpu.roll(acc, HEAD_DIM, 1) + jnp.concatenate(sk, axis=0)
            outs.append(acc / den)
        o_ref[pl.ds(r0, w), :] = jnp.concatenate(
            [jnp.where(lo, outs[0][0:w], outs[1][0:w]), jnp.where(lo, outs[0][w:2 * w], outs[1][w:2 * w])],
            axis=1).astype(o_ref.dtype)
        return 0

    lax.fori_loop(0, q_ref.shape[0] // w, body, 0)


def _window_attention(sink, qkv_lat, qkv_ctx, batch, tq):
    seq = qkv_lat.shape[0] // batch
    lc = qkv_ctx.shape[0] // batch
    nq = seq // tq
    return pl.pallas_call(
        _wattn_kernel,
        out_shape=jax.ShapeDtypeStruct((batch * seq, 256), BF16),
        grid=(batch, nq),
        in_specs=[pl.BlockSpec(memory_space=pltpu.SMEM),
                  pl.BlockSpec((tq, 256), lambda b, i: (b * nq + i, 0)),
                  pl.BlockSpec((seq, LANES), lambda b, i: (b, 2)),
                  pl.BlockSpec((seq, LANES), lambda b, i: (b, 3)),
                  pl.BlockSpec((lc, LANES), lambda b, i: (b, 2)),
                  pl.BlockSpec((lc, LANES), lambda b, i: (b, 3))],
        out_specs=pl.BlockSpec((tq, 256), lambda b, i: (b * nq + i, 0)),
        scratch_shapes=[pltpu.VMEM((3, WINDOW, 3 * WINDOW), F32)],
        compiler_params=_cparams(("parallel", "parallel"), 32),
        name="window_attention",
    )(sink, qkv_lat, qkv_lat, qkv_lat, qkv_ctx, qkv_ctx)


def _cattn_kernel(sink_ref, c_ref, d_ref, oc_ref, od_ref):
    t = c_ref.shape[0]
    for ref, out, has_sink in ((c_ref, oc_ref, False), (d_ref, od_ref, True)):
        qs = _stack_heads(ref[:, 0:256])
        s = _dot_nt(qs, ref[:, 256:384])
        m = jnp.max(s, axis=-1, keepdims=True)
        if has_sink:
            sink = _sink_rows(sink_ref, t)
            m = jnp.maximum(m, sink)
        p = jnp.exp2(s - m)
        den = jnp.sum(p, axis=-1, keepdims=True)
        if has_sink:
            den = den + jnp.exp2(sink - m)
        acc = _dot(p.astype(BF16), ref[:, 384:512])
        out[...] = _unstack_heads(acc / den, t).astype(out.dtype)


def _context_attention(sink, qkv_c, qkv_d, batch):
    lc = qkv_c.shape[0] // batch
    blk = lambda width: pl.BlockSpec((lc, width), lambda b: (b, 0))
    return pl.pallas_call(
        _cattn_kernel,
        out_shape=(jax.ShapeDtypeStruct((batch * lc, 256), BF16),) * 2,
        grid=(batch,),
        in_specs=[pl.BlockSpec(memory_space=pltpu.SMEM), blk(512), blk(512)],
        out_specs=(blk(256), blk(256)),
        compiler_params=_cparams(("parallel",), 32),
        name="context_attention",
    )(sink, qkv_c, qkv_d)


def _out_ffn_kernel(x_ref, ya_ref, yb_ref, yc_ref, yd_ref, mod_ref, wo_ref, g_ref, w1_ref, w2_ref, o_ref, *, hchunk):
    m = mod_ref[0]
    gw = GROUP_WIDTH
    mix = _dot(ya_ref[...], wo_ref[0:gw, :])
    for i, y_ref in enumerate((yb_ref, yc_ref, yd_ref), start=1):
        mix = mix + _dot(y_ref[...], wo_ref[i * gw:(i + 1) * gw, :])
    x1 = x_ref[...] + m[2:3] * mix
    ms = jnp.mean(x1 * x1, axis=-1, keepdims=True)
    h = ((x1 * lax.rsqrt(ms + EPS) * g_ref[...]) * (1.0 + m[4:5]) + m[3:4]).astype(BF16)
    acc = None
    for c in range(w1_ref.shape[1] // hchunk):
        u = jnp.maximum(_dot(h, w1_ref[:, c * hchunk:(c + 1) * hchunk]), 0.0)
        t = _dot((u * u).astype(BF16), w2_ref[c * hchunk:(c + 1) * hchunk, :])
        acc = t if acc is None else acc + t
    o_ref[...] = x1 + m[5:6] * acc


def _out_ffn(x2d, ys, mod, mod_row_of_tile, w_out, g, w1, w2, tm):
    rows, d = x2d.shape
    hidden = w1.shape[1]
    row_blk = lambda width: pl.BlockSpec((tm, width), lambda i: (i, 0))
    const = lambda *shape: pl.BlockSpec(shape, lambda i: (0,) * len(shape), pipeline_mode=pl.Buffered(1))
    return pl.pallas_call(
        functools.partial(_out_ffn_kernel, hchunk=1024),
        out_shape=jax.ShapeDtypeStruct((rows, d), F32),
        grid=(rows // tm,),
        in_specs=[row_blk(d), row_blk(256), row_blk(256), row_blk(256), row_blk(256),
                  pl.BlockSpec((1, N_MOD, d), lambda i: (mod_row_of_tile(i), 0, 0)),
                  const(d, d), pl.BlockSpec((1, d), lambda i: (0, 0)), const(d, hidden), const(hidden, d)],
        out_specs=row_blk(d),
        compiler_params=_cparams(("parallel",), 56),
        name="out_ffn",
    )(x2d, *ys, mod, w_out, g, w1, w2)


def _permute_q_heads(cols):
    q = cols[..., 0:256].reshape(cols.shape[:-1] + (Q_HEADS, HEAD_DIM))
    q = q[..., Q_HEAD_ORDER, :].reshape(cols.shape[:-1] + (256,))
    return jnp.concatenate([q, cols[..., 256:512]], axis=-1)


def _relayout_w_in(w):
    d = w.shape[0]
    ssd_in = GROUP_WIDTH + SSD_CONV_CH + 2 * SSD_HEADS
    a0, b0, c0, d0 = 0, ssd_in, ssd_in + 512, ssd_in + 1024
    dt = jnp.pad(w[:, GROUP_WIDTH + SSD_CONV_CH:ssd_in], ((0, 0), (0, LANES - 2 * SSD_HEADS)))
    return jnp.concatenate([w[:, a0:GROUP_WIDTH + SSD_CONV_CH], w[:, b0:c0], _permute_q_heads(w[:, c0:d0]),
                            _permute_q_heads(w[:, d0:d0 + 512]), dt], axis=1).astype(BF16)


def _relayout_w_out(w):
    rows = []
    for base in (512, 768):
        blk = w[base:base + 256].reshape(Q_HEADS, HEAD_DIM, -1)
        rows.append(blk[jnp.array(Q_HEAD_ORDER)].reshape(256, -1))
    return jnp.concatenate([w[0:512]] + rows, axis=0).astype(BF16)


def _block_diag(w):
    out = jnp.zeros((LRU_WIDTH, LRU_WIDTH), w.dtype)
    bw = LRU_WIDTH // LRU_BLOCKS
    for k in range(LRU_BLOCKS):
        out = out.at[k * bw:(k + 1) * bw, k * bw:(k + 1) * bw].set(w[k])
    return out


def _rope_tables(length):
    rows = length // GRID_W
    row = jnp.repeat(jnp.arange(rows, dtype=F32), GRID_W)
    col = jnp.tile(jnp.arange(GRID_W, dtype=F32), rows)
    inv = ROPE_THETA ** (-jnp.arange(0, ROPE_AXIS_DIM, 2, dtype=F32) / ROPE_AXIS_DIM)
    half = ROPE_AXIS_DIM // 2
    ang = jnp.concatenate([row[:, None] * inv] * 2 + [col[:, None] * inv] * 2, axis=1)
    sign = jnp.tile(jnp.concatenate([-jnp.ones((half,), F32), jnp.ones((half,), F32)]), 2)
    cos = jnp.tile(jnp.cos(ang), (1, LANES // HEAD_DIM))
    sin = jnp.tile(jnp.sin(ang) * sign, (1, LANES // HEAD_DIM))
    return cos, sin


def _head_selectors():
    sel = np.zeros((2, LANES, SSD_HEADS * LANES), np.float32)
    for d in range(2):
        for h in range(SSD_HEADS):
            sel[d, SSD_HEADS * d + h, h * LANES:(h + 1) * LANES] = 1.0
    return jnp.asarray(sel, BF16)


def _head_mean_matrix():
    idx = np.arange(256) // HEAD_DIM
    return jnp.asarray((idx[:, None] == idx[None, :]).astype(np.float32) / HEAD_DIM, BF16)


def _pad_lanes(v):
    v = v.reshape(1, -1)
    return jnp.pad(v, ((0, 0), (0, LANES - v.shape[1])))


def kernel(x, c, ctx, c_ctx, w_mod, b_mod, g_mix, w_in, ssd_conv_w, ssd_conv_b, ssd_a_log, ssd_dt_bias, ssd_d, ssd_norm_g, lru_conv_w, lru_conv_b, lru_lambda, lru_w_a, lru_b_a, lru_w_i, lru_b_i, gqa_q_norm, gqa_k_norm, swa_q_norm, swa_k_norm, swa_sink, w_out, g_ffn, w_ffn1, w_ffn2):
    batch, seq, d = x.shape
    lc = ctx.shape[1]
    depth = w_mod.shape[0]
    assert batch + 1 <= 8 and seq % 1024 == 0 and lc % 256 == 0 and seq % GRID_W == 0

    cvecs = jnp.concatenate([c, c_ctx[None, :], jnp.zeros((8 - batch - 1, d), F32)], axis=0)
    mod = _modulation(cvecs, w_mod, b_mod).reshape(depth * 8, N_MOD, d)

    rope = _rope_tables(seq)
    sel = _head_selectors()
    bd = _head_mean_matrix()
    conv_sel = _conv_selectors()
    tm = 512
    tiles_per_batch = seq // tm

    xl = x.reshape(batch * seq, d)
    xc = ctx.reshape(batch * lc, d)
    for l in range(depth):
        need_ctx = l < depth - 1
        lat_row = lambda i, l=l: l * 8 + i // tiles_per_batch
        ctx_row = lambda i, l=l: l * 8 + batch
        w_in_l = _relayout_w_in(w_in[l])
        nrm = jnp.stack([jnp.tile(v, 256 // HEAD_DIM) for v in (gqa_q_norm[l], gqa_k_norm[l], swa_q_norm[l], swa_k_norm[l])])
        g1 = g_mix[l].reshape(1, d)
        ssd_l, lru_l, qc_l, qd_l, dt_l = _in_projection(xl, mod, lat_row, g1, w_in_l, nrm, bd, rope, tm, tiles_per_batch)
        ssd_c, lru_c, qc_c, qd_c, dt_c = _in_projection(xc, mod, ctx_row, g1, w_in_l, nrm, bd, None, 256, 1)

        head_params = jnp.concatenate([_pad_lanes(ssd_dt_bias[l]), _pad_lanes(ssd_a_log[l]), jnp.zeros((6, LANES), F32)], axis=0)
        ya_l, ya_c = _ssd_mixer(ssd_l, ssd_c, dt_l, dt_c, conv_sel, ssd_conv_w[l], ssd_conv_b[l].reshape(1, -1), head_params, sel,
                                jnp.repeat(ssd_d[l], HEAD_DIM).reshape(1, -1), ssd_norm_g[l].reshape(1, -1), batch)

        w_gates = jnp.concatenate([_block_diag(lru_w_a[l, 0]), _block_diag(lru_w_i[l, 0]),
                                   _block_diag(lru_w_a[l, 1]), _block_diag(lru_w_i[l, 1])], axis=1).astype(BF16)
        b_gates = jnp.concatenate([lru_b_a[l, 0], lru_b_i[l, 0], lru_b_a[l, 1], lru_b_i[l, 1]]).reshape(1, -1)
        yb_l, yb_c = _lru_mixer(lru_l, lru_c, conv_sel, lru_conv_w[l], lru_conv_b[l].reshape(1, -1), w_gates, b_gates,
                                lru_lambda[l].reshape(1, -1), batch)

        yc_l = _global_attention(qc_l, qc_c, gqa_q_norm[l], gqa_k_norm[l], batch, 256, 512)
        yd_l = _window_attention(swa_sink[l], qd_l, qd_c, batch, 1024)

        w_out_l = _relayout_w_out(w_out[l])
        g2 = g_ffn[l].reshape(1, d)
        w1 = w_ffn1[l].astype(BF16)
        w2 = w_ffn2[l].astype(BF16)
        xl = _out_ffn(xl, (ya_l, yb_l, yc_l, yd_l), mod, lat_row, w_out_l, g2, w1, w2, tm)
        if need_ctx:
            yc_c, yd_c = _context_attention(swa_sink[l], qc_c, qd_c, batch)
            xc = _out_ffn(xc, (ya_c, yb_c, yc_c, yd_c), mod, ctx_row, w_out_l, g2, w1, w2, 256)
    return xl.reshape(batch, seq, d)
```

```python
import functools
import math

import numpy as np
import jax
import jax.numpy as jnp
from jax import lax
from jax.experimental import pallas as pl
from jax.experimental.pallas import tpu as pltpu

F32 = jnp.float32
BF16 = jnp.bfloat16

EPS = 1e-6
GRID_W = 64
N_MOD = 6
GROUP_WIDTH = 256
SSD_HEADS = 4
SSD_STATE = 128
SSD_CONV = 4
SSD_CHUNK = 128
SSD_CONV_CH = 768
LRU_WIDTH = 256
LRU_BLOCKS = 4
LRU_C = 8.0
HEAD_DIM = 64
Q_HEADS = 4
WINDOW = 128
ROPE_THETA = 10000.0
ROPE_AXIS_DIM = 32
LOG2_E = math.log2(math.e)
Q_SCALE = HEAD_DIM ** -0.5 * LOG2_E

LANES = 128
BF16_ROWS = 16

COL_SSD, COL_LRU, COL_C, COL_D, COL_DT, COL_END = 0, 1024, 1536, 2048, 2560, 2688
Q_HEAD_ORDER = (0, 2, 1, 3)

CONV_WIN = SSD_CHUNK + 2 * BF16_ROWS


def _cparams(sem, vmem_mib):
    return pltpu.CompilerParams(dimension_semantics=sem, vmem_limit_bytes=vmem_mib * 1024 * 1024)


def _dot(a, b):
    return jnp.dot(a, b, preferred_element_type=F32)


def _dot_nt(a, b):
    return lax.dot_general(a, b, (((1,), (1,)), ((), ())), preferred_element_type=F32)


def _split_bf16(a, parts):
    out = []
    for _ in range(parts - 1):
        hi = a.astype(BF16)
        out.append(hi)
        a = a - hi.astype(F32)
    out.append(a.astype(BF16))
    return out


def _sigmoid(x):
    return 0.5 * jnp.tanh(0.5 * x) + 0.5


def _silu(x):
    h = 0.5 * x
    return h + h * jnp.tanh(h)


def _softplus(x):
    return jnp.maximum(x, 0.0) + jnp.log1p(jnp.exp(-jnp.abs(x)))


def _gelu_tanh(x):
    return 0.5 * x * (1.0 + jnp.tanh(math.sqrt(2.0 / math.pi) * (x + 0.044715 * (x * x * x))))


def _lane_iota(shape):
    return lax.broadcasted_iota(jnp.int32, shape, len(shape) - 1)


def _row_iota(shape):
    return lax.broadcasted_iota(jnp.int32, shape, len(shape) - 2)


def _mod_kernel(cv_ref, w_ref, b_ref, o_ref):
    cv = cv_ref[...]
    s = _silu(cv)
    w = w_ref[0]
    s_hi, s_lo = _split_bf16(s, 2)
    w_hi, w_lo = _split_bf16(w, 2)
    acc = _dot(s_hi, w_hi) + _dot(s_lo, w_hi) + _dot(s_hi, w_lo)
    o_ref[0] = acc + b_ref[0]


def _modulation(cvecs, w_mod, b_mod):
    depth, d, n = w_mod.shape
    tn = 1024
    return pl.pallas_call(
        _mod_kernel,
        out_shape=jax.ShapeDtypeStruct((depth, 8, n), F32),
        grid=(depth, n // tn),
        in_specs=[pl.BlockSpec((8, d), lambda l, j: (0, 0)),
                  pl.BlockSpec((1, d, tn), lambda l, j: (l, 0, j)),
                  pl.BlockSpec((1, 1, tn), lambda l, j: (l, 0, j))],
        out_specs=pl.BlockSpec((1, 8, tn), lambda l, j: (l, 0, j)),
        compiler_params=_cparams(("parallel", "parallel"), 32),
        name="modulation",
    )(cvecs, w_mod, b_mod.reshape(depth, 1, n))


def _head_norm_rope(p, g, bd, cos, sin, scale):
    ms = _dot((p * p).astype(BF16), bd)
    y = p * lax.rsqrt(ms + EPS) * g
    if cos is not None:
        w = y.shape[-1]
        first = (_lane_iota(y.shape) % ROPE_AXIS_DIM) < (ROPE_AXIS_DIM // 2)
        partner = jnp.where(first, pltpu.roll(y, w - ROPE_AXIS_DIM // 2, 1), pltpu.roll(y, ROPE_AXIS_DIM // 2, 1))
        y = y * cos + partner * sin
    if scale != 1.0:
        y = y * scale
    return y


def _inproj_kernel(*refs, use_rope):
    if use_rope:
        x_ref, mod_ref, g_ref, w_ref, nrm_ref, bd_ref, cos_ref, sin_ref, ssd_ref, lru_ref, c_ref, d_ref, dt_ref = refs
    else:
        x_ref, mod_ref, g_ref, w_ref, nrm_ref, bd_ref, ssd_ref, lru_ref, c_ref, d_ref, dt_ref = refs
    x = x_ref[...]
    ms = jnp.mean(x * x, axis=-1, keepdims=True)
    m = mod_ref[0]
    h = (x * lax.rsqrt(ms + EPS) * g_ref[0]) * (1.0 + m[1:2]) + m[0:1]
    hb = h.astype(BF16)
    ssd_ref[...] = _dot(hb, w_ref[0, :, COL_SSD:COL_LRU]).astype(BF16)
    lru_ref[...] = _dot(hb, w_ref[0, :, COL_LRU:COL_C]).astype(BF16)
    dt_ref[...] = _dot(hb, w_ref[0, :, COL_DT:COL_END])
    bd = bd_ref[...]
    if use_rope:
        cos1, sin1 = cos_ref[...], sin_ref[...]
        cos2 = jnp.concatenate([cos1, cos1], axis=1)
        sin2 = jnp.concatenate([sin1, sin1], axis=1)
    else:
        cos1 = sin1 = cos2 = sin2 = None
    for o_ref, c0, row in ((c_ref, COL_C, 0), (d_ref, COL_D, 2)):
        p = _dot(hb, w_ref[0, :, c0:c0 + 512])
        q = _head_norm_rope(p[:, 0:256], nrm_ref[0, row:row + 1, :], bd, cos2, sin2, Q_SCALE)
        k = _head_norm_rope(p[:, 256:384], nrm_ref[0, row + 1:row + 2, 0:128], bd[0:128, 0:128], cos1, sin1, 1.0)
        o_ref[:, 0:256] = q.astype(BF16)
        o_ref[:, 256:384] = k.astype(BF16)
        o_ref[:, 384:512] = p[:, 384:512].astype(BF16)


def _in_projection(layer, x2d, mod, mod_row_of_tile, g, w, nrm, bd, rope, tm, rope_tiles):
    rows, d = x2d.shape
    use_rope = rope is not None
    in_specs = [pl.BlockSpec((tm, d), lambda i: (i, 0)),
                pl.BlockSpec((1, N_MOD, d), lambda i: (mod_row_of_tile(i), 0, 0)),
                pl.BlockSpec((1, 1, d), lambda i: (layer, 0, 0)),
                pl.BlockSpec((1, d, COL_END), lambda i: (layer, 0, 0), pipeline_mode=pl.Buffered(1)),
                pl.BlockSpec((1, 4, 256), lambda i: (layer, 0, 0)),
                pl.BlockSpec((256, 256), lambda i: (0, 0))]
    args = [x2d, mod, g, w, nrm, bd]
    if use_rope:
        in_specs += [pl.BlockSpec((tm, LANES), lambda i: (i % rope_tiles, 0))] * 2
        args += list(rope)
    widths = (1024, 512, 512, 512, LANES)
    dtypes = (BF16, BF16, BF16, BF16, F32)
    return pl.pallas_call(
        functools.partial(_inproj_kernel, use_rope=use_rope),
        out_shape=tuple(jax.ShapeDtypeStruct((rows, wd), dt) for wd, dt in zip(widths, dtypes)),
        grid=(rows // tm,),
        in_specs=in_specs,
        out_specs=tuple(pl.BlockSpec((tm, wd), lambda i: (i, 0)) for wd in widths),
        compiler_params=_cparams(("parallel",), 48),
        name="in_projection",
    )(*args)


def _conv_selectors():
    row = np.arange(SSD_CONV * SSD_CHUNK)
    want = (row % SSD_CHUNK) + (row // SSD_CHUNK) - SSD_CONV // 2
    col = np.arange(CONV_WIN)
    sel = np.stack([(col[None, :] == want[:, None] + BF16_ROWS * case) for case in range(3)])
    return jnp.asarray(sel.astype(np.float32), BF16)


def _conv_chunk(seq_ref, col0, width, length, r0, sel_ref, w_ref, b_ref):
    ws = jnp.clip(r0 - BF16_ROWS, 0, length - CONV_WIN)
    ws = pl.multiple_of(ws, BF16_ROWS)
    xw = seq_ref[pl.ds(ws, CONV_WIN), col0:col0 + width]
    sh = _dot(sel_ref[(r0 - ws) // BF16_ROWS], xw)
    w = w_ref[0]
    acc = b_ref[0] + sh[0:SSD_CHUNK] * w[0:1]
    for k in range(1, SSD_CONV):
        acc = acc + sh[k * SSD_CHUNK:(k + 1) * SSD_CHUNK] * w[k:k + 1]
    return acc


def _ssd_kernel(pl_ref, pc_ref, dtl_ref, dtc_ref, csel_ref, cw_ref, cb_ref, hp_ref, sel_ref, dsk_ref, ng_ref,
                yl_ref, yc_ref, accl_ref, accc_ref, xbcl_ref, xbcc_ref):
    q = SSD_CHUNK
    lane8 = _lane_iota((1, LANES)) < 2 * SSD_HEADS
    dt_bias = hp_ref[0, 0:1, :]
    a_row = jnp.where(lane8, -jnp.exp(hp_ref[0, 1:2, :]), 0.0)
    ii = _row_iota((q, q))
    jj = _lane_iota((q, q))
    lo = _lane_iota((q, LANES)) < HEAD_DIM

    def chunk(seq_ref, xbc_ref, dt_ref, length, r0, d, state):
        if d == 0:
            conv = _conv_chunk(seq_ref, GROUP_WIDTH, SSD_CONV_CH, length, r0, csel_ref, cw_ref, cb_ref)
            xbc_ref[pl.ds(r0, q), :] = _silu(conv).astype(BF16)
        xbc = xbc_ref[pl.ds(r0, q), :].astype(F32)
        xs, bm, cm = xbc[:, 0:256], xbc[:, 256:512], xbc[:, 512:768]
        dtv = _softplus(dt_ref[pl.ds(r0, q), :] + dt_bias)
        da = dtv * a_row
        mask = (jj <= ii) if d == 0 else (jj >= ii)
        tri = jnp.where(mask, 1.0, 0.0).astype(BF16)
        acum = _dot(jnp.concatenate([tri, tri], axis=1), jnp.concatenate(_split_bf16(da, 2), axis=0))
        stacked = jnp.concatenate([dtv, acum], axis=0)
        ex = _dot(jnp.concatenate(_split_bf16(stacked, 2), axis=1), sel_ref[d])
        dtb, acb = ex[0:q], ex[q:2 * q]
        act = acum.T
        edge = q - 1 if d == 0 else 0
        per_head = []
        for h in range(SSD_HEADS):
            ac = acb[:, h * LANES:(h + 1) * LANES]
            ar = act[SSD_HEADS * d + h:SSD_HEADS * d + h + 1, :]
            lmat = jnp.where(mask, jnp.exp(ac - ar), 0.0)
            alast = ac[edge:edge + 1, :]
            dth = dtb[:, h * LANES:(h + 1) * LANES]
            per_head.append((lmat, dth, jnp.exp(alast - ac) * dth, jnp.exp(ac), jnp.exp(alast)))
        zblk = jnp.zeros((SSD_STATE, LANES), BF16)

        def block_diag(a0, a1):
            return jnp.concatenate([jnp.concatenate([a0, zblk], axis=1), jnp.concatenate([zblk, a1], axis=1)], axis=0)

        bts = [bm[:, g * LANES:(g + 1) * LANES].T.astype(BF16) for g in range(2)]
        cmb = cm.astype(BF16)
        cb_all = _dot(cmb, block_diag(bts[0], bts[1]))
        yo_all = _dot(cmb, block_diag(state[0].astype(BF16), state[1].astype(BF16)))
        ys, new_state = [], []
        for g in range(2):
            (l0, dt0, we0, ei0, cd0), (l1, dt1, we1, ei1, cd1) = per_head[2 * g], per_head[2 * g + 1]
            xg = xs[:, g * LANES:(g + 1) * LANES]
            cb = cb_all[:, g * LANES:(g + 1) * LANES]
            xdt = xg * jnp.where(lo, dt0, dt1)
            scores = jnp.concatenate([(cb * l0).astype(BF16), (cb * l1).astype(BF16)], axis=1)
            xpair = jnp.concatenate([jnp.where(lo, xdt, 0.0), jnp.where(lo, 0.0, xdt)], axis=0).astype(BF16)
            y = _dot(scores, xpair) + yo_all[:, g * LANES:(g + 1) * LANES] * jnp.where(lo, ei0, ei1)
            s_new = _dot(bts[g], (xg * jnp.where(lo, we0, we1)).astype(BF16))
            new_state.append(state[g] * jnp.where(lo, cd0, cd1) + s_new)
            ys.append(y)
        return jnp.concatenate(ys, axis=1), xs, tuple(new_state)

    def sweep(seq_ref, xbc_ref, dt_ref, acc_ref, out_ref, d, state):
        length = seq_ref.shape[0]
        nc = length // q

        def body(i, st):
            c = i if d == 0 else nc - 1 - i
            r0 = pl.multiple_of(c * q, q)
            y, xs, st = chunk(seq_ref, xbc_ref, dt_ref, length, r0, d, st)
            if d == 0:
                acc_ref[pl.ds(r0, q), :] = y
            else:
                y = acc_ref[pl.ds(r0, q), :] + y + dsk_ref[0] * xs
                z = seq_ref[pl.ds(r0, q), 0:GROUP_WIDTH].astype(F32)
                t = y * _silu(z)
                ms = jnp.mean(t * t, axis=-1, keepdims=True)
                out_ref[pl.ds(r0, q), :] = (t * lax.rsqrt(ms + EPS) * ng_ref[0]).astype(out_ref.dtype)
            return st

        return lax.fori_loop(0, nc, body, state, unroll=min(nc, 4))

    zero = (jnp.zeros((SSD_STATE, LANES), F32), jnp.zeros((SSD_STATE, LANES), F32))
    for d in range(2):
        st = sweep(pc_ref, xbcc_ref, dtc_ref, accc_ref, yc_ref, d, zero)
        sweep(pl_ref, xbcl_ref, dtl_ref, accl_ref, yl_ref, d, st)


def _ssd_mixer(layer, p_lat, p_ctx, dt_lat, dt_ctx, conv_sel, conv_w, conv_b, head_params, sel, d_skip, norm_g, batch):
    seq = p_lat.shape[0] // batch
    lc = p_ctx.shape[0] // batch
    const = lambda *shape: pl.BlockSpec(shape, lambda b: (0,) * len(shape))
    of_layer = lambda *shape: pl.BlockSpec((1,) + shape, lambda b: (layer,) + (0,) * len(shape))
    return pl.pallas_call(
        _ssd_kernel,
        out_shape=(jax.ShapeDtypeStruct((batch * seq, GROUP_WIDTH), BF16),
                   jax.ShapeDtypeStruct((batch * lc, GROUP_WIDTH), BF16)),
        grid=(batch,),
        in_specs=[pl.BlockSpec((seq, 1024), lambda b: (b, 0)),
                  pl.BlockSpec((lc, 1024), lambda b: (b, 0)),
                  pl.BlockSpec((seq, LANES), lambda b: (b, 0)),
                  pl.BlockSpec((lc, LANES), lambda b: (b, 0)),
                  const(3, SSD_CONV * SSD_CHUNK, CONV_WIN),
                  of_layer(SSD_CONV, SSD_CONV_CH), of_layer(1, SSD_CONV_CH), of_layer(8, LANES),
                  const(2, 2 * LANES, SSD_HEADS * LANES), of_layer(1, GROUP_WIDTH), of_layer(1, GROUP_WIDTH)],
        out_specs=(pl.BlockSpec((seq, GROUP_WIDTH), lambda b: (b, 0)),
                   pl.BlockSpec((lc, GROUP_WIDTH), lambda b: (b, 0))),
        scratch_shapes=[pltpu.VMEM((seq, GROUP_WIDTH), F32), pltpu.VMEM((lc, GROUP_WIDTH), F32),
                        pltpu.VMEM((seq, SSD_CONV_CH), BF16), pltpu.VMEM((lc, SSD_CONV_CH), BF16)],
        compiler_params=_cparams(("parallel",), 48),
        name="ssd_mixer",
    )(p_lat, p_ctx, dt_lat, dt_ctx, conv_sel, conv_w, conv_b, head_params, sel, d_skip, norm_g)


LRU_SEGMENTS = 8


def _lru_kernel(pl_ref, pc_ref, csel_ref, cw_ref, cb_ref, wg_ref, bg_ref, lam_ref, ol_ref, oc_ref, a_scr, b_scr, cin_scr):
    q = SSD_CHUNK
    nseg = LRU_SEGMENTS
    clam = -LRU_C * _softplus(-lam_ref[0])

    def run(seq_ref, out_ref, h0):
        length = seq_ref.shape[0]
        seglen = length // nseg
        sub = min(q, seglen)

        def seg_rows(t0):
            return pl.ds((t0 % seglen) * nseg + t0 // seglen, sub, stride=nseg)

        def gates(c, _):
            r0 = pl.multiple_of(c * q, q)
            xc = _conv_chunk(seq_ref, LRU_WIDTH, LRU_WIDTH, length, r0, csel_ref, cw_ref, cb_ref)
            gt = _dot(xc.astype(BF16), wg_ref[0]) + bg_ref[0]
            for d in range(2):
                r = _sigmoid(gt[:, d * 512:d * 512 + 256])
                ig = _sigmoid(gt[:, d * 512 + 256:(d + 1) * 512])
                la = clam[:, d * 256:(d + 1) * 256] * r
                a = jnp.exp(la)
                bv = jnp.sqrt(-jnp.tanh(la) * (a * a + 1.0)) * (ig * xc)
                for hf in range(2):
                    for j in range(q // sub):
                        rows = seg_rows(r0 + j * sub)
                        a_scr[2 * d + hf, rows, :] = a[j * sub:(j + 1) * sub, hf * LANES:(hf + 1) * LANES]
                        b_scr[2 * d + hf, rows, :] = bv[j * sub:(j + 1) * sub, hf * LANES:(hf + 1) * LANES]
            return 0

        lax.fori_loop(0, length // q, gates, 0, unroll=min(length // q, 4))

        def scan(k, carry):
            hs, ps = carry
            new_h, new_p = [], []
            for qd in range(4):
                s = k if qd < 2 else seglen - 1 - k
                idx = (qd, pl.ds(pl.multiple_of(s * nseg, nseg), nseg), slice(None))
                a = a_scr[idx]
                h = a * hs[qd] + b_scr[idx]
                p = ps[qd] * a
                b_scr[idx] = h
                a_scr[idx] = p
                new_h.append(h)
                new_p.append(p)
            return tuple(new_h), tuple(new_p)

        z8 = jnp.zeros((nseg, LANES), F32)
        o8 = jnp.ones((nseg, LANES), F32)
        hend, pend = lax.fori_loop(0, seglen, scan, ((z8,) * 4, (o8,) * 4), unroll=4)

        h_out = []
        for qd in range(4):
            c = h0[qd]
            order = range(nseg) if qd < 2 else range(nseg - 1, -1, -1)
            for k in order:
                cin_scr[qd, k:k + 1, :] = c
                c = hend[qd][k:k + 1, :] + pend[qd][k:k + 1, :] * c
            h_out.append(c)

        def emit(c, _):
            r0 = pl.multiple_of(c * sub, sub)
            rows = seg_rows(r0)
            hq = [b_scr[qd, rows, :] + a_scr[qd, rows, :] * cin_scr[qd, pl.ds(r0 // seglen, 1), :] for qd in range(4)]
            hsum = jnp.concatenate([hq[0] + hq[2], hq[1] + hq[3]], axis=1)
            gate = seq_ref[pl.ds(r0, sub), 0:LRU_WIDTH].astype(F32)
            out_ref[pl.ds(r0, sub), :] = (hsum * _gelu_tanh(gate)).astype(out_ref.dtype)
            return 0

        lax.fori_loop(0, length // sub, emit, 0)
        return h_out

    z1 = jnp.zeros((1, LANES), F32)
    states = run(pc_ref, oc_ref, [z1] * 4)
    run(pl_ref, ol_ref, states)


def _lru_mixer(layer, p_lat, p_ctx, conv_sel, conv_w, conv_b, w_gates, b_gates, lam, batch):
    seq = p_lat.shape[0] // batch
    lc = p_ctx.shape[0] // batch
    const = lambda *shape: pl.BlockSpec(shape, lambda b: (0,) * len(shape))
    of_layer = lambda *shape: pl.BlockSpec((1,) + shape, lambda b: (layer,) + (0,) * len(shape))
    return pl.pallas_call(
        _lru_kernel,
        out_shape=(jax.ShapeDtypeStruct((batch * seq, LRU_WIDTH), BF16),
                   jax.ShapeDtypeStruct((batch * lc, LRU_WIDTH), BF16)),
        grid=(batch,),
        in_specs=[pl.BlockSpec((seq, 512), lambda b: (b, 0)),
                  pl.BlockSpec((lc, 512), lambda b: (b, 0)),
                  const(3, SSD_CONV * SSD_CHUNK, CONV_WIN),
                  of_layer(SSD_CONV, LRU_WIDTH), of_layer(1, LRU_WIDTH), of_layer(LRU_WIDTH, 1024), of_layer(1, 1024),
                  of_layer(1, 512)],
        out_specs=(pl.BlockSpec((seq, LRU_WIDTH), lambda b: (b, 0)),
                   pl.BlockSpec((lc, LRU_WIDTH), lambda b: (b, 0))),
        scratch_shapes=[pltpu.VMEM((4, seq, LANES), F32), pltpu.VMEM((4, seq, LANES), F32),
                        pltpu.VMEM((4, LRU_SEGMENTS, LANES), F32)],
        compiler_params=_cparams(("parallel",), 48),
        name="lru_mixer",
    )(p_lat, p_ctx, conv_sel, conv_w, conv_b, w_gates, b_gates, lam)


def _stack_heads(qblk):
    lo = _lane_iota((qblk.shape[0], LANES)) < HEAD_DIM
    zero = jnp.zeros((), qblk.dtype)
    s0, s1 = qblk[:, 0:LANES], qblk[:, LANES:2 * LANES]
    return jnp.concatenate([jnp.where(lo, s0, zero), jnp.where(lo, s1, zero),
                            jnp.where(lo, zero, s0), jnp.where(lo, zero, s1)], axis=0)


def _unstack_heads(acc, t):
    lo = _lane_iota((t, LANES)) < HEAD_DIM
    return jnp.concatenate([jnp.where(lo, acc[0:t], acc[2 * t:3 * t]),
                            jnp.where(lo, acc[t:2 * t], acc[3 * t:4 * t])], axis=1)


MAX_SHIFT_BOUND = 50.0


def _gattn_kernel(bound_ref, q_ref, kl_ref, vl_ref, kc_ref, vc_ref, o_ref, qs_scr, m_scr, acc_scr, *, layer, tk, online):
    tq = q_ref.shape[0]
    half = 2 * tq
    qs_scr[...] = _stack_heads(q_ref[...])
    acc_scr[...] = jnp.zeros(acc_scr.shape, F32)
    if online:
        m_scr[...] = jnp.full(m_scr.shape, -jnp.inf, F32)

    def step(k, v):
        lo = _lane_iota(v.shape) < HEAD_DIM
        one = jnp.ones((), v.dtype)
        s = _dot_nt(qs_scr[...], k)
        if online:
            m_old = m_scr[...]
            shift = jnp.maximum(m_old, jnp.max(s, axis=-1, keepdims=True))
            m_scr[...] = shift
            acc_scr[...] = acc_scr[...] * jnp.exp2(m_old - shift)
        else:
            shift = bound_ref[layer]
        p = jnp.exp2(s - shift).astype(BF16)
        acc_scr[0:half, :] += _dot(p[0:half], jnp.where(lo, v, one))
        acc_scr[half:, :] += _dot(p[half:], jnp.where(lo, one, v))

    def body(j, _):
        r0 = pl.multiple_of(j * tk, tk)
        step(kl_ref[pl.ds(r0, tk), :], vl_ref[pl.ds(r0, tk), :])
        return 0

    lax.fori_loop(0, kl_ref.shape[0] // tk, body, 0, unroll=True)
    step(kc_ref[...], vc_ref[...])
    acc = acc_scr[...]
    o_ref[...] = _unstack_heads(acc / pltpu.roll(acc, HEAD_DIM, 1), tq).astype(o_ref.dtype)


def _logit_bounds(q_gain, k_gain):
    return HEAD_DIM * Q_SCALE * 1.02 * jnp.max(jnp.abs(q_gain), axis=-1) * jnp.max(jnp.abs(k_gain), axis=-1)


def _global_attention(layer, bound, qkv_lat, qkv_ctx, batch, tq, tk):
    seq = qkv_lat.shape[0] // batch
    lc = qkv_ctx.shape[0] // batch
    nq = seq // tq

    def call(online):
        return pl.pallas_call(
            functools.partial(_gattn_kernel, layer=layer, tk=tk, online=online),
            out_shape=jax.ShapeDtypeStruct((batch * seq, 256), BF16),
            grid=(batch, nq),
            in_specs=[pl.BlockSpec(memory_space=pltpu.SMEM),
                      pl.BlockSpec((tq, 256), lambda b, i: (b * nq + i, 0)),
                      pl.BlockSpec((seq, LANES), lambda b, i: (b, 2)),
                      pl.BlockSpec((seq, LANES), lambda b, i: (b, 3)),
                      pl.BlockSpec((lc, LANES), lambda b, i: (b, 2)),
                      pl.BlockSpec((lc, LANES), lambda b, i: (b, 3))],
            out_specs=pl.BlockSpec((tq, 256), lambda b, i: (b * nq + i, 0)),
            scratch_shapes=[pltpu.VMEM((4 * tq, LANES), BF16), pltpu.VMEM((4 * tq, 1), F32),
                            pltpu.VMEM((4 * tq, LANES), F32)],
            compiler_params=_cparams(("parallel", "parallel"), 48),
            name="global_attention_online" if online else "global_attention",
        )(bound, qkv_lat, qkv_lat, qkv_lat, qkv_ctx, qkv_ctx)

    return lax.cond(bound[layer] <= MAX_SHIFT_BOUND, lambda: call(False), lambda: call(True))


def _sink_rows(sink_ref, layer, t):
    head_of_row = _row_iota((4 * t, 1)) // t
    out = jnp.zeros((4 * t, 1), F32)
    for h in range(Q_HEADS):
        out = jnp.where(head_of_row == h, sink_ref[layer, h] * LOG2_E, out)
    return out


def _wattn_kernel(sink_ref, q_ref, kl_ref, vl_ref, kc_ref, vc_ref, o_ref, bias_scr, *, layer):
    w = WINDOW
    seq = kl_ref.shape[0]
    band = 3 * w
    dmat = _lane_iota((w, band)) - _row_iota((w, band))
    for case in range(3):
        bias_scr[case] = jnp.where(jnp.abs(dmat - case * w) <= w, 0.0, -jnp.inf)
    one = jnp.ones((), BF16)
    kc, vc = kc_ref[...], vc_ref[...]
    lo_c = _lane_iota(vc.shape) < HEAD_DIM
    vc_sum = (jnp.where(lo_c, vc, one), jnp.where(lo_c, one, vc))
    lo_b = _lane_iota((band, LANES)) < HEAD_DIM
    lo = _lane_iota((w, LANES)) < HEAD_DIM

    def body(n, _):
        r0 = pl.multiple_of(n * w, w)
        gq = pl.program_id(1) * q_ref.shape[0] + r0
        ks = pl.multiple_of(jnp.clip(gq - w, 0, seq - band), w)
        bias = bias_scr[(gq - ks) // w]
        kb, vb = kl_ref[pl.ds(ks, band), :], vl_ref[pl.ds(ks, band), :]
        vb_sum = (jnp.where(lo_b, vb, one), jnp.where(lo_b, one, vb))
        qs = _stack_heads(q_ref[pl.ds(r0, w), :])
        s_band = _dot_nt(qs, kb)
        s_ctx = _dot_nt(qs, kc)
        outs = []
        for g in range(2):
            pb, pc, sk = [], [], []
            for h in (2 * g, 2 * g + 1):
                sb = s_band[h * w:(h + 1) * w] + bias
                sc = s_ctx[h * w:(h + 1) * w]
                sink = sink_ref[layer, h] * LOG2_E
                m = jnp.maximum(jnp.maximum(jnp.max(sb, axis=-1, keepdims=True),
                                            jnp.max(sc, axis=-1, keepdims=True)), sink)
                pb.append(jnp.exp2((sb - m).astype(BF16)))
                pc.append(jnp.exp2((sc - m).astype(BF16)))
                sk.append(jnp.exp2(sink - m))
            acc = _dot(jnp.concatenate(pb, axis=0), vb_sum[g]) + _dot(jnp.concatenate(pc, axis=0), vc_sum[g])
            den = pltpu.roll(acc, HEAD_DIM, 1) + jnp.concatenate(sk, axis=0)
            outs.append(acc / den)
        o_ref[pl.ds(r0, w), :] = jnp.concatenate(
            [jnp.where(lo, outs[0][0:w], outs[1][0:w]), jnp.where(lo, outs[0][w:2 * w], outs[1][w:2 * w])],
            axis=1).astype(o_ref.dtype)
        return 0

    lax.fori_loop(0, q_ref.shape[0] // w, body, 0, unroll=4)


def _window_attention(layer, sink, qkv_lat, qkv_ctx, batch, tq):
    seq = qkv_lat.shape[0] // batch
    lc = qkv_ctx.shape[0] // batch
    nq = seq // tq
    return pl.pallas_call(
        functools.partial(_wattn_kernel, layer=layer),
        out_shape=jax.ShapeDtypeStruct((batch * seq, 256), BF16),
        grid=(batch, nq),
        in_specs=[pl.BlockSpec(memory_space=pltpu.SMEM),
                  pl.BlockSpec((tq, 256), lambda b, i: (b * nq + i, 0)),
                  pl.BlockSpec((seq, LANES), lambda b, i: (b, 2)),
                  pl.BlockSpec((seq, LANES), lambda b, i: (b, 3)),
                  pl.BlockSpec((lc, LANES), lambda b, i: (b, 2)),
                  pl.BlockSpec((lc, LANES), lambda b, i: (b, 3))],
        out_specs=pl.BlockSpec((tq, 256), lambda b, i: (b * nq + i, 0)),
        scratch_shapes=[pltpu.VMEM((3, WINDOW, 3 * WINDOW), F32)],
        compiler_params=_cparams(("parallel", "parallel"), 32),
        name="window_attention",
    )(sink, qkv_lat, qkv_lat, qkv_lat, qkv_ctx, qkv_ctx)


def _cattn_kernel(sink_ref, c_ref, d_ref, oc_ref, od_ref, *, layer):
    t = c_ref.shape[0]
    for ref, out, has_sink in ((c_ref, oc_ref, False), (d_ref, od_ref, True)):
        qs = _stack_heads(ref[:, 0:256])
        s = _dot_nt(qs, ref[:, 256:384])
        m = jnp.max(s, axis=-1, keepdims=True)
        if has_sink:
            sink = _sink_rows(sink_ref, layer, t)
            m = jnp.maximum(m, sink)
        p = jnp.exp2(s - m)
        den = jnp.sum(p, axis=-1, keepdims=True)
        if has_sink:
            den = den + jnp.exp2(sink - m)
        acc = _dot(p.astype(BF16), ref[:, 384:512])
        out[...] = _unstack_heads(acc / den, t).astype(out.dtype)


def _context_attention(layer, sink, qkv_c, qkv_d, batch):
    lc = qkv_c.shape[0] // batch
    blk = lambda width: pl.BlockSpec((lc, width), lambda b: (b, 0))
    return pl.pallas_call(
        functools.partial(_cattn_kernel, layer=layer),
        out_shape=(jax.ShapeDtypeStruct((batch * lc, 256), BF16),) * 2,
        grid=(batch,),
        in_specs=[pl.BlockSpec(memory_space=pltpu.SMEM), blk(512), blk(512)],
        out_specs=(blk(256), blk(256)),
        compiler_params=_cparams(("parallel",), 32),
        name="context_attention",
    )(sink, qkv_c, qkv_d)


def _out_ffn_kernel(x_ref, ya_ref, yb_ref, yc_ref, yd_ref, mod_ref, wo_ref, g_ref, w1_ref, w2_ref, o_ref, *, hchunk):
    m = mod_ref[0]
    gw = GROUP_WIDTH
    mix = _dot(ya_ref[...], wo_ref[0, 0:gw, :])
    for i, y_ref in enumerate((yb_ref, yc_ref, yd_ref), start=1):
        mix = mix + _dot(y_ref[...], wo_ref[0, i * gw:(i + 1) * gw, :])
    x1 = x_ref[...] + m[2:3] * mix
    ms = jnp.mean(x1 * x1, axis=-1, keepdims=True)
    h = ((x1 * lax.rsqrt(ms + EPS) * g_ref[0]) * (1.0 + m[4:5]) + m[3:4]).astype(BF16)
    acc = None
    for c in range(w1_ref.shape[2] // hchunk):
        u = jnp.maximum(_dot(h, w1_ref[0, :, c * hchunk:(c + 1) * hchunk]), 0.0)
        t = _dot((u * u).astype(BF16), w2_ref[0, c * hchunk:(c + 1) * hchunk, :])
        acc = t if acc is None else acc + t
    o_ref[...] = x1 + m[5:6] * acc


def _out_ffn(layer, x2d, ys, mod, mod_row_of_tile, w_out, g, w1, w2, tm):
    rows, d = x2d.shape
    hidden = w1.shape[2]
    row_blk = lambda width: pl.BlockSpec((tm, width), lambda i: (i, 0))
    const = lambda *shape: pl.BlockSpec((1,) + shape, lambda i: (layer,) + (0,) * len(shape), pipeline_mode=pl.Buffered(1))
    return pl.pallas_call(
        functools.partial(_out_ffn_kernel, hchunk=1024),
        out_shape=jax.ShapeDtypeStruct((rows, d), F32),
        grid=(rows // tm,),
        in_specs=[row_blk(d), row_blk(256), row_blk(256), row_blk(256), row_blk(256),
                  pl.BlockSpec((1, N_MOD, d), lambda i: (mod_row_of_tile(i), 0, 0)),
                  const(d, d), pl.BlockSpec((1, 1, d), lambda i: (layer, 0, 0)), const(d, hidden), const(hidden, d)],
        out_specs=row_blk(d),
        compiler_params=_cparams(("parallel",), 56),
        name="out_ffn",
    )(x2d, *ys, mod, w_out, g, w1, w2)


def _permute_q_heads(cols):
    q = cols[..., 0:256].reshape(cols.shape[:-1] + (Q_HEADS, HEAD_DIM))
    q = q[..., Q_HEAD_ORDER, :].reshape(cols.shape[:-1] + (256,))
    return jnp.concatenate([q, cols[..., 256:512]], axis=-1)


def _relayout_w_in(w):
    ssd_in = GROUP_WIDTH + SSD_CONV_CH + 2 * SSD_HEADS
    b0, c0, d0 = ssd_in, ssd_in + 512, ssd_in + 1024
    dt = w[..., GROUP_WIDTH + SSD_CONV_CH:ssd_in]
    dt = jnp.pad(dt, [(0, 0)] * (w.ndim - 1) + [(0, LANES - 2 * SSD_HEADS)])
    return jnp.concatenate([w[..., 0:GROUP_WIDTH + SSD_CONV_CH], w[..., b0:c0], _permute_q_heads(w[..., c0:d0]),
                            _permute_q_heads(w[..., d0:d0 + 512]), dt], axis=-1).astype(BF16)


def _relayout_w_out(w):
    depth, _, d = w.shape
    attn = w[:, 512:1024].reshape(depth, 2, Q_HEADS, HEAD_DIM, d)[:, :, Q_HEAD_ORDER].reshape(depth, 512, d)
    return jnp.concatenate([w[:, 0:512], attn], axis=1).astype(BF16)


def _block_diag(w):
    eye = jnp.eye(LRU_BLOCKS, dtype=w.dtype)
    out = jnp.einsum('...kij,kn->...kinj', w, eye)
    return out.reshape(w.shape[:-3] + (LRU_WIDTH, LRU_WIDTH))


def _rope_tables(length):
    rows = length // GRID_W
    row = jnp.repeat(jnp.arange(rows, dtype=F32), GRID_W)
    col = jnp.tile(jnp.arange(GRID_W, dtype=F32), rows)
    inv = ROPE_THETA ** (-jnp.arange(0, ROPE_AXIS_DIM, 2, dtype=F32) / ROPE_AXIS_DIM)
    half = ROPE_AXIS_DIM // 2
    ang = jnp.concatenate([row[:, None] * inv] * 2 + [col[:, None] * inv] * 2, axis=1)
    sign = jnp.tile(jnp.concatenate([-jnp.ones((half,), F32), jnp.ones((half,), F32)]), 2)
    cos = jnp.tile(jnp.cos(ang), (1, LANES // HEAD_DIM))
    sin = jnp.tile(jnp.sin(ang) * sign, (1, LANES // HEAD_DIM))
    return cos, sin


def _head_selectors():
    sel = np.zeros((2, 2 * LANES, SSD_HEADS * LANES), np.float32)
    for d in range(2):
        for h in range(SSD_HEADS):
            for part in range(2):
                sel[d, part * LANES + SSD_HEADS * d + h, h * LANES:(h + 1) * LANES] = 1.0
    return jnp.asarray(sel, BF16)


def _head_mean_matrix():
    idx = np.arange(256) // HEAD_DIM
    return jnp.asarray((idx[:, None] == idx[None, :]).astype(np.float32) / HEAD_DIM, BF16)


def kernel(x, c, ctx, c_ctx, w_mod, b_mod, g_mix, w_in, ssd_conv_w, ssd_conv_b, ssd_a_log, ssd_dt_bias, ssd_d, ssd_norm_g, lru_conv_w, lru_conv_b, lru_lambda, lru_w_a, lru_b_a, lru_w_i, lru_b_i, gqa_q_norm, gqa_k_norm, swa_q_norm, swa_k_norm, swa_sink, w_out, g_ffn, w_ffn1, w_ffn2):
    batch, seq, d = x.shape
    lc = ctx.shape[1]
    depth = w_mod.shape[0]
    assert batch + 1 <= 8 and seq % 1024 == 0 and lc % 256 == 0 and seq % GRID_W == 0

    cvecs = jnp.concatenate([c, c_ctx[None, :], jnp.zeros((8 - batch - 1, d), F32)], axis=0)
    mod = _modulation(cvecs, w_mod, b_mod).reshape(depth * 8, N_MOD, d)

    rope = _rope_tables(seq)
    sel = _head_selectors()
    bd = _head_mean_matrix()
    conv_sel = _conv_selectors()
    w_in_r = _relayout_w_in(w_in)
    w_out_r = _relayout_w_out(w_out)
    w1, w2 = w_ffn1.astype(BF16), w_ffn2.astype(BF16)
    g1, g2 = g_mix.reshape(depth, 1, d), g_ffn.reshape(depth, 1, d)
    nrm = jnp.tile(jnp.stack([gqa_q_norm, gqa_k_norm, swa_q_norm, swa_k_norm], axis=1), (1, 1, 256 // HEAD_DIM))
    lane_pad = ((0, 0), (0, 0), (0, LANES - 2 * SSD_HEADS))
    head_params = jnp.concatenate([jnp.pad(ssd_dt_bias.reshape(depth, 1, -1), lane_pad),
                                   jnp.pad(ssd_a_log.reshape(depth, 1, -1), lane_pad),
                                   jnp.zeros((depth, 6, LANES), F32)], axis=1)
    ssd_cb = ssd_conv_b.reshape(depth, 1, -1)
    d_skip = jnp.repeat(ssd_d, HEAD_DIM, axis=-1).reshape(depth, 1, -1)
    ssd_ng = ssd_norm_g.reshape(depth, 1, -1)
    bd_a, bd_i = _block_diag(lru_w_a), _block_diag(lru_w_i)
    w_gates = jnp.concatenate([bd_a[:, 0], bd_i[:, 0], bd_a[:, 1], bd_i[:, 1]], axis=-1).astype(BF16)
    b_gates = jnp.concatenate([lru_b_a[:, 0], lru_b_i[:, 0], lru_b_a[:, 1], lru_b_i[:, 1]], axis=-1).reshape(depth, 1, -1)
    lru_cb = lru_conv_b.reshape(depth, 1, -1)
    lam = lru_lambda.reshape(depth, 1, -1)
    bounds = _logit_bounds(gqa_q_norm, gqa_k_norm)

    tm = 512
    tiles_per_batch = seq // tm
    xl = x.reshape(batch * seq, d)
    xc = ctx.reshape(batch * lc, d)
    for l in range(depth):
        need_ctx = l < depth - 1
        lat_row = lambda i, l=l: l * 8 + i // tiles_per_batch
        ctx_row = lambda i, l=l: l * 8 + batch
        ssd_l, lru_l, qc_l, qd_l, dt_l = _in_projection(l, xl, mod, lat_row, g1, w_in_r, nrm, bd, rope, tm, tiles_per_batch)
        ssd_c, lru_c, qc_c, qd_c, dt_c = _in_projection(l, xc, mod, ctx_row, g1, w_in_r, nrm, bd, None, 256, 1)
        ya_l, ya_c = _ssd_mixer(l, ssd_l, ssd_c, dt_l, dt_c, conv_sel, ssd_conv_w, ssd_cb, head_params, sel, d_skip,
                                ssd_ng, batch)
        yb_l, yb_c = _lru_mixer(l, lru_l, lru_c, conv_sel, lru_conv_w, lru_cb, w_gates, b_gates, lam, batch)
        yc_l = _global_attention(l, bounds, qc_l, qc_c, batch, 256, 512)
        yd_l = _window_attention(l, swa_sink, qd_l, qd_c, batch, seq)
        xl = _out_ffn(l, xl, (ya_l, yb_l, yc_l, yd_l), mod, lat_row, w_out_r, g2, w1, w2, tm)
        if need_ctx:
            yc_c, yd_c = _context_attention(l, swa_sink, qc_c, qd_c, batch)
            xc = _out_ffn(l, xc, (ya_c, yb_c, yc_c, yd_c), mod, ctx_row, w_out_r, g2, w1, w2, 256)
    return xl.reshape(batch, seq, d)
```

```python
import functools
import math

import numpy as np
import jax
import jax.numpy as jnp
from jax import lax
from jax.experimental import pallas as pl
from jax.experimental.pallas import tpu as pltpu

F32 = jnp.float32
BF16 = jnp.bfloat16

EPS = 1e-6
GRID_W = 64
N_MOD = 6
GROUP_WIDTH = 256
SSD_HEADS = 4
SSD_STATE = 128
SSD_CONV = 4
SSD_CHUNK = 128
SSD_CONV_CH = 768
LRU_WIDTH = 256
LRU_BLOCKS = 4
LRU_C = 8.0
HEAD_DIM = 64
Q_HEADS = 4
WINDOW = 128
ROPE_THETA = 10000.0
ROPE_AXIS_DIM = 32
LOG2_E = math.log2(math.e)
Q_SCALE = HEAD_DIM ** -0.5 * LOG2_E

LANES = 128
BF16_ROWS = 16

COL_SSD, COL_LRU, COL_C, COL_D, COL_DT, COL_END = 0, 1024, 1536, 2048, 2560, 2688
Q_HEAD_ORDER = (0, 2, 1, 3)

CONV_WIN = SSD_CHUNK + 2 * BF16_ROWS


def _cparams(sem, vmem_mib):
    return pltpu.CompilerParams(dimension_semantics=sem, vmem_limit_bytes=vmem_mib * 1024 * 1024)


def _dot(a, b):
    return jnp.dot(a, b, preferred_element_type=F32)


def _dot_nt(a, b):
    return lax.dot_general(a, b, (((1,), (1,)), ((), ())), preferred_element_type=F32)


def _split_bf16(a, parts):
    out = []
    for _ in range(parts - 1):
        hi = a.astype(BF16)
        out.append(hi)
        a = a - hi.astype(F32)
    out.append(a.astype(BF16))
    return out


def _sigmoid(x):
    return 0.5 * jnp.tanh(0.5 * x) + 0.5


def _silu(x):
    h = 0.5 * x
    return h + h * jnp.tanh(h)


def _softplus(x):
    return jnp.maximum(x, 0.0) + jnp.log1p(jnp.exp(-jnp.abs(x)))


def _gelu_tanh(x):
    return 0.5 * x * (1.0 + jnp.tanh(math.sqrt(2.0 / math.pi) * (x + 0.044715 * (x * x * x))))


def _lane_iota(shape):
    return lax.broadcasted_iota(jnp.int32, shape, len(shape) - 1)


def _row_iota(shape):
    return lax.broadcasted_iota(jnp.int32, shape, len(shape) - 2)


def _mod_kernel(cv_ref, w_ref, b_ref, o_ref):
    cv = cv_ref[...]
    s = _silu(cv)
    w = w_ref[0]
    s_hi, s_lo = _split_bf16(s, 2)
    w_hi, w_lo = _split_bf16(w, 2)
    acc = _dot(s_hi, w_hi) + _dot(s_lo, w_hi) + _dot(s_hi, w_lo)
    o_ref[0] = acc + b_ref[0]


def _modulation(cvecs, w_mod, b_mod):
    depth, d, n = w_mod.shape
    tn = 1024
    return pl.pallas_call(
        _mod_kernel,
        out_shape=jax.ShapeDtypeStruct((depth, 8, n), F32),
        grid=(depth, n // tn),
        in_specs=[pl.BlockSpec((8, d), lambda l, j: (0, 0)),
                  pl.BlockSpec((1, d, tn), lambda l, j: (l, 0, j)),
                  pl.BlockSpec((1, 1, tn), lambda l, j: (l, 0, j))],
        out_specs=pl.BlockSpec((1, 8, tn), lambda l, j: (l, 0, j)),
        compiler_params=_cparams(("parallel", "parallel"), 32),
        name="modulation",
    )(cvecs, w_mod, b_mod.reshape(depth, 1, n))


def _head_norm_rope(p, g, bd, cos, sin, scale):
    ms = _dot((p * p).astype(BF16), bd)
    y = p * lax.rsqrt(ms + EPS) * g
    if cos is not None:
        w = y.shape[-1]
        first = (_lane_iota(y.shape) % ROPE_AXIS_DIM) < (ROPE_AXIS_DIM // 2)
        partner = jnp.where(first, pltpu.roll(y, w - ROPE_AXIS_DIM // 2, 1), pltpu.roll(y, ROPE_AXIS_DIM // 2, 1))
        y = y * cos + partner * sin
    if scale != 1.0:
        y = y * scale
    return y


def _inproj_kernel(*refs, use_rope):
    if use_rope:
        x_ref, mod_ref, g_ref, w_ref, nrm_ref, bd_ref, cos_ref, sin_ref, ssd_ref, lru_ref, c_ref, d_ref, dt_ref = refs
    else:
        x_ref, mod_ref, g_ref, w_ref, nrm_ref, bd_ref, ssd_ref, lru_ref, c_ref, d_ref, dt_ref = refs
    x = x_ref[...]
    ms = jnp.mean(x * x, axis=-1, keepdims=True)
    m = mod_ref[0]
    h = (x * lax.rsqrt(ms + EPS) * g_ref[0]) * (1.0 + m[1:2]) + m[0:1]
    hb = h.astype(BF16)
    bd = bd_ref[...]
    if use_rope:
        cos1, sin1 = cos_ref[...], sin_ref[...]
        cos2 = jnp.concatenate([cos1, cos1], axis=1)
        sin2 = jnp.concatenate([sin1, sin1], axis=1)
    else:
        cos1 = sin1 = cos2 = sin2 = None
    for o_ref, c0, row in ((c_ref, COL_C, 0), (d_ref, COL_D, 2)):
        p = _dot(hb, w_ref[0, :, c0:c0 + 512])
        q = _head_norm_rope(p[:, 0:256], nrm_ref[0, row:row + 1, :], bd, cos2, sin2, Q_SCALE)
        k = _head_norm_rope(p[:, 256:384], nrm_ref[0, row + 1:row + 2, 0:128], bd[0:128, 0:128], cos1, sin1, 1.0)
        o_ref[:, 0:256] = q.astype(BF16)
        o_ref[:, 256:384] = k.astype(BF16)
        o_ref[:, 384:512] = p[:, 384:512].astype(BF16)
    ssd_ref[...] = _dot(hb, w_ref[0, :, COL_SSD:COL_LRU]).astype(BF16)
    lru_ref[...] = _dot(hb, w_ref[0, :, COL_LRU:COL_C]).astype(BF16)
    dt_ref[...] = _dot(hb, w_ref[0, :, COL_DT:COL_END])


def _in_projection(layer, x2d, mod, mod_row_of_tile, g, w, nrm, bd, rope, tm, rope_tiles):
    rows, d = x2d.shape
    use_rope = rope is not None
    in_specs = [pl.BlockSpec((tm, d), lambda i: (i, 0)),
                pl.BlockSpec((1, N_MOD, d), lambda i: (mod_row_of_tile(i), 0, 0)),
                pl.BlockSpec((1, 1, d), lambda i: (layer, 0, 0)),
                pl.BlockSpec((1, d, COL_END), lambda i: (layer, 0, 0), pipeline_mode=pl.Buffered(1)),
                pl.BlockSpec((1, 4, 256), lambda i: (layer, 0, 0)),
                pl.BlockSpec((256, 256), lambda i: (0, 0))]
    args = [x2d, mod, g, w, nrm, bd]
    if use_rope:
        in_specs += [pl.BlockSpec((tm, LANES), lambda i: (i % rope_tiles, 0))] * 2
        args += list(rope)
    widths = (1024, 512, 512, 512, LANES)
    dtypes = (BF16, BF16, BF16, BF16, F32)
    return pl.pallas_call(
        functools.partial(_inproj_kernel, use_rope=use_rope),
        out_shape=tuple(jax.ShapeDtypeStruct((rows, wd), dt) for wd, dt in zip(widths, dtypes)),
        grid=(rows // tm,),
        in_specs=in_specs,
        out_specs=tuple(pl.BlockSpec((tm, wd), lambda i: (i, 0)) for wd in widths),
        compiler_params=_cparams(("parallel",), 56),
        name="in_projection",
    )(*args)


def _conv_selectors():
    row = np.arange(SSD_CONV * SSD_CHUNK)
    want = (row % SSD_CHUNK) + (row // SSD_CHUNK) - SSD_CONV // 2
    col = np.arange(CONV_WIN)
    sel = np.stack([(col[None, :] == want[:, None] + BF16_ROWS * case) for case in range(3)])
    return jnp.asarray(sel.astype(np.float32), BF16)


def _conv_chunk(seq_ref, col0, width, length, r0, sel_ref, w_ref, b_ref):
    ws = jnp.clip(r0 - BF16_ROWS, 0, length - CONV_WIN)
    ws = pl.multiple_of(ws, BF16_ROWS)
    xw = seq_ref[pl.ds(ws, CONV_WIN), col0:col0 + width]
    sh = _dot(sel_ref[(r0 - ws) // BF16_ROWS], xw)
    w = w_ref[0]
    acc = b_ref[0] + sh[0:SSD_CHUNK] * w[0:1]
    for k in range(1, SSD_CONV):
        acc = acc + sh[k * SSD_CHUNK:(k + 1) * SSD_CHUNK] * w[k:k + 1]
    return acc


def _ssd_kernel(pl_ref, pc_ref, dtl_ref, dtc_ref, csel_ref, cw_ref, cb_ref, hp_ref, sel_ref, dsk_ref, ng_ref,
                yl_ref, yc_ref, accl_ref, accc_ref, xbcl_ref, xbcc_ref):
    q = SSD_CHUNK
    lane8 = _lane_iota((1, LANES)) < 2 * SSD_HEADS
    dt_bias = hp_ref[0, 0:1, :]
    a_row = jnp.where(lane8, -jnp.exp(hp_ref[0, 1:2, :]) * LOG2_E, 0.0)
    ii = _row_iota((q, q))
    jj = _lane_iota((q, q))
    lo = _lane_iota((q, LANES)) < HEAD_DIM
    masks = (jj <= ii, jj >= ii)
    tris = [jnp.where(m, 1.0, 0.0).astype(BF16) for m in masks]
    tri_pairs = [jnp.concatenate([t, t], axis=1) for t in tris]
    zblk = jnp.zeros((SSD_STATE, LANES), BF16)

    def block_diag(a0, a1):
        return jnp.concatenate([jnp.concatenate([a0, zblk], axis=1), jnp.concatenate([zblk, a1], axis=1)], axis=0)

    def chunk(seq_ref, xbc_ref, dt_ref, length, r0, d, state):
        if d == 0:
            conv = _conv_chunk(seq_ref, GROUP_WIDTH, SSD_CONV_CH, length, r0, csel_ref, cw_ref, cb_ref)
            xbc_ref[pl.ds(r0, q), :] = _silu(conv).astype(BF16)
        xs = xbc_ref[pl.ds(r0, q), 0:256].astype(F32)
        bm = xbc_ref[pl.ds(r0, q), 256:512].astype(F32)
        cmb = xbc_ref[pl.ds(r0, q), 512:768]
        dtv = _softplus(dt_ref[pl.ds(r0, q), :] + dt_bias)
        da = dtv * a_row
        mask = masks[d]
        acum = _dot(tri_pairs[d], jnp.concatenate(_split_bf16(da, 2), axis=0))
        stacked = jnp.concatenate([dtv, acum], axis=0)
        ex = _dot(jnp.concatenate(_split_bf16(stacked, 2), axis=1), sel_ref[d])
        dtb, acb = ex[0:q], ex[q:2 * q]
        act = acum.T
        edge = q - 1 if d == 0 else 0
        per_head = []
        for h in range(SSD_HEADS):
            ac = acb[:, h * LANES:(h + 1) * LANES]
            ar = act[SSD_HEADS * d + h:SSD_HEADS * d + h + 1, :]
            lmat = jnp.where(mask, jnp.exp2(ac - ar), 0.0)
            alast = ac[edge:edge + 1, :]
            dth = dtb[:, h * LANES:(h + 1) * LANES]
            per_head.append((lmat, dth, jnp.exp2(alast - ac) * dth, jnp.exp2(ac), jnp.exp2(alast)))
        bts = [bm[:, g * LANES:(g + 1) * LANES].T.astype(BF16) for g in range(2)]
        cb_all = _dot(cmb, block_diag(bts[0], bts[1]))
        yo_all = _dot(cmb, block_diag(state[0].astype(BF16), state[1].astype(BF16)))
        ys, new_state = [], []
        for g in range(2):
            (l0, dt0, we0, ei0, cd0), (l1, dt1, we1, ei1, cd1) = per_head[2 * g], per_head[2 * g + 1]
            xg = xs[:, g * LANES:(g + 1) * LANES]
            cb = cb_all[:, g * LANES:(g + 1) * LANES]
            xdt = xg * jnp.where(lo, dt0, dt1)
            scores = jnp.concatenate([(cb * l0).astype(BF16), (cb * l1).astype(BF16)], axis=1)
            xpair = jnp.concatenate([jnp.where(lo, xdt, 0.0), jnp.where(lo, 0.0, xdt)], axis=0).astype(BF16)
            y = _dot(scores, xpair) + yo_all[:, g * LANES:(g + 1) * LANES] * jnp.where(lo, ei0, ei1)
            s_new = _dot(bts[g], (xg * jnp.where(lo, we0, we1)).astype(BF16))
            new_state.append(state[g] * jnp.where(lo, cd0, cd1) + s_new)
            ys.append(y)
        return jnp.concatenate(ys, axis=1), xs, tuple(new_state)

    def sweep(seq_ref, xbc_ref, dt_ref, acc_ref, out_ref, d, state):
        length = seq_ref.shape[0]
        nc = length // q

        def body(i, st):
            c = i if d == 0 else nc - 1 - i
            r0 = pl.multiple_of(c * q, q)
            y, xs, st = chunk(seq_ref, xbc_ref, dt_ref, length, r0, d, st)
            if d == 0:
                acc_ref[pl.ds(r0, q), :] = y
            else:
                y = acc_ref[pl.ds(r0, q), :] + y + dsk_ref[0] * xs
                z = seq_ref[pl.ds(r0, q), 0:GROUP_WIDTH].astype(F32)
                t = y * _silu(z)
                ms = jnp.mean(t * t, axis=-1, keepdims=True)
                out_ref[pl.ds(r0, q), :] = (t * lax.rsqrt(ms + EPS) * ng_ref[0]).astype(out_ref.dtype)
            return st

        return lax.fori_loop(0, nc, body, state, unroll=min(nc, 4))

    zero = (jnp.zeros((SSD_STATE, LANES), F32), jnp.zeros((SSD_STATE, LANES), F32))
    for d in range(2):
        st = sweep(pc_ref, xbcc_ref, dtc_ref, accc_ref, yc_ref, d, zero)
        sweep(pl_ref, xbcl_ref, dtl_ref, accl_ref, yl_ref, d, st)


def _ssd_mixer(layer, p_lat, p_ctx, dt_lat, dt_ctx, conv_sel, conv_w, conv_b, head_params, sel, d_skip, norm_g, batch):
    seq = p_lat.shape[0] // batch
    lc = p_ctx.shape[0] // batch
    const = lambda *shape: pl.BlockSpec(shape, lambda b: (0,) * len(shape))
    of_layer = lambda *shape: pl.BlockSpec((1,) + shape, lambda b: (layer,) + (0,) * len(shape))
    return pl.pallas_call(
        _ssd_kernel,
        out_shape=(jax.ShapeDtypeStruct((batch * seq, GROUP_WIDTH), BF16),
                   jax.ShapeDtypeStruct((batch * lc, GROUP_WIDTH), BF16)),
        grid=(batch,),
        in_specs=[pl.BlockSpec((seq, 1024), lambda b: (b, 0)),
                  pl.BlockSpec((lc, 1024), lambda b: (b, 0)),
                  pl.BlockSpec((seq, LANES), lambda b: (b, 0)),
                  pl.BlockSpec((lc, LANES), lambda b: (b, 0)),
                  const(3, SSD_CONV * SSD_CHUNK, CONV_WIN),
                  of_layer(SSD_CONV, SSD_CONV_CH), of_layer(1, SSD_CONV_CH), of_layer(8, LANES),
                  const(2, 2 * LANES, SSD_HEADS * LANES), of_layer(1, GROUP_WIDTH), of_layer(1, GROUP_WIDTH)],
        out_specs=(pl.BlockSpec((seq, GROUP_WIDTH), lambda b: (b, 0)),
                   pl.BlockSpec((lc, GROUP_WIDTH), lambda b: (b, 0))),
        scratch_shapes=[pltpu.VMEM((seq, GROUP_WIDTH), F32), pltpu.VMEM((lc, GROUP_WIDTH), F32),
                        pltpu.VMEM((seq, SSD_CONV_CH), BF16), pltpu.VMEM((lc, SSD_CONV_CH), BF16)],
        compiler_params=_cparams(("parallel",), 48),
        name="ssd_mixer",
    )(p_lat, p_ctx, dt_lat, dt_ctx, conv_sel, conv_w, conv_b, head_params, sel, d_skip, norm_g)


LRU_SEGMENTS = 8


def _lru_kernel(pl_ref, pc_ref, csel_ref, cw_ref, cb_ref, wg_ref, bg_ref, lam_ref, ol_ref, oc_ref, a_scr, b_scr, cin_scr):
    q = SSD_CHUNK
    nseg = LRU_SEGMENTS
    clam = -LRU_C * _softplus(-lam_ref[0])

    def run(seq_ref, out_ref, h0):
        length = seq_ref.shape[0]
        seglen = length // nseg
        sub = min(q, seglen)

        def seg_rows(t0):
            return pl.ds((t0 % seglen) * nseg + t0 // seglen, sub, stride=nseg)

        def gates(c, _):
            r0 = pl.multiple_of(c * q, q)
            xc = _conv_chunk(seq_ref, LRU_WIDTH, LRU_WIDTH, length, r0, csel_ref, cw_ref, cb_ref)
            gt = _dot(xc.astype(BF16), wg_ref[0]) + bg_ref[0]
            for d in range(2):
                r = _sigmoid(gt[:, d * 512:d * 512 + 256])
                ig = _sigmoid(gt[:, d * 512 + 256:(d + 1) * 512])
                la = clam[:, d * 256:(d + 1) * 256] * r
                a = jnp.exp(la)
                bv = jnp.sqrt(-jnp.tanh(la) * (a * a + 1.0)) * (ig * xc)
                for hf in range(2):
                    for j in range(q // sub):
                        rows = seg_rows(r0 + j * sub)
                        a_scr[2 * d + hf, rows, :] = a[j * sub:(j + 1) * sub, hf * LANES:(hf + 1) * LANES]
                        b_scr[2 * d + hf, rows, :] = bv[j * sub:(j + 1) * sub, hf * LANES:(hf + 1) * LANES]
            return 0

        lax.fori_loop(0, length // q, gates, 0, unroll=min(length // q, 4))

        def scan(k, carry):
            hs, ps = carry
            new_h, new_p = [], []
            for qd in range(4):
                s = k if qd < 2 else seglen - 1 - k
                idx = (qd, pl.ds(pl.multiple_of(s * nseg, nseg), nseg), slice(None))
                a = a_scr[idx]
                h = a * hs[qd] + b_scr[idx]
                p = ps[qd] * a
                b_scr[idx] = h
                a_scr[idx] = p
                new_h.append(h)
                new_p.append(p)
            return tuple(new_h), tuple(new_p)

        z8 = jnp.zeros((nseg, LANES), F32)
        o8 = jnp.ones((nseg, LANES), F32)
        hend, pend = lax.fori_loop(0, seglen, scan, ((z8,) * 4, (o8,) * 4), unroll=4)

        h_out = []
        for qd in range(4):
            c = h0[qd]
            order = range(nseg) if qd < 2 else range(nseg - 1, -1, -1)
            for k in order:
                cin_scr[qd, k:k + 1, :] = c
                c = hend[qd][k:k + 1, :] + pend[qd][k:k + 1, :] * c
            h_out.append(c)

        def emit(c, _):
            r0 = pl.multiple_of(c * sub, sub)
            rows = seg_rows(r0)
            hq = [b_scr[qd, rows, :] + a_scr[qd, rows, :] * cin_scr[qd, pl.ds(r0 // seglen, 1), :] for qd in range(4)]
            hsum = jnp.concatenate([hq[0] + hq[2], hq[1] + hq[3]], axis=1)
            gate = seq_ref[pl.ds(r0, sub), 0:LRU_WIDTH].astype(F32)
            out_ref[pl.ds(r0, sub), :] = (hsum * _gelu_tanh(gate)).astype(out_ref.dtype)
            return 0

        lax.fori_loop(0, length // sub, emit, 0)
        return h_out

    z1 = jnp.zeros((1, LANES), F32)
    states = run(pc_ref, oc_ref, [z1] * 4)
    run(pl_ref, ol_ref, states)


def _lru_mixer(layer, p_lat, p_ctx, conv_sel, conv_w, conv_b, w_gates, b_gates, lam, batch):
    seq = p_lat.shape[0] // batch
    lc = p_ctx.shape[0] // batch
    const = lambda *shape: pl.BlockSpec(shape, lambda b: (0,) * len(shape))
    of_layer = lambda *shape: pl.BlockSpec((1,) + shape, lambda b: (layer,) + (0,) * len(shape))
    return pl.pallas_call(
        _lru_kernel,
        out_shape=(jax.ShapeDtypeStruct((batch * seq, LRU_WIDTH), BF16),
                   jax.ShapeDtypeStruct((batch * lc, LRU_WIDTH), BF16)),
        grid=(batch,),
        in_specs=[pl.BlockSpec((seq, 512), lambda b: (b, 0)),
                  pl.BlockSpec((lc, 512), lambda b: (b, 0)),
                  const(3, SSD_CONV * SSD_CHUNK, CONV_WIN),
                  of_layer(SSD_CONV, LRU_WIDTH), of_layer(1, LRU_WIDTH), of_layer(LRU_WIDTH, 1024), of_layer(1, 1024),
                  of_layer(1, 512)],
        out_specs=(pl.BlockSpec((seq, LRU_WIDTH), lambda b: (b, 0)),
                   pl.BlockSpec((lc, LRU_WIDTH), lambda b: (b, 0))),
        scratch_shapes=[pltpu.VMEM((4, seq, LANES), F32), pltpu.VMEM((4, seq, LANES), F32),
                        pltpu.VMEM((4, LRU_SEGMENTS, LANES), F32)],
        compiler_params=_cparams(("parallel",), 48),
        name="lru_mixer",
    )(p_lat, p_ctx, conv_sel, conv_w, conv_b, w_gates, b_gates, lam)


def _stack_heads(qblk):
    lo = _lane_iota((qblk.shape[0], LANES)) < HEAD_DIM
    zero = jnp.zeros((), qblk.dtype)
    s0, s1 = qblk[:, 0:LANES], qblk[:, LANES:2 * LANES]
    return jnp.concatenate([jnp.where(lo, s0, zero), jnp.where(lo, s1, zero),
                            jnp.where(lo, zero, s0), jnp.where(lo, zero, s1)], axis=0)


def _unstack_heads(acc, t):
    lo = _lane_iota((t, LANES)) < HEAD_DIM
    return jnp.concatenate([jnp.where(lo, acc[0:t], acc[2 * t:3 * t]),
                            jnp.where(lo, acc[t:2 * t], acc[3 * t:4 * t])], axis=1)


MAX_SHIFT_BOUND = 50.0


def _gattn_kernel(bound_ref, q_ref, kl_ref, vl_ref, kc_ref, vc_ref, o_ref, qs_scr, m_scr, acc_scr, *, layer, tk, online):
    tq = q_ref.shape[0]
    half = 2 * tq
    qs_scr[...] = _stack_heads(q_ref[...])
    acc_scr[...] = jnp.zeros(acc_scr.shape, F32)
    if online:
        m_scr[...] = jnp.full(m_scr.shape, -jnp.inf, F32)

    def step(k, v):
        lo = _lane_iota(v.shape) < HEAD_DIM
        one = jnp.ones((), v.dtype)
        s = _dot_nt(qs_scr[...], k)
        if online:
            m_old = m_scr[...]
            shift = jnp.maximum(m_old, jnp.max(s, axis=-1, keepdims=True))
            m_scr[...] = shift
            acc_scr[...] = acc_scr[...] * jnp.exp2(m_old - shift)
        else:
            shift = bound_ref[layer]
        p = jnp.exp2(s - shift).astype(BF16)
        acc_scr[0:half, :] += _dot(p[0:half], jnp.where(lo, v, one))
        acc_scr[half:, :] += _dot(p[half:], jnp.where(lo, one, v))

    def body(j, _):
        r0 = pl.multiple_of(j * tk, tk)
        step(kl_ref[pl.ds(r0, tk), :], vl_ref[pl.ds(r0, tk), :])
        return 0

    lax.fori_loop(0, kl_ref.shape[0] // tk, body, 0, unroll=True)
    step(kc_ref[...], vc_ref[...])
    acc = acc_scr[...]
    o_ref[...] = _unstack_heads(acc / pltpu.roll(acc, HEAD_DIM, 1), tq).astype(o_ref.dtype)


def _logit_bounds(q_gain, k_gain):
    return HEAD_DIM * Q_SCALE * 1.02 * jnp.max(jnp.abs(q_gain), axis=-1) * jnp.max(jnp.abs(k_gain), axis=-1)


def _global_attention(layer, bound, qkv_lat, qkv_ctx, batch, tq, tk):
    seq = qkv_lat.shape[0] // batch
    lc = qkv_ctx.shape[0] // batch
    nq = seq // tq

    def call(online):
        return pl.pallas_call(
            functools.partial(_gattn_kernel, layer=layer, tk=tk, online=online),
            out_shape=jax.ShapeDtypeStruct((batch * seq, 256), BF16),
            grid=(batch, nq),
            in_specs=[pl.BlockSpec(memory_space=pltpu.SMEM),
                      pl.BlockSpec((tq, 256), lambda b, i: (b * nq + i, 0)),
                      pl.BlockSpec((seq, LANES), lambda b, i: (b, 2)),
                      pl.BlockSpec((seq, LANES), lambda b, i: (b, 3)),
                      pl.BlockSpec((lc, LANES), lambda b, i: (b, 2)),
                      pl.BlockSpec((lc, LANES), lambda b, i: (b, 3))],
            out_specs=pl.BlockSpec((tq, 256), lambda b, i: (b * nq + i, 0)),
            scratch_shapes=[pltpu.VMEM((4 * tq, LANES), BF16), pltpu.VMEM((4 * tq, 1), F32),
                            pltpu.VMEM((4 * tq, LANES), F32)],
            compiler_params=_cparams(("parallel", "parallel"), 48),
            name="global_attention_online" if online else "global_attention",
        )(bound, qkv_lat, qkv_lat, qkv_lat, qkv_ctx, qkv_ctx)

    return lax.cond(bound[layer] <= MAX_SHIFT_BOUND, lambda: call(False), lambda: call(True))


def _sink_rows(sink_ref, layer, t):
    head_of_row = _row_iota((4 * t, 1)) // t
    out = jnp.zeros((4 * t, 1), F32)
    for h in range(Q_HEADS):
        out = jnp.where(head_of_row == h, sink_ref[layer, h] * LOG2_E, out)
    return out


def _wattn_kernel(bound_ref, sink_ref, q_ref, kl_ref, vl_ref, kc_ref, vc_ref, o_ref, bias_scr, *, layer, online):
    w = WINDOW
    seq = kl_ref.shape[0]
    band = 3 * w
    shift = 0.0 if online else bound_ref[layer]
    dmat = _lane_iota((w, band)) - _row_iota((w, band))
    for case in range(3):
        bias_scr[case] = jnp.where(jnp.abs(dmat - case * w) <= w, -shift, -jnp.inf)
    one = jnp.ones((), BF16)
    kc, vc = kc_ref[...], vc_ref[...]
    lo_c = _lane_iota(vc.shape) < HEAD_DIM
    vc_sum = (jnp.where(lo_c, vc, one), jnp.where(lo_c, one, vc))
    lo_b = _lane_iota((band, LANES)) < HEAD_DIM
    lo = _lane_iota((w, LANES)) < HEAD_DIM

    def body(n, _):
        r0 = pl.multiple_of(n * w, w)
        gq = pl.program_id(1) * q_ref.shape[0] + r0
        ks = pl.multiple_of(jnp.clip(gq - w, 0, seq - band), w)
        bias = bias_scr[(gq - ks) // w]
        kb, vb = kl_ref[pl.ds(ks, band), :], vl_ref[pl.ds(ks, band), :]
        vb_sum = (jnp.where(lo_b, vb, one), jnp.where(lo_b, one, vb))
        qs = _stack_heads(q_ref[pl.ds(r0, w), :])
        s_band = _dot_nt(qs, kb)
        s_ctx = _dot_nt(qs, kc)
        outs = []
        for g in range(2):
            pb, pc, sk = [], [], []
            for h in (2 * g, 2 * g + 1):
                sb = s_band[h * w:(h + 1) * w] + bias
                sc = s_ctx[h * w:(h + 1) * w] - shift
                sink = sink_ref[layer, h] * LOG2_E - shift
                if online:
                    m = jnp.maximum(jnp.maximum(jnp.max(sb, axis=-1, keepdims=True),
                                                jnp.max(sc, axis=-1, keepdims=True)), sink)
                    sb, sc, sink = sb - m, sc - m, sink - m
                pb.append(jnp.exp2(sb).astype(BF16))
                pc.append(jnp.exp2(sc).astype(BF16))
                sk.append(jnp.exp2(sink) if online else jnp.full((w, 1), jnp.exp2(sink), F32))
            acc = _dot(jnp.concatenate(pb, axis=0), vb_sum[g]) + _dot(jnp.concatenate(pc, axis=0), vc_sum[g])
            den = pltpu.roll(acc, HEAD_DIM, 1) + jnp.concatenate(sk, axis=0)
            outs.append(acc / den)
        o_ref[pl.ds(r0, w), :] = jnp.concatenate(
            [jnp.where(lo, outs[0][0:w], outs[1][0:w]), jnp.where(lo, outs[0][w:2 * w], outs[1][w:2 * w])],
            axis=1).astype(o_ref.dtype)
        return 0

    lax.fori_loop(0, q_ref.shape[0] // w, body, 0, unroll=4)


def _window_attention(layer, bound, sink, qkv_lat, qkv_ctx, batch, tq):
    seq = qkv_lat.shape[0] // batch
    lc = qkv_ctx.shape[0] // batch
    nq = seq // tq

    def call(online):
        return pl.pallas_call(
            functools.partial(_wattn_kernel, layer=layer, online=online),
            out_shape=jax.ShapeDtypeStruct((batch * seq, 256), BF16),
            grid=(batch, nq),
            in_specs=[pl.BlockSpec(memory_space=pltpu.SMEM), pl.BlockSpec(memory_space=pltpu.SMEM),
                      pl.BlockSpec((tq, 256), lambda b, i: (b * nq + i, 0)),
                      pl.BlockSpec((seq, LANES), lambda b, i: (b, 2)),
                      pl.BlockSpec((seq, LANES), lambda b, i: (b, 3)),
                      pl.BlockSpec((lc, LANES), lambda b, i: (b, 2)),
                      pl.BlockSpec((lc, LANES), lambda b, i: (b, 3))],
            out_specs=pl.BlockSpec((tq, 256), lambda b, i: (b * nq + i, 0)),
            scratch_shapes=[pltpu.VMEM((3, WINDOW, 3 * WINDOW), F32)],
            compiler_params=_cparams(("parallel", "parallel"), 32),
            name="window_attention_online" if online else "window_attention",
        )(bound, sink, qkv_lat, qkv_lat, qkv_lat, qkv_ctx, qkv_ctx)

    return lax.cond(bound[layer] <= MAX_SHIFT_BOUND, lambda: call(False), lambda: call(True))


def _cattn_kernel(sink_ref, c_ref, d_ref, oc_ref, od_ref, *, layer):
    t = c_ref.shape[0]
    for ref, out, has_sink in ((c_ref, oc_ref, False), (d_ref, od_ref, True)):
        qs = _stack_heads(ref[:, 0:256])
        s = _dot_nt(qs, ref[:, 256:384])
        m = jnp.max(s, axis=-1, keepdims=True)
        if has_sink:
            sink = _sink_rows(sink_ref, layer, t)
            m = jnp.maximum(m, sink)
        p = jnp.exp2(s - m)
        den = jnp.sum(p, axis=-1, keepdims=True)
        if has_sink:
            den = den + jnp.exp2(sink - m)
        acc = _dot(p.astype(BF16), ref[:, 384:512])
        out[...] = _unstack_heads(acc / den, t).astype(out.dtype)


def _context_attention(layer, sink, qkv_c, qkv_d, batch):
    lc = qkv_c.shape[0] // batch
    blk = lambda width: pl.BlockSpec((lc, width), lambda b: (b, 0))
    return pl.pallas_call(
        functools.partial(_cattn_kernel, layer=layer),
        out_shape=(jax.ShapeDtypeStruct((batch * lc, 256), BF16),) * 2,
        grid=(batch,),
        in_specs=[pl.BlockSpec(memory_space=pltpu.SMEM), blk(512), blk(512)],
        out_specs=(blk(256), blk(256)),
        compiler_params=_cparams(("parallel",), 32),
        name="context_attention",
    )(sink, qkv_c, qkv_d)


def _out_ffn_kernel(x_ref, ya_ref, yb_ref, yc_ref, yd_ref, mod_ref, wo_ref, g_ref, w1_ref, w2_ref, o_ref, *, hchunk):
    m = mod_ref[0]
    gw = GROUP_WIDTH
    mix = _dot(ya_ref[...], wo_ref[0, 0:gw, :])
    for i, y_ref in enumerate((yb_ref, yc_ref, yd_ref), start=1):
        mix = mix + _dot(y_ref[...], wo_ref[0, i * gw:(i + 1) * gw, :])
    x1 = x_ref[...] + m[2:3] * mix
    ms = jnp.mean(x1 * x1, axis=-1, keepdims=True)
    h = ((x1 * lax.rsqrt(ms + EPS) * g_ref[0]) * (1.0 + m[4:5]) + m[3:4]).astype(BF16)
    acc = None
    for c in range(w1_ref.shape[2] // hchunk):
        u = jnp.maximum(_dot(h, w1_ref[0, :, c * hchunk:(c + 1) * hchunk]), 0.0)
        t = _dot((u * u).astype(BF16), w2_ref[0, c * hchunk:(c + 1) * hchunk, :])
        acc = t if acc is None else acc + t
    o_ref[...] = x1 + m[5:6] * acc


def _out_ffn(layer, x2d, ys, mod, mod_row_of_tile, w_out, g, w1, w2, tm):
    rows, d = x2d.shape
    hidden = w1.shape[2]
    row_blk = lambda width: pl.BlockSpec((tm, width), lambda i: (i, 0))
    const = lambda *shape: pl.BlockSpec((1,) + shape, lambda i: (layer,) + (0,) * len(shape), pipeline_mode=pl.Buffered(1))
    return pl.pallas_call(
        functools.partial(_out_ffn_kernel, hchunk=1024),
        out_shape=jax.ShapeDtypeStruct((rows, d), F32),
        grid=(rows // tm,),
        in_specs=[row_blk(d), row_blk(256), row_blk(256), row_blk(256), row_blk(256),
                  pl.BlockSpec((1, N_MOD, d), lambda i: (mod_row_of_tile(i), 0, 0)),
                  const(d, d), pl.BlockSpec((1, 1, d), lambda i: (layer, 0, 0)), const(d, hidden), const(hidden, d)],
        out_specs=row_blk(d),
        compiler_params=_cparams(("parallel",), 56),
        name="out_ffn",
    )(x2d, *ys, mod, w_out, g, w1, w2)


def _permute_q_heads(cols):
    q = cols[..., 0:256].reshape(cols.shape[:-1] + (Q_HEADS, HEAD_DIM))
    q = q[..., Q_HEAD_ORDER, :].reshape(cols.shape[:-1] + (256,))
    return jnp.concatenate([q, cols[..., 256:512]], axis=-1)


def _relayout_w_in(w):
    ssd_in = GROUP_WIDTH + SSD_CONV_CH + 2 * SSD_HEADS
    b0, c0, d0 = ssd_in, ssd_in + 512, ssd_in + 1024
    dt = w[..., GROUP_WIDTH + SSD_CONV_CH:ssd_in]
    dt = jnp.pad(dt, [(0, 0)] * (w.ndim - 1) + [(0, LANES - 2 * SSD_HEADS)])
    return jnp.concatenate([w[..., 0:GROUP_WIDTH + SSD_CONV_CH], w[..., b0:c0], _permute_q_heads(w[..., c0:d0]),
                            _permute_q_heads(w[..., d0:d0 + 512]), dt], axis=-1).astype(BF16)


def _relayout_w_out(w):
    depth, _, d = w.shape
    attn = w[:, 512:1024].reshape(depth, 2, Q_HEADS, HEAD_DIM, d)[:, :, Q_HEAD_ORDER].reshape(depth, 512, d)
    return jnp.concatenate([w[:, 0:512], attn], axis=1).astype(BF16)


def _block_diag(w):
    eye = jnp.eye(LRU_BLOCKS, dtype=w.dtype)
    out = jnp.einsum('...kij,kn->...kinj', w, eye)
    return out.reshape(w.shape[:-3] + (LRU_WIDTH, LRU_WIDTH))


def _rope_tables(length):
    rows = length // GRID_W
    row = jnp.repeat(jnp.arange(rows, dtype=F32), GRID_W)
    col = jnp.tile(jnp.arange(GRID_W, dtype=F32), rows)
    inv = ROPE_THETA ** (-jnp.arange(0, ROPE_AXIS_DIM, 2, dtype=F32) / ROPE_AXIS_DIM)
    half = ROPE_AXIS_DIM // 2
    ang = jnp.concatenate([row[:, None] * inv] * 2 + [col[:, None] * inv] * 2, axis=1)
    sign = jnp.tile(jnp.concatenate([-jnp.ones((half,), F32), jnp.ones((half,), F32)]), 2)
    cos = jnp.tile(jnp.cos(ang), (1, LANES // HEAD_DIM))
    sin = jnp.tile(jnp.sin(ang) * sign, (1, LANES // HEAD_DIM))
    return cos, sin


def _head_selectors():
    sel = np.zeros((2, 2 * LANES, SSD_HEADS * LANES), np.float32)
    for d in range(2):
        for h in range(SSD_HEADS):
            for part in range(2):
                sel[d, part * LANES + SSD_HEADS * d + h, h * LANES:(h + 1) * LANES] = 1.0
    return jnp.asarray(sel, BF16)


def _head_mean_matrix():
    idx = np.arange(256) // HEAD_DIM
    return jnp.asarray((idx[:, None] == idx[None, :]).astype(np.float32) / HEAD_DIM, BF16)


def kernel(x, c, ctx, c_ctx, w_mod, b_mod, g_mix, w_in, ssd_conv_w, ssd_conv_b, ssd_a_log, ssd_dt_bias, ssd_d, ssd_norm_g, lru_conv_w, lru_conv_b, lru_lambda, lru_w_a, lru_b_a, lru_w_i, lru_b_i, gqa_q_norm, gqa_k_norm, swa_q_norm, swa_k_norm, swa_sink, w_out, g_ffn, w_ffn1, w_ffn2):
    batch, seq, d = x.shape
    lc = ctx.shape[1]
    depth = w_mod.shape[0]
    assert batch + 1 <= 8 and seq % 1024 == 0 and lc % 256 == 0 and seq % GRID_W == 0

    cvecs = jnp.concatenate([c, c_ctx[None, :], jnp.zeros((8 - batch - 1, d), F32)], axis=0)
    mod = _modulation(cvecs, w_mod, b_mod).reshape(depth * 8, N_MOD, d)

    rope = _rope_tables(seq)
    sel = _head_selectors()
    bd = _head_mean_matrix()
    conv_sel = _conv_selectors()
    w_in_r = _relayout_w_in(w_in)
    w_out_r = _relayout_w_out(w_out)
    w1, w2 = w_ffn1.astype(BF16), w_ffn2.astype(BF16)
    g1, g2 = g_mix.reshape(depth, 1, d), g_ffn.reshape(depth, 1, d)
    nrm = jnp.tile(jnp.stack([gqa_q_norm, gqa_k_norm, swa_q_norm, swa_k_norm], axis=1), (1, 1, 256 // HEAD_DIM))
    lane_pad = ((0, 0), (0, 0), (0, LANES - 2 * SSD_HEADS))
    head_params = jnp.concatenate([jnp.pad(ssd_dt_bias.reshape(depth, 1, -1), lane_pad),
                                   jnp.pad(ssd_a_log.reshape(depth, 1, -1), lane_pad),
                                   jnp.zeros((depth, 6, LANES), F32)], axis=1)
    ssd_cb = ssd_conv_b.reshape(depth, 1, -1)
    d_skip = jnp.repeat(ssd_d, HEAD_DIM, axis=-1).reshape(depth, 1, -1)
    ssd_ng = ssd_norm_g.reshape(depth, 1, -1)
    bd_a, bd_i = _block_diag(lru_w_a), _block_diag(lru_w_i)
    w_gates = jnp.concatenate([bd_a[:, 0], bd_i[:, 0], bd_a[:, 1], bd_i[:, 1]], axis=-1).astype(BF16)
    b_gates = jnp.concatenate([lru_b_a[:, 0], lru_b_i[:, 0], lru_b_a[:, 1], lru_b_i[:, 1]], axis=-1).reshape(depth, 1, -1)
    lru_cb = lru_conv_b.reshape(depth, 1, -1)
    lam = lru_lambda.reshape(depth, 1, -1)
    bounds = _logit_bounds(gqa_q_norm, gqa_k_norm)
    swa_bounds = _logit_bounds(swa_q_norm, swa_k_norm)

    tm = 1024
    tiles_per_batch = seq // tm
    xl = x.reshape(batch * seq, d)
    xc = ctx.reshape(batch * lc, d)
    for l in range(depth):
        need_ctx = l < depth - 1
        lat_row = lambda i, l=l: l * 8 + i // tiles_per_batch
        ctx_row = lambda i, l=l: l * 8 + batch
        ssd_l, lru_l, qc_l, qd_l, dt_l = _in_projection(l, xl, mod, lat_row, g1, w_in_r, nrm, bd, rope, tm, tiles_per_batch)
        ssd_c, lru_c, qc_c, qd_c, dt_c = _in_projection(l, xc, mod, ctx_row, g1, w_in_r, nrm, bd, None, 256, 1)
        ya_l, ya_c = _ssd_mixer(l, ssd_l, ssd_c, dt_l, dt_c, conv_sel, ssd_conv_w, ssd_cb, head_params, sel, d_skip,
                                ssd_ng, batch)
        yb_l, yb_c = _lru_mixer(l, lru_l, lru_c, conv_sel, lru_conv_w, lru_cb, w_gates, b_gates, lam, batch)
        yc_l = _global_attention(l, bounds, qc_l, qc_c, batch, 512, 512)
        yd_l = _window_attention(l, swa_bounds, swa_sink, qd_l, qd_c, batch, seq)
        xl = _out_ffn(l, xl, (ya_l, yb_l, yc_l, yd_l), mod, lambda i, l=l: l * 8 + i // (seq // 512), w_out_r, g2, w1, w2, 512)
        if need_ctx:
            yc_c, yd_c = _context_attention(l, swa_sink, qc_c, qd_c, batch)
            xc = _out_ffn(l, xc, (ya_c, yb_c, yc_c, yd_c), mod, ctx_row, w_out_r, g2, w1, w2, 512)
    return xl.reshape(batch, seq, d)
```

```python
import functools
import math

import numpy as np
import jax
import jax.numpy as jnp
from jax import lax
from jax.experimental import pallas as pl
from jax.experimental.pallas import tpu as pltpu

F32 = jnp.float32
BF16 = jnp.bfloat16

EPS = 1e-6
GRID_W = 64
N_MOD = 6
GROUP_WIDTH = 256
SSD_HEADS = 4
SSD_STATE = 128
SSD_CONV = 4
SSD_CHUNK = 128
SSD_CONV_CH = 768
LRU_WIDTH = 256
LRU_BLOCKS = 4
LRU_C = 8.0
HEAD_DIM = 64
Q_HEADS = 4
WINDOW = 128
ROPE_THETA = 10000.0
ROPE_AXIS_DIM = 32
LOG2_E = math.log2(math.e)
Q_SCALE = HEAD_DIM ** -0.5 * LOG2_E

LANES = 128
BF16_ROWS = 16

COL_SSD, COL_LRU, COL_C, COL_D, COL_DT, COL_END = 0, 1024, 1536, 2048, 2560, 2688
Q_HEAD_ORDER = (0, 2, 1, 3)

CONV_WIN = SSD_CHUNK + 2 * BF16_ROWS


def _cparams(sem, vmem_mib):
    return pltpu.CompilerParams(dimension_semantics=sem, vmem_limit_bytes=vmem_mib * 1024 * 1024)


def _dot(a, b):
    return jnp.dot(a, b, preferred_element_type=F32)


def _dot_nt(a, b):
    return lax.dot_general(a, b, (((1,), (1,)), ((), ())), preferred_element_type=F32)


def _split_bf16(a, parts):
    out = []
    for _ in range(parts - 1):
        hi = a.astype(BF16)
        out.append(hi)
        a = a - hi.astype(F32)
    out.append(a.astype(BF16))
    return out


def _silu(x):
    h = 0.5 * x
    return h + h * jnp.tanh(h)


def _softplus(x):
    return jnp.maximum(x, 0.0) + jnp.log1p(jnp.exp(-jnp.abs(x)))


def _gelu_tanh(x):
    return 0.5 * x * (1.0 + jnp.tanh(math.sqrt(2.0 / math.pi) * (x + 0.044715 * (x * x * x))))


def _lane_iota(shape):
    return lax.broadcasted_iota(jnp.int32, shape, len(shape) - 1)


def _row_iota(shape):
    return lax.broadcasted_iota(jnp.int32, shape, len(shape) - 2)


def _mod_kernel(cv_ref, w_ref, b_ref, o_ref):
    cv = cv_ref[...]
    s = _silu(cv)
    w = w_ref[0]
    s_hi, s_lo = _split_bf16(s, 2)
    w_hi, w_lo = _split_bf16(w, 2)
    acc = _dot(s_hi, w_hi) + _dot(s_lo, w_hi) + _dot(s_hi, w_lo)
    o_ref[0] = acc + b_ref[0]


def _modulation(cvecs, w_mod, b_mod):
    depth, d, n = w_mod.shape
    tn = 1024
    return pl.pallas_call(
        _mod_kernel,
        out_shape=jax.ShapeDtypeStruct((depth, 8, n), F32),
        grid=(depth, n // tn),
        in_specs=[pl.BlockSpec((8, d), lambda l, j: (0, 0)),
                  pl.BlockSpec((1, d, tn), lambda l, j: (l, 0, j)),
                  pl.BlockSpec((1, 1, tn), lambda l, j: (l, 0, j))],
        out_specs=pl.BlockSpec((1, 8, tn), lambda l, j: (l, 0, j)),
        compiler_params=_cparams(("parallel", "parallel"), 32),
        name="modulation",
    )(cvecs, w_mod, b_mod.reshape(depth, 1, n))


def _head_norm_rope(p, g, bd, cos, sin, scale):
    ms = _dot((p * p).astype(BF16), bd)
    y = p * lax.rsqrt(ms + EPS) * g
    if cos is not None:
        w = y.shape[-1]
        first = (_lane_iota(y.shape) % ROPE_AXIS_DIM) < (ROPE_AXIS_DIM // 2)
        partner = jnp.where(first, pltpu.roll(y, w - ROPE_AXIS_DIM // 2, 1), pltpu.roll(y, ROPE_AXIS_DIM // 2, 1))
        y = y * cos + partner * sin
    if scale != 1.0:
        y = y * scale
    return y


def _inproj_kernel(*refs, use_rope):
    if use_rope:
        x_ref, mod_ref, g_ref, w_ref, nrm_ref, bd_ref, cos_ref, sin_ref, ssd_ref, lru_ref, c_ref, d_ref, dt_ref = refs
    else:
        x_ref, mod_ref, g_ref, w_ref, nrm_ref, bd_ref, ssd_ref, lru_ref, c_ref, d_ref, dt_ref = refs
    x = x_ref[...]
    ms = jnp.mean(x * x, axis=-1, keepdims=True)
    m = mod_ref[0]
    h = (x * lax.rsqrt(ms + EPS) * g_ref[0]) * (1.0 + m[1:2]) + m[0:1]
    hb = h.astype(BF16)
    bd = bd_ref[...]
    if use_rope:
        cos1, sin1 = cos_ref[...], sin_ref[...]
        cos2 = jnp.concatenate([cos1, cos1], axis=1)
        sin2 = jnp.concatenate([sin1, sin1], axis=1)
    else:
        cos1 = sin1 = cos2 = sin2 = None
    for o_ref, c0, row in ((c_ref, COL_C, 0), (d_ref, COL_D, 2)):
        p = _dot(hb, w_ref[0, :, c0:c0 + 512])
        q = _head_norm_rope(p[:, 0:256], nrm_ref[0, row:row + 1, :], bd, cos2, sin2, Q_SCALE)
        k = _head_norm_rope(p[:, 256:384], nrm_ref[0, row + 1:row + 2, 0:128], bd[0:128, 0:128], cos1, sin1, 1.0)
        o_ref[:, 0:256] = q.astype(BF16)
        o_ref[:, 256:384] = k.astype(BF16)
        o_ref[:, 384:512] = p[:, 384:512].astype(BF16)
    ssd_ref[...] = _dot(hb, w_ref[0, :, COL_SSD:COL_LRU]).astype(BF16)
    lru_ref[...] = _dot(hb, w_ref[0, :, COL_LRU:COL_C]).astype(BF16)
    dt_ref[...] = _dot(hb, w_ref[0, :, COL_DT:COL_END])


def _in_projection(layer, x2d, mod, mod_row_of_tile, g, w, nrm, bd, rope, tm, rope_tiles):
    rows, d = x2d.shape
    use_rope = rope is not None
    in_specs = [pl.BlockSpec((tm, d), lambda i: (i, 0)),
                pl.BlockSpec((1, N_MOD, d), lambda i: (mod_row_of_tile(i), 0, 0)),
                pl.BlockSpec((1, 1, d), lambda i: (layer, 0, 0)),
                pl.BlockSpec((1, d, COL_END), lambda i: (layer, 0, 0), pipeline_mode=pl.Buffered(1)),
                pl.BlockSpec((1, 4, 256), lambda i: (layer, 0, 0)),
                pl.BlockSpec((256, 256), lambda i: (0, 0))]
    args = [x2d, mod, g, w, nrm, bd]
    if use_rope:
        in_specs += [pl.BlockSpec((tm, LANES), lambda i: (i % rope_tiles, 0))] * 2
        args += list(rope)
    widths = (1024, 512, 512, 512, LANES)
    dtypes = (BF16, BF16, BF16, BF16, F32)
    return pl.pallas_call(
        functools.partial(_inproj_kernel, use_rope=use_rope),
        out_shape=tuple(jax.ShapeDtypeStruct((rows, wd), dt) for wd, dt in zip(widths, dtypes)),
        grid=(rows // tm,),
        in_specs=in_specs,
        out_specs=tuple(pl.BlockSpec((tm, wd), lambda i: (i, 0)) for wd in widths),
        compiler_params=_cparams(("parallel",), 56),
        name="in_projection",
    )(*args)


def _conv_selectors():
    row = np.arange(SSD_CONV * SSD_CHUNK)
    want = (row % SSD_CHUNK) + (row // SSD_CHUNK) - SSD_CONV // 2
    col = np.arange(CONV_WIN)
    sel = np.stack([(col[None, :] == want[:, None] + BF16_ROWS * case) for case in range(3)])
    return jnp.asarray(sel.astype(np.float32), BF16)


def _conv_chunk(seq_ref, col0, width, length, r0, sel_ref, w_ref, b_ref):
    ws = jnp.clip(r0 - BF16_ROWS, 0, length - CONV_WIN)
    ws = pl.multiple_of(ws, BF16_ROWS)
    xw = seq_ref[pl.ds(ws, CONV_WIN), col0:col0 + width]
    sh = _dot(sel_ref[(r0 - ws) // BF16_ROWS], xw)
    w = w_ref[0]
    acc = b_ref[0] + sh[0:SSD_CHUNK] * w[0:1]
    for k in range(1, SSD_CONV):
        acc = acc + sh[k * SSD_CHUNK:(k + 1) * SSD_CHUNK] * w[k:k + 1]
    return acc


def _ssd_kernel(pl_ref, pc_ref, dtl_ref, dtc_ref, csel_ref, cw_ref, cb_ref, hp_ref, sel_ref, dsk_ref, ng_ref,
                yl_ref, yc_ref, accl_ref, accc_ref, xbcl_ref, xbcc_ref):
    q = SSD_CHUNK
    lane8 = _lane_iota((1, LANES)) < 2 * SSD_HEADS
    dt_bias = hp_ref[0, 0:1, :]
    a_row = jnp.where(lane8, -jnp.exp(hp_ref[0, 1:2, :]) * LOG2_E, 0.0)
    ii = _row_iota((q, q))
    jj = _lane_iota((q, q))
    lo = _lane_iota((q, LANES)) < HEAD_DIM
    masks = (jj <= ii, jj >= ii)
    tris = [jnp.where(m, 1.0, 0.0).astype(BF16) for m in masks]
    tri_pairs = [jnp.concatenate([t, t], axis=1) for t in tris]
    zblk = jnp.zeros((SSD_STATE, LANES), BF16)

    def block_diag(a0, a1):
        return jnp.concatenate([jnp.concatenate([a0, zblk], axis=1), jnp.concatenate([zblk, a1], axis=1)], axis=0)

    def chunk(seq_ref, xbc_ref, dt_ref, length, r0, d, state):
        if d == 0:
            conv = _conv_chunk(seq_ref, GROUP_WIDTH, SSD_CONV_CH, length, r0, csel_ref, cw_ref, cb_ref)
            xbc_ref[pl.ds(r0, q), :] = _silu(conv).astype(BF16)
        xs = xbc_ref[pl.ds(r0, q), 0:256].astype(F32)
        bm = xbc_ref[pl.ds(r0, q), 256:512].astype(F32)
        cmb = xbc_ref[pl.ds(r0, q), 512:768]
        dtv = _softplus(dt_ref[pl.ds(r0, q), :] + dt_bias)
        da = dtv * a_row
        mask = masks[d]
        acum = _dot(tri_pairs[d], jnp.concatenate(_split_bf16(da, 2), axis=0))
        stacked = jnp.concatenate([dtv, acum], axis=0)
        ex = _dot(jnp.concatenate(_split_bf16(stacked, 2), axis=1), sel_ref[d])
        dtb, acb = ex[0:q], ex[q:2 * q]
        act = acum.T
        edge = q - 1 if d == 0 else 0
        per_head = []
        for h in range(SSD_HEADS):
            ac = acb[:, h * LANES:(h + 1) * LANES]
            ar = act[SSD_HEADS * d + h:SSD_HEADS * d + h + 1, :]
            lmat = jnp.where(mask, jnp.exp2(ac - ar), 0.0)
            alast = ac[edge:edge + 1, :]
            dth = dtb[:, h * LANES:(h + 1) * LANES]
            per_head.append((lmat, dth, jnp.exp2(alast - ac) * dth, jnp.exp2(ac), jnp.exp2(alast)))
        bts = [bm[:, g * LANES:(g + 1) * LANES].T.astype(BF16) for g in range(2)]
        cb_all = _dot(cmb, block_diag(bts[0], bts[1]))
        yo_all = _dot(cmb, block_diag(state[0].astype(BF16), state[1].astype(BF16)))
        ys, new_state = [], []
        for g in range(2):
            (l0, dt0, we0, ei0, cd0), (l1, dt1, we1, ei1, cd1) = per_head[2 * g], per_head[2 * g + 1]
            xg = xs[:, g * LANES:(g + 1) * LANES]
            cb = cb_all[:, g * LANES:(g + 1) * LANES]
            xdt = xg * jnp.where(lo, dt0, dt1)
            scores = jnp.concatenate([(cb * l0).astype(BF16), (cb * l1).astype(BF16)], axis=1)
            xpair = jnp.concatenate([jnp.where(lo, xdt, 0.0), jnp.where(lo, 0.0, xdt)], axis=0).astype(BF16)
            y = _dot(scores, xpair) + yo_all[:, g * LANES:(g + 1) * LANES] * jnp.where(lo, ei0, ei1)
            s_new = _dot(bts[g], (xg * jnp.where(lo, we0, we1)).astype(BF16))
            new_state.append(state[g] * jnp.where(lo, cd0, cd1) + s_new)
            ys.append(y)
        return jnp.concatenate(ys, axis=1), xs, tuple(new_state)

    def sweep(seq_ref, xbc_ref, dt_ref, acc_ref, out_ref, d, state):
        length = seq_ref.shape[0]
        nc = length // q

        def body(i, st):
            c = i if d == 0 else nc - 1 - i
            r0 = pl.multiple_of(c * q, q)
            y, xs, st = chunk(seq_ref, xbc_ref, dt_ref, length, r0, d, st)
            if d == 0:
                acc_ref[pl.ds(r0, q), :] = y
            else:
                y = acc_ref[pl.ds(r0, q), :] + y + dsk_ref[0] * xs
                z = seq_ref[pl.ds(r0, q), 0:GROUP_WIDTH].astype(F32)
                t = y * _silu(z)
                ms = jnp.mean(t * t, axis=-1, keepdims=True)
                out_ref[pl.ds(r0, q), :] = (t * lax.rsqrt(ms + EPS) * ng_ref[0]).astype(out_ref.dtype)
            return st

        return lax.fori_loop(0, nc, body, state, unroll=min(nc, 4))

    zero = (jnp.zeros((SSD_STATE, LANES), F32), jnp.zeros((SSD_STATE, LANES), F32))
    for d in range(2):
        st = sweep(pc_ref, xbcc_ref, dtc_ref, accc_ref, yc_ref, d, zero)
        sweep(pl_ref, xbcl_ref, dtl_ref, accl_ref, yl_ref, d, st)


def _ssd_mixer(layer, p_lat, p_ctx, dt_lat, dt_ctx, conv_sel, conv_w, conv_b, head_params, sel, d_skip, norm_g, batch):
    seq = p_lat.shape[0] // batch
    lc = p_ctx.shape[0] // batch
    const = lambda *shape: pl.BlockSpec(shape, lambda b: (0,) * len(shape))
    of_layer = lambda *shape: pl.BlockSpec((1,) + shape, lambda b: (layer,) + (0,) * len(shape))
    return pl.pallas_call(
        _ssd_kernel,
        out_shape=(jax.ShapeDtypeStruct((batch * seq, GROUP_WIDTH), BF16),
                   jax.ShapeDtypeStruct((batch * lc, GROUP_WIDTH), BF16)),
        grid=(batch,),
        in_specs=[pl.BlockSpec((seq, 1024), lambda b: (b, 0)),
                  pl.BlockSpec((lc, 1024), lambda b: (b, 0)),
                  pl.BlockSpec((seq, LANES), lambda b: (b, 0)),
                  pl.BlockSpec((lc, LANES), lambda b: (b, 0)),
                  const(3, SSD_CONV * SSD_CHUNK, CONV_WIN),
                  of_layer(SSD_CONV, SSD_CONV_CH), of_layer(1, SSD_CONV_CH), of_layer(8, LANES),
                  const(2, 2 * LANES, SSD_HEADS * LANES), of_layer(1, GROUP_WIDTH), of_layer(1, GROUP_WIDTH)],
        out_specs=(pl.BlockSpec((seq, GROUP_WIDTH), lambda b: (b, 0)),
                   pl.BlockSpec((lc, GROUP_WIDTH), lambda b: (b, 0))),
        scratch_shapes=[pltpu.VMEM((seq, GROUP_WIDTH), F32), pltpu.VMEM((lc, GROUP_WIDTH), F32),
                        pltpu.VMEM((seq, SSD_CONV_CH), BF16), pltpu.VMEM((lc, SSD_CONV_CH), BF16)],
        compiler_params=_cparams(("parallel",), 48),
        name="ssd_mixer",
    )(p_lat, p_ctx, dt_lat, dt_ctx, conv_sel, conv_w, conv_b, head_params, sel, d_skip, norm_g)


LRU_SEGMENTS = 8


def _lru_kernel(pl_ref, pc_ref, csel_ref, cw_ref, cb_ref, wg_ref, bg_ref, lam_ref, ol_ref, oc_ref, a_scr, b_scr, cin_scr):
    q = SSD_CHUNK
    nseg = LRU_SEGMENTS
    nhc = (0.5 * LRU_C) * _softplus(-lam_ref[0])

    def run(seq_ref, out_ref, h0):
        length = seq_ref.shape[0]
        seglen = length // nseg
        sub = min(q, seglen)

        def seg_rows(t0):
            return pl.ds((t0 % seglen) * nseg + t0 // seglen, sub, stride=nseg)

        def gates(c, _):
            r0 = pl.multiple_of(c * q, q)
            xc = _conv_chunk(seq_ref, LRU_WIDTH, LRU_WIDTH, length, r0, csel_ref, cw_ref, cb_ref)
            t = jnp.tanh(_dot(xc.astype(BF16), wg_ref[0]) + bg_ref[0])
            hx = 0.5 * xc
            for d in range(2):
                c = nhc[:, d * 256:(d + 1) * 256]
                nla = c * t[:, d * 512:d * 512 + 256] + c
                a = jnp.exp2(nla * -LOG2_E)
                y = jnp.tanh(nla) * (a * a + 1.0)
                root = jnp.where(y > 0.0, y * lax.rsqrt(y), 0.0)
                bv = root * (hx * t[:, d * 512 + 256:(d + 1) * 512] + hx)
                for hf in range(2):
                    for j in range(q // sub):
                        rows = seg_rows(r0 + j * sub)
                        a_scr[2 * d + hf, rows, :] = a[j * sub:(j + 1) * sub, hf * LANES:(hf + 1) * LANES]
                        b_scr[2 * d + hf, rows, :] = bv[j * sub:(j + 1) * sub, hf * LANES:(hf + 1) * LANES]
            return 0

        lax.fori_loop(0, length // q, gates, 0, unroll=min(length // q, 4))

        def scan(k, carry):
            hs, ps = carry
            new_h, new_p = [], []
            for qd in range(4):
                steps = [4 * k + i if qd < 2 else seglen - 1 - (4 * k + i) for i in range(4)]
                idxs = [(qd, pl.ds(pl.multiple_of(st * nseg, nseg), nseg), slice(None)) for st in steps]
                a = [a_scr[ix] for ix in idxs]
                b = [b_scr[ix] for ix in idxs]
                h, p = hs[qd], ps[qd]
                a01, b01 = a[1] * a[0], a[1] * b[0] + b[1]
                a23, b23 = a[3] * a[2], a[3] * b[2] + b[3]
                a03, b03 = a23 * a01, a23 * b01 + b23
                h1 = a[0] * h + b[0]
                h2 = a01 * h + b01
                h3 = a[2] * h2 + b[2]
                h4 = a03 * h + b03
                p2 = p * a01
                for ix, hv, pv in zip(idxs, (h1, h2, h3, h4), (p * a[0], p2, p2 * a[2], p * a03)):
                    b_scr[ix] = hv
                    a_scr[ix] = pv
                new_h.append(h4)
                new_p.append(p * a03)
            return tuple(new_h), tuple(new_p)

        z8 = jnp.zeros((nseg, LANES), F32)
        o8 = jnp.ones((nseg, LANES), F32)
        hend, pend = lax.fori_loop(0, seglen // 4, scan, ((z8,) * 4, (o8,) * 4), unroll=1)

        h_out = []
        for qd in range(4):
            c = h0[qd]
            order = range(nseg) if qd < 2 else range(nseg - 1, -1, -1)
            for k in order:
                cin_scr[qd, k:k + 1, :] = c
                c = hend[qd][k:k + 1, :] + pend[qd][k:k + 1, :] * c
            h_out.append(c)

        def emit(c, _):
            r0 = pl.multiple_of(c * sub, sub)
            rows = seg_rows(r0)
            hq = [b_scr[qd, rows, :] + a_scr[qd, rows, :] * cin_scr[qd, pl.ds(r0 // seglen, 1), :] for qd in range(4)]
            hsum = jnp.concatenate([hq[0] + hq[2], hq[1] + hq[3]], axis=1)
            gate = seq_ref[pl.ds(r0, sub), 0:LRU_WIDTH].astype(F32)
            out_ref[pl.ds(r0, sub), :] = (hsum * _gelu_tanh(gate)).astype(out_ref.dtype)
            return 0

        lax.fori_loop(0, length // sub, emit, 0)
        return h_out

    z1 = jnp.zeros((1, LANES), F32)
    states = run(pc_ref, oc_ref, [z1] * 4)
    run(pl_ref, ol_ref, states)


def _lru_mixer(layer, p_lat, p_ctx, conv_sel, conv_w, conv_b, w_gates, b_gates, lam, batch):
    seq = p_lat.shape[0] // batch
    lc = p_ctx.shape[0] // batch
    const = lambda *shape: pl.BlockSpec(shape, lambda b: (0,) * len(shape))
    of_layer = lambda *shape: pl.BlockSpec((1,) + shape, lambda b: (layer,) + (0,) * len(shape))
    return pl.pallas_call(
        _lru_kernel,
        out_shape=(jax.ShapeDtypeStruct((batch * seq, LRU_WIDTH), BF16),
                   jax.ShapeDtypeStruct((batch * lc, LRU_WIDTH), BF16)),
        grid=(batch,),
        in_specs=[pl.BlockSpec((seq, 512), lambda b: (b, 0)),
                  pl.BlockSpec((lc, 512), lambda b: (b, 0)),
                  const(3, SSD_CONV * SSD_CHUNK, CONV_WIN),
                  of_layer(SSD_CONV, LRU_WIDTH), of_layer(1, LRU_WIDTH), of_layer(LRU_WIDTH, 1024), of_layer(1, 1024),
                  of_layer(1, 512)],
        out_specs=(pl.BlockSpec((seq, LRU_WIDTH), lambda b: (b, 0)),
                   pl.BlockSpec((lc, LRU_WIDTH), lambda b: (b, 0))),
        scratch_shapes=[pltpu.VMEM((4, seq, LANES), F32), pltpu.VMEM((4, seq, LANES), F32),
                        pltpu.VMEM((4, LRU_SEGMENTS, LANES), F32)],
        compiler_params=_cparams(("parallel",), 48),
        name="lru_mixer",
    )(p_lat, p_ctx, conv_sel, conv_w, conv_b, w_gates, b_gates, lam)


def _stack_heads(qblk):
    lo = _lane_iota((qblk.shape[0], LANES)) < HEAD_DIM
    zero = jnp.zeros((), qblk.dtype)
    s0, s1 = qblk[:, 0:LANES], qblk[:, LANES:2 * LANES]
    return jnp.concatenate([jnp.where(lo, s0, zero), jnp.where(lo, s1, zero),
                            jnp.where(lo, zero, s0), jnp.where(lo, zero, s1)], axis=0)


def _unstack_heads(acc, t):
    lo = _lane_iota((t, LANES)) < HEAD_DIM
    return jnp.concatenate([jnp.where(lo, acc[0:t], acc[2 * t:3 * t]),
                            jnp.where(lo, acc[t:2 * t], acc[3 * t:4 * t])], axis=1)


MAX_SHIFT_BOUND = 50.0


def _gattn_kernel(bound_ref, q_ref, kl_ref, vl_ref, kc_ref, vc_ref, o_ref, qs_scr, m_scr, acc_scr, *, layer, tk, online):
    tq = q_ref.shape[0]
    half = 2 * tq
    qs_scr[...] = _stack_heads(q_ref[...])
    acc_scr[...] = jnp.zeros(acc_scr.shape, F32)
    if online:
        m_scr[...] = jnp.full(m_scr.shape, -jnp.inf, F32)

    def step(k, v):
        lo = _lane_iota(v.shape) < HEAD_DIM
        one = jnp.ones((), v.dtype)
        s = _dot_nt(qs_scr[...], k)
        if online:
            m_old = m_scr[...]
            shift = jnp.maximum(m_old, jnp.max(s, axis=-1, keepdims=True))
            m_scr[...] = shift
            acc_scr[...] = acc_scr[...] * jnp.exp2(m_old - shift)
        else:
            shift = bound_ref[layer]
        p = jnp.exp2(s - shift).astype(BF16)
        acc_scr[0:half, :] += _dot(p[0:half], jnp.where(lo, v, one))
        acc_scr[half:, :] += _dot(p[half:], jnp.where(lo, one, v))

    def body(j, _):
        r0 = pl.multiple_of(j * tk, tk)
        step(kl_ref[pl.ds(r0, tk), :], vl_ref[pl.ds(r0, tk), :])
        return 0

    lax.fori_loop(0, kl_ref.shape[0] // tk, body, 0, unroll=True)
    step(kc_ref[...], vc_ref[...])
    acc = acc_scr[...]
    o_ref[...] = _unstack_heads(acc / pltpu.roll(acc, HEAD_DIM, 1), tq).astype(o_ref.dtype)


def _logit_bounds(q_gain, k_gain):
    return HEAD_DIM * Q_SCALE * 1.02 * jnp.max(jnp.abs(q_gain), axis=-1) * jnp.max(jnp.abs(k_gain), axis=-1)


def _global_attention(layer, bound, qkv_lat, qkv_ctx, batch, tq, tk):
    seq = qkv_lat.shape[0] // batch
    lc = qkv_ctx.shape[0] // batch
    nq = seq // tq

    def call(online):
        return pl.pallas_call(
            functools.partial(_gattn_kernel, layer=layer, tk=tk, online=online),
            out_shape=jax.ShapeDtypeStruct((batch * seq, 256), BF16),
            grid=(batch, nq),
            in_specs=[pl.BlockSpec(memory_space=pltpu.SMEM),
                      pl.BlockSpec((tq, 256), lambda b, i: (b * nq + i, 0)),
                      pl.BlockSpec((seq, LANES), lambda b, i: (b, 2)),
                      pl.BlockSpec((seq, LANES), lambda b, i: (b, 3)),
                      pl.BlockSpec((lc, LANES), lambda b, i: (b, 2)),
                      pl.BlockSpec((lc, LANES), lambda b, i: (b, 3))],
            out_specs=pl.BlockSpec((tq, 256), lambda b, i: (b * nq + i, 0)),
            scratch_shapes=[pltpu.VMEM((4 * tq, LANES), BF16), pltpu.VMEM((4 * tq, 1), F32),
                            pltpu.VMEM((4 * tq, LANES), F32)],
            compiler_params=_cparams(("parallel", "parallel"), 48),
            name="global_attention_online" if online else "global_attention",
        )(bound, qkv_lat, qkv_lat, qkv_lat, qkv_ctx, qkv_ctx)

    return lax.cond(bound[layer] <= MAX_SHIFT_BOUND, lambda: call(False), lambda: call(True))


def _sink_rows(sink_ref, layer, t):
    head_of_row = _row_iota((4 * t, 1)) // t
    out = jnp.zeros((4 * t, 1), F32)
    for h in range(Q_HEADS):
        out = jnp.where(head_of_row == h, sink_ref[layer, h] * LOG2_E, out)
    return out


def _wattn_kernel(bound_ref, sink_ref, q_ref, kl_ref, vl_ref, kc_ref, vc_ref, o_ref, bias_scr, *, layer, online):
    w = WINDOW
    seq = kl_ref.shape[0]
    band = 3 * w
    shift = 0.0 if online else bound_ref[layer]
    dmat = _lane_iota((w, band)) - _row_iota((w, band))
    for case in range(3):
        bias_scr[case] = jnp.where(jnp.abs(dmat - case * w) <= w, -shift, -jnp.inf)
    one = jnp.ones((), BF16)
    kc, vc = kc_ref[...], vc_ref[...]
    lo_c = _lane_iota(vc.shape) < HEAD_DIM
    vc_sum = (jnp.where(lo_c, vc, one), jnp.where(lo_c, one, vc))
    lo_b = _lane_iota((band, LANES)) < HEAD_DIM
    lo = _lane_iota((w, LANES)) < HEAD_DIM

    def body(n, _):
        r0 = pl.multiple_of(n * w, w)
        gq = pl.program_id(1) * q_ref.shape[0] + r0
        ks = pl.multiple_of(jnp.clip(gq - w, 0, seq - band), w)
        bias = bias_scr[(gq - ks) // w]
        kb, vb = kl_ref[pl.ds(ks, band), :], vl_ref[pl.ds(ks, band), :]
        vb_sum = (jnp.where(lo_b, vb, one), jnp.where(lo_b, one, vb))
        qs = _stack_heads(q_ref[pl.ds(r0, w), :])
        s_band = _dot_nt(qs, kb)
        s_ctx = _dot_nt(qs, kc)
        outs = []
        for g in range(2):
            pb, pc, sk = [], [], []
            for h in (2 * g, 2 * g + 1):
                sb = s_band[h * w:(h + 1) * w] + bias
                sc = s_ctx[h * w:(h + 1) * w] - shift
                sink = sink_ref[layer, h] * LOG2_E - shift
                if online:
                    m = jnp.maximum(jnp.maximum(jnp.max(sb, axis=-1, keepdims=True),
                                                jnp.max(sc, axis=-1, keepdims=True)), sink)
                    sb, sc, sink = sb - m, sc - m, sink - m
                pb.append(jnp.exp2(sb).astype(BF16))
                pc.append(jnp.exp2(sc).astype(BF16))
                sk.append(jnp.exp2(sink) if online else jnp.full((w, 1), jnp.exp2(sink), F32))
            acc = _dot(jnp.concatenate(pb, axis=0), vb_sum[g]) + _dot(jnp.concatenate(pc, axis=0), vc_sum[g])
            den = pltpu.roll(acc, HEAD_DIM, 1) + jnp.concatenate(sk, axis=0)
            outs.append(acc / den)
        o_ref[pl.ds(r0, w), :] = jnp.concatenate(
            [jnp.where(lo, outs[0][0:w], outs[1][0:w]), jnp.where(lo, outs[0][w:2 * w], outs[1][w:2 * w])],
            axis=1).astype(o_ref.dtype)
        return 0

    lax.fori_loop(0, q_ref.shape[0] // w, body, 0, unroll=4)


def _window_attention(layer, bound, sink, qkv_lat, qkv_ctx, batch, tq):
    seq = qkv_lat.shape[0] // batch
    lc = qkv_ctx.shape[0] // batch
    nq = seq // tq

    def call(online):
        return pl.pallas_call(
            functools.partial(_wattn_kernel, layer=layer, online=online),
            out_shape=jax.ShapeDtypeStruct((batch * seq, 256), BF16),
            grid=(batch, nq),
            in_specs=[pl.BlockSpec(memory_space=pltpu.SMEM), pl.BlockSpec(memory_space=pltpu.SMEM),
                      pl.BlockSpec((tq, 256), lambda b, i: (b * nq + i, 0)),
                      pl.BlockSpec((seq, LANES), lambda b, i: (b, 2)),
                      pl.BlockSpec((seq, LANES), lambda b, i: (b, 3)),
                      pl.BlockSpec((lc, LANES), lambda b, i: (b, 2)),
                      pl.BlockSpec((lc, LANES), lambda b, i: (b, 3))],
            out_specs=pl.BlockSpec((tq, 256), lambda b, i: (b * nq + i, 0)),
            scratch_shapes=[pltpu.VMEM((3, WINDOW, 3 * WINDOW), F32)],
            compiler_params=_cparams(("parallel", "parallel"), 32),
            name="window_attention_online" if online else "window_attention",
        )(bound, sink, qkv_lat, qkv_lat, qkv_lat, qkv_ctx, qkv_ctx)

    return lax.cond(bound[layer] <= MAX_SHIFT_BOUND, lambda: call(False), lambda: call(True))


def _cattn_kernel(sink_ref, c_ref, d_ref, oc_ref, od_ref, *, layer):
    t = c_ref.shape[0]
    for ref, out, has_sink in ((c_ref, oc_ref, False), (d_ref, od_ref, True)):
        qs = _stack_heads(ref[:, 0:256])
        s = _dot_nt(qs, ref[:, 256:384])
        m = jnp.max(s, axis=-1, keepdims=True)
        if has_sink:
            sink = _sink_rows(sink_ref, layer, t)
            m = jnp.maximum(m, sink)
        p = jnp.exp2(s - m)
        den = jnp.sum(p, axis=-1, keepdims=True)
        if has_sink:
            den = den + jnp.exp2(sink - m)
        acc = _dot(p.astype(BF16), ref[:, 384:512])
        out[...] = _unstack_heads(acc / den, t).astype(out.dtype)


def _context_attention(layer, sink, qkv_c, qkv_d, batch):
    lc = qkv_c.shape[0] // batch
    blk = lambda width: pl.BlockSpec((lc, width), lambda b: (b, 0))
    return pl.pallas_call(
        functools.partial(_cattn_kernel, layer=layer),
        out_shape=(jax.ShapeDtypeStruct((batch * lc, 256), BF16),) * 2,
        grid=(batch,),
        in_specs=[pl.BlockSpec(memory_space=pltpu.SMEM), blk(512), blk(512)],
        out_specs=(blk(256), blk(256)),
        compiler_params=_cparams(("parallel",), 32),
        name="context_attention",
    )(sink, qkv_c, qkv_d)


def _out_ffn_kernel(x_ref, ya_ref, yb_ref, yc_ref, yd_ref, mod_ref, wo_ref, g_ref, w1_ref, w2_ref, o_ref, *, hchunk):
    m = mod_ref[0]
    gw = GROUP_WIDTH
    mix = _dot(ya_ref[...], wo_ref[0, 0:gw, :])
    for i, y_ref in enumerate((yb_ref, yc_ref, yd_ref), start=1):
        mix = mix + _dot(y_ref[...], wo_ref[0, i * gw:(i + 1) * gw, :])
    x1 = x_ref[...] + m[2:3] * mix
    ms = jnp.mean(x1 * x1, axis=-1, keepdims=True)
    h = ((x1 * lax.rsqrt(ms + EPS) * g_ref[0]) * (1.0 + m[4:5]) + m[3:4]).astype(BF16)
    acc = None
    for c in range(w1_ref.shape[2] // hchunk):
        u = jnp.maximum(_dot(h, w1_ref[0, :, c * hchunk:(c + 1) * hchunk]), 0.0)
        t = _dot((u * u).astype(BF16), w2_ref[0, c * hchunk:(c + 1) * hchunk, :])
        acc = t if acc is None else acc + t
    o_ref[...] = x1 + m[5:6] * acc


def _out_ffn(layer, x2d, ys, mod, mod_row_of_tile, w_out, g, w1, w2, tm):
    rows, d = x2d.shape
    hidden = w1.shape[2]
    row_blk = lambda width: pl.BlockSpec((tm, width), lambda i: (i, 0))
    const = lambda *shape: pl.BlockSpec((1,) + shape, lambda i: (layer,) + (0,) * len(shape), pipeline_mode=pl.Buffered(1))
    return pl.pallas_call(
        functools.partial(_out_ffn_kernel, hchunk=1024),
        out_shape=jax.ShapeDtypeStruct((rows, d), F32),
        grid=(rows // tm,),
        in_specs=[row_blk(d), row_blk(256), row_blk(256), row_blk(256), row_blk(256),
                  pl.BlockSpec((1, N_MOD, d), lambda i: (mod_row_of_tile(i), 0, 0)),
                  const(d, d), pl.BlockSpec((1, 1, d), lambda i: (layer, 0, 0)), const(d, hidden), const(hidden, d)],
        out_specs=row_blk(d),
        compiler_params=_cparams(("parallel",), 56),
        name="out_ffn",
    )(x2d, *ys, mod, w_out, g, w1, w2)


def _permute_q_heads(cols):
    q = cols[..., 0:256].reshape(cols.shape[:-1] + (Q_HEADS, HEAD_DIM))
    q = q[..., Q_HEAD_ORDER, :].reshape(cols.shape[:-1] + (256,))
    return jnp.concatenate([q, cols[..., 256:512]], axis=-1)


def _relayout_w_in(w):
    ssd_in = GROUP_WIDTH + SSD_CONV_CH + 2 * SSD_HEADS
    b0, c0, d0 = ssd_in, ssd_in + 512, ssd_in + 1024
    dt = w[..., GROUP_WIDTH + SSD_CONV_CH:ssd_in]
    dt = jnp.pad(dt, [(0, 0)] * (w.ndim - 1) + [(0, LANES - 2 * SSD_HEADS)])
    return jnp.concatenate([w[..., 0:GROUP_WIDTH + SSD_CONV_CH], w[..., b0:c0], _permute_q_heads(w[..., c0:d0]),
                            _permute_q_heads(w[..., d0:d0 + 512]), dt], axis=-1).astype(BF16)


def _relayout_w_out(w):
    depth, _, d = w.shape
    attn = w[:, 512:1024].reshape(depth, 2, Q_HEADS, HEAD_DIM, d)[:, :, Q_HEAD_ORDER].reshape(depth, 512, d)
    return jnp.concatenate([w[:, 0:512], attn], axis=1).astype(BF16)


def _block_diag(w):
    eye = jnp.eye(LRU_BLOCKS, dtype=w.dtype)
    out = jnp.einsum('...kij,kn->...kinj', w, eye)
    return out.reshape(w.shape[:-3] + (LRU_WIDTH, LRU_WIDTH))


def _rope_tables(length):
    rows = length // GRID_W
    row = jnp.repeat(jnp.arange(rows, dtype=F32), GRID_W)
    col = jnp.tile(jnp.arange(GRID_W, dtype=F32), rows)
    inv = ROPE_THETA ** (-jnp.arange(0, ROPE_AXIS_DIM, 2, dtype=F32) / ROPE_AXIS_DIM)
    half = ROPE_AXIS_DIM // 2
    ang = jnp.concatenate([row[:, None] * inv] * 2 + [col[:, None] * inv] * 2, axis=1)
    sign = jnp.tile(jnp.concatenate([-jnp.ones((half,), F32), jnp.ones((half,), F32)]), 2)
    cos = jnp.tile(jnp.cos(ang), (1, LANES // HEAD_DIM))
    sin = jnp.tile(jnp.sin(ang) * sign, (1, LANES // HEAD_DIM))
    return cos, sin


def _head_selectors():
    sel = np.zeros((2, 2 * LANES, SSD_HEADS * LANES), np.float32)
    for d in range(2):
        for h in range(SSD_HEADS):
            for part in range(2):
                sel[d, part * LANES + SSD_HEADS * d + h, h * LANES:(h + 1) * LANES] = 1.0
    return jnp.asarray(sel, BF16)


def _head_mean_matrix():
    idx = np.arange(256) // HEAD_DIM
    return jnp.asarray((idx[:, None] == idx[None, :]).astype(np.float32) / HEAD_DIM, BF16)


def kernel(x, c, ctx, c_ctx, w_mod, b_mod, g_mix, w_in, ssd_conv_w, ssd_conv_b, ssd_a_log, ssd_dt_bias, ssd_d, ssd_norm_g, lru_conv_w, lru_conv_b, lru_lambda, lru_w_a, lru_b_a, lru_w_i, lru_b_i, gqa_q_norm, gqa_k_norm, swa_q_norm, swa_k_norm, swa_sink, w_out, g_ffn, w_ffn1, w_ffn2):
    batch, seq, d = x.shape
    lc = ctx.shape[1]
    depth = w_mod.shape[0]
    assert batch + 1 <= 8 and seq % 1024 == 0 and lc % 256 == 0 and seq % GRID_W == 0

    cvecs = jnp.concatenate([c, c_ctx[None, :], jnp.zeros((8 - batch - 1, d), F32)], axis=0)
    mod = _modulation(cvecs, w_mod, b_mod).reshape(depth * 8, N_MOD, d)

    rope = _rope_tables(seq)
    sel = _head_selectors()
    bd = _head_mean_matrix()
    conv_sel = _conv_selectors()
    w_in_r = _relayout_w_in(w_in)
    w_out_r = _relayout_w_out(w_out)
    w1, w2 = w_ffn1.astype(BF16), w_ffn2.astype(BF16)
    g1, g2 = g_mix.reshape(depth, 1, d), g_ffn.reshape(depth, 1, d)
    nrm = jnp.tile(jnp.stack([gqa_q_norm, gqa_k_norm, swa_q_norm, swa_k_norm], axis=1), (1, 1, 256 // HEAD_DIM))
    lane_pad = ((0, 0), (0, 0), (0, LANES - 2 * SSD_HEADS))
    head_params = jnp.concatenate([jnp.pad(ssd_dt_bias.reshape(depth, 1, -1), lane_pad),
                                   jnp.pad(ssd_a_log.reshape(depth, 1, -1), lane_pad),
                                   jnp.zeros((depth, 6, LANES), F32)], axis=1)
    ssd_cb = ssd_conv_b.reshape(depth, 1, -1)
    d_skip = jnp.repeat(ssd_d, HEAD_DIM, axis=-1).reshape(depth, 1, -1)
    ssd_ng = ssd_norm_g.reshape(depth, 1, -1)
    bd_a, bd_i = _block_diag(lru_w_a), _block_diag(lru_w_i)
    w_gates = (0.5 * jnp.concatenate([bd_a[:, 0], bd_i[:, 0], bd_a[:, 1], bd_i[:, 1]], axis=-1)).astype(BF16)
    b_gates = 0.5 * jnp.concatenate([lru_b_a[:, 0], lru_b_i[:, 0], lru_b_a[:, 1], lru_b_i[:, 1]], axis=-1).reshape(depth, 1, -1)
    lru_cb = lru_conv_b.reshape(depth, 1, -1)
    lam = lru_lambda.reshape(depth, 1, -1)
    bounds = _logit_bounds(gqa_q_norm, gqa_k_norm)
    swa_bounds = _logit_bounds(swa_q_norm, swa_k_norm)

    tm = 1024
    tiles_per_batch = seq // tm
    xl = x.reshape(batch * seq, d)
    xc = ctx.reshape(batch * lc, d)
    for l in range(depth):
        need_ctx = l < depth - 1
        lat_row = lambda i, l=l: l * 8 + i // tiles_per_batch
        ctx_row = lambda i, l=l: l * 8 + batch
        ssd_l, lru_l, qc_l, qd_l, dt_l = _in_projection(l, xl, mod, lat_row, g1, w_in_r, nrm, bd, rope, tm, tiles_per_batch)
        ssd_c, lru_c, qc_c, qd_c, dt_c = _in_projection(l, xc, mod, ctx_row, g1, w_in_r, nrm, bd, None, 256, 1)
        ya_l, ya_c = _ssd_mixer(l, ssd_l, ssd_c, dt_l, dt_c, conv_sel, ssd_conv_w, ssd_cb, head_params, sel, d_skip,
                                ssd_ng, batch)
        yb_l, yb_c = _lru_mixer(l, lru_l, lru_c, conv_sel, lru_conv_w, lru_cb, w_gates, b_gates, lam, batch)
        yc_l = _global_attention(l, bounds, qc_l, qc_c, batch, 512, 512)
        yd_l = _window_attention(l, swa_bounds, swa_sink, qd_l, qd_c, batch, seq)
        xl = _out_ffn(l, xl, (ya_l, yb_l, yc_l, yd_l), mod, lambda i, l=l: l * 8 + i // (seq // 512), w_out_r, g2, w1, w2, 512)
        if need_ctx:
            yc_c, yd_c = _context_attention(l, swa_sink, qc_c, qd_c, batch)
            xc = _out_ffn(l, xc, (ya_c, yb_c, yc_c, yd_c), mod, ctx_row, w_out_r, g2, w1, w2, 512)
    return xl.reshape(batch, seq, d)
```

```python
import functools
import math

import numpy as np
import jax
import jax.numpy as jnp
from jax import lax
from jax.experimental import pallas as pl
from jax.experimental.pallas import tpu as pltpu

F32 = jnp.float32
BF16 = jnp.bfloat16

EPS = 1e-6
GRID_W = 64
N_MOD = 6
GROUP_WIDTH = 256
SSD_HEADS = 4
SSD_STATE = 128
SSD_CONV = 4
SSD_CHUNK = 128
SSD_CONV_CH = 768
LRU_WIDTH = 256
LRU_BLOCKS = 4
LRU_C = 8.0
HEAD_DIM = 64
Q_HEADS = 4
WINDOW = 128
ROPE_THETA = 10000.0
ROPE_AXIS_DIM = 32
LOG2_E = math.log2(math.e)
Q_SCALE = HEAD_DIM ** -0.5 * LOG2_E

LANES = 128
BF16_ROWS = 16

COL_SSD, COL_LRU, COL_C, COL_D, COL_DT, COL_END = 0, 1024, 1536, 2048, 2560, 2688
Q_HEAD_ORDER = (0, 2, 1, 3)

CONV_WIN = SSD_CHUNK + 2 * BF16_ROWS

IN_PROJ_ROWS = 1024
FFN_ROWS = 512
GATTN_Q_ROWS = 512
GATTN_KV_ROWS = 512


def _cparams(sem, vmem_mib):
    return pltpu.CompilerParams(dimension_semantics=sem, vmem_limit_bytes=vmem_mib * 1024 * 1024)


def _dot(a, b):
    return jnp.dot(a, b, preferred_element_type=F32)


def _dot_nt(a, b):
    return lax.dot_general(a, b, (((1,), (1,)), ((), ())), preferred_element_type=F32)


def _split_bf16(a, parts):
    out = []
    for _ in range(parts - 1):
        hi = a.astype(BF16)
        out.append(hi)
        a = a - hi.astype(F32)
    out.append(a.astype(BF16))
    return out


def _silu(x):
    h = 0.5 * x
    return h + h * jnp.tanh(h)


def _softplus(x):
    return jnp.maximum(x, 0.0) + jnp.log1p(jnp.exp(-jnp.abs(x)))


def _gelu_tanh(x):
    return 0.5 * x * (1.0 + jnp.tanh(math.sqrt(2.0 / math.pi) * (x + 0.044715 * (x * x * x))))


def _lane_iota(shape):
    return lax.broadcasted_iota(jnp.int32, shape, len(shape) - 1)


def _row_iota(shape):
    return lax.broadcasted_iota(jnp.int32, shape, len(shape) - 2)


def _mod_kernel(cv_ref, w_ref, b_ref, o_ref):
    cv = cv_ref[...]
    s = _silu(cv)
    w = w_ref[0]
    s_hi, s_lo = _split_bf16(s, 2)
    w_hi, w_lo = _split_bf16(w, 2)
    acc = _dot(s_hi, w_hi) + _dot(s_lo, w_hi) + _dot(s_hi, w_lo)
    o_ref[0] = acc + b_ref[0]


def _modulation(cvecs, w_mod, b_mod):
    depth, d, n = w_mod.shape
    tn = 1024
    return pl.pallas_call(
        _mod_kernel,
        out_shape=jax.ShapeDtypeStruct((depth, 8, n), F32),
        grid=(depth, n // tn),
        in_specs=[pl.BlockSpec((8, d), lambda l, j: (0, 0)),
                  pl.BlockSpec((1, d, tn), lambda l, j: (l, 0, j)),
                  pl.BlockSpec((1, 1, tn), lambda l, j: (l, 0, j))],
        out_specs=pl.BlockSpec((1, 8, tn), lambda l, j: (l, 0, j)),
        compiler_params=_cparams(("parallel", "parallel"), 32),
        name="modulation",
    )(cvecs, w_mod, b_mod.reshape(depth, 1, n))


def _head_norm_rope(p, g, bd, cos, sin, scale):
    ms = _dot((p * p).astype(BF16), bd)
    y = p * lax.rsqrt(ms + EPS) * g
    if cos is not None:
        w = y.shape[-1]
        first = (_lane_iota(y.shape) % ROPE_AXIS_DIM) < (ROPE_AXIS_DIM // 2)
        partner = jnp.where(first, pltpu.roll(y, w - ROPE_AXIS_DIM // 2, 1), pltpu.roll(y, ROPE_AXIS_DIM // 2, 1))
        y = y * cos + partner * sin
    if scale != 1.0:
        y = y * scale
    return y


def _inproj_kernel(*refs, use_rope):
    if use_rope:
        x_ref, mod_ref, g_ref, w_ref, nrm_ref, bd_ref, cos_ref, sin_ref, ssd_ref, lru_ref, c_ref, d_ref, dt_ref = refs
    else:
        x_ref, mod_ref, g_ref, w_ref, nrm_ref, bd_ref, ssd_ref, lru_ref, c_ref, d_ref, dt_ref = refs
    x = x_ref[...]
    ms = jnp.mean(x * x, axis=-1, keepdims=True)
    m = mod_ref[0]
    h = (x * lax.rsqrt(ms + EPS) * g_ref[0]) * (1.0 + m[1:2]) + m[0:1]
    hb = h.astype(BF16)
    bd = bd_ref[...]
    if use_rope:
        cos1, sin1 = cos_ref[...], sin_ref[...]
        cos2 = jnp.concatenate([cos1, cos1], axis=1)
        sin2 = jnp.concatenate([sin1, sin1], axis=1)
    else:
        cos1 = sin1 = cos2 = sin2 = None
    for o_ref, c0, row in ((c_ref, COL_C, 0), (d_ref, COL_D, 2)):
        p = _dot(hb, w_ref[0, :, c0:c0 + 512])
        q = _head_norm_rope(p[:, 0:256], nrm_ref[0, row:row + 1, :], bd, cos2, sin2, Q_SCALE)
        k = _head_norm_rope(p[:, 256:384], nrm_ref[0, row + 1:row + 2, 0:128], bd[0:128, 0:128], cos1, sin1, 1.0)
        o_ref[:, 0:256] = q.astype(BF16)
        o_ref[:, 256:384] = k.astype(BF16)
        o_ref[:, 384:512] = p[:, 384:512].astype(BF16)
    ssd_ref[...] = _dot(hb, w_ref[0, :, COL_SSD:COL_LRU]).astype(BF16)
    lru_ref[...] = _dot(hb, w_ref[0, :, COL_LRU:COL_C]).astype(BF16)
    dt_ref[...] = _dot(hb, w_ref[0, :, COL_DT:COL_END])


def _in_projection(layer, x2d, mod, mod_row_of_tile, g, w, nrm, bd, rope, tm, rope_tiles):
    rows, d = x2d.shape
    use_rope = rope is not None
    in_specs = [pl.BlockSpec((tm, d), lambda i: (i, 0)),
                pl.BlockSpec((1, N_MOD, d), lambda i: (mod_row_of_tile(i), 0, 0)),
                pl.BlockSpec((1, 1, d), lambda i: (layer, 0, 0)),
                pl.BlockSpec((1, d, COL_END), lambda i: (layer, 0, 0), pipeline_mode=pl.Buffered(1)),
                pl.BlockSpec((1, 4, 256), lambda i: (layer, 0, 0)),
                pl.BlockSpec((256, 256), lambda i: (0, 0))]
    args = [x2d, mod, g, w, nrm, bd]
    if use_rope:
        in_specs += [pl.BlockSpec((tm, LANES), lambda i: (i % rope_tiles, 0))] * 2
        args += list(rope)
    widths = (1024, 512, 512, 512, LANES)
    dtypes = (BF16, BF16, BF16, BF16, F32)
    return pl.pallas_call(
        functools.partial(_inproj_kernel, use_rope=use_rope),
        out_shape=tuple(jax.ShapeDtypeStruct((rows, wd), dt) for wd, dt in zip(widths, dtypes)),
        grid=(rows // tm,),
        in_specs=in_specs,
        out_specs=tuple(pl.BlockSpec((tm, wd), lambda i: (i, 0)) for wd in widths),
        compiler_params=_cparams(("parallel",), 56),
        name="in_projection",
    )(*args)


def _conv_selectors():
    row = np.arange(SSD_CONV * SSD_CHUNK)
    want = (row % SSD_CHUNK) + (row // SSD_CHUNK) - SSD_CONV // 2
    col = np.arange(CONV_WIN)
    sel = np.stack([(col[None, :] == want[:, None] + BF16_ROWS * case) for case in range(3)])
    return jnp.asarray(sel.astype(np.float32), BF16)


def _conv_chunk(seq_ref, col0, width, length, r0, sel_ref, w_ref, b_ref):
    ws = jnp.clip(r0 - BF16_ROWS, 0, length - CONV_WIN)
    ws = pl.multiple_of(ws, BF16_ROWS)
    xw = seq_ref[pl.ds(ws, CONV_WIN), col0:col0 + width]
    sh = _dot(sel_ref[(r0 - ws) // BF16_ROWS], xw)
    w = w_ref[0]
    acc = b_ref[0] + sh[0:SSD_CHUNK] * w[0:1]
    for k in range(1, SSD_CONV):
        acc = acc + sh[k * SSD_CHUNK:(k + 1) * SSD_CHUNK] * w[k:k + 1]
    return acc


def _ssd_kernel(pl_ref, pc_ref, dtl_ref, dtc_ref, csel_ref, cw_ref, cb_ref, hp_ref, sel_ref, dsk_ref, ng_ref,
                yl_ref, yc_ref, accl_ref, accc_ref, xbcl_ref, xbcc_ref):
    q = SSD_CHUNK
    lane8 = _lane_iota((1, LANES)) < 2 * SSD_HEADS
    dt_bias = hp_ref[0, 0:1, :]
    a_row = jnp.where(lane8, -jnp.exp(hp_ref[0, 1:2, :]) * LOG2_E, 0.0)
    ii = _row_iota((q, q))
    jj = _lane_iota((q, q))
    lo = _lane_iota((q, LANES)) < HEAD_DIM
    masks = (jj <= ii, jj >= ii)
    tris = [jnp.where(m, 1.0, 0.0).astype(BF16) for m in masks]
    tri_pairs = [jnp.concatenate([t, t], axis=1) for t in tris]
    zblk = jnp.zeros((SSD_STATE, LANES), BF16)

    def block_diag(a0, a1):
        return jnp.concatenate([jnp.concatenate([a0, zblk], axis=1), jnp.concatenate([zblk, a1], axis=1)], axis=0)

    def chunk(seq_ref, xbc_ref, dt_ref, length, r0, d, state):
        if d == 0:
            conv = _conv_chunk(seq_ref, GROUP_WIDTH, SSD_CONV_CH, length, r0, csel_ref, cw_ref, cb_ref)
            xbc_ref[pl.ds(r0, q), :] = _silu(conv).astype(BF16)
        xs = xbc_ref[pl.ds(r0, q), 0:256].astype(F32)
        bm = xbc_ref[pl.ds(r0, q), 256:512].astype(F32)
        cmb = xbc_ref[pl.ds(r0, q), 512:768]
        dtv = _softplus(dt_ref[pl.ds(r0, q), :] + dt_bias)
        da = dtv * a_row
        mask = masks[d]
        acum = _dot(tri_pairs[d], jnp.concatenate(_split_bf16(da, 2), axis=0))
        stacked = jnp.concatenate([dtv, acum], axis=0)
        ex = _dot(jnp.concatenate(_split_bf16(stacked, 2), axis=1), sel_ref[d])
        dtb, acb = ex[0:q], ex[q:2 * q]
        act = acum.T
        edge = q - 1 if d == 0 else 0
        per_head = []
        for h in range(SSD_HEADS):
            ac = acb[:, h * LANES:(h + 1) * LANES]
            ar = act[SSD_HEADS * d + h:SSD_HEADS * d + h + 1, :]
            lmat = jnp.where(mask, jnp.exp2(ac - ar), 0.0)
            alast = ac[edge:edge + 1, :]
            dth = dtb[:, h * LANES:(h + 1) * LANES]
            per_head.append((lmat, dth, jnp.exp2(alast - ac) * dth, jnp.exp2(ac), jnp.exp2(alast)))
        bts = [bm[:, g * LANES:(g + 1) * LANES].T.astype(BF16) for g in range(2)]
        cb_all = _dot(cmb, block_diag(bts[0], bts[1]))
        yo_all = _dot(cmb, block_diag(state[0].astype(BF16), state[1].astype(BF16)))
        ys, new_state = [], []
        for g in range(2):
            (l0, dt0, we0, ei0, cd0), (l1, dt1, we1, ei1, cd1) = per_head[2 * g], per_head[2 * g + 1]
            xg = xs[:, g * LANES:(g + 1) * LANES]
            cb = cb_all[:, g * LANES:(g + 1) * LANES]
            xdt = xg * jnp.where(lo, dt0, dt1)
            scores = jnp.concatenate([(cb * l0).astype(BF16), (cb * l1).astype(BF16)], axis=1)
            xpair = jnp.concatenate([jnp.where(lo, xdt, 0.0), jnp.where(lo, 0.0, xdt)], axis=0).astype(BF16)
            y = _dot(scores, xpair) + yo_all[:, g * LANES:(g + 1) * LANES] * jnp.where(lo, ei0, ei1)
            s_new = _dot(bts[g], (xg * jnp.where(lo, we0, we1)).astype(BF16))
            new_state.append(state[g] * jnp.where(lo, cd0, cd1) + s_new)
            ys.append(y)
        return jnp.concatenate(ys, axis=1), xs, tuple(new_state)

    def sweep(seq_ref, xbc_ref, dt_ref, acc_ref, out_ref, d, state):
        length = seq_ref.shape[0]
        nc = length // q

        def body(i, st):
            c = i if d == 0 else nc - 1 - i
            r0 = pl.multiple_of(c * q, q)
            y, xs, st = chunk(seq_ref, xbc_ref, dt_ref, length, r0, d, st)
            if d == 0:
                acc_ref[pl.ds(r0, q), :] = y
            else:
                y = acc_ref[pl.ds(r0, q), :] + y + dsk_ref[0] * xs
                z = seq_ref[pl.ds(r0, q), 0:GROUP_WIDTH].astype(F32)
                t = y * _silu(z)
                ms = jnp.mean(t * t, axis=-1, keepdims=True)
                out_ref[pl.ds(r0, q), :] = (t * lax.rsqrt(ms + EPS) * ng_ref[0]).astype(out_ref.dtype)
            return st

        return lax.fori_loop(0, nc, body, state, unroll=min(nc, 8))

    zero = (jnp.zeros((SSD_STATE, LANES), F32), jnp.zeros((SSD_STATE, LANES), F32))
    for d in range(2):
        st = sweep(pc_ref, xbcc_ref, dtc_ref, accc_ref, yc_ref, d, zero)
        sweep(pl_ref, xbcl_ref, dtl_ref, accl_ref, yl_ref, d, st)


def _ssd_mixer(layer, p_lat, p_ctx, dt_lat, dt_ctx, conv_sel, conv_w, conv_b, head_params, sel, d_skip, norm_g, batch):
    seq = p_lat.shape[0] // batch
    lc = p_ctx.shape[0] // batch
    const = lambda *shape: pl.BlockSpec(shape, lambda b: (0,) * len(shape))
    of_layer = lambda *shape: pl.BlockSpec((1,) + shape, lambda b: (layer,) + (0,) * len(shape))
    return pl.pallas_call(
        _ssd_kernel,
        out_shape=(jax.ShapeDtypeStruct((batch * seq, GROUP_WIDTH), BF16),
                   jax.ShapeDtypeStruct((batch * lc, GROUP_WIDTH), BF16)),
        grid=(batch,),
        in_specs=[pl.BlockSpec((seq, 1024), lambda b: (b, 0)),
                  pl.BlockSpec((lc, 1024), lambda b: (b, 0)),
                  pl.BlockSpec((seq, LANES), lambda b: (b, 0)),
                  pl.BlockSpec((lc, LANES), lambda b: (b, 0)),
                  const(3, SSD_CONV * SSD_CHUNK, CONV_WIN),
                  of_layer(SSD_CONV, SSD_CONV_CH), of_layer(1, SSD_CONV_CH), of_layer(8, LANES),
                  const(2, 2 * LANES, SSD_HEADS * LANES), of_layer(1, GROUP_WIDTH), of_layer(1, GROUP_WIDTH)],
        out_specs=(pl.BlockSpec((seq, GROUP_WIDTH), lambda b: (b, 0)),
                   pl.BlockSpec((lc, GROUP_WIDTH), lambda b: (b, 0))),
        scratch_shapes=[pltpu.VMEM((seq, GROUP_WIDTH), F32), pltpu.VMEM((lc, GROUP_WIDTH), F32),
                        pltpu.VMEM((seq, SSD_CONV_CH), BF16), pltpu.VMEM((lc, SSD_CONV_CH), BF16)],
        compiler_params=_cparams(("parallel",), 48),
        name="ssd_mixer",
    )(p_lat, p_ctx, dt_lat, dt_ctx, conv_sel, conv_w, conv_b, head_params, sel, d_skip, norm_g)


LRU_SEGMENTS = 8


def _lru_kernel(pl_ref, pc_ref, csel_ref, cw_ref, cb_ref, wg_ref, bg_ref, lam_ref, ol_ref, oc_ref, a_scr, b_scr, cin_scr):
    q = SSD_CHUNK
    nseg = LRU_SEGMENTS
    nhc = (0.5 * LRU_C) * _softplus(-lam_ref[0])

    def run(seq_ref, out_ref, h0):
        length = seq_ref.shape[0]
        seglen = length // nseg
        sub = min(q, seglen)

        def seg_rows(t0):
            return pl.ds((t0 % seglen) * nseg + t0 // seglen, sub, stride=nseg)

        def gates(c, _):
            r0 = pl.multiple_of(c * q, q)
            xc = _conv_chunk(seq_ref, LRU_WIDTH, LRU_WIDTH, length, r0, csel_ref, cw_ref, cb_ref)
            t = jnp.tanh(_dot(xc.astype(BF16), wg_ref[0]) + bg_ref[0])
            hx = 0.5 * xc
            for d in range(2):
                c = nhc[:, d * 256:(d + 1) * 256]
                nla = c * t[:, d * 512:d * 512 + 256] + c
                a = jnp.exp2(nla * -LOG2_E)
                y = jnp.tanh(nla) * (a * a + 1.0)
                root = jnp.where(y > 0.0, y * lax.rsqrt(y), 0.0)
                bv = root * (hx * t[:, d * 512 + 256:(d + 1) * 512] + hx)
                for hf in range(2):
                    for j in range(q // sub):
                        rows = seg_rows(r0 + j * sub)
                        a_scr[2 * d + hf, rows, :] = a[j * sub:(j + 1) * sub, hf * LANES:(hf + 1) * LANES]
                        b_scr[2 * d + hf, rows, :] = bv[j * sub:(j + 1) * sub, hf * LANES:(hf + 1) * LANES]
            return 0

        lax.fori_loop(0, length // q, gates, 0, unroll=min(length // q, 4))

        def scan(k, carry):
            hs, ps = carry
            new_h, new_p = [], []
            for qd in range(4):
                steps = [4 * k + i if qd < 2 else seglen - 1 - (4 * k + i) for i in range(4)]
                idxs = [(qd, pl.ds(pl.multiple_of(st * nseg, nseg), nseg), slice(None)) for st in steps]
                a = [a_scr[ix] for ix in idxs]
                b = [b_scr[ix] for ix in idxs]
                h, p = hs[qd], ps[qd]
                a01, b01 = a[1] * a[0], a[1] * b[0] + b[1]
                a23, b23 = a[3] * a[2], a[3] * b[2] + b[3]
                a03, b03 = a23 * a01, a23 * b01 + b23
                h1 = a[0] * h + b[0]
                h2 = a01 * h + b01
                h3 = a[2] * h2 + b[2]
                h4 = a03 * h + b03
                p2 = p * a01
                for ix, hv, pv in zip(idxs, (h1, h2, h3, h4), (p * a[0], p2, p2 * a[2], p * a03)):
                    b_scr[ix] = hv
                    a_scr[ix] = pv
                new_h.append(h4)
                new_p.append(p * a03)
            return tuple(new_h), tuple(new_p)

        z8 = jnp.zeros((nseg, LANES), F32)
        o8 = jnp.ones((nseg, LANES), F32)
        hend, pend = lax.fori_loop(0, seglen // 4, scan, ((z8,) * 4, (o8,) * 4), unroll=1)

        h_out = []
        for qd in range(4):
            c = h0[qd]
            order = range(nseg) if qd < 2 else range(nseg - 1, -1, -1)
            for k in order:
                cin_scr[qd, k:k + 1, :] = c
                c = hend[qd][k:k + 1, :] + pend[qd][k:k + 1, :] * c
            h_out.append(c)

        def emit(c, _):
            r0 = pl.multiple_of(c * sub, sub)
            rows = seg_rows(r0)
            hq = [b_scr[qd, rows, :] + a_scr[qd, rows, :] * cin_scr[qd, pl.ds(r0 // seglen, 1), :] for qd in range(4)]
            hsum = jnp.concatenate([hq[0] + hq[2], hq[1] + hq[3]], axis=1)
            gate = seq_ref[pl.ds(r0, sub), 0:LRU_WIDTH].astype(F32)
            out_ref[pl.ds(r0, sub), :] = (hsum * _gelu_tanh(gate)).astype(out_ref.dtype)
            return 0

        lax.fori_loop(0, length // sub, emit, 0)
        return h_out

    z1 = jnp.zeros((1, LANES), F32)
    states = run(pc_ref, oc_ref, [z1] * 4)
    run(pl_ref, ol_ref, states)


def _lru_mixer(layer, p_lat, p_ctx, conv_sel, conv_w, conv_b, w_gates, b_gates, lam, batch):
    seq = p_lat.shape[0] // batch
    lc = p_ctx.shape[0] // batch
    const = lambda *shape: pl.BlockSpec(shape, lambda b: (0,) * len(shape))
    of_layer = lambda *shape: pl.BlockSpec((1,) + shape, lambda b: (layer,) + (0,) * len(shape))
    return pl.pallas_call(
        _lru_kernel,
        out_shape=(jax.ShapeDtypeStruct((batch * seq, LRU_WIDTH), BF16),
                   jax.ShapeDtypeStruct((batch * lc, LRU_WIDTH), BF16)),
        grid=(batch,),
        in_specs=[pl.BlockSpec((seq, 512), lambda b: (b, 0)),
                  pl.BlockSpec((lc, 512), lambda b: (b, 0)),
                  const(3, SSD_CONV * SSD_CHUNK, CONV_WIN),
                  of_layer(SSD_CONV, LRU_WIDTH), of_layer(1, LRU_WIDTH), of_layer(LRU_WIDTH, 1024), of_layer(1, 1024),
                  of_layer(1, 512)],
        out_specs=(pl.BlockSpec((seq, LRU_WIDTH), lambda b: (b, 0)),
                   pl.BlockSpec((lc, LRU_WIDTH), lambda b: (b, 0))),
        scratch_shapes=[pltpu.VMEM((4, seq, LANES), F32), pltpu.VMEM((4, seq, LANES), F32),
                        pltpu.VMEM((4, LRU_SEGMENTS, LANES), F32)],
        compiler_params=_cparams(("parallel",), 48),
        name="lru_mixer",
    )(p_lat, p_ctx, conv_sel, conv_w, conv_b, w_gates, b_gates, lam)


def _stack_heads(qblk):
    lo = _lane_iota((qblk.shape[0], LANES)) < HEAD_DIM
    zero = jnp.zeros((), qblk.dtype)
    s0, s1 = qblk[:, 0:LANES], qblk[:, LANES:2 * LANES]
    return jnp.concatenate([jnp.where(lo, s0, zero), jnp.where(lo, s1, zero),
                            jnp.where(lo, zero, s0), jnp.where(lo, zero, s1)], axis=0)


def _unstack_heads(acc, t):
    lo = _lane_iota((t, LANES)) < HEAD_DIM
    return jnp.concatenate([jnp.where(lo, acc[0:t], acc[2 * t:3 * t]),
                            jnp.where(lo, acc[t:2 * t], acc[3 * t:4 * t])], axis=1)


MAX_SHIFT_BOUND = 50.0


def _gattn_kernel(bound_ref, q_ref, kl_ref, vl_ref, kc_ref, vc_ref, o_ref, qs_scr, m_scr, acc_scr, *, layer, tk, online):
    tq = q_ref.shape[0]
    half = 2 * tq
    qs_scr[...] = _stack_heads(q_ref[...])
    acc_scr[...] = jnp.zeros(acc_scr.shape, F32)
    if online:
        m_scr[...] = jnp.full(m_scr.shape, -jnp.inf, F32)

    def step(k, v):
        lo = _lane_iota(v.shape) < HEAD_DIM
        one = jnp.ones((), v.dtype)
        s = _dot_nt(qs_scr[...], k)
        if online:
            m_old = m_scr[...]
            shift = jnp.maximum(m_old, jnp.max(s, axis=-1, keepdims=True))
            m_scr[...] = shift
            acc_scr[...] = acc_scr[...] * jnp.exp2(m_old - shift)
        else:
            shift = bound_ref[layer]
        p = jnp.exp2(s - shift).astype(BF16)
        acc_scr[0:half, :] += _dot(p[0:half], jnp.where(lo, v, one))
        acc_scr[half:, :] += _dot(p[half:], jnp.where(lo, one, v))

    def body(j, _):
        r0 = pl.multiple_of(j * tk, tk)
        step(kl_ref[pl.ds(r0, tk), :], vl_ref[pl.ds(r0, tk), :])
        return 0

    lax.fori_loop(0, kl_ref.shape[0] // tk, body, 0, unroll=True)
    step(kc_ref[...], vc_ref[...])
    acc = acc_scr[...]
    o_ref[...] = _unstack_heads(acc / pltpu.roll(acc, HEAD_DIM, 1), tq).astype(o_ref.dtype)


def _logit_bounds(q_gain, k_gain):
    return HEAD_DIM * Q_SCALE * 1.02 * jnp.max(jnp.abs(q_gain), axis=-1) * jnp.max(jnp.abs(k_gain), axis=-1)


def _global_attention(layer, bound, qkv_lat, qkv_ctx, batch, tq, tk):
    seq = qkv_lat.shape[0] // batch
    lc = qkv_ctx.shape[0] // batch
    nq = seq // tq

    def call(online):
        return pl.pallas_call(
            functools.partial(_gattn_kernel, layer=layer, tk=tk, online=online),
            out_shape=jax.ShapeDtypeStruct((batch * seq, 256), BF16),
            grid=(batch, nq),
            in_specs=[pl.BlockSpec(memory_space=pltpu.SMEM),
                      pl.BlockSpec((tq, 256), lambda b, i: (b * nq + i, 0)),
                      pl.BlockSpec((seq, LANES), lambda b, i: (b, 2)),
                      pl.BlockSpec((seq, LANES), lambda b, i: (b, 3)),
                      pl.BlockSpec((lc, LANES), lambda b, i: (b, 2)),
                      pl.BlockSpec((lc, LANES), lambda b, i: (b, 3))],
            out_specs=pl.BlockSpec((tq, 256), lambda b, i: (b * nq + i, 0)),
            scratch_shapes=[pltpu.VMEM((4 * tq, LANES), BF16), pltpu.VMEM((4 * tq, 1), F32),
                            pltpu.VMEM((4 * tq, LANES), F32)],
            compiler_params=_cparams(("parallel", "parallel"), 48),
            name="global_attention_online" if online else "global_attention",
        )(bound, qkv_lat, qkv_lat, qkv_lat, qkv_ctx, qkv_ctx)

    return lax.cond(bound[layer] <= MAX_SHIFT_BOUND, lambda: call(False), lambda: call(True))


def _sink_rows(sink_ref, layer, t):
    head_of_row = _row_iota((4 * t, 1)) // t
    out = jnp.zeros((4 * t, 1), F32)
    for h in range(Q_HEADS):
        out = jnp.where(head_of_row == h, sink_ref[layer, h] * LOG2_E, out)
    return out


def _wattn_kernel(bound_ref, sink_ref, q_ref, kl_ref, vl_ref, kc_ref, vc_ref, o_ref, bias_scr, *, layer, online):
    w = WINDOW
    seq = kl_ref.shape[0]
    band = 3 * w
    shift = 0.0 if online else bound_ref[layer]
    dmat = _lane_iota((w, band)) - _row_iota((w, band))
    for case in range(3):
        bias_scr[case] = jnp.where(jnp.abs(dmat - case * w) <= w, -shift, -jnp.inf)
    one = jnp.ones((), BF16)
    kc, vc = kc_ref[...], vc_ref[...]
    lo_c = _lane_iota(vc.shape) < HEAD_DIM
    vc_sum = (jnp.where(lo_c, vc, one), jnp.where(lo_c, one, vc))
    lo_b = _lane_iota((band, LANES)) < HEAD_DIM
    lo = _lane_iota((w, LANES)) < HEAD_DIM

    def body(n, _):
        r0 = pl.multiple_of(n * w, w)
        gq = pl.program_id(1) * q_ref.shape[0] + r0
        ks = pl.multiple_of(jnp.clip(gq - w, 0, seq - band), w)
        bias = bias_scr[(gq - ks) // w]
        kb, vb = kl_ref[pl.ds(ks, band), :], vl_ref[pl.ds(ks, band), :]
        vb_sum = (jnp.where(lo_b, vb, one), jnp.where(lo_b, one, vb))
        qs = _stack_heads(q_ref[pl.ds(r0, w), :])
        s_band = _dot_nt(qs, kb)
        s_ctx = _dot_nt(qs, kc)
        outs = []
        for g in range(2):
            pb, pc, sk = [], [], []
            for h in (2 * g, 2 * g + 1):
                sb = s_band[h * w:(h + 1) * w] + bias
                sc = s_ctx[h * w:(h + 1) * w] - shift
                sink = sink_ref[layer, h] * LOG2_E - shift
                if online:
                    m = jnp.maximum(jnp.maximum(jnp.max(sb, axis=-1, keepdims=True),
                                                jnp.max(sc, axis=-1, keepdims=True)), sink)
                    sb, sc, sink = sb - m, sc - m, sink - m
                pb.append(jnp.exp2(sb).astype(BF16))
                pc.append(jnp.exp2(sc).astype(BF16))
                sk.append(jnp.exp2(sink) if online else jnp.full((w, 1), jnp.exp2(sink), F32))
            acc = _dot(jnp.concatenate(pb, axis=0), vb_sum[g]) + _dot(jnp.concatenate(pc, axis=0), vc_sum[g])
            den = pltpu.roll(acc, HEAD_DIM, 1) + jnp.concatenate(sk, axis=0)
            outs.append(acc / den)
        o_ref[pl.ds(r0, w), :] = jnp.concatenate(
            [jnp.where(lo, outs[0][0:w], outs[1][0:w]), jnp.where(lo, outs[0][w:2 * w], outs[1][w:2 * w])],
            axis=1).astype(o_ref.dtype)
        return 0

    lax.fori_loop(0, q_ref.shape[0] // w, body, 0, unroll=4)


def _window_attention(layer, bound, sink, qkv_lat, qkv_ctx, batch, tq):
    seq = qkv_lat.shape[0] // batch
    lc = qkv_ctx.shape[0] // batch
    nq = seq // tq

    def call(online):
        return pl.pallas_call(
            functools.partial(_wattn_kernel, layer=layer, online=online),
            out_shape=jax.ShapeDtypeStruct((batch * seq, 256), BF16),
            grid=(batch, nq),
            in_specs=[pl.BlockSpec(memory_space=pltpu.SMEM), pl.BlockSpec(memory_space=pltpu.SMEM),
                      pl.BlockSpec((tq, 256), lambda b, i: (b * nq + i, 0)),
                      pl.BlockSpec((seq, LANES), lambda b, i: (b, 2)),
                      pl.BlockSpec((seq, LANES), lambda b, i: (b, 3)),
                      pl.BlockSpec((lc, LANES), lambda b, i: (b, 2)),
                      pl.BlockSpec((lc, LANES), lambda b, i: (b, 3))],
            out_specs=pl.BlockSpec((tq, 256), lambda b, i: (b * nq + i, 0)),
            scratch_shapes=[pltpu.VMEM((3, WINDOW, 3 * WINDOW), F32)],
            compiler_params=_cparams(("parallel", "parallel"), 32),
            name="window_attention_online" if online else "window_attention",
        )(bound, sink, qkv_lat, qkv_lat, qkv_lat, qkv_ctx, qkv_ctx)

    return lax.cond(bound[layer] <= MAX_SHIFT_BOUND, lambda: call(False), lambda: call(True))


def _cattn_kernel(sink_ref, c_ref, d_ref, oc_ref, od_ref, *, layer):
    t = c_ref.shape[0]
    for ref, out, has_sink in ((c_ref, oc_ref, False), (d_ref, od_ref, True)):
        qs = _stack_heads(ref[:, 0:256])
        s = _dot_nt(qs, ref[:, 256:384])
        m = jnp.max(s, axis=-1, keepdims=True)
        if has_sink:
            sink = _sink_rows(sink_ref, layer, t)
            m = jnp.maximum(m, sink)
        p = jnp.exp2(s - m)
        den = jnp.sum(p, axis=-1, keepdims=True)
        if has_sink:
            den = den + jnp.exp2(sink - m)
        acc = _dot(p.astype(BF16), ref[:, 384:512])
        out[...] = _unstack_heads(acc / den, t).astype(out.dtype)


def _context_attention(layer, sink, qkv_c, qkv_d, batch):
    lc = qkv_c.shape[0] // batch
    blk = lambda width: pl.BlockSpec((lc, width), lambda b: (b, 0))
    return pl.pallas_call(
        functools.partial(_cattn_kernel, layer=layer),
        out_shape=(jax.ShapeDtypeStruct((batch * lc, 256), BF16),) * 2,
        grid=(batch,),
        in_specs=[pl.BlockSpec(memory_space=pltpu.SMEM), blk(512), blk(512)],
        out_specs=(blk(256), blk(256)),
        compiler_params=_cparams(("parallel",), 32),
        name="context_attention",
    )(sink, qkv_c, qkv_d)


def _out_ffn_kernel(x_ref, ya_ref, yb_ref, yc_ref, yd_ref, mod_ref, wo_ref, g_ref, w1_ref, w2_ref, o_ref, *, hchunk):
    m = mod_ref[0]
    gw = GROUP_WIDTH
    mix = _dot(ya_ref[...], wo_ref[0, 0:gw, :])
    for i, y_ref in enumerate((yb_ref, yc_ref, yd_ref), start=1):
        mix = mix + _dot(y_ref[...], wo_ref[0, i * gw:(i + 1) * gw, :])
    x1 = x_ref[...] + m[2:3] * mix
    ms = jnp.mean(x1 * x1, axis=-1, keepdims=True)
    h = ((x1 * lax.rsqrt(ms + EPS) * g_ref[0]) * (1.0 + m[4:5]) + m[3:4]).astype(BF16)
    acc = None
    for c in range(w1_ref.shape[2] // hchunk):
        u = jnp.maximum(_dot(h, w1_ref[0, :, c * hchunk:(c + 1) * hchunk]), 0.0)
        t = _dot((u * u).astype(BF16), w2_ref[0, c * hchunk:(c + 1) * hchunk, :])
        acc = t if acc is None else acc + t
    o_ref[...] = x1 + m[5:6] * acc


def _out_ffn(layer, x2d, ys, mod, mod_row_of_tile, w_out, g, w1, w2, tm):
    rows, d = x2d.shape
    hidden = w1.shape[2]
    row_blk = lambda width: pl.BlockSpec((tm, width), lambda i: (i, 0))
    const = lambda *shape: pl.BlockSpec((1,) + shape, lambda i: (layer,) + (0,) * len(shape), pipeline_mode=pl.Buffered(1))
    return pl.pallas_call(
        functools.partial(_out_ffn_kernel, hchunk=1024),
        out_shape=jax.ShapeDtypeStruct((rows, d), F32),
        grid=(rows // tm,),
        in_specs=[row_blk(d), row_blk(256), row_blk(256), row_blk(256), row_blk(256),
                  pl.BlockSpec((1, N_MOD, d), lambda i: (mod_row_of_tile(i), 0, 0)),
                  const(d, d), pl.BlockSpec((1, 1, d), lambda i: (layer, 0, 0)), const(d, hidden), const(hidden, d)],
        out_specs=row_blk(d),
        compiler_params=_cparams(("parallel",), 56),
        name="out_ffn",
    )(x2d, *ys, mod, w_out, g, w1, w2)


def _permute_q_heads(cols):
    q = cols[..., 0:256].reshape(cols.shape[:-1] + (2, 2, HEAD_DIM))
    q = jnp.swapaxes(q, -3, -2).reshape(cols.shape[:-1] + (256,))
    return jnp.concatenate([q, cols[..., 256:512]], axis=-1)


def _relayout_w_in(w):
    ssd_in = GROUP_WIDTH + SSD_CONV_CH + 2 * SSD_HEADS
    b0, c0, d0 = ssd_in, ssd_in + 512, ssd_in + 1024
    dt = w[..., GROUP_WIDTH + SSD_CONV_CH:ssd_in]
    dt = jnp.pad(dt, [(0, 0)] * (w.ndim - 1) + [(0, LANES - 2 * SSD_HEADS)])
    return jnp.concatenate([w[..., 0:GROUP_WIDTH + SSD_CONV_CH], w[..., b0:c0], _permute_q_heads(w[..., c0:d0]),
                            _permute_q_heads(w[..., d0:d0 + 512]), dt], axis=-1).astype(BF16)


def _relayout_w_out(w):
    depth, _, d = w.shape
    attn = jnp.swapaxes(w[:, 512:1024].reshape(depth, 2, 2, 2, HEAD_DIM, d), 2, 3).reshape(depth, 512, d)
    return jnp.concatenate([w[:, 0:512], attn], axis=1).astype(BF16)


def _block_diag(w):
    eye = jnp.eye(LRU_BLOCKS, dtype=w.dtype)
    out = jnp.einsum('...kij,kn->...kinj', w, eye)
    return out.reshape(w.shape[:-3] + (LRU_WIDTH, LRU_WIDTH))


def _rope_tables(length):
    pos = np.arange(length)
    inv = ROPE_THETA ** (-np.arange(0, ROPE_AXIS_DIM, 2, dtype=np.float64) / ROPE_AXIS_DIM)
    row = (pos // GRID_W)[:, None] * inv
    col = (pos % GRID_W)[:, None] * inv
    ang = np.concatenate([row, row, col, col], axis=1)
    half = ROPE_AXIS_DIM // 2
    sign = np.tile(np.concatenate([-np.ones(half), np.ones(half)]), 2)
    reps = (1, LANES // HEAD_DIM)
    return (jnp.asarray(np.tile(np.cos(ang), reps), F32), jnp.asarray(np.tile(np.sin(ang) * sign, reps), F32))


def _head_selectors():
    sel = np.zeros((2, 2 * LANES, SSD_HEADS * LANES), np.float32)
    for d in range(2):
        for h in range(SSD_HEADS):
            for part in range(2):
                sel[d, part * LANES + SSD_HEADS * d + h, h * LANES:(h + 1) * LANES] = 1.0
    return jnp.asarray(sel, BF16)


def _head_mean_matrix():
    idx = np.arange(256) // HEAD_DIM
    return jnp.asarray((idx[:, None] == idx[None, :]).astype(np.float32) / HEAD_DIM, BF16)


def kernel(x, c, ctx, c_ctx, w_mod, b_mod, g_mix, w_in, ssd_conv_w, ssd_conv_b, ssd_a_log, ssd_dt_bias, ssd_d, ssd_norm_g, lru_conv_w, lru_conv_b, lru_lambda, lru_w_a, lru_b_a, lru_w_i, lru_b_i, gqa_q_norm, gqa_k_norm, swa_q_norm, swa_k_norm, swa_sink, w_out, g_ffn, w_ffn1, w_ffn2):
    batch, seq, d = x.shape
    lc = ctx.shape[1]
    depth = w_mod.shape[0]
    assert batch + 1 <= 8 and seq % 1024 == 0 and lc % 256 == 0 and seq % GRID_W == 0

    cvecs = jnp.concatenate([c, c_ctx[None, :], jnp.zeros((8 - batch - 1, d), F32)], axis=0)
    mod = _modulation(cvecs, w_mod, b_mod).reshape(depth * 8, N_MOD, d)

    rope = _rope_tables(seq)
    sel = _head_selectors()
    bd = _head_mean_matrix()
    conv_sel = _conv_selectors()
    w_in_r = _relayout_w_in(w_in)
    w_out_r = _relayout_w_out(w_out)
    w1, w2 = w_ffn1.astype(BF16), w_ffn2.astype(BF16)
    g1, g2 = g_mix.reshape(depth, 1, d), g_ffn.reshape(depth, 1, d)
    nrm = jnp.tile(jnp.stack([gqa_q_norm, gqa_k_norm, swa_q_norm, swa_k_norm], axis=1), (1, 1, 256 // HEAD_DIM))
    lane_pad = ((0, 0), (0, 0), (0, LANES - 2 * SSD_HEADS))
    head_params = jnp.concatenate([jnp.pad(ssd_dt_bias.reshape(depth, 1, -1), lane_pad),
                                   jnp.pad(ssd_a_log.reshape(depth, 1, -1), lane_pad),
                                   jnp.zeros((depth, 6, LANES), F32)], axis=1)
    ssd_cb = ssd_conv_b.reshape(depth, 1, -1)
    d_skip = jnp.repeat(ssd_d, HEAD_DIM, axis=-1).reshape(depth, 1, -1)
    ssd_ng = ssd_norm_g.reshape(depth, 1, -1)
    bd_a, bd_i = _block_diag(lru_w_a), _block_diag(lru_w_i)
    w_gates = (0.5 * jnp.concatenate([bd_a[:, 0], bd_i[:, 0], bd_a[:, 1], bd_i[:, 1]], axis=-1)).astype(BF16)
    b_gates = 0.5 * jnp.concatenate([lru_b_a[:, 0], lru_b_i[:, 0], lru_b_a[:, 1], lru_b_i[:, 1]], axis=-1).reshape(depth, 1, -1)
    lru_cb = lru_conv_b.reshape(depth, 1, -1)
    lam = lru_lambda.reshape(depth, 1, -1)
    bounds = _logit_bounds(gqa_q_norm, gqa_k_norm)
    swa_bounds = _logit_bounds(swa_q_norm, swa_k_norm)

    xl = x.reshape(batch * seq, d)
    xc = ctx.reshape(batch * lc, d)
    for l in range(depth):
        need_ctx = l < depth - 1
        lat_row = lambda rows: (lambda i, l=l: l * 8 + i // (seq // rows))
        ctx_row = lambda i, l=l: l * 8 + batch
        ssd_l, lru_l, qc_l, qd_l, dt_l = _in_projection(l, xl, mod, lat_row(IN_PROJ_ROWS), g1, w_in_r, nrm, bd, rope,
                                                        IN_PROJ_ROWS, seq // IN_PROJ_ROWS)
        ssd_c, lru_c, qc_c, qd_c, dt_c = _in_projection(l, xc, mod, ctx_row, g1, w_in_r, nrm, bd, None, lc, 1)
        ya_l, ya_c = _ssd_mixer(l, ssd_l, ssd_c, dt_l, dt_c, conv_sel, ssd_conv_w, ssd_cb, head_params, sel, d_skip,
                                ssd_ng, batch)
        yb_l, yb_c = _lru_mixer(l, lru_l, lru_c, conv_sel, lru_conv_w, lru_cb, w_gates, b_gates, lam, batch)
        yc_l = _global_attention(l, bounds, qc_l, qc_c, batch, GATTN_Q_ROWS, GATTN_KV_ROWS)
        yd_l = _window_attention(l, swa_bounds, swa_sink, qd_l, qd_c, batch, seq)
        xl = _out_ffn(l, xl, (ya_l, yb_l, yc_l, yd_l), mod, lat_row(FFN_ROWS), w_out_r, g2, w1, w2, FFN_ROWS)
        if need_ctx:
            yc_c, yd_c = _context_attention(l, swa_sink, qc_c, qd_c, batch)
            xc = _out_ffn(l, xc, (ya_c, yb_c, yc_c, yd_c), mod, ctx_row, w_out_r, g2, w1, w2, FFN_ROWS)
    return xl.reshape(batch, seq, d)
```

```python
import functools
import math

import numpy as np
import jax
import jax.numpy as jnp
from jax import lax
from jax.experimental import pallas as pl
from jax.experimental.pallas import tpu as pltpu

F32 = jnp.float32
BF16 = jnp.bfloat16

EPS = 1e-6
GRID_W = 64
N_MOD = 6
GROUP_WIDTH = 256
SSD_HEADS = 4
SSD_STATE = 128
SSD_CONV = 4
SSD_CHUNK = 128
SSD_CONV_CH = 768
LRU_WIDTH = 256
LRU_BLOCKS = 4
LRU_C = 8.0
HEAD_DIM = 64
Q_HEADS = 4
WINDOW = 128
ROPE_THETA = 10000.0
ROPE_AXIS_DIM = 32
LOG2_E = math.log2(math.e)
Q_SCALE = HEAD_DIM ** -0.5 * LOG2_E

LANES = 128
BF16_ROWS = 16

COL_SSD, COL_LRU, COL_C, COL_D, COL_DT, COL_END = 0, 1024, 1536, 2048, 2560, 2688
Q_HEAD_ORDER = (0, 2, 1, 3)

CONV_WIN = SSD_CHUNK + 2 * BF16_ROWS

IN_PROJ_ROWS = 1024
FFN_ROWS = 512
GATTN_Q_ROWS = 512
GATTN_KV_ROWS = 512
SSD_GROUP = 8


def _cparams(sem, vmem_mib):
    return pltpu.CompilerParams(dimension_semantics=sem, vmem_limit_bytes=vmem_mib * 1024 * 1024)


def _dot(a, b):
    return jnp.dot(a, b, preferred_element_type=F32)


def _dot_nt(a, b):
    return lax.dot_general(a, b, (((1,), (1,)), ((), ())), preferred_element_type=F32)


def _split_bf16(a, parts):
    out = []
    for _ in range(parts - 1):
        hi = a.astype(BF16)
        out.append(hi)
        a = a - hi.astype(F32)
    out.append(a.astype(BF16))
    return out


def _silu(x):
    h = 0.5 * x
    return h + h * jnp.tanh(h)


def _softplus(x):
    return jnp.maximum(x, 0.0) + jnp.log1p(jnp.exp(-jnp.abs(x)))


def _gelu_tanh(x):
    return 0.5 * x * (1.0 + jnp.tanh(math.sqrt(2.0 / math.pi) * (x + 0.044715 * (x * x * x))))


def _lane_iota(shape):
    return lax.broadcasted_iota(jnp.int32, shape, len(shape) - 1)


def _row_iota(shape):
    return lax.broadcasted_iota(jnp.int32, shape, len(shape) - 2)


def _mod_kernel(cv_ref, w_ref, b_ref, o_ref):
    cv = cv_ref[...]
    s = _silu(cv)
    w = w_ref[0]
    s_hi, s_lo = _split_bf16(s, 2)
    w_hi, w_lo = _split_bf16(w, 2)
    acc = _dot(s_hi, w_hi) + _dot(s_lo, w_hi) + _dot(s_hi, w_lo)
    o_ref[0] = acc + b_ref[0]


def _modulation(cvecs, w_mod, b_mod):
    depth, d, n = w_mod.shape
    tn = 1024
    return pl.pallas_call(
        _mod_kernel,
        out_shape=jax.ShapeDtypeStruct((depth, 8, n), F32),
        grid=(depth, n // tn),
        in_specs=[pl.BlockSpec((8, d), lambda l, j: (0, 0)),
                  pl.BlockSpec((1, d, tn), lambda l, j: (l, 0, j)),
                  pl.BlockSpec((1, 1, tn), lambda l, j: (l, 0, j))],
        out_specs=pl.BlockSpec((1, 8, tn), lambda l, j: (l, 0, j)),
        compiler_params=_cparams(("parallel", "parallel"), 32),
        name="modulation",
    )(cvecs, w_mod, b_mod.reshape(depth, 1, n))


def _head_norm_rope(p, g, bd, cos, sin, scale):
    ms = _dot((p * p).astype(BF16), bd)
    y = p * lax.rsqrt(ms + EPS) * g
    if cos is not None:
        w = y.shape[-1]
        first = (_lane_iota(y.shape) % ROPE_AXIS_DIM) < (ROPE_AXIS_DIM // 2)
        partner = jnp.where(first, pltpu.roll(y, w - ROPE_AXIS_DIM // 2, 1), pltpu.roll(y, ROPE_AXIS_DIM // 2, 1))
        y = y * cos + partner * sin
    if scale != 1.0:
        y = y * scale
    return y


def _inproj_kernel(*refs, use_rope):
    if use_rope:
        x_ref, mod_ref, g_ref, w_ref, nrm_ref, bd_ref, cos_ref, sin_ref, ssd_ref, lru_ref, c_ref, d_ref, dt_ref = refs
    else:
        x_ref, mod_ref, g_ref, w_ref, nrm_ref, bd_ref, ssd_ref, lru_ref, c_ref, d_ref, dt_ref = refs
    x = x_ref[...]
    ms = jnp.mean(x * x, axis=-1, keepdims=True)
    m = mod_ref[0]
    h = (x * lax.rsqrt(ms + EPS) * g_ref[0]) * (1.0 + m[1:2]) + m[0:1]
    hb = h.astype(BF16)
    bd = bd_ref[...]
    if use_rope:
        cos1, sin1 = cos_ref[...], sin_ref[...]
        cos2 = jnp.concatenate([cos1, cos1], axis=1)
        sin2 = jnp.concatenate([sin1, sin1], axis=1)
    else:
        cos1 = sin1 = cos2 = sin2 = None
    for o_ref, c0, row in ((c_ref, COL_C, 0), (d_ref, COL_D, 2)):
        p = _dot(hb, w_ref[0, :, c0:c0 + 512])
        q = _head_norm_rope(p[:, 0:256], nrm_ref[0, row:row + 1, :], bd, cos2, sin2, Q_SCALE)
        k = _head_norm_rope(p[:, 256:384], nrm_ref[0, row + 1:row + 2, 0:128], bd[0:128, 0:128], cos1, sin1, 1.0)
        o_ref[:, 0:256] = q.astype(BF16)
        o_ref[:, 256:384] = k.astype(BF16)
        o_ref[:, 384:512] = p[:, 384:512].astype(BF16)
    ssd_ref[...] = _dot(hb, w_ref[0, :, COL_SSD:COL_LRU]).astype(BF16)
    lru_ref[...] = _dot(hb, w_ref[0, :, COL_LRU:COL_C]).astype(BF16)
    dt_ref[...] = _dot(hb, w_ref[0, :, COL_DT:COL_END])


def _in_projection(layer, x2d, mod, mod_row_of_tile, g, w, nrm, bd, rope, tm, rope_tiles):
    rows, d = x2d.shape
    use_rope = rope is not None
    in_specs = [pl.BlockSpec((tm, d), lambda i: (i, 0)),
                pl.BlockSpec((1, N_MOD, d), lambda i: (mod_row_of_tile(i), 0, 0)),
                pl.BlockSpec((1, 1, d), lambda i: (layer, 0, 0)),
                pl.BlockSpec((1, d, COL_END), lambda i: (layer, 0, 0), pipeline_mode=pl.Buffered(1)),
                pl.BlockSpec((1, 4, 256), lambda i: (layer, 0, 0)),
                pl.BlockSpec((256, 256), lambda i: (0, 0))]
    args = [x2d, mod, g, w, nrm, bd]
    if use_rope:
        in_specs += [pl.BlockSpec((tm, LANES), lambda i: (i % rope_tiles, 0))] * 2
        args += list(rope)
    widths = (1024, 512, 512, 512, LANES)
    dtypes = (BF16, BF16, BF16, BF16, F32)
    return pl.pallas_call(
        functools.partial(_inproj_kernel, use_rope=use_rope),
        out_shape=tuple(jax.ShapeDtypeStruct((rows, wd), dt) for wd, dt in zip(widths, dtypes)),
        grid=(rows // tm,),
        in_specs=in_specs,
        out_specs=tuple(pl.BlockSpec((tm, wd), lambda i: (i, 0)) for wd in widths),
        compiler_params=_cparams(("parallel",), 56),
        name="in_projection",
    )(*args)


def _conv_selectors():
    row = np.arange(SSD_CONV * SSD_CHUNK)
    want = (row % SSD_CHUNK) + (row // SSD_CHUNK) - SSD_CONV // 2
    col = np.arange(CONV_WIN)
    sel = np.stack([(col[None, :] == want[:, None] + BF16_ROWS * case) for case in range(3)])
    return jnp.asarray(sel.astype(np.float32), BF16)


def _conv_chunk(seq_ref, col0, width, length, r0, sel_ref, w_ref, b_ref):
    ws = jnp.clip(r0 - BF16_ROWS, 0, length - CONV_WIN)
    ws = pl.multiple_of(ws, BF16_ROWS)
    xw = seq_ref[pl.ds(ws, CONV_WIN), col0:col0 + width]
    sh = _dot(sel_ref[(r0 - ws) // BF16_ROWS], xw)
    w = w_ref[0]
    acc = b_ref[0] + sh[0:SSD_CHUNK] * w[0:1]
    for k in range(1, SSD_CONV):
        acc = acc + sh[k * SSD_CHUNK:(k + 1) * SSD_CHUNK] * w[k:k + 1]
    return acc


def _ssd_kernel(pl_ref, pc_ref, dtl_ref, dtc_ref, csel_ref, cw_ref, cb_ref, hp_ref, sel_ref, dsk_ref, ng_ref,
                yl_ref, yc_ref, accl_ref, accc_ref, xbcl_ref, xbcc_ref):
    q = SSD_CHUNK
    lane8 = _lane_iota((1, LANES)) < 2 * SSD_HEADS
    dt_bias = hp_ref[0, 0:1, :]
    a_row = jnp.where(lane8, -jnp.exp(hp_ref[0, 1:2, :]) * LOG2_E, 0.0)
    ii = _row_iota((q, q))
    jj = _lane_iota((q, q))
    lo = _lane_iota((q, LANES)) < HEAD_DIM
    masks = (jj <= ii, jj >= ii)
    tris = [jnp.where(m, 1.0, 0.0).astype(BF16) for m in masks]
    tri_pairs = [jnp.concatenate([t, t], axis=1) for t in tris]
    zblk = jnp.zeros((SSD_STATE, LANES), BF16)

    def block_diag(a0, a1):
        return jnp.concatenate([jnp.concatenate([a0, zblk], axis=1), jnp.concatenate([zblk, a1], axis=1)], axis=0)

    def chunk(xbc_ref, r0, d, state, acum, dtb, acb):
        xs = xbc_ref[pl.ds(r0, q), 0:256].astype(F32)
        bm = xbc_ref[pl.ds(r0, q), 256:512].astype(F32)
        cmb = xbc_ref[pl.ds(r0, q), 512:768]
        mask = masks[d]
        act = acum.T
        edge = q - 1 if d == 0 else 0
        per_head = []
        for h in range(SSD_HEADS):
            ac = acb[:, h * LANES:(h + 1) * LANES]
            ar = act[SSD_HEADS * d + h:SSD_HEADS * d + h + 1, :]
            lmat = jnp.where(mask, jnp.exp2(ac - ar), 0.0)
            alast = ac[edge:edge + 1, :]
            dth = dtb[:, h * LANES:(h + 1) * LANES]
            per_head.append((lmat, dth, jnp.exp2(alast - ac) * dth, jnp.exp2(ac), jnp.exp2(alast)))
        bts = [bm[:, g * LANES:(g + 1) * LANES].T.astype(BF16) for g in range(2)]
        cb_all = _dot(cmb, block_diag(bts[0], bts[1]))
        yo_all = _dot(cmb, block_diag(state[0].astype(BF16), state[1].astype(BF16)))
        ys, new_state = [], []
        for g in range(2):
            (l0, dt0, we0, ei0, cd0), (l1, dt1, we1, ei1, cd1) = per_head[2 * g], per_head[2 * g + 1]
            xg = xs[:, g * LANES:(g + 1) * LANES]
            cb = cb_all[:, g * LANES:(g + 1) * LANES]
            xdt = xg * jnp.where(lo, dt0, dt1)
            scores = jnp.concatenate([(cb * l0).astype(BF16), (cb * l1).astype(BF16)], axis=1)
            xpair = jnp.concatenate([jnp.where(lo, xdt, 0.0), jnp.where(lo, 0.0, xdt)], axis=0).astype(BF16)
            y = _dot(scores, xpair) + yo_all[:, g * LANES:(g + 1) * LANES] * jnp.where(lo, ei0, ei1)
            s_new = _dot(bts[g], (xg * jnp.where(lo, we0, we1)).astype(BF16))
            new_state.append(state[g] * jnp.where(lo, cd0, cd1) + s_new)
            ys.append(y)
        return jnp.concatenate(ys, axis=1), xs, tuple(new_state)

    def sweep(seq_ref, xbc_ref, dt_ref, acc_ref, out_ref, d, state):
        length = seq_ref.shape[0]
        nc = length // q
        gsz = min(nc, SSD_GROUP)
        ngroups = nc // gsz

        def body(gi, st):
            g0 = pl.multiple_of((gi if d == 0 else ngroups - 1 - gi) * (gsz * q), gsz * q)
            if d == 0:
                for c in range(gsz):
                    conv = _conv_chunk(seq_ref, GROUP_WIDTH, SSD_CONV_CH, length, g0 + c * q, csel_ref, cw_ref, cb_ref)
                    xbc_ref[pl.ds(g0 + c * q, q), :] = _silu(conv).astype(BF16)
            dtv = _softplus(dt_ref[pl.ds(g0, gsz * q), :] + dt_bias)
            hi, lo_part = _split_bf16(dtv * a_row, 2)
            rhs = jnp.concatenate([jnp.concatenate([hi[c * q:(c + 1) * q], lo_part[c * q:(c + 1) * q]], axis=0)
                                   for c in range(gsz)], axis=1)
            acum = _dot(tri_pairs[d], rhs)
            stacked = jnp.concatenate([blk for c in range(gsz)
                                       for blk in (dtv[c * q:(c + 1) * q], acum[:, c * LANES:(c + 1) * LANES])], axis=0)
            ex = _dot(jnp.concatenate(_split_bf16(stacked, 2), axis=1), sel_ref[d])
            for i in range(gsz):
                c = i if d == 0 else gsz - 1 - i
                r0 = g0 + c * q
                y, xs, st = chunk(xbc_ref, r0, d, st, acum[:, c * LANES:(c + 1) * LANES],
                                  ex[2 * c * q:(2 * c + 1) * q], ex[(2 * c + 1) * q:(2 * c + 2) * q])
                if d == 0:
                    acc_ref[pl.ds(r0, q), :] = y
                else:
                    y = acc_ref[pl.ds(r0, q), :] + y + dsk_ref[0] * xs
                    z = seq_ref[pl.ds(r0, q), 0:GROUP_WIDTH].astype(F32)
                    t = y * _silu(z)
                    ms = jnp.mean(t * t, axis=-1, keepdims=True)
                    out_ref[pl.ds(r0, q), :] = (t * lax.rsqrt(ms + EPS) * ng_ref[0]).astype(out_ref.dtype)
            return st

        return lax.fori_loop(0, ngroups, body, state)

    zero = (jnp.zeros((SSD_STATE, LANES), F32), jnp.zeros((SSD_STATE, LANES), F32))
    for d in range(2):
        st = sweep(pc_ref, xbcc_ref, dtc_ref, accc_ref, yc_ref, d, zero)
        sweep(pl_ref, xbcl_ref, dtl_ref, accl_ref, yl_ref, d, st)


def _ssd_mixer(layer, p_lat, p_ctx, dt_lat, dt_ctx, conv_sel, conv_w, conv_b, head_params, sel, d_skip, norm_g, batch):
    seq = p_lat.shape[0] // batch
    lc = p_ctx.shape[0] // batch
    const = lambda *shape: pl.BlockSpec(shape, lambda b: (0,) * len(shape))
    of_layer = lambda *shape: pl.BlockSpec((1,) + shape, lambda b: (layer,) + (0,) * len(shape))
    return pl.pallas_call(
        _ssd_kernel,
        out_shape=(jax.ShapeDtypeStruct((batch * seq, GROUP_WIDTH), BF16),
                   jax.ShapeDtypeStruct((batch * lc, GROUP_WIDTH), BF16)),
        grid=(batch,),
        in_specs=[pl.BlockSpec((seq, 1024), lambda b: (b, 0)),
                  pl.BlockSpec((lc, 1024), lambda b: (b, 0)),
                  pl.BlockSpec((seq, LANES), lambda b: (b, 0)),
                  pl.BlockSpec((lc, LANES), lambda b: (b, 0)),
                  const(3, SSD_CONV * SSD_CHUNK, CONV_WIN),
                  of_layer(SSD_CONV, SSD_CONV_CH), of_layer(1, SSD_CONV_CH), of_layer(8, LANES),
                  const(2, 2 * LANES, SSD_HEADS * LANES), of_layer(1, GROUP_WIDTH), of_layer(1, GROUP_WIDTH)],
        out_specs=(pl.BlockSpec((seq, GROUP_WIDTH), lambda b: (b, 0)),
                   pl.BlockSpec((lc, GROUP_WIDTH), lambda b: (b, 0))),
        scratch_shapes=[pltpu.VMEM((seq, GROUP_WIDTH), F32), pltpu.VMEM((lc, GROUP_WIDTH), F32),
                        pltpu.VMEM((seq, SSD_CONV_CH), BF16), pltpu.VMEM((lc, SSD_CONV_CH), BF16)],
        compiler_params=_cparams(("parallel",), 48),
        name="ssd_mixer",
    )(p_lat, p_ctx, dt_lat, dt_ctx, conv_sel, conv_w, conv_b, head_params, sel, d_skip, norm_g)


LRU_SEGMENTS = 8
LRU_GROUP = 8


def _lru_kernel(pl_ref, pc_ref, csel_ref, cw_ref, cb_ref, wg_ref, bg_ref, lam_ref, ol_ref, oc_ref, a_scr, b_scr, cin_scr):
    q = SSD_CHUNK
    nseg = LRU_SEGMENTS
    nhc = (0.5 * LRU_C) * _softplus(-lam_ref[0])

    def run(seq_ref, out_ref, h0):
        length = seq_ref.shape[0]
        seglen = length // nseg
        sub = min(q, seglen)

        def seg_rows(t0):
            return pl.ds((t0 % seglen) * nseg + t0 // seglen, sub, stride=nseg)

        gsz = min(length // q, LRU_GROUP)

        def gates(gi, _):
            g0 = pl.multiple_of(gi * (gsz * q), gsz * q)
            xcs = [_conv_chunk(seq_ref, LRU_WIDTH, LRU_WIDTH, length, g0 + c * q, csel_ref, cw_ref, cb_ref)
                   for c in range(gsz)]
            xc_all = jnp.concatenate(xcs, axis=0)
            t_all = jnp.tanh(_dot(xc_all.astype(BF16), wg_ref[0]) + bg_ref[0])
            for ci in range(gsz):
                r0 = g0 + ci * q
                t = t_all[ci * q:(ci + 1) * q]
                hx = 0.5 * xcs[ci]
                for d in range(2):
                    c = nhc[:, d * 256:(d + 1) * 256]
                    nla = c * t[:, d * 512:d * 512 + 256] + c
                    a = jnp.exp2(nla * -LOG2_E)
                    y = jnp.tanh(nla) * (a * a + 1.0)
                    root = jnp.where(y > 0.0, y * lax.rsqrt(y), 0.0)
                    bv = root * (hx * t[:, d * 512 + 256:(d + 1) * 512] + hx)
                    for hf in range(2):
                        for j in range(q // sub):
                            rows = seg_rows(r0 + j * sub)
                            a_scr[2 * d + hf, rows, :] = a[j * sub:(j + 1) * sub, hf * LANES:(hf + 1) * LANES]
                            b_scr[2 * d + hf, rows, :] = bv[j * sub:(j + 1) * sub, hf * LANES:(hf + 1) * LANES]
            return 0

        lax.fori_loop(0, length // (gsz * q), gates, 0)

        def scan(k, carry):
            hs, ps = carry
            new_h, new_p = [], []
            for qd in range(4):
                steps = [4 * k + i if qd < 2 else seglen - 1 - (4 * k + i) for i in range(4)]
                idxs = [(qd, pl.ds(pl.multiple_of(st * nseg, nseg), nseg), slice(None)) for st in steps]
                a = [a_scr[ix] for ix in idxs]
                b = [b_scr[ix] for ix in idxs]
                h, p = hs[qd], ps[qd]
                a01, b01 = a[1] * a[0], a[1] * b[0] + b[1]
                a23, b23 = a[3] * a[2], a[3] * b[2] + b[3]
                a03, b03 = a23 * a01, a23 * b01 + b23
                h1 = a[0] * h + b[0]
                h2 = a01 * h + b01
                h3 = a[2] * h2 + b[2]
                h4 = a03 * h + b03
                p2 = p * a01
                for ix, hv, pv in zip(idxs, (h1, h2, h3, h4), (p * a[0], p2, p2 * a[2], p * a03)):
                    b_scr[ix] = hv
                    a_scr[ix] = pv
                new_h.append(h4)
                new_p.append(p * a03)
            return tuple(new_h), tuple(new_p)

        z8 = jnp.zeros((nseg, LANES), F32)
        o8 = jnp.ones((nseg, LANES), F32)
        hend, pend = lax.fori_loop(0, seglen // 4, scan, ((z8,) * 4, (o8,) * 4), unroll=1)

        h_out = []
        for qd in range(4):
            c = h0[qd]
            order = range(nseg) if qd < 2 else range(nseg - 1, -1, -1)
            for k in order:
                cin_scr[qd, k:k + 1, :] = c
                c = hend[qd][k:k + 1, :] + pend[qd][k:k + 1, :] * c
            h_out.append(c)

        def emit(c, _):
            r0 = pl.multiple_of(c * sub, sub)
            rows = seg_rows(r0)
            hq = [b_scr[qd, rows, :] + a_scr[qd, rows, :] * cin_scr[qd, pl.ds(r0 // seglen, 1), :] for qd in range(4)]
            hsum = jnp.concatenate([hq[0] + hq[2], hq[1] + hq[3]], axis=1)
            gate = seq_ref[pl.ds(r0, sub), 0:LRU_WIDTH].astype(F32)
            out_ref[pl.ds(r0, sub), :] = (hsum * _gelu_tanh(gate)).astype(out_ref.dtype)
            return 0

        lax.fori_loop(0, length // sub, emit, 0)
        return h_out

    z1 = jnp.zeros((1, LANES), F32)
    states = run(pc_ref, oc_ref, [z1] * 4)
    run(pl_ref, ol_ref, states)


def _lru_mixer(layer, p_lat, p_ctx, conv_sel, conv_w, conv_b, w_gates, b_gates, lam, batch):
    seq = p_lat.shape[0] // batch
    lc = p_ctx.shape[0] // batch
    const = lambda *shape: pl.BlockSpec(shape, lambda b: (0,) * len(shape))
    of_layer = lambda *shape: pl.BlockSpec((1,) + shape, lambda b: (layer,) + (0,) * len(shape))
    return pl.pallas_call(
        _lru_kernel,
        out_shape=(jax.ShapeDtypeStruct((batch * seq, LRU_WIDTH), BF16),
                   jax.ShapeDtypeStruct((batch * lc, LRU_WIDTH), BF16)),
        grid=(batch,),
        in_specs=[pl.BlockSpec((seq, 512), lambda b: (b, 0)),
                  pl.BlockSpec((lc, 512), lambda b: (b, 0)),
                  const(3, SSD_CONV * SSD_CHUNK, CONV_WIN),
                  of_layer(SSD_CONV, LRU_WIDTH), of_layer(1, LRU_WIDTH), of_layer(LRU_WIDTH, 1024), of_layer(1, 1024),
                  of_layer(1, 512)],
        out_specs=(pl.BlockSpec((seq, LRU_WIDTH), lambda b: (b, 0)),
                   pl.BlockSpec((lc, LRU_WIDTH), lambda b: (b, 0))),
        scratch_shapes=[pltpu.VMEM((4, seq, LANES), F32), pltpu.VMEM((4, seq, LANES), F32),
                        pltpu.VMEM((4, LRU_SEGMENTS, LANES), F32)],
        compiler_params=_cparams(("parallel",), 48),
        name="lru_mixer",
    )(p_lat, p_ctx, conv_sel, conv_w, conv_b, w_gates, b_gates, lam)


def _stack_heads(qblk):
    lo = _lane_iota((qblk.shape[0], LANES)) < HEAD_DIM
    zero = jnp.zeros((), qblk.dtype)
    s0, s1 = qblk[:, 0:LANES], qblk[:, LANES:2 * LANES]
    return jnp.concatenate([jnp.where(lo, s0, zero), jnp.where(lo, s1, zero),
                            jnp.where(lo, zero, s0), jnp.where(lo, zero, s1)], axis=0)


def _unstack_heads(acc, t):
    lo = _lane_iota((t, LANES)) < HEAD_DIM
    return jnp.concatenate([jnp.where(lo, acc[0:t], acc[2 * t:3 * t]),
                            jnp.where(lo, acc[t:2 * t], acc[3 * t:4 * t])], axis=1)


MAX_SHIFT_BOUND = 50.0


def _gattn_kernel(bound_ref, q_ref, kl_ref, vl_ref, kc_ref, vc_ref, o_ref, qs_scr, m_scr, acc_scr, *, layer, tk, online):
    tq = q_ref.shape[0]
    half = 2 * tq
    qs_scr[...] = _stack_heads(q_ref[...])
    acc_scr[...] = jnp.zeros(acc_scr.shape, F32)
    if online:
        m_scr[...] = jnp.full(m_scr.shape, -jnp.inf, F32)

    def step(k, v):
        lo = _lane_iota(v.shape) < HEAD_DIM
        one = jnp.ones((), v.dtype)
        s = _dot_nt(qs_scr[...], k)
        if online:
            m_old = m_scr[...]
            shift = jnp.maximum(m_old, jnp.max(s, axis=-1, keepdims=True))
            m_scr[...] = shift
            acc_scr[...] = acc_scr[...] * jnp.exp2(m_old - shift)
        else:
            shift = bound_ref[layer]
        p = jnp.exp2(s - shift).astype(BF16)
        acc_scr[0:half, :] += _dot(p[0:half], jnp.where(lo, v, one))
        acc_scr[half:, :] += _dot(p[half:], jnp.where(lo, one, v))

    def body(j, _):
        r0 = pl.multiple_of(j * tk, tk)
        step(kl_ref[pl.ds(r0, tk), :], vl_ref[pl.ds(r0, tk), :])
        return 0

    lax.fori_loop(0, kl_ref.shape[0] // tk, body, 0, unroll=True)
    step(kc_ref[...], vc_ref[...])
    acc = acc_scr[...]
    o_ref[...] = _unstack_heads(acc / pltpu.roll(acc, HEAD_DIM, 1), tq).astype(o_ref.dtype)


def _logit_bounds(q_gain, k_gain):
    return HEAD_DIM * Q_SCALE * 1.02 * jnp.max(jnp.abs(q_gain), axis=-1) * jnp.max(jnp.abs(k_gain), axis=-1)


def _global_attention(layer, bound, qkv_lat, qkv_ctx, batch, tq, tk):
    seq = qkv_lat.shape[0] // batch
    lc = qkv_ctx.shape[0] // batch
    nq = seq // tq

    def call(online):
        return pl.pallas_call(
            functools.partial(_gattn_kernel, layer=layer, tk=tk, online=online),
            out_shape=jax.ShapeDtypeStruct((batch * seq, 256), BF16),
            grid=(batch, nq),
            in_specs=[pl.BlockSpec(memory_space=pltpu.SMEM),
                      pl.BlockSpec((tq, 256), lambda b, i: (b * nq + i, 0)),
                      pl.BlockSpec((seq, LANES), lambda b, i: (b, 2)),
                      pl.BlockSpec((seq, LANES), lambda b, i: (b, 3)),
                      pl.BlockSpec((lc, LANES), lambda b, i: (b, 2)),
                      pl.BlockSpec((lc, LANES), lambda b, i: (b, 3))],
            out_specs=pl.BlockSpec((tq, 256), lambda b, i: (b * nq + i, 0)),
            scratch_shapes=[pltpu.VMEM((4 * tq, LANES), BF16), pltpu.VMEM((4 * tq, 1), F32),
                            pltpu.VMEM((4 * tq, LANES), F32)],
            compiler_params=_cparams(("parallel", "parallel"), 48),
            name="global_attention_online" if online else "global_attention",
        )(bound, qkv_lat, qkv_lat, qkv_lat, qkv_ctx, qkv_ctx)

    return lax.cond(bound[layer] <= MAX_SHIFT_BOUND, lambda: call(False), lambda: call(True))


def _sink_rows(sink_ref, layer, t):
    head_of_row = _row_iota((4 * t, 1)) // t
    out = jnp.zeros((4 * t, 1), F32)
    for h in range(Q_HEADS):
        out = jnp.where(head_of_row == h, sink_ref[layer, h] * LOG2_E, out)
    return out


def _wattn_kernel(bound_ref, sink_ref, q_ref, kl_ref, vl_ref, kc_ref, vc_ref, o_ref, bias_scr, *, layer, online):
    w = WINDOW
    seq = kl_ref.shape[0]
    band = 3 * w
    shift = 0.0 if online else bound_ref[layer]
    dmat = _lane_iota((w, band)) - _row_iota((w, band))
    for case in range(3):
        bias_scr[case] = jnp.where(jnp.abs(dmat - case * w) <= w, -shift, -jnp.inf)
    one = jnp.ones((), BF16)
    kc, vc = kc_ref[...], vc_ref[...]
    lo_c = _lane_iota(vc.shape) < HEAD_DIM
    vc_sum = (jnp.where(lo_c, vc, one), jnp.where(lo_c, one, vc))
    lo_b = _lane_iota((band, LANES)) < HEAD_DIM
    lo = _lane_iota((w, LANES)) < HEAD_DIM

    def body(n, _):
        r0 = pl.multiple_of(n * w, w)
        gq = pl.program_id(1) * q_ref.shape[0] + r0
        ks = pl.multiple_of(jnp.clip(gq - w, 0, seq - band), w)
        bias = bias_scr[(gq - ks) // w]
        kb, vb = kl_ref[pl.ds(ks, band), :], vl_ref[pl.ds(ks, band), :]
        vb_sum = (jnp.where(lo_b, vb, one), jnp.where(lo_b, one, vb))
        qs = _stack_heads(q_ref[pl.ds(r0, w), :])
        s_band = _dot_nt(qs, kb)
        s_ctx = _dot_nt(qs, kc)
        outs = []
        for g in range(2):
            pb, pc, sk = [], [], []
            for h in (2 * g, 2 * g + 1):
                sb = s_band[h * w:(h + 1) * w] + bias
                sc = s_ctx[h * w:(h + 1) * w] - shift
                sink = sink_ref[layer, h] * LOG2_E - shift
                if online:
                    m = jnp.maximum(jnp.maximum(jnp.max(sb, axis=-1, keepdims=True),
                                                jnp.max(sc, axis=-1, keepdims=True)), sink)
                    sb, sc, sink = sb - m, sc - m, sink - m
                pb.append(jnp.exp2(sb).astype(BF16))
                pc.append(jnp.exp2(sc).astype(BF16))
                sk.append(jnp.exp2(sink) if online else jnp.full((w, 1), jnp.exp2(sink), F32))
            acc = _dot(jnp.concatenate(pb, axis=0), vb_sum[g]) + _dot(jnp.concatenate(pc, axis=0), vc_sum[g])
            den = pltpu.roll(acc, HEAD_DIM, 1) + jnp.concatenate(sk, axis=0)
            outs.append(acc / den)
        o_ref[pl.ds(r0, w), :] = jnp.concatenate(
            [jnp.where(lo, outs[0][0:w], outs[1][0:w]), jnp.where(lo, outs[0][w:2 * w], outs[1][w:2 * w])],
            axis=1).astype(o_ref.dtype)
        return 0

    lax.fori_loop(0, q_ref.shape[0] // w, body, 0, unroll=4)


def _window_attention(layer, bound, sink, qkv_lat, qkv_ctx, batch, tq):
    seq = qkv_lat.shape[0] // batch
    lc = qkv_ctx.shape[0] // batch
    nq = seq // tq

    def call(online):
        return pl.pallas_call(
            functools.partial(_wattn_kernel, layer=layer, online=online),
            out_shape=jax.ShapeDtypeStruct((batch * seq, 256), BF16),
            grid=(batch, nq),
            in_specs=[pl.BlockSpec(memory_space=pltpu.SMEM), pl.BlockSpec(memory_space=pltpu.SMEM),
                      pl.BlockSpec((tq, 256), lambda b, i: (b * nq + i, 0)),
                      pl.BlockSpec((seq, LANES), lambda b, i: (b, 2)),
                      pl.BlockSpec((seq, LANES), lambda b, i: (b, 3)),
                      pl.BlockSpec((lc, LANES), lambda b, i: (b, 2)),
                      pl.BlockSpec((lc, LANES), lambda b, i: (b, 3))],
            out_specs=pl.BlockSpec((tq, 256), lambda b, i: (b * nq + i, 0)),
            scratch_shapes=[pltpu.VMEM((3, WINDOW, 3 * WINDOW), F32)],
            compiler_params=_cparams(("parallel", "parallel"), 32),
            name="window_attention_online" if online else "window_attention",
        )(bound, sink, qkv_lat, qkv_lat, qkv_lat, qkv_ctx, qkv_ctx)

    return lax.cond(bound[layer] <= MAX_SHIFT_BOUND, lambda: call(False), lambda: call(True))


def _cattn_kernel(sink_ref, c_ref, d_ref, oc_ref, od_ref, *, layer):
    t = c_ref.shape[0]
    for ref, out, has_sink in ((c_ref, oc_ref, False), (d_ref, od_ref, True)):
        qs = _stack_heads(ref[:, 0:256])
        s = _dot_nt(qs, ref[:, 256:384])
        m = jnp.max(s, axis=-1, keepdims=True)
        if has_sink:
            sink = _sink_rows(sink_ref, layer, t)
            m = jnp.maximum(m, sink)
        p = jnp.exp2(s - m)
        den = jnp.sum(p, axis=-1, keepdims=True)
        if has_sink:
            den = den + jnp.exp2(sink - m)
        acc = _dot(p.astype(BF16), ref[:, 384:512])
        out[...] = _unstack_heads(acc / den, t).astype(out.dtype)


def _context_attention(layer, sink, qkv_c, qkv_d, batch):
    lc = qkv_c.shape[0] // batch
    blk = lambda width: pl.BlockSpec((lc, width), lambda b: (b, 0))
    return pl.pallas_call(
        functools.partial(_cattn_kernel, layer=layer),
        out_shape=(jax.ShapeDtypeStruct((batch * lc, 256), BF16),) * 2,
        grid=(batch,),
        in_specs=[pl.BlockSpec(memory_space=pltpu.SMEM), blk(512), blk(512)],
        out_specs=(blk(256), blk(256)),
        compiler_params=_cparams(("parallel",), 32),
        name="context_attention",
    )(sink, qkv_c, qkv_d)


def _out_ffn_kernel(x_ref, ya_ref, yb_ref, yc_ref, yd_ref, mod_ref, wo_ref, g_ref, w1_ref, w2_ref, o_ref, *, hchunk):
    m = mod_ref[0]
    gw = GROUP_WIDTH
    mix = _dot(ya_ref[...], wo_ref[0, 0:gw, :])
    for i, y_ref in enumerate((yb_ref, yc_ref, yd_ref), start=1):
        mix = mix + _dot(y_ref[...], wo_ref[0, i * gw:(i + 1) * gw, :])
    x1 = x_ref[...] + m[2:3] * mix
    ms = jnp.mean(x1 * x1, axis=-1, keepdims=True)
    h = ((x1 * lax.rsqrt(ms + EPS) * g_ref[0]) * (1.0 + m[4:5]) + m[3:4]).astype(BF16)
    acc = None
    for c in range(w1_ref.shape[2] // hchunk):
        u = jnp.maximum(_dot(h, w1_ref[0, :, c * hchunk:(c + 1) * hchunk]), 0.0)
        t = _dot((u * u).astype(BF16), w2_ref[0, c * hchunk:(c + 1) * hchunk, :])
        acc = t if acc is None else acc + t
    o_ref[...] = x1 + m[5:6] * acc


def _out_ffn(layer, x2d, ys, mod, mod_row_of_tile, w_out, g, w1, w2, tm):
    rows, d = x2d.shape
    hidden = w1.shape[2]
    row_blk = lambda width: pl.BlockSpec((tm, width), lambda i: (i, 0))
    const = lambda *shape: pl.BlockSpec((1,) + shape, lambda i: (layer,) + (0,) * len(shape), pipeline_mode=pl.Buffered(1))
    return pl.pallas_call(
        functools.partial(_out_ffn_kernel, hchunk=1024),
        out_shape=jax.ShapeDtypeStruct((rows, d), F32),
        grid=(rows // tm,),
        in_specs=[row_blk(d), row_blk(256), row_blk(256), row_blk(256), row_blk(256),
                  pl.BlockSpec((1, N_MOD, d), lambda i: (mod_row_of_tile(i), 0, 0)),
                  const(d, d), pl.BlockSpec((1, 1, d), lambda i: (layer, 0, 0)), const(d, hidden), const(hidden, d)],
        out_specs=row_blk(d),
        compiler_params=_cparams(("parallel",), 56),
        name="out_ffn",
    )(x2d, *ys, mod, w_out, g, w1, w2)


def _permute_q_heads(cols):
    q = cols[..., 0:256].reshape(cols.shape[:-1] + (2, 2, HEAD_DIM))
    q = jnp.swapaxes(q, -3, -2).reshape(cols.shape[:-1] + (256,))
    return jnp.concatenate([q, cols[..., 256:512]], axis=-1)


def _relayout_w_in(w):
    ssd_in = GROUP_WIDTH + SSD_CONV_CH + 2 * SSD_HEADS
    b0, c0, d0 = ssd_in, ssd_in + 512, ssd_in + 1024
    dt = w[..., GROUP_WIDTH + SSD_CONV_CH:ssd_in]
    dt = jnp.pad(dt, [(0, 0)] * (w.ndim - 1) + [(0, LANES - 2 * SSD_HEADS)])
    return jnp.concatenate([w[..., 0:GROUP_WIDTH + SSD_CONV_CH], w[..., b0:c0], _permute_q_heads(w[..., c0:d0]),
                            _permute_q_heads(w[..., d0:d0 + 512]), dt], axis=-1).astype(BF16)


def _relayout_w_out(w):
    depth, _, d = w.shape
    attn = jnp.swapaxes(w[:, 512:1024].reshape(depth, 2, 2, 2, HEAD_DIM, d), 2, 3).reshape(depth, 512, d)
    return jnp.concatenate([w[:, 0:512], attn], axis=1).astype(BF16)


def _block_diag(w):
    eye = jnp.eye(LRU_BLOCKS, dtype=w.dtype)
    out = jnp.einsum('...kij,kn->...kinj', w, eye)
    return out.reshape(w.shape[:-3] + (LRU_WIDTH, LRU_WIDTH))


def _rope_tables(length):
    pos = np.arange(length)
    inv = ROPE_THETA ** (-np.arange(0, ROPE_AXIS_DIM, 2, dtype=np.float64) / ROPE_AXIS_DIM)
    row = (pos // GRID_W)[:, None] * inv
    col = (pos % GRID_W)[:, None] * inv
    ang = np.concatenate([row, row, col, col], axis=1)
    half = ROPE_AXIS_DIM // 2
    sign = np.tile(np.concatenate([-np.ones(half), np.ones(half)]), 2)
    reps = (1, LANES // HEAD_DIM)
    return (jnp.asarray(np.tile(np.cos(ang), reps), F32), jnp.asarray(np.tile(np.sin(ang) * sign, reps), F32))


def _head_selectors():
    sel = np.zeros((2, 2 * LANES, SSD_HEADS * LANES), np.float32)
    for d in range(2):
        for h in range(SSD_HEADS):
            for part in range(2):
                sel[d, part * LANES + SSD_HEADS * d + h, h * LANES:(h + 1) * LANES] = 1.0
    return jnp.asarray(sel, BF16)


def _head_mean_matrix():
    idx = np.arange(256) // HEAD_DIM
    return jnp.asarray((idx[:, None] == idx[None, :]).astype(np.float32) / HEAD_DIM, BF16)


def kernel(x, c, ctx, c_ctx, w_mod, b_mod, g_mix, w_in, ssd_conv_w, ssd_conv_b, ssd_a_log, ssd_dt_bias, ssd_d, ssd_norm_g, lru_conv_w, lru_conv_b, lru_lambda, lru_w_a, lru_b_a, lru_w_i, lru_b_i, gqa_q_norm, gqa_k_norm, swa_q_norm, swa_k_norm, swa_sink, w_out, g_ffn, w_ffn1, w_ffn2):
    batch, seq, d = x.shape
    lc = ctx.shape[1]
    depth = w_mod.shape[0]
    assert batch + 1 <= 8 and seq % 1024 == 0 and lc % 256 == 0 and seq % GRID_W == 0

    cvecs = jnp.concatenate([c, c_ctx[None, :], jnp.zeros((8 - batch - 1, d), F32)], axis=0)
    mod = _modulation(cvecs, w_mod, b_mod).reshape(depth * 8, N_MOD, d)

    rope = _rope_tables(seq)
    sel = _head_selectors()
    bd = _head_mean_matrix()
    conv_sel = _conv_selectors()
    w_in_r = _relayout_w_in(w_in)
    w_out_r = _relayout_w_out(w_out)
    w1, w2 = w_ffn1.astype(BF16), w_ffn2.astype(BF16)
    g1, g2 = g_mix.reshape(depth, 1, d), g_ffn.reshape(depth, 1, d)
    nrm = jnp.tile(jnp.stack([gqa_q_norm, gqa_k_norm, swa_q_norm, swa_k_norm], axis=1), (1, 1, 256 // HEAD_DIM))
    lane_pad = ((0, 0), (0, 0), (0, LANES - 2 * SSD_HEADS))
    head_params = jnp.concatenate([jnp.pad(ssd_dt_bias.reshape(depth, 1, -1), lane_pad),
                                   jnp.pad(ssd_a_log.reshape(depth, 1, -1), lane_pad),
                                   jnp.zeros((depth, 6, LANES), F32)], axis=1)
    ssd_cb = ssd_conv_b.reshape(depth, 1, -1)
    d_skip = jnp.repeat(ssd_d, HEAD_DIM, axis=-1).reshape(depth, 1, -1)
    ssd_ng = ssd_norm_g.reshape(depth, 1, -1)
    bd_a, bd_i = _block_diag(lru_w_a), _block_diag(lru_w_i)
    w_gates = (0.5 * jnp.concatenate([bd_a[:, 0], bd_i[:, 0], bd_a[:, 1], bd_i[:, 1]], axis=-1)).astype(BF16)
    b_gates = 0.5 * jnp.concatenate([lru_b_a[:, 0], lru_b_i[:, 0], lru_b_a[:, 1], lru_b_i[:, 1]], axis=-1).reshape(depth, 1, -1)
    lru_cb = lru_conv_b.reshape(depth, 1, -1)
    lam = lru_lambda.reshape(depth, 1, -1)
    bounds = _logit_bounds(gqa_q_norm, gqa_k_norm)
    swa_bounds = _logit_bounds(swa_q_norm, swa_k_norm)

    xl = x.reshape(batch * seq, d)
    xc = ctx.reshape(batch * lc, d)
    for l in range(depth):
        need_ctx = l < depth - 1
        lat_row = lambda rows: (lambda i, l=l: l * 8 + i // (seq // rows))
        ctx_row = lambda i, l=l: l * 8 + batch
        ssd_l, lru_l, qc_l, qd_l, dt_l = _in_projection(l, xl, mod, lat_row(IN_PROJ_ROWS), g1, w_in_r, nrm, bd, rope,
                                                        IN_PROJ_ROWS, seq // IN_PROJ_ROWS)
        ssd_c, lru_c, qc_c, qd_c, dt_c = _in_projection(l, xc, mod, ctx_row, g1, w_in_r, nrm, bd, None, lc, 1)
        ya_l, ya_c = _ssd_mixer(l, ssd_l, ssd_c, dt_l, dt_c, conv_sel, ssd_conv_w, ssd_cb, head_params, sel, d_skip,
                                ssd_ng, batch)
        yb_l, yb_c = _lru_mixer(l, lru_l, lru_c, conv_sel, lru_conv_w, lru_cb, w_gates, b_gates, lam, batch)
        yc_l = _global_attention(l, bounds, qc_l, qc_c, batch, GATTN_Q_ROWS, GATTN_KV_ROWS)
        yd_l = _window_attention(l, swa_bounds, swa_sink, qd_l, qd_c, batch, seq)
        xl = _out_ffn(l, xl, (ya_l, yb_l, yc_l, yd_l), mod, lat_row(FFN_ROWS), w_out_r, g2, w1, w2, FFN_ROWS)
        if need_ctx:
            yc_c, yd_c = _context_attention(l, swa_sink, qc_c, qd_c, batch)
            xc = _out_ffn(l, xc, (ya_c, yb_c, yc_c, yd_c), mod, ctx_row, w_out_r, g2, w1, w2, FFN_ROWS)
    return xl.reshape(batch, seq, d)
```

```python
import functools
import math

import numpy as np
import jax
import jax.numpy as jnp
from jax import lax
from jax.experimental import pallas as pl
from jax.experimental.pallas import tpu as pltpu

F32 = jnp.float32
BF16 = jnp.bfloat16

EPS = 1e-6
GRID_W = 64
N_MOD = 6
GROUP_WIDTH = 256
SSD_HEADS = 4
SSD_STATE = 128
SSD_CONV = 4
SSD_CHUNK = 128
SSD_CONV_CH = 768
LRU_WIDTH = 256
LRU_BLOCKS = 4
LRU_C = 8.0
HEAD_DIM = 64
Q_HEADS = 4
WINDOW = 128
ROPE_THETA = 10000.0
ROPE_AXIS_DIM = 32
LOG2_E = math.log2(math.e)
Q_SCALE = HEAD_DIM ** -0.5 * LOG2_E

LANES = 128
BF16_ROWS = 16

COL_SSD, COL_LRU, COL_C, COL_D, COL_DT, COL_END = 0, 1024, 1536, 2048, 2560, 2688
Q_HEAD_ORDER = (0, 2, 1, 3)

CONV_WIN = SSD_CHUNK + 2 * BF16_ROWS

IN_PROJ_ROWS = 1024
FFN_ROWS = 512
GATTN_Q_ROWS = 512
GATTN_KV_ROWS = 512
SSD_GROUP = 8


def _cparams(sem, vmem_mib):
    return pltpu.CompilerParams(dimension_semantics=sem, vmem_limit_bytes=vmem_mib * 1024 * 1024)


def _dot(a, b):
    return jnp.dot(a, b, preferred_element_type=F32)


def _dot_nt(a, b):
    return lax.dot_general(a, b, (((1,), (1,)), ((), ())), preferred_element_type=F32)


def _split_bf16(a, parts):
    out = []
    for _ in range(parts - 1):
        hi = a.astype(BF16)
        out.append(hi)
        a = a - hi.astype(F32)
    out.append(a.astype(BF16))
    return out


def _silu(x):
    h = 0.5 * x
    return h + h * jnp.tanh(h)


def _softplus(x):
    return jnp.maximum(x, 0.0) + jnp.log1p(jnp.exp(-jnp.abs(x)))


def _gelu_tanh(x):
    return 0.5 * x * (1.0 + jnp.tanh(math.sqrt(2.0 / math.pi) * (x + 0.044715 * (x * x * x))))


def _lane_iota(shape):
    return lax.broadcasted_iota(jnp.int32, shape, len(shape) - 1)


def _row_iota(shape):
    return lax.broadcasted_iota(jnp.int32, shape, len(shape) - 2)


def _mod_kernel(cv_ref, w_ref, b_ref, o_ref):
    cv = cv_ref[...]
    s = _silu(cv)
    w = w_ref[0]
    s_hi, s_lo = _split_bf16(s, 2)
    w_hi, w_lo = _split_bf16(w, 2)
    acc = _dot(s_hi, w_hi) + _dot(s_lo, w_hi) + _dot(s_hi, w_lo)
    o_ref[0] = acc + b_ref[0]


def _modulation(cvecs, w_mod, b_mod):
    depth, d, n = w_mod.shape
    tn = 1024
    return pl.pallas_call(
        _mod_kernel,
        out_shape=jax.ShapeDtypeStruct((depth, 8, n), F32),
        grid=(depth, n // tn),
        in_specs=[pl.BlockSpec((8, d), lambda l, j: (0, 0)),
                  pl.BlockSpec((1, d, tn), lambda l, j: (l, 0, j)),
                  pl.BlockSpec((1, 1, tn), lambda l, j: (l, 0, j))],
        out_specs=pl.BlockSpec((1, 8, tn), lambda l, j: (l, 0, j)),
        compiler_params=_cparams(("parallel", "parallel"), 32),
        name="modulation",
    )(cvecs, w_mod, b_mod.reshape(depth, 1, n))


def _head_norm_rope(p, g, bd, cos, sin, scale):
    ms = _dot((p * p).astype(BF16), bd)
    y = p * lax.rsqrt(ms + EPS) * g
    if cos is not None:
        w = y.shape[-1]
        first = (_lane_iota(y.shape) % ROPE_AXIS_DIM) < (ROPE_AXIS_DIM // 2)
        partner = jnp.where(first, pltpu.roll(y, w - ROPE_AXIS_DIM // 2, 1), pltpu.roll(y, ROPE_AXIS_DIM // 2, 1))
        y = y * cos + partner * sin
    if scale != 1.0:
        y = y * scale
    return y


def _inproj_kernel(*refs, use_rope):
    if use_rope:
        x_ref, mod_ref, g_ref, w_ref, nrm_ref, bd_ref, cos_ref, sin_ref, ssd_ref, lru_ref, c_ref, d_ref, dt_ref = refs
    else:
        x_ref, mod_ref, g_ref, w_ref, nrm_ref, bd_ref, ssd_ref, lru_ref, c_ref, d_ref, dt_ref = refs
    x = x_ref[...]
    ms = jnp.mean(x * x, axis=-1, keepdims=True)
    m = mod_ref[0]
    h = (x * lax.rsqrt(ms + EPS) * g_ref[0]) * (1.0 + m[1:2]) + m[0:1]
    hb = h.astype(BF16)
    bd = bd_ref[...]
    if use_rope:
        cos1, sin1 = cos_ref[...], sin_ref[...]
        cos2 = jnp.concatenate([cos1, cos1], axis=1)
        sin2 = jnp.concatenate([sin1, sin1], axis=1)
    else:
        cos1 = sin1 = cos2 = sin2 = None
    for o_ref, c0, row in ((c_ref, COL_C, 0), (d_ref, COL_D, 2)):
        p = _dot(hb, w_ref[0, :, c0:c0 + 512])
        q = _head_norm_rope(p[:, 0:256], nrm_ref[0, row:row + 1, :], bd, cos2, sin2, Q_SCALE)
        k = _head_norm_rope(p[:, 256:384], nrm_ref[0, row + 1:row + 2, 0:128], bd[0:128, 0:128], cos1, sin1, 1.0)
        o_ref[:, 0:256] = q.astype(BF16)
        o_ref[:, 256:384] = k.astype(BF16)
        o_ref[:, 384:512] = p[:, 384:512].astype(BF16)
    ssd_ref[...] = _dot(hb, w_ref[0, :, COL_SSD:COL_LRU]).astype(BF16)
    lru_ref[...] = _dot(hb, w_ref[0, :, COL_LRU:COL_C]).astype(BF16)
    dt_ref[...] = _dot(hb, w_ref[0, :, COL_DT:COL_END])


def _in_projection(layer, x2d, mod, mod_row_of_tile, g, w, nrm, bd, rope, tm, rope_tiles):
    rows, d = x2d.shape
    use_rope = rope is not None
    in_specs = [pl.BlockSpec((tm, d), lambda i: (i, 0)),
                pl.BlockSpec((1, N_MOD, d), lambda i: (mod_row_of_tile(i), 0, 0)),
                pl.BlockSpec((1, 1, d), lambda i: (layer, 0, 0)),
                pl.BlockSpec((1, d, COL_END), lambda i: (layer, 0, 0), pipeline_mode=pl.Buffered(1)),
                pl.BlockSpec((1, 4, 256), lambda i: (layer, 0, 0)),
                pl.BlockSpec((256, 256), lambda i: (0, 0))]
    args = [x2d, mod, g, w, nrm, bd]
    if use_rope:
        in_specs += [pl.BlockSpec((tm, LANES), lambda i: (i % rope_tiles, 0))] * 2
        args += list(rope)
    widths = (1024, 512, 512, 512, LANES)
    dtypes = (BF16, BF16, BF16, BF16, F32)
    return pl.pallas_call(
        functools.partial(_inproj_kernel, use_rope=use_rope),
        out_shape=tuple(jax.ShapeDtypeStruct((rows, wd), dt) for wd, dt in zip(widths, dtypes)),
        grid=(rows // tm,),
        in_specs=in_specs,
        out_specs=tuple(pl.BlockSpec((tm, wd), lambda i: (i, 0)) for wd in widths),
        compiler_params=_cparams(("parallel",), 56),
        name="in_projection",
    )(*args)


def _conv_selectors():
    row = np.arange(SSD_CONV * SSD_CHUNK)
    want = (row % SSD_CHUNK) + (row // SSD_CHUNK) - SSD_CONV // 2
    col = np.arange(CONV_WIN)
    sel = np.stack([(col[None, :] == want[:, None] + BF16_ROWS * case) for case in range(3)])
    return jnp.asarray(sel.astype(np.float32), BF16)


def _conv_chunk(seq_ref, col0, width, length, r0, sel_ref, w_ref, b_ref):
    ws = jnp.clip(r0 - BF16_ROWS, 0, length - CONV_WIN)
    ws = pl.multiple_of(ws, BF16_ROWS)
    xw = seq_ref[pl.ds(ws, CONV_WIN), col0:col0 + width]
    sh = _dot(sel_ref[(r0 - ws) // BF16_ROWS], xw)
    w = w_ref[0]
    acc = b_ref[0] + sh[0:SSD_CHUNK] * w[0:1]
    for k in range(1, SSD_CONV):
        acc = acc + sh[k * SSD_CHUNK:(k + 1) * SSD_CHUNK] * w[k:k + 1]
    return acc


def _ssd_kernel(pl_ref, pc_ref, dtl_ref, dtc_ref, csel_ref, cw_ref, cb_ref, hp_ref, sel_ref, dsk_ref, ng_ref,
                yl_ref, yc_ref, accl_ref, accc_ref, xbcl_ref, xbcc_ref):
    q = SSD_CHUNK
    lane8 = _lane_iota((1, LANES)) < 2 * SSD_HEADS
    dt_bias = hp_ref[0, 0:1, :]
    a_row = jnp.where(lane8, -jnp.exp(hp_ref[0, 1:2, :]) * LOG2_E, 0.0)
    ii = _row_iota((q, q))
    jj = _lane_iota((q, q))
    lo = _lane_iota((q, LANES)) < HEAD_DIM
    masks = (jj <= ii, jj >= ii)
    tris = [jnp.where(m, 1.0, 0.0).astype(BF16) for m in masks]
    tri_pairs = [jnp.concatenate([t, t], axis=1) for t in tris]
    zblk = jnp.zeros((SSD_STATE, LANES), BF16)

    def block_diag(a0, a1):
        return jnp.concatenate([jnp.concatenate([a0, zblk], axis=1), jnp.concatenate([zblk, a1], axis=1)], axis=0)

    def chunk(xbc_ref, r0, d, state, acum, dtb, acb):
        xs = xbc_ref[pl.ds(r0, q), 0:256].astype(F32)
        bm = xbc_ref[pl.ds(r0, q), 256:512].astype(F32)
        cmb = xbc_ref[pl.ds(r0, q), 512:768]
        mask = masks[d]
        act = acum.T
        edge = q - 1 if d == 0 else 0
        per_head = []
        for h in range(SSD_HEADS):
            ac = acb[:, h * LANES:(h + 1) * LANES]
            ar = act[SSD_HEADS * d + h:SSD_HEADS * d + h + 1, :]
            lmat = jnp.where(mask, jnp.exp2(ac - ar), 0.0)
            alast = ac[edge:edge + 1, :]
            dth = dtb[:, h * LANES:(h + 1) * LANES]
            per_head.append((lmat, dth, jnp.exp2(alast - ac) * dth, jnp.exp2(ac), jnp.exp2(alast)))
        bts = [bm[:, g * LANES:(g + 1) * LANES].T.astype(BF16) for g in range(2)]
        cb_all = _dot(cmb, block_diag(bts[0], bts[1]))
        yo_all = _dot(cmb, block_diag(state[0].astype(BF16), state[1].astype(BF16)))
        ys, new_state = [], []
        for g in range(2):
            (l0, dt0, we0, ei0, cd0), (l1, dt1, we1, ei1, cd1) = per_head[2 * g], per_head[2 * g + 1]
            xg = xs[:, g * LANES:(g + 1) * LANES]
            cb = cb_all[:, g * LANES:(g + 1) * LANES]
            xdt = xg * jnp.where(lo, dt0, dt1)
            scores = jnp.concatenate([(cb * l0).astype(BF16), (cb * l1).astype(BF16)], axis=1)
            xpair = jnp.concatenate([jnp.where(lo, xdt, 0.0), jnp.where(lo, 0.0, xdt)], axis=0).astype(BF16)
            y = _dot(scores, xpair) + yo_all[:, g * LANES:(g + 1) * LANES] * jnp.where(lo, ei0, ei1)
            s_new = _dot(bts[g], (xg * jnp.where(lo, we0, we1)).astype(BF16))
            new_state.append(state[g] * jnp.where(lo, cd0, cd1) + s_new)
            ys.append(y)
        return jnp.concatenate(ys, axis=1), xs, tuple(new_state)

    def sweep(seq_ref, xbc_ref, dt_ref, acc_ref, out_ref, d, state):
        length = seq_ref.shape[0]
        nc = length // q
        gsz = min(nc, SSD_GROUP)
        ngroups = nc // gsz

        def body(gi, st):
            g0 = pl.multiple_of((gi if d == 0 else ngroups - 1 - gi) * (gsz * q), gsz * q)
            if d == 0:
                for c in range(gsz):
                    conv = _conv_chunk(seq_ref, GROUP_WIDTH, SSD_CONV_CH, length, g0 + c * q, csel_ref, cw_ref, cb_ref)
                    xbc_ref[pl.ds(g0 + c * q, q), :] = _silu(conv).astype(BF16)
            dtv = _softplus(dt_ref[pl.ds(g0, gsz * q), :] + dt_bias)
            hi, lo_part = _split_bf16(dtv * a_row, 2)
            rhs = jnp.concatenate([jnp.concatenate([hi[c * q:(c + 1) * q], lo_part[c * q:(c + 1) * q]], axis=0)
                                   for c in range(gsz)], axis=1)
            acum = _dot(tri_pairs[d], rhs)
            stacked = jnp.concatenate([blk for c in range(gsz)
                                       for blk in (dtv[c * q:(c + 1) * q], acum[:, c * LANES:(c + 1) * LANES])], axis=0)
            ex = _dot(jnp.concatenate(_split_bf16(stacked, 2), axis=1), sel_ref[d])
            for i in range(gsz):
                c = i if d == 0 else gsz - 1 - i
                r0 = g0 + c * q
                y, xs, st = chunk(xbc_ref, r0, d, st, acum[:, c * LANES:(c + 1) * LANES],
                                  ex[2 * c * q:(2 * c + 1) * q], ex[(2 * c + 1) * q:(2 * c + 2) * q])
                if d == 0:
                    acc_ref[pl.ds(r0, q), :] = y
                else:
                    y = acc_ref[pl.ds(r0, q), :] + y + dsk_ref[0] * xs
                    z = seq_ref[pl.ds(r0, q), 0:GROUP_WIDTH].astype(F32)
                    t = y * _silu(z)
                    ms = jnp.mean(t * t, axis=-1, keepdims=True)
                    out_ref[pl.ds(r0, q), :] = (t * lax.rsqrt(ms + EPS) * ng_ref[0]).astype(out_ref.dtype)
            return st

        return lax.fori_loop(0, ngroups, body, state)

    zero = (jnp.zeros((SSD_STATE, LANES), F32), jnp.zeros((SSD_STATE, LANES), F32))
    for d in range(2):
        st = sweep(pc_ref, xbcc_ref, dtc_ref, accc_ref, yc_ref, d, zero)
        sweep(pl_ref, xbcl_ref, dtl_ref, accl_ref, yl_ref, d, st)


def _ssd_mixer(layer, p_lat, p_ctx, dt_lat, dt_ctx, conv_sel, conv_w, conv_b, head_params, sel, d_skip, norm_g, batch):
    seq = p_lat.shape[0] // batch
    lc = p_ctx.shape[0] // batch
    const = lambda *shape: pl.BlockSpec(shape, lambda b: (0,) * len(shape))
    of_layer = lambda *shape: pl.BlockSpec((1,) + shape, lambda b: (layer,) + (0,) * len(shape))
    return pl.pallas_call(
        _ssd_kernel,
        out_shape=(jax.ShapeDtypeStruct((batch * seq, GROUP_WIDTH), BF16),
                   jax.ShapeDtypeStruct((batch * lc, GROUP_WIDTH), BF16)),
        grid=(batch,),
        in_specs=[pl.BlockSpec((seq, 1024), lambda b: (b, 0)),
                  pl.BlockSpec((lc, 1024), lambda b: (b, 0)),
                  pl.BlockSpec((seq, LANES), lambda b: (b, 0)),
                  pl.BlockSpec((lc, LANES), lambda b: (b, 0)),
                  const(3, SSD_CONV * SSD_CHUNK, CONV_WIN),
                  of_layer(SSD_CONV, SSD_CONV_CH), of_layer(1, SSD_CONV_CH), of_layer(8, LANES),
                  const(2, 2 * LANES, SSD_HEADS * LANES), of_layer(1, GROUP_WIDTH), of_layer(1, GROUP_WIDTH)],
        out_specs=(pl.BlockSpec((seq, GROUP_WIDTH), lambda b: (b, 0)),
                   pl.BlockSpec((lc, GROUP_WIDTH), lambda b: (b, 0))),
        scratch_shapes=[pltpu.VMEM((seq, GROUP_WIDTH), F32), pltpu.VMEM((lc, GROUP_WIDTH), F32),
                        pltpu.VMEM((seq, SSD_CONV_CH), BF16), pltpu.VMEM((lc, SSD_CONV_CH), BF16)],
        compiler_params=_cparams(("parallel",), 48),
        name="ssd_mixer",
    )(p_lat, p_ctx, dt_lat, dt_ctx, conv_sel, conv_w, conv_b, head_params, sel, d_skip, norm_g)


LRU_SEGMENTS = 8
LRU_GROUP = 8


def _lru_kernel(pl_ref, pc_ref, csel_ref, cw_ref, cb_ref, wg_ref, bg_ref, lam_ref, ol_ref, oc_ref, a_scr, b_scr, cin_scr):
    q = SSD_CHUNK
    nseg = LRU_SEGMENTS
    nhc = (0.5 * LRU_C) * _softplus(-lam_ref[0])

    def run(seq_ref, out_ref, h0):
        length = seq_ref.shape[0]
        seglen = length // nseg
        sub = min(q, seglen)

        def seg_rows(t0):
            return pl.ds((t0 % seglen) * nseg + t0 // seglen, sub, stride=nseg)

        gsz = min(length // q, LRU_GROUP)

        def gates(gi, _):
            g0 = pl.multiple_of(gi * (gsz * q), gsz * q)
            xcs = [_conv_chunk(seq_ref, LRU_WIDTH, LRU_WIDTH, length, g0 + c * q, csel_ref, cw_ref, cb_ref)
                   for c in range(gsz)]
            xc_all = jnp.concatenate(xcs, axis=0)
            t_all = jnp.tanh(_dot(xc_all.astype(BF16), wg_ref[0]) + bg_ref[0])
            for ci in range(gsz):
                r0 = g0 + ci * q
                t = t_all[ci * q:(ci + 1) * q]
                hx = 0.5 * xcs[ci]
                for d in range(2):
                    c = nhc[:, d * 256:(d + 1) * 256]
                    nla = c * t[:, d * 512:d * 512 + 256] + c
                    a = jnp.exp2(nla * -LOG2_E)
                    y = jnp.tanh(nla) * (a * a + 1.0)
                    root = jnp.where(y > 0.0, y * lax.rsqrt(y), 0.0)
                    bv = root * (hx * t[:, d * 512 + 256:(d + 1) * 512] + hx)
                    for hf in range(2):
                        for j in range(q // sub):
                            rows = seg_rows(r0 + j * sub)
                            a_scr[2 * d + hf, rows, :] = a[j * sub:(j + 1) * sub, hf * LANES:(hf + 1) * LANES]
                            b_scr[2 * d + hf, rows, :] = bv[j * sub:(j + 1) * sub, hf * LANES:(hf + 1) * LANES]
            return 0

        lax.fori_loop(0, length // (gsz * q), gates, 0)

        def scan(k, carry):
            hs, ps = carry
            new_h, new_p = [], []
            for qd in range(4):
                steps = [4 * k + i if qd < 2 else seglen - 1 - (4 * k + i) for i in range(4)]
                idxs = [(qd, pl.ds(pl.multiple_of(st * nseg, nseg), nseg), slice(None)) for st in steps]
                a = [a_scr[ix] for ix in idxs]
                b = [b_scr[ix] for ix in idxs]
                h, p = hs[qd], ps[qd]
                a01, b01 = a[1] * a[0], a[1] * b[0] + b[1]
                a23, b23 = a[3] * a[2], a[3] * b[2] + b[3]
                a03, b03 = a23 * a01, a23 * b01 + b23
                h1 = a[0] * h + b[0]
                h2 = a01 * h + b01
                h3 = a[2] * h2 + b[2]
                h4 = a03 * h + b03
                p2 = p * a01
                for ix, hv, pv in zip(idxs, (h1, h2, h3, h4), (p * a[0], p2, p2 * a[2], p * a03)):
                    b_scr[ix] = hv
                    a_scr[ix] = pv
                new_h.append(h4)
                new_p.append(p * a03)
            return tuple(new_h), tuple(new_p)

        z8 = jnp.zeros((nseg, LANES), F32)
        o8 = jnp.ones((nseg, LANES), F32)
        hend, pend = lax.fori_loop(0, seglen // 4, scan, ((z8,) * 4, (o8,) * 4), unroll=1)

        h_out = []
        for qd in range(4):
            c = h0[qd]
            order = range(nseg) if qd < 2 else range(nseg - 1, -1, -1)
            for k in order:
                cin_scr[qd, k:k + 1, :] = c
                c = hend[qd][k:k + 1, :] + pend[qd][k:k + 1, :] * c
            h_out.append(c)

        cins = [cin_scr[qd] for qd in range(4)]

        def fixup(st, _):
            for qd in range(4):
                idx = (qd, pl.ds(pl.multiple_of(st * nseg, nseg), nseg), slice(None))
                b_scr[idx] = b_scr[idx] + a_scr[idx] * cins[qd]
            return 0

        lax.fori_loop(0, seglen, fixup, 0, unroll=8)

        def emit(c, _):
            r0 = pl.multiple_of(c * sub, sub)
            rows = seg_rows(r0)
            hq = [b_scr[qd, rows, :] for qd in range(4)]
            hsum = jnp.concatenate([hq[0] + hq[2], hq[1] + hq[3]], axis=1)
            gate = seq_ref[pl.ds(r0, sub), 0:LRU_WIDTH].astype(F32)
            out_ref[pl.ds(r0, sub), :] = (hsum * _gelu_tanh(gate)).astype(out_ref.dtype)
            return 0

        lax.fori_loop(0, length // sub, emit, 0)
        return h_out

    z1 = jnp.zeros((1, LANES), F32)
    states = run(pc_ref, oc_ref, [z1] * 4)
    run(pl_ref, ol_ref, states)


def _lru_mixer(layer, p_lat, p_ctx, conv_sel, conv_w, conv_b, w_gates, b_gates, lam, batch):
    seq = p_lat.shape[0] // batch
    lc = p_ctx.shape[0] // batch
    const = lambda *shape: pl.BlockSpec(shape, lambda b: (0,) * len(shape))
    of_layer = lambda *shape: pl.BlockSpec((1,) + shape, lambda b: (layer,) + (0,) * len(shape))
    return pl.pallas_call(
        _lru_kernel,
        out_shape=(jax.ShapeDtypeStruct((batch * seq, LRU_WIDTH), BF16),
                   jax.ShapeDtypeStruct((batch * lc, LRU_WIDTH), BF16)),
        grid=(batch,),
        in_specs=[pl.BlockSpec((seq, 512), lambda b: (b, 0)),
                  pl.BlockSpec((lc, 512), lambda b: (b, 0)),
                  const(3, SSD_CONV * SSD_CHUNK, CONV_WIN),
                  of_layer(SSD_CONV, LRU_WIDTH), of_layer(1, LRU_WIDTH), of_layer(LRU_WIDTH, 1024), of_layer(1, 1024),
                  of_layer(1, 512)],
        out_specs=(pl.BlockSpec((seq, LRU_WIDTH), lambda b: (b, 0)),
                   pl.BlockSpec((lc, LRU_WIDTH), lambda b: (b, 0))),
        scratch_shapes=[pltpu.VMEM((4, seq, LANES), F32), pltpu.VMEM((4, seq, LANES), F32),
                        pltpu.VMEM((4, LRU_SEGMENTS, LANES), F32)],
        compiler_params=_cparams(("parallel",), 48),
        name="lru_mixer",
    )(p_lat, p_ctx, conv_sel, conv_w, conv_b, w_gates, b_gates, lam)


def _stack_heads(qblk):
    lo = _lane_iota((qblk.shape[0], LANES)) < HEAD_DIM
    zero = jnp.zeros((), qblk.dtype)
    s0, s1 = qblk[:, 0:LANES], qblk[:, LANES:2 * LANES]
    return jnp.concatenate([jnp.where(lo, s0, zero), jnp.where(lo, s1, zero),
                            jnp.where(lo, zero, s0), jnp.where(lo, zero, s1)], axis=0)


def _unstack_heads(acc, t):
    lo = _lane_iota((t, LANES)) < HEAD_DIM
    return jnp.concatenate([jnp.where(lo, acc[0:t], acc[2 * t:3 * t]),
                            jnp.where(lo, acc[t:2 * t], acc[3 * t:4 * t])], axis=1)


MAX_SHIFT_BOUND = 50.0


def _gattn_kernel(bound_ref, q_ref, kl_ref, vl_ref, kc_ref, vc_ref, o_ref, qs_scr, m_scr, acc_scr, *, layer, tk, online):
    tq = q_ref.shape[0]
    half = 2 * tq
    qs_scr[...] = _stack_heads(q_ref[...])
    acc_scr[...] = jnp.zeros(acc_scr.shape, F32)
    if online:
        m_scr[...] = jnp.full(m_scr.shape, -jnp.inf, F32)

    def step(k, v):
        lo = _lane_iota(v.shape) < HEAD_DIM
        one = jnp.ones((), v.dtype)
        s = _dot_nt(qs_scr[...], k)
        if online:
            m_old = m_scr[...]
            shift = jnp.maximum(m_old, jnp.max(s, axis=-1, keepdims=True))
            m_scr[...] = shift
            acc_scr[...] = acc_scr[...] * jnp.exp2(m_old - shift)
        else:
            shift = bound_ref[layer]
        p = jnp.exp2(s - shift).astype(BF16)
        acc_scr[0:half, :] += _dot(p[0:half], jnp.where(lo, v, one))
        acc_scr[half:, :] += _dot(p[half:], jnp.where(lo, one, v))

    def body(j, _):
        r0 = pl.multiple_of(j * tk, tk)
        step(kl_ref[pl.ds(r0, tk), :], vl_ref[pl.ds(r0, tk), :])
        return 0

    lax.fori_loop(0, kl_ref.shape[0] // tk, body, 0, unroll=True)
    step(kc_ref[...], vc_ref[...])
    acc = acc_scr[...]
    o_ref[...] = _unstack_heads(acc / pltpu.roll(acc, HEAD_DIM, 1), tq).astype(o_ref.dtype)


def _logit_bounds(q_gain, k_gain):
    return HEAD_DIM * Q_SCALE * 1.02 * jnp.max(jnp.abs(q_gain), axis=-1) * jnp.max(jnp.abs(k_gain), axis=-1)


def _global_attention(layer, bound, qkv_lat, qkv_ctx, batch, tq, tk):
    seq = qkv_lat.shape[0] // batch
    lc = qkv_ctx.shape[0] // batch
    nq = seq // tq

    def call(online):
        return pl.pallas_call(
            functools.partial(_gattn_kernel, layer=layer, tk=tk, online=online),
            out_shape=jax.ShapeDtypeStruct((batch * seq, 256), BF16),
            grid=(batch, nq),
            in_specs=[pl.BlockSpec(memory_space=pltpu.SMEM),
                      pl.BlockSpec((tq, 256), lambda b, i: (b * nq + i, 0)),
                      pl.BlockSpec((seq, LANES), lambda b, i: (b, 2)),
                      pl.BlockSpec((seq, LANES), lambda b, i: (b, 3)),
                      pl.BlockSpec((lc, LANES), lambda b, i: (b, 2)),
                      pl.BlockSpec((lc, LANES), lambda b, i: (b, 3))],
            out_specs=pl.BlockSpec((tq, 256), lambda b, i: (b * nq + i, 0)),
            scratch_shapes=[pltpu.VMEM((4 * tq, LANES), BF16), pltpu.VMEM((4 * tq, 1), F32),
                            pltpu.VMEM((4 * tq, LANES), F32)],
            compiler_params=_cparams(("parallel", "parallel"), 48),
            name="global_attention_online" if online else "global_attention",
        )(bound, qkv_lat, qkv_lat, qkv_lat, qkv_ctx, qkv_ctx)

    return lax.cond(bound[layer] <= MAX_SHIFT_BOUND, lambda: call(False), lambda: call(True))


def _sink_rows(sink_ref, layer, t):
    head_of_row = _row_iota((4 * t, 1)) // t
    out = jnp.zeros((4 * t, 1), F32)
    for h in range(Q_HEADS):
        out = jnp.where(head_of_row == h, sink_ref[layer, h] * LOG2_E, out)
    return out


def _wattn_kernel(bound_ref, sink_ref, q_ref, kl_ref, vl_ref, kc_ref, vc_ref, o_ref, bias_scr, *, layer, online):
    w = WINDOW
    seq = kl_ref.shape[0]
    band = 3 * w
    shift = 0.0 if online else bound_ref[layer]
    dmat = _lane_iota((w, band)) - _row_iota((w, band))
    for case in range(3):
        bias_scr[case] = jnp.where(jnp.abs(dmat - case * w) <= w, -shift, -jnp.inf)
    one = jnp.ones((), BF16)
    kc, vc = kc_ref[...], vc_ref[...]
    lo_c = _lane_iota(vc.shape) < HEAD_DIM
    vc_sum = (jnp.where(lo_c, vc, one), jnp.where(lo_c, one, vc))
    lo_b = _lane_iota((band, LANES)) < HEAD_DIM
    lo = _lane_iota((w, LANES)) < HEAD_DIM

    def body(n, _):
        r0 = pl.multiple_of(n * w, w)
        gq = pl.program_id(1) * q_ref.shape[0] + r0
        ks = pl.multiple_of(jnp.clip(gq - w, 0, seq - band), w)
        bias = bias_scr[(gq - ks) // w]
        kb, vb = kl_ref[pl.ds(ks, band), :], vl_ref[pl.ds(ks, band), :]
        vb_sum = (jnp.where(lo_b, vb, one), jnp.where(lo_b, one, vb))
        qs = _stack_heads(q_ref[pl.ds(r0, w), :])
        s_band = _dot_nt(qs, kb)
        s_ctx = _dot_nt(qs, kc)
        outs = []
        for g in range(2):
            pb, pc, sk = [], [], []
            for h in (2 * g, 2 * g + 1):
                sb = s_band[h * w:(h + 1) * w] + bias
                sc = s_ctx[h * w:(h + 1) * w] - shift
                sink = sink_ref[layer, h] * LOG2_E - shift
                if online:
                    m = jnp.maximum(jnp.maximum(jnp.max(sb, axis=-1, keepdims=True),
                                                jnp.max(sc, axis=-1, keepdims=True)), sink)
                    sb, sc, sink = sb - m, sc - m, sink - m
                pb.append(jnp.exp2(sb).astype(BF16))
                pc.append(jnp.exp2(sc).astype(BF16))
                sk.append(jnp.exp2(sink) if online else jnp.full((w, 1), jnp.exp2(sink), F32))
            acc = _dot(jnp.concatenate(pb, axis=0), vb_sum[g]) + _dot(jnp.concatenate(pc, axis=0), vc_sum[g])
            den = pltpu.roll(acc, HEAD_DIM, 1) + jnp.concatenate(sk, axis=0)
            outs.append(acc / den)
        o_ref[pl.ds(r0, w), :] = jnp.concatenate(
            [jnp.where(lo, outs[0][0:w], outs[1][0:w]), jnp.where(lo, outs[0][w:2 * w], outs[1][w:2 * w])],
            axis=1).astype(o_ref.dtype)
        return 0

    lax.fori_loop(0, q_ref.shape[0] // w, body, 0, unroll=8)


def _window_attention(layer, bound, sink, qkv_lat, qkv_ctx, batch, tq):
    seq = qkv_lat.shape[0] // batch
    lc = qkv_ctx.shape[0] // batch
    nq = seq // tq

    def call(online):
        return pl.pallas_call(
            functools.partial(_wattn_kernel, layer=layer, online=online),
            out_shape=jax.ShapeDtypeStruct((batch * seq, 256), BF16),
            grid=(batch, nq),
            in_specs=[pl.BlockSpec(memory_space=pltpu.SMEM), pl.BlockSpec(memory_space=pltpu.SMEM),
                      pl.BlockSpec((tq, 256), lambda b, i: (b * nq + i, 0)),
                      pl.BlockSpec((seq, LANES), lambda b, i: (b, 2)),
                      pl.BlockSpec((seq, LANES), lambda b, i: (b, 3)),
                      pl.BlockSpec((lc, LANES), lambda b, i: (b, 2)),
                      pl.BlockSpec((lc, LANES), lambda b, i: (b, 3))],
            out_specs=pl.BlockSpec((tq, 256), lambda b, i: (b * nq + i, 0)),
            scratch_shapes=[pltpu.VMEM((3, WINDOW, 3 * WINDOW), F32)],
            compiler_params=_cparams(("parallel", "parallel"), 32),
            name="window_attention_online" if online else "window_attention",
        )(bound, sink, qkv_lat, qkv_lat, qkv_lat, qkv_ctx, qkv_ctx)

    return lax.cond(bound[layer] <= MAX_SHIFT_BOUND, lambda: call(False), lambda: call(True))


def _cattn_kernel(sink_ref, c_ref, d_ref, oc_ref, od_ref, *, layer):
    t = c_ref.shape[0]
    for ref, out, has_sink in ((c_ref, oc_ref, False), (d_ref, od_ref, True)):
        qs = _stack_heads(ref[:, 0:256])
        s = _dot_nt(qs, ref[:, 256:384])
        m = jnp.max(s, axis=-1, keepdims=True)
        if has_sink:
            sink = _sink_rows(sink_ref, layer, t)
            m = jnp.maximum(m, sink)
        p = jnp.exp2(s - m)
        den = jnp.sum(p, axis=-1, keepdims=True)
        if has_sink:
            den = den + jnp.exp2(sink - m)
        acc = _dot(p.astype(BF16), ref[:, 384:512])
        out[...] = _unstack_heads(acc / den, t).astype(out.dtype)


def _context_attention(layer, sink, qkv_c, qkv_d, batch):
    lc = qkv_c.shape[0] // batch
    blk = lambda width: pl.BlockSpec((lc, width), lambda b: (b, 0))
    return pl.pallas_call(
        functools.partial(_cattn_kernel, layer=layer),
        out_shape=(jax.ShapeDtypeStruct((batch * lc, 256), BF16),) * 2,
        grid=(batch,),
        in_specs=[pl.BlockSpec(memory_space=pltpu.SMEM), blk(512), blk(512)],
        out_specs=(blk(256), blk(256)),
        compiler_params=_cparams(("parallel",), 32),
        name="context_attention",
    )(sink, qkv_c, qkv_d)


def _out_ffn_kernel(x_ref, ya_ref, yb_ref, yc_ref, yd_ref, mod_ref, wo_ref, g_ref, w1_ref, w2_ref, o_ref, *, hchunk):
    m = mod_ref[0]
    gw = GROUP_WIDTH
    mix = _dot(ya_ref[...], wo_ref[0, 0:gw, :])
    for i, y_ref in enumerate((yb_ref, yc_ref, yd_ref), start=1):
        mix = mix + _dot(y_ref[...], wo_ref[0, i * gw:(i + 1) * gw, :])
    x1 = x_ref[...] + m[2:3] * mix
    ms = jnp.mean(x1 * x1, axis=-1, keepdims=True)
    h = ((x1 * lax.rsqrt(ms + EPS) * g_ref[0]) * (1.0 + m[4:5]) + m[3:4]).astype(BF16)
    acc = None
    for c in range(w1_ref.shape[2] // hchunk):
        u = jnp.maximum(_dot(h, w1_ref[0, :, c * hchunk:(c + 1) * hchunk]), 0.0)
        t = _dot((u * u).astype(BF16), w2_ref[0, c * hchunk:(c + 1) * hchunk, :])
        acc = t if acc is None else acc + t
    o_ref[...] = x1 + m[5:6] * acc


def _out_ffn(layer, x2d, ys, mod, mod_row_of_tile, w_out, g, w1, w2, tm):
    rows, d = x2d.shape
    hidden = w1.shape[2]
    row_blk = lambda width: pl.BlockSpec((tm, width), lambda i: (i, 0))
    const = lambda *shape: pl.BlockSpec((1,) + shape, lambda i: (layer,) + (0,) * len(shape), pipeline_mode=pl.Buffered(1))
    return pl.pallas_call(
        functools.partial(_out_ffn_kernel, hchunk=1024),
        out_shape=jax.ShapeDtypeStruct((rows, d), F32),
        grid=(rows // tm,),
        in_specs=[row_blk(d), row_blk(256), row_blk(256), row_blk(256), row_blk(256),
                  pl.BlockSpec((1, N_MOD, d), lambda i: (mod_row_of_tile(i), 0, 0)),
                  const(d, d), pl.BlockSpec((1, 1, d), lambda i: (layer, 0, 0)), const(d, hidden), const(hidden, d)],
        out_specs=row_blk(d),
        compiler_params=_cparams(("parallel",), 56),
        name="out_ffn",
    )(x2d, *ys, mod, w_out, g, w1, w2)


def _permute_q_heads(cols):
    q = cols[..., 0:256].reshape(cols.shape[:-1] + (2, 2, HEAD_DIM))
    q = jnp.swapaxes(q, -3, -2).reshape(cols.shape[:-1] + (256,))
    return jnp.concatenate([q, cols[..., 256:512]], axis=-1)


def _relayout_w_in(w):
    ssd_in = GROUP_WIDTH + SSD_CONV_CH + 2 * SSD_HEADS
    b0, c0, d0 = ssd_in, ssd_in + 512, ssd_in + 1024
    dt = w[..., GROUP_WIDTH + SSD_CONV_CH:ssd_in]
    dt = jnp.pad(dt, [(0, 0)] * (w.ndim - 1) + [(0, LANES - 2 * SSD_HEADS)])
    return jnp.concatenate([w[..., 0:GROUP_WIDTH + SSD_CONV_CH], w[..., b0:c0], _permute_q_heads(w[..., c0:d0]),
                            _permute_q_heads(w[..., d0:d0 + 512]), dt], axis=-1).astype(BF16)


def _relayout_w_out(w):
    depth, _, d = w.shape
    attn = jnp.swapaxes(w[:, 512:1024].reshape(depth, 2, 2, 2, HEAD_DIM, d), 2, 3).reshape(depth, 512, d)
    return jnp.concatenate([w[:, 0:512], attn], axis=1).astype(BF16)


def _block_diag(w):
    eye = jnp.eye(LRU_BLOCKS, dtype=w.dtype)
    out = jnp.einsum('...kij,kn->...kinj', w, eye)
    return out.reshape(w.shape[:-3] + (LRU_WIDTH, LRU_WIDTH))


def _rope_tables(length):
    pos = np.arange(length)
    inv = ROPE_THETA ** (-np.arange(0, ROPE_AXIS_DIM, 2, dtype=np.float64) / ROPE_AXIS_DIM)
    row = (pos // GRID_W)[:, None] * inv
    col = (pos % GRID_W)[:, None] * inv
    ang = np.concatenate([row, row, col, col], axis=1)
    half = ROPE_AXIS_DIM // 2
    sign = np.tile(np.concatenate([-np.ones(half), np.ones(half)]), 2)
    reps = (1, LANES // HEAD_DIM)
    return (jnp.asarray(np.tile(np.cos(ang), reps), F32), jnp.asarray(np.tile(np.sin(ang) * sign, reps), F32))


def _head_selectors():
    sel = np.zeros((2, 2 * LANES, SSD_HEADS * LANES), np.float32)
    for d in range(2):
        for h in range(SSD_HEADS):
            for part in range(2):
                sel[d, part * LANES + SSD_HEADS * d + h, h * LANES:(h + 1) * LANES] = 1.0
    return jnp.asarray(sel, BF16)


def _head_mean_matrix():
    idx = np.arange(256) // HEAD_DIM
    return jnp.asarray((idx[:, None] == idx[None, :]).astype(np.float32) / HEAD_DIM, BF16)


def kernel(x, c, ctx, c_ctx, w_mod, b_mod, g_mix, w_in, ssd_conv_w, ssd_conv_b, ssd_a_log, ssd_dt_bias, ssd_d, ssd_norm_g, lru_conv_w, lru_conv_b, lru_lambda, lru_w_a, lru_b_a, lru_w_i, lru_b_i, gqa_q_norm, gqa_k_norm, swa_q_norm, swa_k_norm, swa_sink, w_out, g_ffn, w_ffn1, w_ffn2):
    batch, seq, d = x.shape
    lc = ctx.shape[1]
    depth = w_mod.shape[0]
    assert batch + 1 <= 8 and seq % 1024 == 0 and lc % 256 == 0 and seq % GRID_W == 0

    cvecs = jnp.concatenate([c, c_ctx[None, :], jnp.zeros((8 - batch - 1, d), F32)], axis=0)
    mod = _modulation(cvecs, w_mod, b_mod).reshape(depth * 8, N_MOD, d)

    rope = _rope_tables(seq)
    sel = _head_selectors()
    bd = _head_mean_matrix()
    conv_sel = _conv_selectors()
    w_in_r = _relayout_w_in(w_in)
    w_out_r = _relayout_w_out(w_out)
    w1, w2 = w_ffn1.astype(BF16), w_ffn2.astype(BF16)
    g1, g2 = g_mix.reshape(depth, 1, d), g_ffn.reshape(depth, 1, d)
    nrm = jnp.tile(jnp.stack([gqa_q_norm, gqa_k_norm, swa_q_norm, swa_k_norm], axis=1), (1, 1, 256 // HEAD_DIM))
    lane_pad = ((0, 0), (0, 0), (0, LANES - 2 * SSD_HEADS))
    head_params = jnp.concatenate([jnp.pad(ssd_dt_bias.reshape(depth, 1, -1), lane_pad),
                                   jnp.pad(ssd_a_log.reshape(depth, 1, -1), lane_pad),
                                   jnp.zeros((depth, 6, LANES), F32)], axis=1)
    ssd_cb = ssd_conv_b.reshape(depth, 1, -1)
    d_skip = jnp.repeat(ssd_d, HEAD_DIM, axis=-1).reshape(depth, 1, -1)
    ssd_ng = ssd_norm_g.reshape(depth, 1, -1)
    bd_a, bd_i = _block_diag(lru_w_a), _block_diag(lru_w_i)
    w_gates = (0.5 * jnp.concatenate([bd_a[:, 0], bd_i[:, 0], bd_a[:, 1], bd_i[:, 1]], axis=-1)).astype(BF16)
    b_gates = 0.5 * jnp.concatenate([lru_b_a[:, 0], lru_b_i[:, 0], lru_b_a[:, 1], lru_b_i[:, 1]], axis=-1).reshape(depth, 1, -1)
    lru_cb = lru_conv_b.reshape(depth, 1, -1)
    lam = lru_lambda.reshape(depth, 1, -1)
    bounds = _logit_bounds(gqa_q_norm, gqa_k_norm)
    swa_bounds = _logit_bounds(swa_q_norm, swa_k_norm)

    xl = x.reshape(batch * seq, d)
    xc = ctx.reshape(batch * lc, d)
    for l in range(depth):
        need_ctx = l < depth - 1
        lat_row = lambda rows: (lambda i, l=l: l * 8 + i // (seq // rows))
        ctx_row = lambda i, l=l: l * 8 + batch
        ssd_l, lru_l, qc_l, qd_l, dt_l = _in_projection(l, xl, mod, lat_row(IN_PROJ_ROWS), g1, w_in_r, nrm, bd, rope,
                                                        IN_PROJ_ROWS, seq // IN_PROJ_ROWS)
        ssd_c, lru_c, qc_c, qd_c, dt_c = _in_projection(l, xc, mod, ctx_row, g1, w_in_r, nrm, bd, None, lc, 1)
        ya_l, ya_c = _ssd_mixer(l, ssd_l, ssd_c, dt_l, dt_c, conv_sel, ssd_conv_w, ssd_cb, head_params, sel, d_skip,
                                ssd_ng, batch)
        yb_l, yb_c = _lru_mixer(l, lru_l, lru_c, conv_sel, lru_conv_w, lru_cb, w_gates, b_gates, lam, batch)
        yc_l = _global_attention(l, bounds, qc_l, qc_c, batch, GATTN_Q_ROWS, GATTN_KV_ROWS)
        yd_l = _window_attention(l, swa_bounds, swa_sink, qd_l, qd_c, batch, seq)
        xl = _out_ffn(l, xl, (ya_l, yb_l, yc_l, yd_l), mod, lat_row(FFN_ROWS), w_out_r, g2, w1, w2, FFN_ROWS)
        if need_ctx:
            yc_c, yd_c = _context_attention(l, swa_sink, qc_c, qd_c, batch)
            xc = _out_ffn(l, xc, (ya_c, yb_c, yc_c, yd_c), mod, ctx_row, w_out_r, g2, w1, w2, FFN_ROWS)
    return xl.reshape(batch, seq, d)
```

```python
import functools
import math

import numpy as np
import jax
import jax.numpy as jnp
from jax import lax
from jax.experimental import pallas as pl
from jax.experimental.pallas import tpu as pltpu

F32 = jnp.float32
BF16 = jnp.bfloat16

EPS = 1e-6
GRID_W = 64
N_MOD = 6
GROUP_WIDTH = 256
SSD_HEADS = 4
SSD_STATE = 128
SSD_CONV = 4
SSD_CHUNK = 128
SSD_CONV_CH = 768
LRU_WIDTH = 256
LRU_BLOCKS = 4
LRU_C = 8.0
HEAD_DIM = 64
Q_HEADS = 4
WINDOW = 128
ROPE_THETA = 10000.0
ROPE_AXIS_DIM = 32
LOG2_E = math.log2(math.e)
Q_SCALE = HEAD_DIM ** -0.5 * LOG2_E

LANES = 128
BF16_ROWS = 16

COL_SSD, COL_LRU, COL_C, COL_D, COL_DT, COL_END = 0, 1024, 1536, 2048, 2560, 2688

CONV_WIN = SSD_CHUNK + 2 * BF16_ROWS

IN_PROJ_ROWS = 1024
FFN_ROWS = 512
GATTN_Q_ROWS = 512
GATTN_KV_ROWS = 512
SSD_GROUP = 8


def _cparams(sem, vmem_mib):
    return pltpu.CompilerParams(dimension_semantics=sem, vmem_limit_bytes=vmem_mib * 1024 * 1024)


def _dot(a, b):
    return jnp.dot(a, b, preferred_element_type=F32)


def _dot_nt(a, b):
    return lax.dot_general(a, b, (((1,), (1,)), ((), ())), preferred_element_type=F32)


def _split_bf16(a, parts):
    out = []
    for _ in range(parts - 1):
        hi = a.astype(BF16)
        out.append(hi)
        a = a - hi.astype(F32)
    out.append(a.astype(BF16))
    return out


def _silu(x):
    h = 0.5 * x
    return h + h * jnp.tanh(h)


def _softplus(x):
    return jnp.maximum(x, 0.0) + jnp.log1p(jnp.exp(-jnp.abs(x)))


def _gelu_tanh(x):
    return 0.5 * x * (1.0 + jnp.tanh(math.sqrt(2.0 / math.pi) * (x + 0.044715 * (x * x * x))))


def _lane_iota(shape):
    return lax.broadcasted_iota(jnp.int32, shape, len(shape) - 1)


def _row_iota(shape):
    return lax.broadcasted_iota(jnp.int32, shape, len(shape) - 2)


def _mod_kernel(cv_ref, w_ref, b_ref, o_ref):
    cv = cv_ref[...]
    s = _silu(cv)
    w = w_ref[0]
    s_hi, s_lo = _split_bf16(s, 2)
    w_hi, w_lo = _split_bf16(w, 2)
    acc = _dot(s_hi, w_hi) + _dot(s_lo, w_hi) + _dot(s_hi, w_lo)
    o_ref[0] = acc + b_ref[0]


def _modulation(cvecs, w_mod, b_mod):
    depth, d, n = w_mod.shape
    tn = 1024
    return pl.pallas_call(
        _mod_kernel,
        out_shape=jax.ShapeDtypeStruct((depth, 8, n), F32),
        grid=(depth, n // tn),
        in_specs=[pl.BlockSpec((8, d), lambda l, j: (0, 0)),
                  pl.BlockSpec((1, d, tn), lambda l, j: (l, 0, j)),
                  pl.BlockSpec((1, 1, tn), lambda l, j: (l, 0, j))],
        out_specs=pl.BlockSpec((1, 8, tn), lambda l, j: (l, 0, j)),
        compiler_params=_cparams(("parallel", "parallel"), 32),
        name="modulation",
    )(cvecs, w_mod, b_mod.reshape(depth, 1, n))


def _head_norm_rope(p, g, bd, cos, sin, scale):
    ms = _dot((p * p).astype(BF16), bd)
    y = p * lax.rsqrt(ms + EPS) * g
    if cos is not None:
        w = y.shape[-1]
        first = (_lane_iota(y.shape) % ROPE_AXIS_DIM) < (ROPE_AXIS_DIM // 2)
        partner = jnp.where(first, pltpu.roll(y, w - ROPE_AXIS_DIM // 2, 1), pltpu.roll(y, ROPE_AXIS_DIM // 2, 1))
        y = y * cos + partner * sin
    if scale != 1.0:
        y = y * scale
    return y


def _inproj_kernel(*refs, use_rope):
    if use_rope:
        x_ref, mod_ref, g_ref, w_ref, nrm_ref, bd_ref, cos_ref, sin_ref, ssd_ref, lru_ref, c_ref, d_ref, dt_ref = refs
    else:
        x_ref, mod_ref, g_ref, w_ref, nrm_ref, bd_ref, ssd_ref, lru_ref, c_ref, d_ref, dt_ref = refs
    x = x_ref[...]
    ms = jnp.mean(x * x, axis=-1, keepdims=True)
    m = mod_ref[0]
    h = (x * lax.rsqrt(ms + EPS) * g_ref[0]) * (1.0 + m[1:2]) + m[0:1]
    hb = h.astype(BF16)
    bd = bd_ref[...]
    if use_rope:
        cos1, sin1 = cos_ref[...], sin_ref[...]
        cos2 = jnp.concatenate([cos1, cos1], axis=1)
        sin2 = jnp.concatenate([sin1, sin1], axis=1)
    else:
        cos1 = sin1 = cos2 = sin2 = None
    for o_ref, c0, row in ((c_ref, COL_C, 0), (d_ref, COL_D, 2)):
        p = _dot(hb, w_ref[0, :, c0:c0 + 512])
        q = _head_norm_rope(p[:, 0:256], nrm_ref[0, row:row + 1, :], bd, cos2, sin2, Q_SCALE)
        k = _head_norm_rope(p[:, 256:384], nrm_ref[0, row + 1:row + 2, 0:128], bd[0:128, 0:128], cos1, sin1, 1.0)
        o_ref[:, 0:256] = q.astype(BF16)
        o_ref[:, 256:384] = k.astype(BF16)
        o_ref[:, 384:512] = p[:, 384:512].astype(BF16)
    ssd_ref[...] = _dot(hb, w_ref[0, :, COL_SSD:COL_LRU]).astype(BF16)
    lru_ref[...] = _dot(hb, w_ref[0, :, COL_LRU:COL_C]).astype(BF16)
    dt_ref[...] = _dot(hb, w_ref[0, :, COL_DT:COL_END])


def _in_projection(layer, x2d, mod, mod_row_of_tile, g, w, nrm, bd, rope, tm, rope_tiles):
    rows, d = x2d.shape
    use_rope = rope is not None
    in_specs = [pl.BlockSpec((tm, d), lambda i: (i, 0)),
                pl.BlockSpec((1, N_MOD, d), lambda i: (mod_row_of_tile(i), 0, 0)),
                pl.BlockSpec((1, 1, d), lambda i: (layer, 0, 0)),
                pl.BlockSpec((1, d, COL_END), lambda i: (layer, 0, 0), pipeline_mode=pl.Buffered(1)),
                pl.BlockSpec((1, 4, 256), lambda i: (layer, 0, 0)),
                pl.BlockSpec((256, 256), lambda i: (0, 0))]
    args = [x2d, mod, g, w, nrm, bd]
    if use_rope:
        in_specs += [pl.BlockSpec((tm, LANES), lambda i: (i % rope_tiles, 0))] * 2
        args += list(rope)
    widths = (1024, 512, 512, 512, LANES)
    dtypes = (BF16, BF16, BF16, BF16, F32)
    return pl.pallas_call(
        functools.partial(_inproj_kernel, use_rope=use_rope),
        out_shape=tuple(jax.ShapeDtypeStruct((rows, wd), dt) for wd, dt in zip(widths, dtypes)),
        grid=(rows // tm,),
        in_specs=in_specs,
        out_specs=tuple(pl.BlockSpec((tm, wd), lambda i: (i, 0)) for wd in widths),
        compiler_params=_cparams(("parallel",), 56),
        name="in_projection",
    )(*args)


def _conv_selectors():
    row = np.arange(SSD_CONV * SSD_CHUNK)
    want = (row % SSD_CHUNK) + (row // SSD_CHUNK) - SSD_CONV // 2
    col = np.arange(CONV_WIN)
    sel = np.stack([(col[None, :] == want[:, None] + BF16_ROWS * case) for case in range(3)])
    return jnp.asarray(sel.astype(np.float32), BF16)


def _conv_chunk(seq_ref, col0, width, length, r0, sel_ref, w_ref, b_ref):
    ws = jnp.clip(r0 - BF16_ROWS, 0, length - CONV_WIN)
    ws = pl.multiple_of(ws, BF16_ROWS)
    xw = seq_ref[pl.ds(ws, CONV_WIN), col0:col0 + width]
    sh = _dot(sel_ref[(r0 - ws) // BF16_ROWS], xw)
    w = w_ref[0]
    acc = b_ref[0] + sh[0:SSD_CHUNK] * w[0:1]
    for k in range(1, SSD_CONV):
        acc = acc + sh[k * SSD_CHUNK:(k + 1) * SSD_CHUNK] * w[k:k + 1]
    return acc


def _ssd_kernel(pl_ref, pc_ref, dtl_ref, dtc_ref, csel_ref, cw_ref, cb_ref, hp_ref, sel_ref, dsk_ref, ng_ref,
                yl_ref, yc_ref, accl_ref, accc_ref, xbcl_ref, xbcc_ref):
    q = SSD_CHUNK
    lane8 = _lane_iota((1, LANES)) < 2 * SSD_HEADS
    dt_bias = hp_ref[0, 0:1, :]
    a_row = jnp.where(lane8, -jnp.exp(hp_ref[0, 1:2, :]) * LOG2_E, 0.0)
    ii = _row_iota((q, q))
    jj = _lane_iota((q, q))
    lo = _lane_iota((q, LANES)) < HEAD_DIM
    masks = (jj <= ii, jj >= ii)
    tris = [jnp.where(m, 1.0, 0.0).astype(BF16) for m in masks]
    tri_pairs = [jnp.concatenate([t, t], axis=1) for t in tris]
    zblk = jnp.zeros((SSD_STATE, LANES), BF16)

    def block_diag(a0, a1):
        return jnp.concatenate([jnp.concatenate([a0, zblk], axis=1), jnp.concatenate([zblk, a1], axis=1)], axis=0)

    def chunk(xbc_ref, r0, d, state, acum, dtb, acb):
        xs = xbc_ref[pl.ds(r0, q), 0:256].astype(F32)
        bm = xbc_ref[pl.ds(r0, q), 256:512].astype(F32)
        cmb = xbc_ref[pl.ds(r0, q), 512:768]
        mask = masks[d]
        act = acum.T
        edge = q - 1 if d == 0 else 0
        per_head = []
        for h in range(SSD_HEADS):
            ac = acb[:, h * LANES:(h + 1) * LANES]
            ar = act[SSD_HEADS * d + h:SSD_HEADS * d + h + 1, :]
            lmat = jnp.where(mask, jnp.exp2(ac - ar), 0.0)
            alast = ac[edge:edge + 1, :]
            dth = dtb[:, h * LANES:(h + 1) * LANES]
            per_head.append((lmat, dth, jnp.exp2(alast - ac) * dth, jnp.exp2(ac), jnp.exp2(alast)))
        bts = [bm[:, g * LANES:(g + 1) * LANES].T.astype(BF16) for g in range(2)]
        cb_all = _dot(cmb, block_diag(bts[0], bts[1]))
        yo_all = _dot(cmb, block_diag(state[0].astype(BF16), state[1].astype(BF16)))
        ys, new_state = [], []
        for g in range(2):
            (l0, dt0, we0, ei0, cd0), (l1, dt1, we1, ei1, cd1) = per_head[2 * g], per_head[2 * g + 1]
            xg = xs[:, g * LANES:(g + 1) * LANES]
            cb = cb_all[:, g * LANES:(g + 1) * LANES]
            xdt = xg * jnp.where(lo, dt0, dt1)
            scores = jnp.concatenate([(cb * l0).astype(BF16), (cb * l1).astype(BF16)], axis=1)
            xpair = jnp.concatenate([jnp.where(lo, xdt, 0.0), jnp.where(lo, 0.0, xdt)], axis=0).astype(BF16)
            y = _dot(scores, xpair) + yo_all[:, g * LANES:(g + 1) * LANES] * jnp.where(lo, ei0, ei1)
            s_new = _dot(bts[g], (xg * jnp.where(lo, we0, we1)).astype(BF16))
            new_state.append(state[g] * jnp.where(lo, cd0, cd1) + s_new)
            ys.append(y)
        return jnp.concatenate(ys, axis=1), xs, tuple(new_state)

    def sweep(seq_ref, xbc_ref, dt_ref, acc_ref, out_ref, d, state):
        length = seq_ref.shape[0]
        nc = length // q
        gsz = min(nc, SSD_GROUP)
        ngroups = nc // gsz

        def body(gi, st):
            g0 = pl.multiple_of((gi if d == 0 else ngroups - 1 - gi) * (gsz * q), gsz * q)
            if d == 0:
                for c in range(gsz):
                    conv = _conv_chunk(seq_ref, GROUP_WIDTH, SSD_CONV_CH, length, g0 + c * q, csel_ref, cw_ref, cb_ref)
                    xbc_ref[pl.ds(g0 + c * q, q), :] = _silu(conv).astype(BF16)
            dtv = _softplus(dt_ref[pl.ds(g0, gsz * q), :] + dt_bias)
            hi, lo_part = _split_bf16(dtv * a_row, 2)
            rhs = jnp.concatenate([jnp.concatenate([hi[c * q:(c + 1) * q], lo_part[c * q:(c + 1) * q]], axis=0)
                                   for c in range(gsz)], axis=1)
            acum = _dot(tri_pairs[d], rhs)
            stacked = jnp.concatenate([blk for c in range(gsz)
                                       for blk in (dtv[c * q:(c + 1) * q], acum[:, c * LANES:(c + 1) * LANES])], axis=0)
            ex = _dot(jnp.concatenate(_split_bf16(stacked, 2), axis=1), sel_ref[d])
            for i in range(gsz):
                c = i if d == 0 else gsz - 1 - i
                r0 = g0 + c * q
                y, xs, st = chunk(xbc_ref, r0, d, st, acum[:, c * LANES:(c + 1) * LANES],
                                  ex[2 * c * q:(2 * c + 1) * q], ex[(2 * c + 1) * q:(2 * c + 2) * q])
                if d == 0:
                    acc_ref[pl.ds(r0, q), :] = y
                else:
                    y = acc_ref[pl.ds(r0, q), :] + y + dsk_ref[0] * xs
                    z = seq_ref[pl.ds(r0, q), 0:GROUP_WIDTH].astype(F32)
                    t = y * _silu(z)
                    ms = jnp.mean(t * t, axis=-1, keepdims=True)
                    out_ref[pl.ds(r0, q), :] = (t * lax.rsqrt(ms + EPS) * ng_ref[0]).astype(out_ref.dtype)
            return st

        return lax.fori_loop(0, ngroups, body, state)

    zero = (jnp.zeros((SSD_STATE, LANES), F32), jnp.zeros((SSD_STATE, LANES), F32))
    for d in range(2):
        st = sweep(pc_ref, xbcc_ref, dtc_ref, accc_ref, yc_ref, d, zero)
        sweep(pl_ref, xbcl_ref, dtl_ref, accl_ref, yl_ref, d, st)


def _ssd_mixer(layer, p_lat, p_ctx, dt_lat, dt_ctx, conv_sel, conv_w, conv_b, head_params, sel, d_skip, norm_g, batch):
    seq = p_lat.shape[0] // batch
    lc = p_ctx.shape[0] // batch
    const = lambda *shape: pl.BlockSpec(shape, lambda b: (0,) * len(shape))
    of_layer = lambda *shape: pl.BlockSpec((1,) + shape, lambda b: (layer,) + (0,) * len(shape))
    return pl.pallas_call(
        _ssd_kernel,
        out_shape=(jax.ShapeDtypeStruct((batch * seq, GROUP_WIDTH), BF16),
                   jax.ShapeDtypeStruct((batch * lc, GROUP_WIDTH), BF16)),
        grid=(batch,),
        in_specs=[pl.BlockSpec((seq, 1024), lambda b: (b, 0)),
                  pl.BlockSpec((lc, 1024), lambda b: (b, 0)),
                  pl.BlockSpec((seq, LANES), lambda b: (b, 0)),
                  pl.BlockSpec((lc, LANES), lambda b: (b, 0)),
                  const(3, SSD_CONV * SSD_CHUNK, CONV_WIN),
                  of_layer(SSD_CONV, SSD_CONV_CH), of_layer(1, SSD_CONV_CH), of_layer(8, LANES),
                  const(2, 2 * LANES, SSD_HEADS * LANES), of_layer(1, GROUP_WIDTH), of_layer(1, GROUP_WIDTH)],
        out_specs=(pl.BlockSpec((seq, GROUP_WIDTH), lambda b: (b, 0)),
                   pl.BlockSpec((lc, GROUP_WIDTH), lambda b: (b, 0))),
        scratch_shapes=[pltpu.VMEM((seq, GROUP_WIDTH), F32), pltpu.VMEM((lc, GROUP_WIDTH), F32),
                        pltpu.VMEM((seq, SSD_CONV_CH), BF16), pltpu.VMEM((lc, SSD_CONV_CH), BF16)],
        compiler_params=_cparams(("parallel",), 48),
        name="ssd_mixer",
    )(p_lat, p_ctx, dt_lat, dt_ctx, conv_sel, conv_w, conv_b, head_params, sel, d_skip, norm_g)


LRU_SEGMENTS = 8
LRU_GROUP = 8


def _lru_kernel(pl_ref, pc_ref, csel_ref, cw_ref, cb_ref, wg_ref, bg_ref, lam_ref, ol_ref, oc_ref, a_scr, b_scr, cin_scr):
    q = SSD_CHUNK
    nseg = LRU_SEGMENTS
    nhc = (0.5 * LRU_C) * _softplus(-lam_ref[0])

    def run(seq_ref, out_ref, h0):
        length = seq_ref.shape[0]
        seglen = length // nseg
        sub = min(q, seglen)

        def seg_rows(t0):
            return pl.ds((t0 % seglen) * nseg + t0 // seglen, sub, stride=nseg)

        gsz = min(length // q, LRU_GROUP)

        def gates(gi, _):
            g0 = pl.multiple_of(gi * (gsz * q), gsz * q)
            xcs = [_conv_chunk(seq_ref, LRU_WIDTH, LRU_WIDTH, length, g0 + c * q, csel_ref, cw_ref, cb_ref)
                   for c in range(gsz)]
            xc_all = jnp.concatenate(xcs, axis=0)
            t_all = jnp.tanh(_dot(xc_all.astype(BF16), wg_ref[0]) + bg_ref[0])
            for ci in range(gsz):
                r0 = g0 + ci * q
                t = t_all[ci * q:(ci + 1) * q]
                hx = 0.5 * xcs[ci]
                for d in range(2):
                    c = nhc[:, d * 256:(d + 1) * 256]
                    nla = c * t[:, d * 512:d * 512 + 256] + c
                    a = jnp.exp2(nla * -LOG2_E)
                    y = jnp.tanh(nla) * (a * a + 1.0)
                    root = jnp.where(y > 0.0, y * lax.rsqrt(y), 0.0)
                    bv = root * (hx * t[:, d * 512 + 256:(d + 1) * 512] + hx)
                    for hf in range(2):
                        for j in range(q // sub):
                            rows = seg_rows(r0 + j * sub)
                            a_scr[2 * d + hf, rows, :] = a[j * sub:(j + 1) * sub, hf * LANES:(hf + 1) * LANES]
                            b_scr[2 * d + hf, rows, :] = bv[j * sub:(j + 1) * sub, hf * LANES:(hf + 1) * LANES]
            return 0

        lax.fori_loop(0, length // (gsz * q), gates, 0)

        def scan(k, carry):
            hs, ps = carry
            new_h, new_p = [], []
            for qd in range(4):
                steps = [4 * k + i if qd < 2 else seglen - 1 - (4 * k + i) for i in range(4)]
                idxs = [(qd, pl.ds(pl.multiple_of(st * nseg, nseg), nseg), slice(None)) for st in steps]
                a = [a_scr[ix] for ix in idxs]
                b = [b_scr[ix] for ix in idxs]
                h, p = hs[qd], ps[qd]
                a01, b01 = a[1] * a[0], a[1] * b[0] + b[1]
                a23, b23 = a[3] * a[2], a[3] * b[2] + b[3]
                a03, b03 = a23 * a01, a23 * b01 + b23
                h1 = a[0] * h + b[0]
                h2 = a01 * h + b01
                h3 = a[2] * h2 + b[2]
                h4 = a03 * h + b03
                p2 = p * a01
                for ix, hv, pv in zip(idxs, (h1, h2, h3, h4), (p * a[0], p2, p2 * a[2], p * a03)):
                    b_scr[ix] = hv
                    a_scr[ix] = pv
                new_h.append(h4)
                new_p.append(p * a03)
            return tuple(new_h), tuple(new_p)

        z8 = jnp.zeros((nseg, LANES), F32)
        o8 = jnp.ones((nseg, LANES), F32)
        hend, pend = lax.fori_loop(0, seglen // 4, scan, ((z8,) * 4, (o8,) * 4), unroll=1)

        h_out = []
        for qd in range(4):
            c = h0[qd]
            order = range(nseg) if qd < 2 else range(nseg - 1, -1, -1)
            for k in order:
                cin_scr[qd, k:k + 1, :] = c
                c = hend[qd][k:k + 1, :] + pend[qd][k:k + 1, :] * c
            h_out.append(c)

        cins = [cin_scr[qd] for qd in range(4)]

        def fixup(st, _):
            for qd in range(4):
                idx = (qd, pl.ds(pl.multiple_of(st * nseg, nseg), nseg), slice(None))
                b_scr[idx] = b_scr[idx] + a_scr[idx] * cins[qd]
            return 0

        lax.fori_loop(0, seglen, fixup, 0, unroll=8)

        def emit(c, _):
            r0 = pl.multiple_of(c * sub, sub)
            rows = seg_rows(r0)
            hq = [b_scr[qd, rows, :] for qd in range(4)]
            hsum = jnp.concatenate([hq[0] + hq[2], hq[1] + hq[3]], axis=1)
            gate = seq_ref[pl.ds(r0, sub), 0:LRU_WIDTH].astype(F32)
            out_ref[pl.ds(r0, sub), :] = (hsum * _gelu_tanh(gate)).astype(out_ref.dtype)
            return 0

        lax.fori_loop(0, length // sub, emit, 0)
        return h_out

    z1 = jnp.zeros((1, LANES), F32)
    states = run(pc_ref, oc_ref, [z1] * 4)
    run(pl_ref, ol_ref, states)


def _lru_mixer(layer, p_lat, p_ctx, conv_sel, conv_w, conv_b, w_gates, b_gates, lam, batch):
    seq = p_lat.shape[0] // batch
    lc = p_ctx.shape[0] // batch
    const = lambda *shape: pl.BlockSpec(shape, lambda b: (0,) * len(shape))
    of_layer = lambda *shape: pl.BlockSpec((1,) + shape, lambda b: (layer,) + (0,) * len(shape))
    return pl.pallas_call(
        _lru_kernel,
        out_shape=(jax.ShapeDtypeStruct((batch * seq, LRU_WIDTH), BF16),
                   jax.ShapeDtypeStruct((batch * lc, LRU_WIDTH), BF16)),
        grid=(batch,),
        in_specs=[pl.BlockSpec((seq, 512), lambda b: (b, 0)),
                  pl.BlockSpec((lc, 512), lambda b: (b, 0)),
                  const(3, SSD_CONV * SSD_CHUNK, CONV_WIN),
                  of_layer(SSD_CONV, LRU_WIDTH), of_layer(1, LRU_WIDTH), of_layer(LRU_WIDTH, 1024), of_layer(1, 1024),
                  of_layer(1, 512)],
        out_specs=(pl.BlockSpec((seq, LRU_WIDTH), lambda b: (b, 0)),
                   pl.BlockSpec((lc, LRU_WIDTH), lambda b: (b, 0))),
        scratch_shapes=[pltpu.VMEM((4, seq, LANES), F32), pltpu.VMEM((4, seq, LANES), F32),
                        pltpu.VMEM((4, LRU_SEGMENTS, LANES), F32)],
        compiler_params=_cparams(("parallel",), 48),
        name="lru_mixer",
    )(p_lat, p_ctx, conv_sel, conv_w, conv_b, w_gates, b_gates, lam)


def _stack_heads(qblk):
    lo = _lane_iota((qblk.shape[0], LANES)) < HEAD_DIM
    zero = jnp.zeros((), qblk.dtype)
    s0, s1 = qblk[:, 0:LANES], qblk[:, LANES:2 * LANES]
    return jnp.concatenate([jnp.where(lo, s0, zero), jnp.where(lo, s1, zero),
                            jnp.where(lo, zero, s0), jnp.where(lo, zero, s1)], axis=0)


def _unstack_heads(acc, t):
    lo = _lane_iota((t, LANES)) < HEAD_DIM
    return jnp.concatenate([jnp.where(lo, acc[0:t], acc[2 * t:3 * t]),
                            jnp.where(lo, acc[t:2 * t], acc[3 * t:4 * t])], axis=1)


MAX_SHIFT_BOUND = 50.0


def _gattn_kernel(bound_ref, q_ref, kl_ref, vl_ref, kc_ref, vc_ref, o_ref, qs_scr, m_scr, acc_scr, *, layer, tk, online):
    tq = q_ref.shape[0]
    half = 2 * tq
    qs_scr[...] = _stack_heads(q_ref[...])
    acc_scr[...] = jnp.zeros(acc_scr.shape, F32)
    if online:
        m_scr[...] = jnp.full(m_scr.shape, -jnp.inf, F32)

    def step(k, v):
        lo = _lane_iota(v.shape) < HEAD_DIM
        one = jnp.ones((), v.dtype)
        s = _dot_nt(qs_scr[...], k)
        if online:
            m_old = m_scr[...]
            shift = jnp.maximum(m_old, jnp.max(s, axis=-1, keepdims=True))
            m_scr[...] = shift
            acc_scr[...] = acc_scr[...] * jnp.exp2(m_old - shift)
        else:
            shift = bound_ref[layer]
        p = jnp.exp2(s - shift).astype(BF16)
        acc_scr[0:half, :] += _dot(p[0:half], jnp.where(lo, v, one))
        acc_scr[half:, :] += _dot(p[half:], jnp.where(lo, one, v))

    def body(j, _):
        r0 = pl.multiple_of(j * tk, tk)
        step(kl_ref[pl.ds(r0, tk), :], vl_ref[pl.ds(r0, tk), :])
        return 0

    lax.fori_loop(0, kl_ref.shape[0] // tk, body, 0, unroll=True)
    step(kc_ref[...], vc_ref[...])
    acc = acc_scr[...]
    o_ref[...] = _unstack_heads(acc / pltpu.roll(acc, HEAD_DIM, 1), tq).astype(o_ref.dtype)


def _logit_bounds(q_gain, k_gain):
    return HEAD_DIM * Q_SCALE * 1.02 * jnp.max(jnp.abs(q_gain), axis=-1) * jnp.max(jnp.abs(k_gain), axis=-1)


def _global_attention(layer, bound, qkv_lat, qkv_ctx, batch, tq, tk):
    seq = qkv_lat.shape[0] // batch
    lc = qkv_ctx.shape[0] // batch
    nq = seq // tq

    def call(online):
        return pl.pallas_call(
            functools.partial(_gattn_kernel, layer=layer, tk=tk, online=online),
            out_shape=jax.ShapeDtypeStruct((batch * seq, 256), BF16),
            grid=(batch, nq),
            in_specs=[pl.BlockSpec(memory_space=pltpu.SMEM),
                      pl.BlockSpec((tq, 256), lambda b, i: (b * nq + i, 0)),
                      pl.BlockSpec((seq, LANES), lambda b, i: (b, 2)),
                      pl.BlockSpec((seq, LANES), lambda b, i: (b, 3)),
                      pl.BlockSpec((lc, LANES), lambda b, i: (b, 2)),
                      pl.BlockSpec((lc, LANES), lambda b, i: (b, 3))],
            out_specs=pl.BlockSpec((tq, 256), lambda b, i: (b * nq + i, 0)),
            scratch_shapes=[pltpu.VMEM((4 * tq, LANES), BF16), pltpu.VMEM((4 * tq, 1), F32),
                            pltpu.VMEM((4 * tq, LANES), F32)],
            compiler_params=_cparams(("parallel", "parallel"), 48),
            name="global_attention_online" if online else "global_attention",
        )(bound, qkv_lat, qkv_lat, qkv_lat, qkv_ctx, qkv_ctx)

    return lax.cond(bound[layer] <= MAX_SHIFT_BOUND, lambda: call(False), lambda: call(True))


def _sink_rows(sink_ref, layer, t):
    head_of_row = _row_iota((4 * t, 1)) // t
    out = jnp.zeros((4 * t, 1), F32)
    for h in range(Q_HEADS):
        out = jnp.where(head_of_row == h, sink_ref[layer, h] * LOG2_E, out)
    return out


def _wattn_kernel(bound_ref, sink_ref, q_ref, kl_ref, vl_ref, kc_ref, vc_ref, o_ref, bias_scr, *, layer, online):
    w = WINDOW
    seq = kl_ref.shape[0]
    band = 3 * w
    shift = 0.0 if online else bound_ref[layer]
    dmat = _lane_iota((w, band)) - _row_iota((w, band))
    for case in range(3):
        bias_scr[case] = jnp.where(jnp.abs(dmat - case * w) <= w, -shift, -jnp.inf)
    one = jnp.ones((), BF16)
    kc, vc = kc_ref[...], vc_ref[...]
    lo_c = _lane_iota(vc.shape) < HEAD_DIM
    vc_sum = (jnp.where(lo_c, vc, one), jnp.where(lo_c, one, vc))
    lo_b = _lane_iota((band, LANES)) < HEAD_DIM
    lo = _lane_iota((w, LANES)) < HEAD_DIM

    def body(n, _):
        r0 = pl.multiple_of(n * w, w)
        gq = pl.program_id(1) * q_ref.shape[0] + r0
        ks = pl.multiple_of(jnp.clip(gq - w, 0, seq - band), w)
        bias = bias_scr[(gq - ks) // w]
        kb, vb = kl_ref[pl.ds(ks, band), :], vl_ref[pl.ds(ks, band), :]
        vb_sum = (jnp.where(lo_b, vb, one), jnp.where(lo_b, one, vb))
        qs = _stack_heads(q_ref[pl.ds(r0, w), :])
        s_band = _dot_nt(qs, kb)
        s_ctx = _dot_nt(qs, kc)
        outs = []
        for g in range(2):
            pb, pc, sk = [], [], []
            for h in (2 * g, 2 * g + 1):
                sb = s_band[h * w:(h + 1) * w] + bias
                sc = s_ctx[h * w:(h + 1) * w] - shift
                sink = sink_ref[layer, h] * LOG2_E - shift
                if online:
                    m = jnp.maximum(jnp.maximum(jnp.max(sb, axis=-1, keepdims=True),
                                                jnp.max(sc, axis=-1, keepdims=True)), sink)
                    sb, sc, sink = sb - m, sc - m, sink - m
                pb.append(jnp.exp2(sb).astype(BF16))
                pc.append(jnp.exp2(sc).astype(BF16))
                sk.append(jnp.exp2(sink) if online else jnp.full((w, 1), jnp.exp2(sink), F32))
            acc = _dot(jnp.concatenate(pb, axis=0), vb_sum[g]) + _dot(jnp.concatenate(pc, axis=0), vc_sum[g])
            den = pltpu.roll(acc, HEAD_DIM, 1) + jnp.concatenate(sk, axis=0)
            outs.append(acc / den)
        o_ref[pl.ds(r0, w), :] = jnp.concatenate(
            [jnp.where(lo, outs[0][0:w], outs[1][0:w]), jnp.where(lo, outs[0][w:2 * w], outs[1][w:2 * w])],
            axis=1).astype(o_ref.dtype)
        return 0

    lax.fori_loop(0, q_ref.shape[0] // w, body, 0, unroll=16)


def _window_attention(layer, bound, sink, qkv_lat, qkv_ctx, batch, tq):
    seq = qkv_lat.shape[0] // batch
    lc = qkv_ctx.shape[0] // batch
    nq = seq // tq

    def call(online):
        return pl.pallas_call(
            functools.partial(_wattn_kernel, layer=layer, online=online),
            out_shape=jax.ShapeDtypeStruct((batch * seq, 256), BF16),
            grid=(batch, nq),
            in_specs=[pl.BlockSpec(memory_space=pltpu.SMEM), pl.BlockSpec(memory_space=pltpu.SMEM),
                      pl.BlockSpec((tq, 256), lambda b, i: (b * nq + i, 0)),
                      pl.BlockSpec((seq, LANES), lambda b, i: (b, 2)),
                      pl.BlockSpec((seq, LANES), lambda b, i: (b, 3)),
                      pl.BlockSpec((lc, LANES), lambda b, i: (b, 2)),
                      pl.BlockSpec((lc, LANES), lambda b, i: (b, 3))],
            out_specs=pl.BlockSpec((tq, 256), lambda b, i: (b * nq + i, 0)),
            scratch_shapes=[pltpu.VMEM((3, WINDOW, 3 * WINDOW), F32)],
            compiler_params=_cparams(("parallel", "parallel"), 32),
            name="window_attention_online" if online else "window_attention",
        )(bound, sink, qkv_lat, qkv_lat, qkv_lat, qkv_ctx, qkv_ctx)

    return lax.cond(bound[layer] <= MAX_SHIFT_BOUND, lambda: call(False), lambda: call(True))


def _cattn_kernel(sink_ref, c_ref, d_ref, oc_ref, od_ref, *, layer):
    t = c_ref.shape[0]
    for ref, out, has_sink in ((c_ref, oc_ref, False), (d_ref, od_ref, True)):
        qs = _stack_heads(ref[:, 0:256])
        s = _dot_nt(qs, ref[:, 256:384])
        m = jnp.max(s, axis=-1, keepdims=True)
        if has_sink:
            sink = _sink_rows(sink_ref, layer, t)
            m = jnp.maximum(m, sink)
        p = jnp.exp2(s - m)
        den = jnp.sum(p, axis=-1, keepdims=True)
        if has_sink:
            den = den + jnp.exp2(sink - m)
        acc = _dot(p.astype(BF16), ref[:, 384:512])
        out[...] = _unstack_heads(acc / den, t).astype(out.dtype)


def _context_attention(layer, sink, qkv_c, qkv_d, batch):
    lc = qkv_c.shape[0] // batch
    blk = lambda width: pl.BlockSpec((lc, width), lambda b: (b, 0))
    return pl.pallas_call(
        functools.partial(_cattn_kernel, layer=layer),
        out_shape=(jax.ShapeDtypeStruct((batch * lc, 256), BF16),) * 2,
        grid=(batch,),
        in_specs=[pl.BlockSpec(memory_space=pltpu.SMEM), blk(512), blk(512)],
        out_specs=(blk(256), blk(256)),
        compiler_params=_cparams(("parallel",), 32),
        name="context_attention",
    )(sink, qkv_c, qkv_d)


def _out_ffn_kernel(x_ref, ya_ref, yb_ref, yc_ref, yd_ref, mod_ref, wo_ref, g_ref, w1_ref, w2_ref, o_ref, *, hchunk):
    m = mod_ref[0]
    gw = GROUP_WIDTH
    mix = _dot(ya_ref[...], wo_ref[0, 0:gw, :])
    for i, y_ref in enumerate((yb_ref, yc_ref, yd_ref), start=1):
        mix = mix + _dot(y_ref[...], wo_ref[0, i * gw:(i + 1) * gw, :])
    x1 = x_ref[...] + m[2:3] * mix
    ms = jnp.mean(x1 * x1, axis=-1, keepdims=True)
    h = ((x1 * lax.rsqrt(ms + EPS) * g_ref[0]) * (1.0 + m[4:5]) + m[3:4]).astype(BF16)
    acc = None
    for c in range(w1_ref.shape[2] // hchunk):
        u = jnp.maximum(_dot(h, w1_ref[0, :, c * hchunk:(c + 1) * hchunk]), 0.0)
        t = _dot((u * u).astype(BF16), w2_ref[0, c * hchunk:(c + 1) * hchunk, :])
        acc = t if acc is None else acc + t
    o_ref[...] = x1 + m[5:6] * acc


def _out_ffn(layer, x2d, ys, mod, mod_row_of_tile, w_out, g, w1, w2, tm):
    rows, d = x2d.shape
    hidden = w1.shape[2]
    row_blk = lambda width: pl.BlockSpec((tm, width), lambda i: (i, 0))
    const = lambda *shape: pl.BlockSpec((1,) + shape, lambda i: (layer,) + (0,) * len(shape), pipeline_mode=pl.Buffered(1))
    return pl.pallas_call(
        functools.partial(_out_ffn_kernel, hchunk=1024),
        out_shape=jax.ShapeDtypeStruct((rows, d), F32),
        grid=(rows // tm,),
        in_specs=[row_blk(d), row_blk(256), row_blk(256), row_blk(256), row_blk(256),
                  pl.BlockSpec((1, N_MOD, d), lambda i: (mod_row_of_tile(i), 0, 0)),
                  const(d, d), pl.BlockSpec((1, 1, d), lambda i: (layer, 0, 0)), const(d, hidden), const(hidden, d)],
        out_specs=row_blk(d),
        compiler_params=_cparams(("parallel",), 56),
        name="out_ffn",
    )(x2d, *ys, mod, w_out, g, w1, w2)


def _permute_q_heads(cols):
    q = cols[..., 0:256].reshape(cols.shape[:-1] + (2, 2, HEAD_DIM))
    q = jnp.swapaxes(q, -3, -2).reshape(cols.shape[:-1] + (256,))
    return jnp.concatenate([q, cols[..., 256:512]], axis=-1)


def _relayout_w_in(w):
    ssd_in = GROUP_WIDTH + SSD_CONV_CH + 2 * SSD_HEADS
    b0, c0, d0 = ssd_in, ssd_in + 512, ssd_in + 1024
    dt = w[..., GROUP_WIDTH + SSD_CONV_CH:ssd_in]
    dt = jnp.pad(dt, [(0, 0)] * (w.ndim - 1) + [(0, LANES - 2 * SSD_HEADS)])
    return jnp.concatenate([w[..., 0:GROUP_WIDTH + SSD_CONV_CH], w[..., b0:c0], _permute_q_heads(w[..., c0:d0]),
                            _permute_q_heads(w[..., d0:d0 + 512]), dt], axis=-1).astype(BF16)


def _relayout_w_out(w):
    depth, _, d = w.shape
    attn = jnp.swapaxes(w[:, 512:1024].reshape(depth, 2, 2, 2, HEAD_DIM, d), 2, 3).reshape(depth, 512, d)
    return jnp.concatenate([w[:, 0:512], attn], axis=1).astype(BF16)


def _block_diag(w):
    eye = jnp.eye(LRU_BLOCKS, dtype=w.dtype)
    out = jnp.einsum('...kij,kn->...kinj', w, eye)
    return out.reshape(w.shape[:-3] + (LRU_WIDTH, LRU_WIDTH))


def _rope_tables(length):
    pos = np.arange(length)
    inv = ROPE_THETA ** (-np.arange(0, ROPE_AXIS_DIM, 2, dtype=np.float64) / ROPE_AXIS_DIM)
    row = (pos // GRID_W)[:, None] * inv
    col = (pos % GRID_W)[:, None] * inv
    ang = np.concatenate([row, row, col, col], axis=1)
    half = ROPE_AXIS_DIM // 2
    sign = np.tile(np.concatenate([-np.ones(half), np.ones(half)]), 2)
    reps = (1, LANES // HEAD_DIM)
    return (jnp.asarray(np.tile(np.cos(ang), reps), F32), jnp.asarray(np.tile(np.sin(ang) * sign, reps), F32))


def _head_selectors():
    sel = np.zeros((2, 2 * LANES, SSD_HEADS * LANES), np.float32)
    for d in range(2):
        for h in range(SSD_HEADS):
            for part in range(2):
                sel[d, part * LANES + SSD_HEADS * d + h, h * LANES:(h + 1) * LANES] = 1.0
    return jnp.asarray(sel, BF16)


def _head_mean_matrix():
    idx = np.arange(256) // HEAD_DIM
    return jnp.asarray((idx[:, None] == idx[None, :]).astype(np.float32) / HEAD_DIM, BF16)


def kernel(x, c, ctx, c_ctx, w_mod, b_mod, g_mix, w_in, ssd_conv_w, ssd_conv_b, ssd_a_log, ssd_dt_bias, ssd_d, ssd_norm_g, lru_conv_w, lru_conv_b, lru_lambda, lru_w_a, lru_b_a, lru_w_i, lru_b_i, gqa_q_norm, gqa_k_norm, swa_q_norm, swa_k_norm, swa_sink, w_out, g_ffn, w_ffn1, w_ffn2):
    batch, seq, d = x.shape
    lc = ctx.shape[1]
    depth = w_mod.shape[0]
    assert batch + 1 <= 8 and seq % 1024 == 0 and lc % 256 == 0 and seq % GRID_W == 0

    cvecs = jnp.concatenate([c, c_ctx[None, :], jnp.zeros((8 - batch - 1, d), F32)], axis=0)
    mod = _modulation(cvecs, w_mod, b_mod).reshape(depth * 8, N_MOD, d)

    rope = _rope_tables(seq)
    sel = _head_selectors()
    bd = _head_mean_matrix()
    conv_sel = _conv_selectors()
    w_in_r = _relayout_w_in(w_in)
    w_out_r = _relayout_w_out(w_out)
    w1, w2 = w_ffn1.astype(BF16), w_ffn2.astype(BF16)
    g1, g2 = g_mix.reshape(depth, 1, d), g_ffn.reshape(depth, 1, d)
    nrm = jnp.tile(jnp.stack([gqa_q_norm, gqa_k_norm, swa_q_norm, swa_k_norm], axis=1), (1, 1, 256 // HEAD_DIM))
    lane_pad = ((0, 0), (0, 0), (0, LANES - 2 * SSD_HEADS))
    head_params = jnp.concatenate([jnp.pad(ssd_dt_bias.reshape(depth, 1, -1), lane_pad),
                                   jnp.pad(ssd_a_log.reshape(depth, 1, -1), lane_pad),
                                   jnp.zeros((depth, 6, LANES), F32)], axis=1)
    ssd_cb = ssd_conv_b.reshape(depth, 1, -1)
    d_skip = jnp.repeat(ssd_d, HEAD_DIM, axis=-1).reshape(depth, 1, -1)
    ssd_ng = ssd_norm_g.reshape(depth, 1, -1)
    bd_a, bd_i = _block_diag(lru_w_a), _block_diag(lru_w_i)
    w_gates = (0.5 * jnp.concatenate([bd_a[:, 0], bd_i[:, 0], bd_a[:, 1], bd_i[:, 1]], axis=-1)).astype(BF16)
    b_gates = 0.5 * jnp.concatenate([lru_b_a[:, 0], lru_b_i[:, 0], lru_b_a[:, 1], lru_b_i[:, 1]], axis=-1).reshape(depth, 1, -1)
    lru_cb = lru_conv_b.reshape(depth, 1, -1)
    lam = lru_lambda.reshape(depth, 1, -1)
    bounds = _logit_bounds(gqa_q_norm, gqa_k_norm)
    swa_bounds = _logit_bounds(swa_q_norm, swa_k_norm)

    xl = x.reshape(batch * seq, d)
    xc = ctx.reshape(batch * lc, d)
    for l in range(depth):
        need_ctx = l < depth - 1
        lat_row = lambda rows: (lambda i, l=l: l * 8 + i // (seq // rows))
        ctx_row = lambda i, l=l: l * 8 + batch
        ssd_l, lru_l, qc_l, qd_l, dt_l = _in_projection(l, xl, mod, lat_row(IN_PROJ_ROWS), g1, w_in_r, nrm, bd, rope,
                                                        IN_PROJ_ROWS, seq // IN_PROJ_ROWS)
        ssd_c, lru_c, qc_c, qd_c, dt_c = _in_projection(l, xc, mod, ctx_row, g1, w_in_r, nrm, bd, None, lc, 1)
        ya_l, ya_c = _ssd_mixer(l, ssd_l, ssd_c, dt_l, dt_c, conv_sel, ssd_conv_w, ssd_cb, head_params, sel, d_skip,
                                ssd_ng, batch)
        yb_l, yb_c = _lru_mixer(l, lru_l, lru_c, conv_sel, lru_conv_w, lru_cb, w_gates, b_gates, lam, batch)
        yc_l = _global_attention(l, bounds, qc_l, qc_c, batch, GATTN_Q_ROWS, GATTN_KV_ROWS)
        yd_l = _window_attention(l, swa_bounds, swa_sink, qd_l, qd_c, batch, seq)
        xl = _out_ffn(l, xl, (ya_l, yb_l, yc_l, yd_l), mod, lat_row(FFN_ROWS), w_out_r, g2, w1, w2, FFN_ROWS)
        if need_ctx:
            yc_c, yd_c = _context_attention(l, swa_sink, qc_c, qd_c, batch)
            xc = _out_ffn(l, xc, (ya_c, yb_c, yc_c, yd_c), mod, ctx_row, w_out_r, g2, w1, w2, FFN_ROWS)
    return xl.reshape(batch, seq, d)
```

```python
import functools
import math

import numpy as np
import jax
import jax.numpy as jnp
from jax import lax
from jax.experimental import pallas as pl
from jax.experimental.pallas import tpu as pltpu

F32 = jnp.float32
BF16 = jnp.bfloat16

EPS = 1e-6
GRID_W = 64
N_MOD = 6
GROUP_WIDTH = 256
SSD_HEADS = 4
SSD_STATE = 128
SSD_CONV = 4
SSD_CHUNK = 128
SSD_CONV_CH = 768
LRU_WIDTH = 256
LRU_BLOCKS = 4
LRU_C = 8.0
HEAD_DIM = 64
Q_HEADS = 4
WINDOW = 128
ROPE_THETA = 10000.0
ROPE_AXIS_DIM = 32
LOG2_E = math.log2(math.e)
Q_SCALE = HEAD_DIM ** -0.5 * LOG2_E

LANES = 128
BF16_ROWS = 16

COL_SSD, COL_LRU, COL_C, COL_D, COL_DT, COL_END = 0, 1024, 1536, 2048, 2560, 2688

CONV_WIN = SSD_CHUNK + 2 * BF16_ROWS

IN_PROJ_ROWS = 1024
FFN_ROWS = 512
GATTN_Q_ROWS = 512
GATTN_KV_ROWS = 512
SSD_GROUP = 8


def _cparams(sem, vmem_mib):
    return pltpu.CompilerParams(dimension_semantics=sem, vmem_limit_bytes=vmem_mib * 1024 * 1024)


def _dot(a, b):
    return jnp.dot(a, b, preferred_element_type=F32)


def _dot_nt(a, b):
    return lax.dot_general(a, b, (((1,), (1,)), ((), ())), preferred_element_type=F32)


def _split_bf16(a, parts):
    out = []
    for _ in range(parts - 1):
        hi = a.astype(BF16)
        out.append(hi)
        a = a - hi.astype(F32)
    out.append(a.astype(BF16))
    return out


def _silu(x):
    h = 0.5 * x
    return h + h * jnp.tanh(h)


def _softplus(x):
    return jnp.maximum(x, 0.0) + jnp.log1p(jnp.exp(-jnp.abs(x)))


def _gelu_tanh(x):
    return 0.5 * x * (1.0 + jnp.tanh(math.sqrt(2.0 / math.pi) * (x + 0.044715 * (x * x * x))))


def _lane_iota(shape):
    return lax.broadcasted_iota(jnp.int32, shape, len(shape) - 1)


def _row_iota(shape):
    return lax.broadcasted_iota(jnp.int32, shape, len(shape) - 2)


def _mod_kernel(cv_ref, w_ref, b_ref, o_ref):
    cv = cv_ref[...]
    s = _silu(cv)
    w = w_ref[0]
    s_hi, s_lo = _split_bf16(s, 2)
    w_hi, w_lo = _split_bf16(w, 2)
    acc = _dot(s_hi, w_hi) + _dot(s_lo, w_hi) + _dot(s_hi, w_lo)
    o_ref[0] = acc + b_ref[0]


def _modulation(cvecs, w_mod, b_mod):
    depth, d, n = w_mod.shape
    tn = 1024
    return pl.pallas_call(
        _mod_kernel,
        out_shape=jax.ShapeDtypeStruct((depth, 8, n), F32),
        grid=(depth, n // tn),
        in_specs=[pl.BlockSpec((8, d), lambda l, j: (0, 0)),
                  pl.BlockSpec((1, d, tn), lambda l, j: (l, 0, j)),
                  pl.BlockSpec((1, 1, tn), lambda l, j: (l, 0, j))],
        out_specs=pl.BlockSpec((1, 8, tn), lambda l, j: (l, 0, j)),
        compiler_params=_cparams(("parallel", "parallel"), 32),
        name="modulation",
    )(cvecs, w_mod, b_mod.reshape(depth, 1, n))


def _head_norm_rope(p, g, bd, cos, sin, scale):
    ms = _dot((p * p).astype(BF16), bd)
    y = p * lax.rsqrt(ms + EPS) * g
    if cos is not None:
        w = y.shape[-1]
        first = (_lane_iota(y.shape) % ROPE_AXIS_DIM) < (ROPE_AXIS_DIM // 2)
        partner = jnp.where(first, pltpu.roll(y, w - ROPE_AXIS_DIM // 2, 1), pltpu.roll(y, ROPE_AXIS_DIM // 2, 1))
        y = y * cos + partner * sin
    if scale != 1.0:
        y = y * scale
    return y


def _inproj_kernel(*refs, use_rope):
    if use_rope:
        x_ref, mod_ref, g_ref, w_ref, nrm_ref, bd_ref, cos_ref, sin_ref, ssd_ref, lru_ref, c_ref, d_ref, dt_ref = refs
    else:
        x_ref, mod_ref, g_ref, w_ref, nrm_ref, bd_ref, ssd_ref, lru_ref, c_ref, d_ref, dt_ref = refs
    x = x_ref[...]
    ms = jnp.mean(x * x, axis=-1, keepdims=True)
    m = mod_ref[0]
    h = (x * lax.rsqrt(ms + EPS) * g_ref[0]) * (1.0 + m[1:2]) + m[0:1]
    hb = h.astype(BF16)
    bd = bd_ref[...]
    if use_rope:
        cos1, sin1 = cos_ref[...], sin_ref[...]
        cos2 = jnp.concatenate([cos1, cos1], axis=1)
        sin2 = jnp.concatenate([sin1, sin1], axis=1)
    else:
        cos1 = sin1 = cos2 = sin2 = None
    for o_ref, c0, row in ((c_ref, COL_C, 0), (d_ref, COL_D, 2)):
        p = _dot(hb, w_ref[0, :, c0:c0 + 512])
        q = _head_norm_rope(p[:, 0:256], nrm_ref[0, row:row + 1, :], bd, cos2, sin2, Q_SCALE)
        k = _head_norm_rope(p[:, 256:384], nrm_ref[0, row + 1:row + 2, 0:128], bd[0:128, 0:128], cos1, sin1, 1.0)
        o_ref[:, 0:256] = q.astype(BF16)
        o_ref[:, 256:384] = k.astype(BF16)
        o_ref[:, 384:512] = p[:, 384:512].astype(BF16)
    ssd_ref[...] = _dot(hb, w_ref[0, :, COL_SSD:COL_LRU]).astype(BF16)
    lru_ref[...] = _dot(hb, w_ref[0, :, COL_LRU:COL_C]).astype(BF16)
    dt_ref[...] = _dot(hb, w_ref[0, :, COL_DT:COL_END])


def _in_projection(layer, x2d, mod, mod_row_of_tile, g, w, nrm, bd, rope, tm, rope_tiles):
    rows, d = x2d.shape
    use_rope = rope is not None
    in_specs = [pl.BlockSpec((tm, d), lambda i: (i, 0)),
                pl.BlockSpec((1, N_MOD, d), lambda i: (mod_row_of_tile(i), 0, 0)),
                pl.BlockSpec((1, 1, d), lambda i: (layer, 0, 0)),
                pl.BlockSpec((1, d, COL_END), lambda i: (layer, 0, 0), pipeline_mode=pl.Buffered(1)),
                pl.BlockSpec((1, 4, 256), lambda i: (layer, 0, 0)),
                pl.BlockSpec((256, 256), lambda i: (0, 0))]
    args = [x2d, mod, g, w, nrm, bd]
    if use_rope:
        in_specs += [pl.BlockSpec((tm, LANES), lambda i: (i % rope_tiles, 0))] * 2
        args += list(rope)
    widths = (1024, 512, 512, 512, LANES)
    dtypes = (BF16, BF16, BF16, BF16, F32)
    return pl.pallas_call(
        functools.partial(_inproj_kernel, use_rope=use_rope),
        out_shape=tuple(jax.ShapeDtypeStruct((rows, wd), dt) for wd, dt in zip(widths, dtypes)),
        grid=(rows // tm,),
        in_specs=in_specs,
        out_specs=tuple(pl.BlockSpec((tm, wd), lambda i: (i, 0)) for wd in widths),
        compiler_params=_cparams(("parallel",), 56),
        name="in_projection",
    )(*args)


def _conv_selectors():
    row = np.arange(SSD_CONV * SSD_CHUNK)
    want = (row % SSD_CHUNK) + (row // SSD_CHUNK) - SSD_CONV // 2
    col = np.arange(CONV_WIN)
    sel = np.stack([(col[None, :] == want[:, None] + BF16_ROWS * case) for case in range(3)])
    return jnp.asarray(sel.astype(np.float32), BF16)


def _conv_chunk(seq_ref, col0, width, length, r0, sel_ref, w_ref, b_ref):
    ws = jnp.clip(r0 - BF16_ROWS, 0, length - CONV_WIN)
    ws = pl.multiple_of(ws, BF16_ROWS)
    xw = seq_ref[pl.ds(ws, CONV_WIN), col0:col0 + width]
    sh = _dot(sel_ref[(r0 - ws) // BF16_ROWS], xw)
    w = w_ref[0]
    acc = b_ref[0] + sh[0:SSD_CHUNK] * w[0:1]
    for k in range(1, SSD_CONV):
        acc = acc + sh[k * SSD_CHUNK:(k + 1) * SSD_CHUNK] * w[k:k + 1]
    return acc


def _ssd_kernel(pl_ref, pc_ref, dtl_ref, dtc_ref, csel_ref, cw_ref, cb_ref, hp_ref, sel_ref, dsk_ref, ng_ref,
                yl_ref, yc_ref, accl_ref, accc_ref, xbcl_ref, xbcc_ref):
    q = SSD_CHUNK
    lane8 = _lane_iota((1, LANES)) < 2 * SSD_HEADS
    dt_bias = hp_ref[0, 0:1, :]
    a_row = jnp.where(lane8, -jnp.exp(hp_ref[0, 1:2, :]) * LOG2_E, 0.0)
    ii = _row_iota((q, q))
    jj = _lane_iota((q, q))
    lo = _lane_iota((q, LANES)) < HEAD_DIM
    masks = (jj <= ii, jj >= ii)
    tris = [jnp.where(m, 1.0, 0.0).astype(BF16) for m in masks]
    tri_pairs = [jnp.concatenate([t, t], axis=1) for t in tris]
    zblk = jnp.zeros((SSD_STATE, LANES), BF16)

    def block_diag(a0, a1):
        return jnp.concatenate([jnp.concatenate([a0, zblk], axis=1), jnp.concatenate([zblk, a1], axis=1)], axis=0)

    def chunk(xbc_ref, r0, d, state, acum, dtb, acb):
        xs = xbc_ref[pl.ds(r0, q), 0:256].astype(F32)
        bm = xbc_ref[pl.ds(r0, q), 256:512].astype(F32)
        cmb = xbc_ref[pl.ds(r0, q), 512:768]
        mask = masks[d]
        act = acum.T
        edge = q - 1 if d == 0 else 0
        per_head = []
        for h in range(SSD_HEADS):
            ac = acb[:, h * LANES:(h + 1) * LANES]
            ar = act[SSD_HEADS * d + h:SSD_HEADS * d + h + 1, :]
            lmat = jnp.where(mask, jnp.exp2(ac - ar), 0.0)
            alast = ac[edge:edge + 1, :]
            dth = dtb[:, h * LANES:(h + 1) * LANES]
            per_head.append((lmat, dth, jnp.exp2(alast - ac) * dth, jnp.exp2(ac), jnp.exp2(alast)))
        bts = [bm[:, g * LANES:(g + 1) * LANES].T.astype(BF16) for g in range(2)]
        cb_all = _dot(cmb, block_diag(bts[0], bts[1]))
        yo_all = _dot(cmb, block_diag(state[0].astype(BF16), state[1].astype(BF16)))
        ys, new_state = [], []
        for g in range(2):
            (l0, dt0, we0, ei0, cd0), (l1, dt1, we1, ei1, cd1) = per_head[2 * g], per_head[2 * g + 1]
            xg = xs[:, g * LANES:(g + 1) * LANES]
            cb = cb_all[:, g * LANES:(g + 1) * LANES]
            xdt = xg * jnp.where(lo, dt0, dt1)
            scores = jnp.concatenate([(cb * l0).astype(BF16), (cb * l1).astype(BF16)], axis=1)
            xpair = jnp.concatenate([jnp.where(lo, xdt, 0.0), jnp.where(lo, 0.0, xdt)], axis=0).astype(BF16)
            y = _dot(scores, xpair) + yo_all[:, g * LANES:(g + 1) * LANES] * jnp.where(lo, ei0, ei1)
            s_new = _dot(bts[g], (xg * jnp.where(lo, we0, we1)).astype(BF16))
            new_state.append(state[g] * jnp.where(lo, cd0, cd1) + s_new)
            ys.append(y)
        return jnp.concatenate(ys, axis=1), xs, tuple(new_state)

    def sweep(seq_ref, xbc_ref, dt_ref, acc_ref, out_ref, d, state):
        length = seq_ref.shape[0]
        nc = length // q
        gsz = min(nc, SSD_GROUP)
        ngroups = nc // gsz

        def body(gi, st):
            g0 = pl.multiple_of((gi if d == 0 else ngroups - 1 - gi) * (gsz * q), gsz * q)
            if d == 0:
                for c in range(gsz):
                    conv = _conv_chunk(seq_ref, GROUP_WIDTH, SSD_CONV_CH, length, g0 + c * q, csel_ref, cw_ref, cb_ref)
                    xbc_ref[pl.ds(g0 + c * q, q), :] = _silu(conv).astype(BF16)
            dtv = _softplus(dt_ref[pl.ds(g0, gsz * q), :] + dt_bias)
            hi, lo_part = _split_bf16(dtv * a_row, 2)
            rhs = jnp.concatenate([jnp.concatenate([hi[c * q:(c + 1) * q], lo_part[c * q:(c + 1) * q]], axis=0)
                                   for c in range(gsz)], axis=1)
            acum = _dot(tri_pairs[d], rhs)
            stacked = jnp.concatenate([blk for c in range(gsz)
                                       for blk in (dtv[c * q:(c + 1) * q], acum[:, c * LANES:(c + 1) * LANES])], axis=0)
            ex = _dot(jnp.concatenate(_split_bf16(stacked, 2), axis=1), sel_ref[d])
            for i in range(gsz):
                c = i if d == 0 else gsz - 1 - i
                r0 = g0 + c * q
                y, xs, st = chunk(xbc_ref, r0, d, st, acum[:, c * LANES:(c + 1) * LANES],
                                  ex[2 * c * q:(2 * c + 1) * q], ex[(2 * c + 1) * q:(2 * c + 2) * q])
                if d == 0:
                    acc_ref[pl.ds(r0, q), :] = y
                else:
                    y = acc_ref[pl.ds(r0, q), :] + y + dsk_ref[0] * xs
                    z = seq_ref[pl.ds(r0, q), 0:GROUP_WIDTH].astype(F32)
                    t = y * _silu(z)
                    ms = jnp.mean(t * t, axis=-1, keepdims=True)
                    out_ref[pl.ds(r0, q), :] = (t * lax.rsqrt(ms + EPS) * ng_ref[0]).astype(out_ref.dtype)
            return st

        return lax.fori_loop(0, ngroups, body, state)

    zero = (jnp.zeros((SSD_STATE, LANES), F32), jnp.zeros((SSD_STATE, LANES), F32))
    for d in range(2):
        st = sweep(pc_ref, xbcc_ref, dtc_ref, accc_ref, yc_ref, d, zero)
        sweep(pl_ref, xbcl_ref, dtl_ref, accl_ref, yl_ref, d, st)


def _ssd_mixer(layer, p_lat, p_ctx, dt_lat, dt_ctx, conv_sel, conv_w, conv_b, head_params, sel, d_skip, norm_g, batch):
    seq = p_lat.shape[0] // batch
    lc = p_ctx.shape[0] // batch
    const = lambda *shape: pl.BlockSpec(shape, lambda b: (0,) * len(shape))
    of_layer = lambda *shape: pl.BlockSpec((1,) + shape, lambda b: (layer,) + (0,) * len(shape))
    return pl.pallas_call(
        _ssd_kernel,
        out_shape=(jax.ShapeDtypeStruct((batch * seq, GROUP_WIDTH), BF16),
                   jax.ShapeDtypeStruct((batch * lc, GROUP_WIDTH), BF16)),
        grid=(batch,),
        in_specs=[pl.BlockSpec((seq, 1024), lambda b: (b, 0)),
                  pl.BlockSpec((lc, 1024), lambda b: (b, 0)),
                  pl.BlockSpec((seq, LANES), lambda b: (b, 0)),
                  pl.BlockSpec((lc, LANES), lambda b: (b, 0)),
                  const(3, SSD_CONV * SSD_CHUNK, CONV_WIN),
                  of_layer(SSD_CONV, SSD_CONV_CH), of_layer(1, SSD_CONV_CH), of_layer(8, LANES),
                  const(2, 2 * LANES, SSD_HEADS * LANES), of_layer(1, GROUP_WIDTH), of_layer(1, GROUP_WIDTH)],
        out_specs=(pl.BlockSpec((seq, GROUP_WIDTH), lambda b: (b, 0)),
                   pl.BlockSpec((lc, GROUP_WIDTH), lambda b: (b, 0))),
        scratch_shapes=[pltpu.VMEM((seq, GROUP_WIDTH), F32), pltpu.VMEM((lc, GROUP_WIDTH), F32),
                        pltpu.VMEM((seq, SSD_CONV_CH), BF16), pltpu.VMEM((lc, SSD_CONV_CH), BF16)],
        compiler_params=_cparams(("parallel",), 48),
        name="ssd_mixer",
    )(p_lat, p_ctx, dt_lat, dt_ctx, conv_sel, conv_w, conv_b, head_params, sel, d_skip, norm_g)


LRU_SEGMENTS = 8
LRU_GROUP = 8


def _lru_kernel(pl_ref, pc_ref, csel_ref, cw_ref, cb_ref, wg_ref, bg_ref, lam_ref, ol_ref, oc_ref, a_scr, b_scr, cin_scr):
    q = SSD_CHUNK
    nseg = LRU_SEGMENTS
    nhc = (0.5 * LRU_C) * _softplus(-lam_ref[0])

    def run(seq_ref, out_ref, h0):
        length = seq_ref.shape[0]
        seglen = length // nseg
        sub = min(q, seglen)

        def seg_rows(t0):
            return pl.ds((t0 % seglen) * nseg + t0 // seglen, sub, stride=nseg)

        gsz = min(length // q, LRU_GROUP)

        def gates(gi, _):
            g0 = pl.multiple_of(gi * (gsz * q), gsz * q)
            xcs = [_conv_chunk(seq_ref, LRU_WIDTH, LRU_WIDTH, length, g0 + c * q, csel_ref, cw_ref, cb_ref)
                   for c in range(gsz)]
            xc_all = jnp.concatenate(xcs, axis=0)
            t_all = jnp.tanh(_dot(xc_all.astype(BF16), wg_ref[0]) + bg_ref[0])
            for ci in range(gsz):
                r0 = g0 + ci * q
                t = t_all[ci * q:(ci + 1) * q]
                hx = 0.5 * xcs[ci]
                for d in range(2):
                    c = nhc[:, d * 256:(d + 1) * 256]
                    nla = c * t[:, d * 512:d * 512 + 256] + c
                    a = jnp.exp2(nla * -LOG2_E)
                    y = jnp.tanh(nla) * (a * a + 1.0)
                    root = jnp.where(y > 0.0, y * lax.rsqrt(y), 0.0)
                    bv = root * (hx * t[:, d * 512 + 256:(d + 1) * 512] + hx)
                    for hf in range(2):
                        for j in range(q // sub):
                            rows = seg_rows(r0 + j * sub)
                            a_scr[2 * d + hf, rows, :] = a[j * sub:(j + 1) * sub, hf * LANES:(hf + 1) * LANES]
                            b_scr[2 * d + hf, rows, :] = bv[j * sub:(j + 1) * sub, hf * LANES:(hf + 1) * LANES]
            return 0

        lax.fori_loop(0, length // (gsz * q), gates, 0)

        def scan(k, carry):
            hs, ps = carry
            new_h, new_p = [], []
            for qd in range(4):
                steps = [4 * k + i if qd < 2 else seglen - 1 - (4 * k + i) for i in range(4)]
                idxs = [(qd, pl.ds(pl.multiple_of(st * nseg, nseg), nseg), slice(None)) for st in steps]
                a = [a_scr[ix] for ix in idxs]
                b = [b_scr[ix] for ix in idxs]
                h, p = hs[qd], ps[qd]
                a01, b01 = a[1] * a[0], a[1] * b[0] + b[1]
                a23, b23 = a[3] * a[2], a[3] * b[2] + b[3]
                a03, b03 = a23 * a01, a23 * b01 + b23
                h1 = a[0] * h + b[0]
                h2 = a01 * h + b01
                h3 = a[2] * h2 + b[2]
                h4 = a03 * h + b03
                p2 = p * a01
                for ix, hv, pv in zip(idxs, (h1, h2, h3, h4), (p * a[0], p2, p2 * a[2], p * a03)):
                    b_scr[ix] = hv
                    a_scr[ix] = pv
                new_h.append(h4)
                new_p.append(p * a03)
            return tuple(new_h), tuple(new_p)

        z8 = jnp.zeros((nseg, LANES), F32)
        o8 = jnp.ones((nseg, LANES), F32)
        hend, pend = lax.fori_loop(0, seglen // 4, scan, ((z8,) * 4, (o8,) * 4), unroll=1)

        h_out = []
        for qd in range(4):
            c = h0[qd]
            order = range(nseg) if qd < 2 else range(nseg - 1, -1, -1)
            for k in order:
                cin_scr[qd, k:k + 1, :] = c
                c = hend[qd][k:k + 1, :] + pend[qd][k:k + 1, :] * c
            h_out.append(c)

        cins = [cin_scr[qd] for qd in range(4)]

        def fixup(st, _):
            for qd in range(4):
                idx = (qd, pl.ds(pl.multiple_of(st * nseg, nseg), nseg), slice(None))
                b_scr[idx] = b_scr[idx] + a_scr[idx] * cins[qd]
            return 0

        lax.fori_loop(0, seglen, fixup, 0, unroll=8)

        def emit(c, _):
            r0 = pl.multiple_of(c * sub, sub)
            rows = seg_rows(r0)
            hq = [b_scr[qd, rows, :] for qd in range(4)]
            hsum = jnp.concatenate([hq[0] + hq[2], hq[1] + hq[3]], axis=1)
            gate = seq_ref[pl.ds(r0, sub), 0:LRU_WIDTH].astype(F32)
            out_ref[pl.ds(r0, sub), :] = (hsum * _gelu_tanh(gate)).astype(out_ref.dtype)
            return 0

        lax.fori_loop(0, length // sub, emit, 0)
        return h_out

    z1 = jnp.zeros((1, LANES), F32)
    states = run(pc_ref, oc_ref, [z1] * 4)
    run(pl_ref, ol_ref, states)


def _lru_mixer(layer, p_lat, p_ctx, conv_sel, conv_w, conv_b, w_gates, b_gates, lam, batch):
    seq = p_lat.shape[0] // batch
    lc = p_ctx.shape[0] // batch
    const = lambda *shape: pl.BlockSpec(shape, lambda b: (0,) * len(shape))
    of_layer = lambda *shape: pl.BlockSpec((1,) + shape, lambda b: (layer,) + (0,) * len(shape))
    return pl.pallas_call(
        _lru_kernel,
        out_shape=(jax.ShapeDtypeStruct((batch * seq, LRU_WIDTH), BF16),
                   jax.ShapeDtypeStruct((batch * lc, LRU_WIDTH), BF16)),
        grid=(batch,),
        in_specs=[pl.BlockSpec((seq, 512), lambda b: (b, 0)),
                  pl.BlockSpec((lc, 512), lambda b: (b, 0)),
                  const(3, SSD_CONV * SSD_CHUNK, CONV_WIN),
                  of_layer(SSD_CONV, LRU_WIDTH), of_layer(1, LRU_WIDTH), of_layer(LRU_WIDTH, 1024), of_layer(1, 1024),
                  of_layer(1, 512)],
        out_specs=(pl.BlockSpec((seq, LRU_WIDTH), lambda b: (b, 0)),
                   pl.BlockSpec((lc, LRU_WIDTH), lambda b: (b, 0))),
        scratch_shapes=[pltpu.VMEM((4, seq, LANES), F32), pltpu.VMEM((4, seq, LANES), F32),
                        pltpu.VMEM((4, LRU_SEGMENTS, LANES), F32)],
        compiler_params=_cparams(("parallel",), 48),
        name="lru_mixer",
    )(p_lat, p_ctx, conv_sel, conv_w, conv_b, w_gates, b_gates, lam)


def _stack_heads(qblk):
    lo = _lane_iota((qblk.shape[0], LANES)) < HEAD_DIM
    zero = jnp.zeros((), qblk.dtype)
    s0, s1 = qblk[:, 0:LANES], qblk[:, LANES:2 * LANES]
    return jnp.concatenate([jnp.where(lo, s0, zero), jnp.where(lo, s1, zero),
                            jnp.where(lo, zero, s0), jnp.where(lo, zero, s1)], axis=0)


def _unstack_heads(acc, t):
    lo = _lane_iota((t, LANES)) < HEAD_DIM
    return jnp.concatenate([jnp.where(lo, acc[0:t], acc[2 * t:3 * t]),
                            jnp.where(lo, acc[t:2 * t], acc[3 * t:4 * t])], axis=1)


MAX_SHIFT_BOUND = 50.0


def _gattn_body(bound_ref, q_ref, kl_ref, vl_ref, kc_ref, vc_ref, o_ref, qs_scr, m_scr, acc_scr, layer, tk, online):
    tq = q_ref.shape[0]
    half = 2 * tq
    qs_scr[...] = _stack_heads(q_ref[...])
    acc_scr[...] = jnp.zeros(acc_scr.shape, F32)
    if online:
        m_scr[...] = jnp.full(m_scr.shape, -jnp.inf, F32)

    def step(k, v):
        lo = _lane_iota(v.shape) < HEAD_DIM
        one = jnp.ones((), v.dtype)
        s = _dot_nt(qs_scr[...], k)
        if online:
            m_old = m_scr[...]
            shift = jnp.maximum(m_old, jnp.max(s, axis=-1, keepdims=True))
            m_scr[...] = shift
            acc_scr[...] = acc_scr[...] * jnp.exp2(m_old - shift)
        else:
            shift = bound_ref[layer]
        p = jnp.exp2(s - shift).astype(BF16)
        acc_scr[0:half, :] += _dot(p[0:half], jnp.where(lo, v, one))
        acc_scr[half:, :] += _dot(p[half:], jnp.where(lo, one, v))

    def body(j, _):
        r0 = pl.multiple_of(j * tk, tk)
        step(kl_ref[pl.ds(r0, tk), :], vl_ref[pl.ds(r0, tk), :])
        return 0

    lax.fori_loop(0, kl_ref.shape[0] // tk, body, 0, unroll=not online)
    step(kc_ref[...], vc_ref[...])
    acc = acc_scr[...]
    o_ref[...] = _unstack_heads(acc / pltpu.roll(acc, HEAD_DIM, 1), tq).astype(o_ref.dtype)


def _gattn_kernel(bound_ref, *refs, layer, tk):
    safe = bound_ref[layer] <= MAX_SHIFT_BOUND

    @pl.when(safe)
    def _():
        _gattn_body(bound_ref, *refs, layer, tk, False)

    @pl.when(jnp.logical_not(safe))
    def _():
        _gattn_body(bound_ref, *refs, layer, tk, True)


def _logit_bounds(q_gain, k_gain):
    return HEAD_DIM * Q_SCALE * 1.02 * jnp.max(jnp.abs(q_gain), axis=-1) * jnp.max(jnp.abs(k_gain), axis=-1)


def _global_attention(layer, bound, qkv_lat, qkv_ctx, batch, tq, tk):
    seq = qkv_lat.shape[0] // batch
    lc = qkv_ctx.shape[0] // batch
    nq = seq // tq

    return pl.pallas_call(
        functools.partial(_gattn_kernel, layer=layer, tk=tk),
        out_shape=jax.ShapeDtypeStruct((batch * seq, 256), BF16),
        grid=(batch, nq),
        in_specs=[pl.BlockSpec(memory_space=pltpu.SMEM),
                  pl.BlockSpec((tq, 256), lambda b, i: (b * nq + i, 0)),
                  pl.BlockSpec((seq, LANES), lambda b, i: (b, 2)),
                  pl.BlockSpec((seq, LANES), lambda b, i: (b, 3)),
                  pl.BlockSpec((lc, LANES), lambda b, i: (b, 2)),
                  pl.BlockSpec((lc, LANES), lambda b, i: (b, 3))],
        out_specs=pl.BlockSpec((tq, 256), lambda b, i: (b * nq + i, 0)),
        scratch_shapes=[pltpu.VMEM((4 * tq, LANES), BF16), pltpu.VMEM((4 * tq, 1), F32),
                        pltpu.VMEM((4 * tq, LANES), F32)],
        compiler_params=_cparams(("parallel", "parallel"), 48),
        name="global_attention",
    )(bound, qkv_lat, qkv_lat, qkv_lat, qkv_ctx, qkv_ctx)


def _sink_rows(sink_ref, layer, t):
    head_of_row = _row_iota((4 * t, 1)) // t
    out = jnp.zeros((4 * t, 1), F32)
    for h in range(Q_HEADS):
        out = jnp.where(head_of_row == h, sink_ref[layer, h] * LOG2_E, out)
    return out


def _wattn_kernel(bound_ref, sink_ref, q_ref, kl_ref, vl_ref, kc_ref, vc_ref, o_ref, bias_scr, *, layer, online):
    w = WINDOW
    seq = kl_ref.shape[0]
    band = 3 * w
    shift = 0.0 if online else bound_ref[layer]
    dmat = _lane_iota((w, band)) - _row_iota((w, band))
    for case in range(3):
        bias_scr[case] = jnp.where(jnp.abs(dmat - case * w) <= w, -shift, -jnp.inf)
    one = jnp.ones((), BF16)
    kc, vc = kc_ref[...], vc_ref[...]
    lo_c = _lane_iota(vc.shape) < HEAD_DIM
    vc_sum = (jnp.where(lo_c, vc, one), jnp.where(lo_c, one, vc))
    lo_b = _lane_iota((band, LANES)) < HEAD_DIM
    lo = _lane_iota((w, LANES)) < HEAD_DIM

    def body(n, _):
        r0 = pl.multiple_of(n * w, w)
        gq = pl.program_id(1) * q_ref.shape[0] + r0
        ks = pl.multiple_of(jnp.clip(gq - w, 0, seq - band), w)
        bias = bias_scr[(gq - ks) // w]
        kb, vb = kl_ref[pl.ds(ks, band), :], vl_ref[pl.ds(ks, band), :]
        vb_sum = (jnp.where(lo_b, vb, one), jnp.where(lo_b, one, vb))
        qs = _stack_heads(q_ref[pl.ds(r0, w), :])
        s_band = _dot_nt(qs, kb)
        s_ctx = _dot_nt(qs, kc)
        outs = []
        for g in range(2):
            pb, pc, sk = [], [], []
            for h in (2 * g, 2 * g + 1):
                sb = s_band[h * w:(h + 1) * w] + bias
                sc = s_ctx[h * w:(h + 1) * w] - shift
                sink = sink_ref[layer, h] * LOG2_E - shift
                if online:
                    m = jnp.maximum(jnp.maximum(jnp.max(sb, axis=-1, keepdims=True),
                                                jnp.max(sc, axis=-1, keepdims=True)), sink)
                    sb, sc, sink = sb - m, sc - m, sink - m
                pb.append(jnp.exp2(sb).astype(BF16))
                pc.append(jnp.exp2(sc).astype(BF16))
                sk.append(jnp.exp2(sink) if online else jnp.full((w, 1), jnp.exp2(sink), F32))
            acc = _dot(jnp.concatenate(pb, axis=0), vb_sum[g]) + _dot(jnp.concatenate(pc, axis=0), vc_sum[g])
            den = pltpu.roll(acc, HEAD_DIM, 1) + jnp.concatenate(sk, axis=0)
            outs.append(acc / den)
        o_ref[pl.ds(r0, w), :] = jnp.concatenate(
            [jnp.where(lo, outs[0][0:w], outs[1][0:w]), jnp.where(lo, outs[0][w:2 * w], outs[1][w:2 * w])],
            axis=1).astype(o_ref.dtype)
        return 0

    lax.fori_loop(0, q_ref.shape[0] // w, body, 0, unroll=8)


def _window_attention(layer, bound, sink, qkv_lat, qkv_ctx, batch, tq):
    seq = qkv_lat.shape[0] // batch
    lc = qkv_ctx.shape[0] // batch
    nq = seq // tq

    def call(online):
        return pl.pallas_call(
            functools.partial(_wattn_kernel, layer=layer, online=online),
            out_shape=jax.ShapeDtypeStruct((batch * seq, 256), BF16),
            grid=(batch, nq),
            in_specs=[pl.BlockSpec(memory_space=pltpu.SMEM), pl.BlockSpec(memory_space=pltpu.SMEM),
                      pl.BlockSpec((tq, 256), lambda b, i: (b * nq + i, 0)),
                      pl.BlockSpec((seq, LANES), lambda b, i: (b, 2)),
                      pl.BlockSpec((seq, LANES), lambda b, i: (b, 3)),
                      pl.BlockSpec((lc, LANES), lambda b, i: (b, 2)),
                      pl.BlockSpec((lc, LANES), lambda b, i: (b, 3))],
            out_specs=pl.BlockSpec((tq, 256), lambda b, i: (b * nq + i, 0)),
            scratch_shapes=[pltpu.VMEM((3, WINDOW, 3 * WINDOW), F32)],
            compiler_params=_cparams(("parallel", "parallel"), 32),
            name="window_attention_online" if online else "window_attention",
        )(bound, sink, qkv_lat, qkv_lat, qkv_lat, qkv_ctx, qkv_ctx)

    return lax.cond(bound[layer] <= MAX_SHIFT_BOUND, lambda: call(False), lambda: call(True))


def _cattn_kernel(sink_ref, c_ref, d_ref, oc_ref, od_ref, *, layer):
    t = c_ref.shape[0]
    for ref, out, has_sink in ((c_ref, oc_ref, False), (d_ref, od_ref, True)):
        qs = _stack_heads(ref[:, 0:256])
        s = _dot_nt(qs, ref[:, 256:384])
        m = jnp.max(s, axis=-1, keepdims=True)
        if has_sink:
            sink = _sink_rows(sink_ref, layer, t)
            m = jnp.maximum(m, sink)
        p = jnp.exp2(s - m)
        den = jnp.sum(p, axis=-1, keepdims=True)
        if has_sink:
            den = den + jnp.exp2(sink - m)
        acc = _dot(p.astype(BF16), ref[:, 384:512])
        out[...] = _unstack_heads(acc / den, t).astype(out.dtype)


def _context_attention(layer, sink, qkv_c, qkv_d, batch):
    lc = qkv_c.shape[0] // batch
    blk = lambda width: pl.BlockSpec((lc, width), lambda b: (b, 0))
    return pl.pallas_call(
        functools.partial(_cattn_kernel, layer=layer),
        out_shape=(jax.ShapeDtypeStruct((batch * lc, 256), BF16),) * 2,
        grid=(batch,),
        in_specs=[pl.BlockSpec(memory_space=pltpu.SMEM), blk(512), blk(512)],
        out_specs=(blk(256), blk(256)),
        compiler_params=_cparams(("parallel",), 32),
        name="context_attention",
    )(sink, qkv_c, qkv_d)


def _out_ffn_kernel(x_ref, ya_ref, yb_ref, yc_ref, yd_ref, mod_ref, wo_ref, g_ref, w1_ref, w2_ref, o_ref, *, hchunk):
    m = mod_ref[0]
    gw = GROUP_WIDTH
    mix = _dot(ya_ref[...], wo_ref[0, 0:gw, :])
    for i, y_ref in enumerate((yb_ref, yc_ref, yd_ref), start=1):
        mix = mix + _dot(y_ref[...], wo_ref[0, i * gw:(i + 1) * gw, :])
    x1 = x_ref[...] + m[2:3] * mix
    ms = jnp.mean(x1 * x1, axis=-1, keepdims=True)
    h = ((x1 * lax.rsqrt(ms + EPS) * g_ref[0]) * (1.0 + m[4:5]) + m[3:4]).astype(BF16)
    acc = None
    for c in range(w1_ref.shape[2] // hchunk):
        u = jnp.maximum(_dot(h, w1_ref[0, :, c * hchunk:(c + 1) * hchunk]), 0.0)
        t = _dot((u * u).astype(BF16), w2_ref[0, c * hchunk:(c + 1) * hchunk, :])
        acc = t if acc is None else acc + t
    o_ref[...] = x1 + m[5:6] * acc


def _out_ffn(layer, x2d, ys, mod, mod_row_of_tile, w_out, g, w1, w2, tm):
    rows, d = x2d.shape
    hidden = w1.shape[2]
    row_blk = lambda width: pl.BlockSpec((tm, width), lambda i: (i, 0))
    const = lambda *shape: pl.BlockSpec((1,) + shape, lambda i: (layer,) + (0,) * len(shape), pipeline_mode=pl.Buffered(1))
    return pl.pallas_call(
        functools.partial(_out_ffn_kernel, hchunk=1024),
        out_shape=jax.ShapeDtypeStruct((rows, d), F32),
        grid=(rows // tm,),
        in_specs=[row_blk(d), row_blk(256), row_blk(256), row_blk(256), row_blk(256),
                  pl.BlockSpec((1, N_MOD, d), lambda i: (mod_row_of_tile(i), 0, 0)),
                  const(d, d), pl.BlockSpec((1, 1, d), lambda i: (layer, 0, 0)), const(d, hidden), const(hidden, d)],
        out_specs=row_blk(d),
        compiler_params=_cparams(("parallel",), 56),
        name="out_ffn",
    )(x2d, *ys, mod, w_out, g, w1, w2)


def _permute_q_heads(cols):
    q = cols[..., 0:256].reshape(cols.shape[:-1] + (2, 2, HEAD_DIM))
    q = jnp.swapaxes(q, -3, -2).reshape(cols.shape[:-1] + (256,))
    return jnp.concatenate([q, cols[..., 256:512]], axis=-1)


def _relayout_w_in(w):
    ssd_in = GROUP_WIDTH + SSD_CONV_CH + 2 * SSD_HEADS
    b0, c0, d0 = ssd_in, ssd_in + 512, ssd_in + 1024
    dt = w[..., GROUP_WIDTH + SSD_CONV_CH:ssd_in]
    dt = jnp.pad(dt, [(0, 0)] * (w.ndim - 1) + [(0, LANES - 2 * SSD_HEADS)])
    return jnp.concatenate([w[..., 0:GROUP_WIDTH + SSD_CONV_CH], w[..., b0:c0], _permute_q_heads(w[..., c0:d0]),
                            _permute_q_heads(w[..., d0:d0 + 512]), dt], axis=-1).astype(BF16)


def _relayout_w_out(w):
    depth, _, d = w.shape
    attn = jnp.swapaxes(w[:, 512:1024].reshape(depth, 2, 2, 2, HEAD_DIM, d), 2, 3).reshape(depth, 512, d)
    return jnp.concatenate([w[:, 0:512], attn], axis=1).astype(BF16)


def _block_diag(w):
    eye = jnp.eye(LRU_BLOCKS, dtype=w.dtype)
    out = jnp.einsum('...kij,kn->...kinj', w, eye)
    return out.reshape(w.shape[:-3] + (LRU_WIDTH, LRU_WIDTH))


def _rope_tables(length):
    pos = np.arange(length)
    inv = ROPE_THETA ** (-np.arange(0, ROPE_AXIS_DIM, 2, dtype=np.float64) / ROPE_AXIS_DIM)
    row = (pos // GRID_W)[:, None] * inv
    col = (pos % GRID_W)[:, None] * inv
    ang = np.concatenate([row, row, col, col], axis=1)
    half = ROPE_AXIS_DIM // 2
    sign = np.tile(np.concatenate([-np.ones(half), np.ones(half)]), 2)
    reps = (1, LANES // HEAD_DIM)
    return (jnp.asarray(np.tile(np.cos(ang), reps), F32), jnp.asarray(np.tile(np.sin(ang) * sign, reps), F32))


def _head_selectors():
    sel = np.zeros((2, 2 * LANES, SSD_HEADS * LANES), np.float32)
    for d in range(2):
        for h in range(SSD_HEADS):
            for part in range(2):
                sel[d, part * LANES + SSD_HEADS * d + h, h * LANES:(h + 1) * LANES] = 1.0
    return jnp.asarray(sel, BF16)


def _head_mean_matrix():
    idx = np.arange(256) // HEAD_DIM
    return jnp.asarray((idx[:, None] == idx[None, :]).astype(np.float32) / HEAD_DIM, BF16)


def kernel(x, c, ctx, c_ctx, w_mod, b_mod, g_mix, w_in, ssd_conv_w, ssd_conv_b, ssd_a_log, ssd_dt_bias, ssd_d, ssd_norm_g, lru_conv_w, lru_conv_b, lru_lambda, lru_w_a, lru_b_a, lru_w_i, lru_b_i, gqa_q_norm, gqa_k_norm, swa_q_norm, swa_k_norm, swa_sink, w_out, g_ffn, w_ffn1, w_ffn2):
    batch, seq, d = x.shape
    lc = ctx.shape[1]
    depth = w_mod.shape[0]
    assert batch + 1 <= 8 and seq % 1024 == 0 and lc % 256 == 0 and seq % GRID_W == 0

    cvecs = jnp.concatenate([c, c_ctx[None, :], jnp.zeros((8 - batch - 1, d), F32)], axis=0)
    mod = _modulation(cvecs, w_mod, b_mod).reshape(depth * 8, N_MOD, d)

    rope = _rope_tables(seq)
    sel = _head_selectors()
    bd = _head_mean_matrix()
    conv_sel = _conv_selectors()
    w_in_r = _relayout_w_in(w_in)
    w_out_r = _relayout_w_out(w_out)
    w1, w2 = w_ffn1.astype(BF16), w_ffn2.astype(BF16)
    g1, g2 = g_mix.reshape(depth, 1, d), g_ffn.reshape(depth, 1, d)
    nrm = jnp.tile(jnp.stack([gqa_q_norm, gqa_k_norm, swa_q_norm, swa_k_norm], axis=1), (1, 1, 256 // HEAD_DIM))
    lane_pad = ((0, 0), (0, 0), (0, LANES - 2 * SSD_HEADS))
    head_params = jnp.concatenate([jnp.pad(ssd_dt_bias.reshape(depth, 1, -1), lane_pad),
                                   jnp.pad(ssd_a_log.reshape(depth, 1, -1), lane_pad),
                                   jnp.zeros((depth, 6, LANES), F32)], axis=1)
    ssd_cb = ssd_conv_b.reshape(depth, 1, -1)
    d_skip = jnp.repeat(ssd_d, HEAD_DIM, axis=-1).reshape(depth, 1, -1)
    ssd_ng = ssd_norm_g.reshape(depth, 1, -1)
    bd_a, bd_i = _block_diag(lru_w_a), _block_diag(lru_w_i)
    w_gates = (0.5 * jnp.concatenate([bd_a[:, 0], bd_i[:, 0], bd_a[:, 1], bd_i[:, 1]], axis=-1)).astype(BF16)
    b_gates = 0.5 * jnp.concatenate([lru_b_a[:, 0], lru_b_i[:, 0], lru_b_a[:, 1], lru_b_i[:, 1]], axis=-1).reshape(depth, 1, -1)
    lru_cb = lru_conv_b.reshape(depth, 1, -1)
    lam = lru_lambda.reshape(depth, 1, -1)
    bounds = _logit_bounds(gqa_q_norm, gqa_k_norm)
    swa_bounds = _logit_bounds(swa_q_norm, swa_k_norm)

    xl = x.reshape(batch * seq, d)
    xc = ctx.reshape(batch * lc, d)
    for l in range(depth):
        need_ctx = l < depth - 1
        lat_row = lambda rows: (lambda i, l=l: l * 8 + i // (seq // rows))
        ctx_row = lambda i, l=l: l * 8 + batch
        ssd_l, lru_l, qc_l, qd_l, dt_l = _in_projection(l, xl, mod, lat_row(IN_PROJ_ROWS), g1, w_in_r, nrm, bd, rope,
                                                        IN_PROJ_ROWS, seq // IN_PROJ_ROWS)
        ssd_c, lru_c, qc_c, qd_c, dt_c = _in_projection(l, xc, mod, ctx_row, g1, w_in_r, nrm, bd, None, lc, 1)
        ya_l, ya_c = _ssd_mixer(l, ssd_l, ssd_c, dt_l, dt_c, conv_sel, ssd_conv_w, ssd_cb, head_params, sel, d_skip,
                                ssd_ng, batch)
        yb_l, yb_c = _lru_mixer(l, lru_l, lru_c, conv_sel, lru_conv_w, lru_cb, w_gates, b_gates, lam, batch)
        yc_l = _global_attention(l, bounds, qc_l, qc_c, batch, GATTN_Q_ROWS, GATTN_KV_ROWS)
        yd_l = _window_attention(l, swa_bounds, swa_sink, qd_l, qd_c, batch, seq)
        xl = _out_ffn(l, xl, (ya_l, yb_l, yc_l, yd_l), mod, lat_row(FFN_ROWS), w_out_r, g2, w1, w2, FFN_ROWS)
        if need_ctx:
            yc_c, yd_c = _context_attention(l, swa_sink, qc_c, qd_c, batch)
            xc = _out_ffn(l, xc, (ya_c, yb_c, yc_c, yd_c), mod, ctx_row, w_out_r, g2, w1, w2, FFN_ROWS)
    return xl.reshape(batch, seq, d)
```

```python
import functools
import math

import numpy as np
import jax
import jax.numpy as jnp
from jax import lax
from jax.experimental import pallas as pl
from jax.experimental.pallas import tpu as pltpu

F32 = jnp.float32
BF16 = jnp.bfloat16

EPS = 1e-6
GRID_W = 64
N_MOD = 6
GROUP_WIDTH = 256
SSD_HEADS = 4
SSD_STATE = 128
SSD_CONV = 4
SSD_CHUNK = 128
SSD_CONV_CH = 768
LRU_WIDTH = 256
LRU_BLOCKS = 4
LRU_C = 8.0
HEAD_DIM = 64
Q_HEADS = 4
WINDOW = 128
ROPE_THETA = 10000.0
ROPE_AXIS_DIM = 32
LOG2_E = math.log2(math.e)
Q_SCALE = HEAD_DIM ** -0.5 * LOG2_E

LANES = 128
BF16_ROWS = 16

COL_SSD, COL_LRU, COL_C, COL_D, COL_DT, COL_END = 0, 1024, 1536, 2048, 2560, 2688

CONV_WIN = SSD_CHUNK + 2 * BF16_ROWS

IN_PROJ_ROWS = 1024
FFN_ROWS = 512
GATTN_Q_ROWS = 512
GATTN_KV_ROWS = 512
SSD_GROUP = 8


def _cparams(sem, vmem_mib):
    return pltpu.CompilerParams(dimension_semantics=sem, vmem_limit_bytes=vmem_mib * 1024 * 1024)


def _dot(a, b):
    return jnp.dot(a, b, preferred_element_type=F32)


def _dot_nt(a, b):
    return lax.dot_general(a, b, (((1,), (1,)), ((), ())), preferred_element_type=F32)


def _split_bf16(a, parts):
    out = []
    for _ in range(parts - 1):
        hi = a.astype(BF16)
        out.append(hi)
        a = a - hi.astype(F32)
    out.append(a.astype(BF16))
    return out


def _silu(x):
    h = 0.5 * x
    return h + h * jnp.tanh(h)


def _softplus(x):
    return jnp.maximum(x, 0.0) + jnp.log1p(jnp.exp(-jnp.abs(x)))


def _gelu_tanh(x):
    return 0.5 * x * (1.0 + jnp.tanh(math.sqrt(2.0 / math.pi) * (x + 0.044715 * (x * x * x))))


def _lane_iota(shape):
    return lax.broadcasted_iota(jnp.int32, shape, len(shape) - 1)


def _row_iota(shape):
    return lax.broadcasted_iota(jnp.int32, shape, len(shape) - 2)


def _mod_kernel(cv_ref, w_ref, b_ref, o_ref):
    cv = cv_ref[...]
    s = _silu(cv)
    w = w_ref[0]
    s_hi, s_lo = _split_bf16(s, 2)
    w_hi, w_lo = _split_bf16(w, 2)
    acc = _dot(s_hi, w_hi) + _dot(s_lo, w_hi) + _dot(s_hi, w_lo)
    o_ref[0] = acc + b_ref[0]


def _modulation(cvecs, w_mod, b_mod):
    depth, d, n = w_mod.shape
    tn = 1024
    return pl.pallas_call(
        _mod_kernel,
        out_shape=jax.ShapeDtypeStruct((depth, 8, n), F32),
        grid=(depth, n // tn),
        in_specs=[pl.BlockSpec((8, d), lambda l, j: (0, 0)),
                  pl.BlockSpec((1, d, tn), lambda l, j: (l, 0, j)),
                  pl.BlockSpec((1, 1, tn), lambda l, j: (l, 0, j))],
        out_specs=pl.BlockSpec((1, 8, tn), lambda l, j: (l, 0, j)),
        compiler_params=_cparams(("parallel", "parallel"), 32),
        name="modulation",
    )(cvecs, w_mod, b_mod.reshape(depth, 1, n))


def _head_norm_rope(p, g, bd, cos, sin, scale):
    ms = _dot((p * p).astype(BF16), bd)
    y = p * lax.rsqrt(ms + EPS) * g
    if cos is not None:
        w = y.shape[-1]
        first = (_lane_iota(y.shape) % ROPE_AXIS_DIM) < (ROPE_AXIS_DIM // 2)
        partner = jnp.where(first, pltpu.roll(y, w - ROPE_AXIS_DIM // 2, 1), pltpu.roll(y, ROPE_AXIS_DIM // 2, 1))
        y = y * cos + partner * sin
    if scale != 1.0:
        y = y * scale
    return y


def _inproj_kernel(*refs, use_rope):
    if use_rope:
        x_ref, mod_ref, g_ref, w_ref, nrm_ref, bd_ref, cos_ref, sin_ref, ssd_ref, lru_ref, c_ref, d_ref, dt_ref = refs
    else:
        x_ref, mod_ref, g_ref, w_ref, nrm_ref, bd_ref, ssd_ref, lru_ref, c_ref, d_ref, dt_ref = refs
    x = x_ref[...]
    ms = jnp.mean(x * x, axis=-1, keepdims=True)
    m = mod_ref[0]
    h = (x * lax.rsqrt(ms + EPS) * g_ref[0]) * (1.0 + m[1:2]) + m[0:1]
    hb = h.astype(BF16)
    bd = bd_ref[...]
    if use_rope:
        cos1, sin1 = cos_ref[...], sin_ref[...]
        cos2 = jnp.concatenate([cos1, cos1], axis=1)
        sin2 = jnp.concatenate([sin1, sin1], axis=1)
    else:
        cos1 = sin1 = cos2 = sin2 = None
    for o_ref, c0, row in ((c_ref, COL_C, 0), (d_ref, COL_D, 2)):
        p = _dot(hb, w_ref[0, :, c0:c0 + 512])
        q = _head_norm_rope(p[:, 0:256], nrm_ref[0, row:row + 1, :], bd, cos2, sin2, Q_SCALE)
        k = _head_norm_rope(p[:, 256:384], nrm_ref[0, row + 1:row + 2, 0:128], bd[0:128, 0:128], cos1, sin1, 1.0)
        o_ref[:, 0:256] = q.astype(BF16)
        o_ref[:, 256:384] = k.astype(BF16)
        o_ref[:, 384:512] = p[:, 384:512].astype(BF16)
    ssd_ref[...] = _dot(hb, w_ref[0, :, COL_SSD:COL_LRU]).astype(BF16)
    lru_ref[...] = _dot(hb, w_ref[0, :, COL_LRU:COL_C]).astype(BF16)
    dt_ref[...] = _dot(hb, w_ref[0, :, COL_DT:COL_END])


def _in_projection(layer, x2d, mod, mod_row_of_tile, g, w, nrm, bd, rope, tm, rope_tiles):
    rows, d = x2d.shape
    use_rope = rope is not None
    in_specs = [pl.BlockSpec((tm, d), lambda i: (i, 0)),
                pl.BlockSpec((1, N_MOD, d), lambda i: (mod_row_of_tile(i), 0, 0)),
                pl.BlockSpec((1, 1, d), lambda i: (layer, 0, 0)),
                pl.BlockSpec((1, d, COL_END), lambda i: (layer, 0, 0), pipeline_mode=pl.Buffered(1)),
                pl.BlockSpec((1, 4, 256), lambda i: (layer, 0, 0)),
                pl.BlockSpec((256, 256), lambda i: (0, 0))]
    args = [x2d, mod, g, w, nrm, bd]
    if use_rope:
        in_specs += [pl.BlockSpec((tm, LANES), lambda i: (i % rope_tiles, 0))] * 2
        args += list(rope)
    widths = (1024, 512, 512, 512, LANES)
    dtypes = (BF16, BF16, BF16, BF16, F32)
    return pl.pallas_call(
        functools.partial(_inproj_kernel, use_rope=use_rope),
        out_shape=tuple(jax.ShapeDtypeStruct((rows, wd), dt) for wd, dt in zip(widths, dtypes)),
        grid=(rows // tm,),
        in_specs=in_specs,
        out_specs=tuple(pl.BlockSpec((tm, wd), lambda i: (i, 0)) for wd in widths),
        compiler_params=_cparams(("parallel",), 56),
        name="in_projection",
    )(*args)


def _conv_selectors():
    row = np.arange(SSD_CONV * SSD_CHUNK)
    want = (row % SSD_CHUNK) + (row // SSD_CHUNK) - SSD_CONV // 2
    col = np.arange(CONV_WIN)
    sel = np.stack([(col[None, :] == want[:, None] + BF16_ROWS * case) for case in range(3)])
    return jnp.asarray(sel.astype(np.float32), BF16)


def _conv_chunk(seq_ref, col0, width, length, r0, sel_ref, w_ref, b_ref):
    ws = jnp.clip(r0 - BF16_ROWS, 0, length - CONV_WIN)
    ws = pl.multiple_of(ws, BF16_ROWS)
    xw = seq_ref[pl.ds(ws, CONV_WIN), col0:col0 + width]
    sh = _dot(sel_ref[(r0 - ws) // BF16_ROWS], xw)
    w = w_ref[0]
    acc = b_ref[0] + sh[0:SSD_CHUNK] * w[0:1]
    for k in range(1, SSD_CONV):
        acc = acc + sh[k * SSD_CHUNK:(k + 1) * SSD_CHUNK] * w[k:k + 1]
    return acc


def _ssd_kernel(pl_ref, pc_ref, dtl_ref, dtc_ref, csel_ref, cw_ref, cb_ref, hp_ref, sel_ref, dsk_ref, ng_ref,
                yl_ref, yc_ref, accl_ref, accc_ref, xbcl_ref, xbcc_ref):
    q = SSD_CHUNK
    lane8 = _lane_iota((1, LANES)) < 2 * SSD_HEADS
    dt_bias = hp_ref[0, 0:1, :]
    a_row = jnp.where(lane8, -jnp.exp(hp_ref[0, 1:2, :]) * LOG2_E, 0.0)
    ii = _row_iota((q, q))
    jj = _lane_iota((q, q))
    lo = _lane_iota((q, LANES)) < HEAD_DIM
    masks = (jj <= ii, jj >= ii)
    tris = [jnp.where(m, 1.0, 0.0).astype(BF16) for m in masks]
    tri_pairs = [jnp.concatenate([t, t], axis=1) for t in tris]
    zblk = jnp.zeros((SSD_STATE, LANES), BF16)

    def block_diag(a0, a1):
        return jnp.concatenate([jnp.concatenate([a0, zblk], axis=1), jnp.concatenate([zblk, a1], axis=1)], axis=0)

    def chunk(xbc_ref, r0, d, state, acum, dtb, acb):
        xs = xbc_ref[pl.ds(r0, q), 0:256].astype(F32)
        bm = xbc_ref[pl.ds(r0, q), 256:512].astype(F32)
        cmb = xbc_ref[pl.ds(r0, q), 512:768]
        mask = masks[d]
        act = acum.T
        edge = q - 1 if d == 0 else 0
        per_head = []
        for h in range(SSD_HEADS):
            ac = acb[:, h * LANES:(h + 1) * LANES]
            ar = act[SSD_HEADS * d + h:SSD_HEADS * d + h + 1, :]
            lmat = jnp.where(mask, jnp.exp2(ac - ar), 0.0)
            alast = ac[edge:edge + 1, :]
            dth = dtb[:, h * LANES:(h + 1) * LANES]
            per_head.append((lmat, dth, jnp.exp2(alast - ac) * dth, jnp.exp2(ac), jnp.exp2(alast)))
        bts = [bm[:, g * LANES:(g + 1) * LANES].T.astype(BF16) for g in range(2)]
        cb_all = _dot(cmb, block_diag(bts[0], bts[1]))
        yo_all = _dot(cmb, block_diag(state[0].astype(BF16), state[1].astype(BF16)))
        ys, new_state = [], []
        for g in range(2):
            (l0, dt0, we0, ei0, cd0), (l1, dt1, we1, ei1, cd1) = per_head[2 * g], per_head[2 * g + 1]
            xg = xs[:, g * LANES:(g + 1) * LANES]
            cb = cb_all[:, g * LANES:(g + 1) * LANES]
            xdt = xg * jnp.where(lo, dt0, dt1)
            scores = jnp.concatenate([(cb * l0).astype(BF16), (cb * l1).astype(BF16)], axis=1)
            xpair = jnp.concatenate([jnp.where(lo, xdt, 0.0), jnp.where(lo, 0.0, xdt)], axis=0).astype(BF16)
            y = _dot(scores, xpair) + yo_all[:, g * LANES:(g + 1) * LANES] * jnp.where(lo, ei0, ei1)
            s_new = _dot(bts[g], (xg * jnp.where(lo, we0, we1)).astype(BF16))
            new_state.append(state[g] * jnp.where(lo, cd0, cd1) + s_new)
            ys.append(y)
        return jnp.concatenate(ys, axis=1), xs, tuple(new_state)

    def sweep(seq_ref, xbc_ref, dt_ref, acc_ref, out_ref, d, state):
        length = seq_ref.shape[0]
        nc = length // q
        gsz = min(nc, SSD_GROUP)
        ngroups = nc // gsz

        def body(gi, st):
            g0 = pl.multiple_of((gi if d == 0 else ngroups - 1 - gi) * (gsz * q), gsz * q)
            if d == 0:
                for c in range(gsz):
                    conv = _conv_chunk(seq_ref, GROUP_WIDTH, SSD_CONV_CH, length, g0 + c * q, csel_ref, cw_ref, cb_ref)
                    xbc_ref[pl.ds(g0 + c * q, q), :] = _silu(conv).astype(BF16)
            dtv = _softplus(dt_ref[pl.ds(g0, gsz * q), :] + dt_bias)
            hi, lo_part = _split_bf16(dtv * a_row, 2)
            rhs = jnp.concatenate([jnp.concatenate([hi[c * q:(c + 1) * q], lo_part[c * q:(c + 1) * q]], axis=0)
                                   for c in range(gsz)], axis=1)
            acum = _dot(tri_pairs[d], rhs)
            stacked = jnp.concatenate([blk for c in range(gsz)
                                       for blk in (dtv[c * q:(c + 1) * q], acum[:, c * LANES:(c + 1) * LANES])], axis=0)
            ex = _dot(jnp.concatenate(_split_bf16(stacked, 2), axis=1), sel_ref[d])
            for i in range(gsz):
                c = i if d == 0 else gsz - 1 - i
                r0 = g0 + c * q
                y, xs, st = chunk(xbc_ref, r0, d, st, acum[:, c * LANES:(c + 1) * LANES],
                                  ex[2 * c * q:(2 * c + 1) * q], ex[(2 * c + 1) * q:(2 * c + 2) * q])
                if d == 0:
                    acc_ref[pl.ds(r0, q), :] = y
                else:
                    y = acc_ref[pl.ds(r0, q), :] + y + dsk_ref[0] * xs
                    z = seq_ref[pl.ds(r0, q), 0:GROUP_WIDTH].astype(F32)
                    t = y * _silu(z)
                    ms = jnp.mean(t * t, axis=-1, keepdims=True)
                    out_ref[pl.ds(r0, q), :] = (t * lax.rsqrt(ms + EPS) * ng_ref[0]).astype(out_ref.dtype)
            return st

        return lax.fori_loop(0, ngroups, body, state)

    zero = (jnp.zeros((SSD_STATE, LANES), F32), jnp.zeros((SSD_STATE, LANES), F32))
    for d in range(2):
        st = sweep(pc_ref, xbcc_ref, dtc_ref, accc_ref, yc_ref, d, zero)
        sweep(pl_ref, xbcl_ref, dtl_ref, accl_ref, yl_ref, d, st)


def _ssd_mixer(layer, p_lat, p_ctx, dt_lat, dt_ctx, conv_sel, conv_w, conv_b, head_params, sel, d_skip, norm_g, batch):
    seq = p_lat.shape[0] // batch
    lc = p_ctx.shape[0] // batch
    const = lambda *shape: pl.BlockSpec(shape, lambda b: (0,) * len(shape))
    of_layer = lambda *shape: pl.BlockSpec((1,) + shape, lambda b: (layer,) + (0,) * len(shape))
    return pl.pallas_call(
        _ssd_kernel,
        out_shape=(jax.ShapeDtypeStruct((batch * seq, GROUP_WIDTH), BF16),
                   jax.ShapeDtypeStruct((batch * lc, GROUP_WIDTH), BF16)),
        grid=(batch,),
        in_specs=[pl.BlockSpec((seq, 1024), lambda b: (b, 0)),
                  pl.BlockSpec((lc, 1024), lambda b: (b, 0)),
                  pl.BlockSpec((seq, LANES), lambda b: (b, 0)),
                  pl.BlockSpec((lc, LANES), lambda b: (b, 0)),
                  const(3, SSD_CONV * SSD_CHUNK, CONV_WIN),
                  of_layer(SSD_CONV, SSD_CONV_CH), of_layer(1, SSD_CONV_CH), of_layer(8, LANES),
                  const(2, 2 * LANES, SSD_HEADS * LANES), of_layer(1, GROUP_WIDTH), of_layer(1, GROUP_WIDTH)],
        out_specs=(pl.BlockSpec((seq, GROUP_WIDTH), lambda b: (b, 0)),
                   pl.BlockSpec((lc, GROUP_WIDTH), lambda b: (b, 0))),
        scratch_shapes=[pltpu.VMEM((seq, GROUP_WIDTH), F32), pltpu.VMEM((lc, GROUP_WIDTH), F32),
                        pltpu.VMEM((seq, SSD_CONV_CH), BF16), pltpu.VMEM((lc, SSD_CONV_CH), BF16)],
        compiler_params=_cparams(("parallel",), 48),
        name="ssd_mixer",
    )(p_lat, p_ctx, dt_lat, dt_ctx, conv_sel, conv_w, conv_b, head_params, sel, d_skip, norm_g)


LRU_SEGMENTS = 8
LRU_GROUP = 8


def _lru_kernel(pl_ref, pc_ref, csel_ref, cw_ref, cb_ref, wg_ref, bg_ref, lam_ref, ol_ref, oc_ref, a_scr, b_scr, cin_scr):
    q = SSD_CHUNK
    nseg = LRU_SEGMENTS
    nhc = (0.5 * LRU_C) * _softplus(-lam_ref[0])

    def run(seq_ref, out_ref, h0):
        length = seq_ref.shape[0]
        seglen = length // nseg
        sub = min(q, seglen)

        def seg_rows(t0):
            return pl.ds((t0 % seglen) * nseg + t0 // seglen, sub, stride=nseg)

        gsz = min(length // q, LRU_GROUP)

        def gates(gi, _):
            g0 = pl.multiple_of(gi * (gsz * q), gsz * q)
            xcs = [_conv_chunk(seq_ref, LRU_WIDTH, LRU_WIDTH, length, g0 + c * q, csel_ref, cw_ref, cb_ref)
                   for c in range(gsz)]
            xc_all = jnp.concatenate(xcs, axis=0)
            t_all = jnp.tanh(_dot(xc_all.astype(BF16), wg_ref[0]) + bg_ref[0])
            for ci in range(gsz):
                r0 = g0 + ci * q
                t = t_all[ci * q:(ci + 1) * q]
                hx = 0.5 * xcs[ci]
                for d in range(2):
                    c = nhc[:, d * 256:(d + 1) * 256]
                    nla = c * t[:, d * 512:d * 512 + 256] + c
                    a = jnp.exp2(nla * -LOG2_E)
                    y = jnp.tanh(nla) * (a * a + 1.0)
                    root = jnp.where(y > 0.0, y * lax.rsqrt(y), 0.0)
                    bv = root * (hx * t[:, d * 512 + 256:(d + 1) * 512] + hx)
                    for hf in range(2):
                        for j in range(q // sub):
                            rows = seg_rows(r0 + j * sub)
                            a_scr[2 * d + hf, rows, :] = a[j * sub:(j + 1) * sub, hf * LANES:(hf + 1) * LANES]
                            b_scr[2 * d + hf, rows, :] = bv[j * sub:(j + 1) * sub, hf * LANES:(hf + 1) * LANES]
            return 0

        lax.fori_loop(0, length // (gsz * q), gates, 0)

        def scan(k, carry):
            hs, ps = carry
            new_h, new_p = [], []
            for qd in range(4):
                steps = [4 * k + i if qd < 2 else seglen - 1 - (4 * k + i) for i in range(4)]
                idxs = [(qd, pl.ds(pl.multiple_of(st * nseg, nseg), nseg), slice(None)) for st in steps]
                a = [a_scr[ix] for ix in idxs]
                b = [b_scr[ix] for ix in idxs]
                h, p = hs[qd], ps[qd]
                a01, b01 = a[1] * a[0], a[1] * b[0] + b[1]
                a23, b23 = a[3] * a[2], a[3] * b[2] + b[3]
                a03, b03 = a23 * a01, a23 * b01 + b23
                h1 = a[0] * h + b[0]
                h2 = a01 * h + b01
                h3 = a[2] * h2 + b[2]
                h4 = a03 * h + b03
                p2 = p * a01
                for ix, hv, pv in zip(idxs, (h1, h2, h3, h4), (p * a[0], p2, p2 * a[2], p * a03)):
                    b_scr[ix] = hv
                    a_scr[ix] = pv
                new_h.append(h4)
                new_p.append(p * a03)
            return tuple(new_h), tuple(new_p)

        z8 = jnp.zeros((nseg, LANES), F32)
        o8 = jnp.ones((nseg, LANES), F32)
        hend, pend = lax.fori_loop(0, seglen // 4, scan, ((z8,) * 4, (o8,) * 4), unroll=1)

        h_out = []
        for qd in range(4):
            c = h0[qd]
            order = range(nseg) if qd < 2 else range(nseg - 1, -1, -1)
            for k in order:
                cin_scr[qd, k:k + 1, :] = c
                c = hend[qd][k:k + 1, :] + pend[qd][k:k + 1, :] * c
            h_out.append(c)

        cins = [cin_scr[qd] for qd in range(4)]

        def fixup(st, _):
            for qd in range(4):
                idx = (qd, pl.ds(pl.multiple_of(st * nseg, nseg), nseg), slice(None))
                b_scr[idx] = b_scr[idx] + a_scr[idx] * cins[qd]
            return 0

        lax.fori_loop(0, seglen, fixup, 0, unroll=8)

        def emit(c, _):
            r0 = pl.multiple_of(c * sub, sub)
            rows = seg_rows(r0)
            hq = [b_scr[qd, rows, :] for qd in range(4)]
            hsum = jnp.concatenate([hq[0] + hq[2], hq[1] + hq[3]], axis=1)
            gate = seq_ref[pl.ds(r0, sub), 0:LRU_WIDTH].astype(F32)
            out_ref[pl.ds(r0, sub), :] = (hsum * _gelu_tanh(gate)).astype(out_ref.dtype)
            return 0

        lax.fori_loop(0, length // sub, emit, 0)
        return h_out

    z1 = jnp.zeros((1, LANES), F32)
    states = run(pc_ref, oc_ref, [z1] * 4)
    run(pl_ref, ol_ref, states)


def _lru_mixer(layer, p_lat, p_ctx, conv_sel, conv_w, conv_b, w_gates, b_gates, lam, batch):
    seq = p_lat.shape[0] // batch
    lc = p_ctx.shape[0] // batch
    const = lambda *shape: pl.BlockSpec(shape, lambda b: (0,) * len(shape))
    of_layer = lambda *shape: pl.BlockSpec((1,) + shape, lambda b: (layer,) + (0,) * len(shape))
    return pl.pallas_call(
        _lru_kernel,
        out_shape=(jax.ShapeDtypeStruct((batch * seq, LRU_WIDTH), BF16),
                   jax.ShapeDtypeStruct((batch * lc, LRU_WIDTH), BF16)),
        grid=(batch,),
        in_specs=[pl.BlockSpec((seq, 512), lambda b: (b, 0)),
                  pl.BlockSpec((lc, 512), lambda b: (b, 0)),
                  const(3, SSD_CONV * SSD_CHUNK, CONV_WIN),
                  of_layer(SSD_CONV, LRU_WIDTH), of_layer(1, LRU_WIDTH), of_layer(LRU_WIDTH, 1024), of_layer(1, 1024),
                  of_layer(1, 512)],
        out_specs=(pl.BlockSpec((seq, LRU_WIDTH), lambda b: (b, 0)),
                   pl.BlockSpec((lc, LRU_WIDTH), lambda b: (b, 0))),
        scratch_shapes=[pltpu.VMEM((4, seq, LANES), F32), pltpu.VMEM((4, seq, LANES), F32),
                        pltpu.VMEM((4, LRU_SEGMENTS, LANES), F32)],
        compiler_params=_cparams(("parallel",), 48),
        name="lru_mixer",
    )(p_lat, p_ctx, conv_sel, conv_w, conv_b, w_gates, b_gates, lam)


def _stack_heads(qblk):
    lo = _lane_iota((qblk.shape[0], LANES)) < HEAD_DIM
    zero = jnp.zeros((), qblk.dtype)
    s0, s1 = qblk[:, 0:LANES], qblk[:, LANES:2 * LANES]
    return jnp.concatenate([jnp.where(lo, s0, zero), jnp.where(lo, s1, zero),
                            jnp.where(lo, zero, s0), jnp.where(lo, zero, s1)], axis=0)


def _unstack_heads(acc, t):
    lo = _lane_iota((t, LANES)) < HEAD_DIM
    return jnp.concatenate([jnp.where(lo, acc[0:t], acc[2 * t:3 * t]),
                            jnp.where(lo, acc[t:2 * t], acc[3 * t:4 * t])], axis=1)


MAX_SHIFT_BOUND = 50.0


def _gattn_body(bound_ref, q_ref, kl_ref, vl_ref, kc_ref, vc_ref, o_ref, qs_scr, m_scr, acc_scr, layer, tk, online):
    tq = q_ref.shape[0]
    half = 2 * tq
    qs_scr[...] = _stack_heads(q_ref[...])
    acc_scr[...] = jnp.zeros(acc_scr.shape, F32)
    if online:
        m_scr[...] = jnp.full(m_scr.shape, -jnp.inf, F32)

    def step(k, v):
        lo = _lane_iota(v.shape) < HEAD_DIM
        one = jnp.ones((), v.dtype)
        s = _dot_nt(qs_scr[...], k)
        if online:
            m_old = m_scr[...]
            shift = jnp.maximum(m_old, jnp.max(s, axis=-1, keepdims=True))
            m_scr[...] = shift
            acc_scr[...] = acc_scr[...] * jnp.exp2(m_old - shift)
        else:
            shift = bound_ref[layer]
        p = jnp.exp2(s - shift).astype(BF16)
        acc_scr[0:half, :] += _dot(p[0:half], jnp.where(lo, v, one))
        acc_scr[half:, :] += _dot(p[half:], jnp.where(lo, one, v))

    def body(j, _):
        r0 = pl.multiple_of(j * tk, tk)
        step(kl_ref[pl.ds(r0, tk), :], vl_ref[pl.ds(r0, tk), :])
        return 0

    lax.fori_loop(0, kl_ref.shape[0] // tk, body, 0, unroll=not online)
    step(kc_ref[...], vc_ref[...])
    acc = acc_scr[...]
    o_ref[...] = _unstack_heads(acc / pltpu.roll(acc, HEAD_DIM, 1), tq).astype(o_ref.dtype)


def _gattn_kernel(bound_ref, *refs, layer, tk):
    safe = bound_ref[layer] <= MAX_SHIFT_BOUND

    @pl.when(safe)
    def _():
        _gattn_body(bound_ref, *refs, layer, tk, False)

    @pl.when(jnp.logical_not(safe))
    def _():
        _gattn_body(bound_ref, *refs, layer, tk, True)


def _logit_bounds(q_gain, k_gain):
    return HEAD_DIM * Q_SCALE * 1.02 * jnp.max(jnp.abs(q_gain), axis=-1) * jnp.max(jnp.abs(k_gain), axis=-1)


def _global_attention(layer, bound, qkv_lat, qkv_ctx, batch, tq, tk):
    seq = qkv_lat.shape[0] // batch
    lc = qkv_ctx.shape[0] // batch
    nq = seq // tq

    return pl.pallas_call(
        functools.partial(_gattn_kernel, layer=layer, tk=tk),
        out_shape=jax.ShapeDtypeStruct((batch * seq, 256), BF16),
        grid=(batch, nq),
        in_specs=[pl.BlockSpec(memory_space=pltpu.SMEM),
                  pl.BlockSpec((tq, 256), lambda b, i: (b * nq + i, 0)),
                  pl.BlockSpec((seq, LANES), lambda b, i: (b, 2)),
                  pl.BlockSpec((seq, LANES), lambda b, i: (b, 3)),
                  pl.BlockSpec((lc, LANES), lambda b, i: (b, 2)),
                  pl.BlockSpec((lc, LANES), lambda b, i: (b, 3))],
        out_specs=pl.BlockSpec((tq, 256), lambda b, i: (b * nq + i, 0)),
        scratch_shapes=[pltpu.VMEM((4 * tq, LANES), BF16), pltpu.VMEM((4 * tq, 1), F32),
                        pltpu.VMEM((4 * tq, LANES), F32)],
        compiler_params=_cparams(("parallel", "parallel"), 48),
        name="global_attention",
    )(bound, qkv_lat, qkv_lat, qkv_lat, qkv_ctx, qkv_ctx)


def _sink_rows(sink_ref, layer, t):
    head_of_row = _row_iota((4 * t, 1)) // t
    out = jnp.zeros((4 * t, 1), F32)
    for h in range(Q_HEADS):
        out = jnp.where(head_of_row == h, sink_ref[layer, h] * LOG2_E, out)
    return out


def _wattn_body(bound_ref, sink_ref, q_ref, kl_ref, vl_ref, kc_ref, vc_ref, o_ref, bias_scr, layer, online):
    w = WINDOW
    seq = kl_ref.shape[0]
    band = 3 * w
    shift = 0.0 if online else bound_ref[layer]
    dmat = _lane_iota((w, band)) - _row_iota((w, band))
    for case in range(3):
        bias_scr[case] = jnp.where(jnp.abs(dmat - case * w) <= w, -shift, -jnp.inf)
    one = jnp.ones((), BF16)
    kc, vc = kc_ref[...], vc_ref[...]
    lo_c = _lane_iota(vc.shape) < HEAD_DIM
    vc_sum = (jnp.where(lo_c, vc, one), jnp.where(lo_c, one, vc))
    lo_b = _lane_iota((band, LANES)) < HEAD_DIM
    lo = _lane_iota((w, LANES)) < HEAD_DIM

    def body(n, _):
        r0 = pl.multiple_of(n * w, w)
        gq = pl.program_id(1) * q_ref.shape[0] + r0
        ks = pl.multiple_of(jnp.clip(gq - w, 0, seq - band), w)
        bias = bias_scr[(gq - ks) // w]
        kb, vb = kl_ref[pl.ds(ks, band), :], vl_ref[pl.ds(ks, band), :]
        vb_sum = (jnp.where(lo_b, vb, one), jnp.where(lo_b, one, vb))
        qs = _stack_heads(q_ref[pl.ds(r0, w), :])
        s_band = _dot_nt(qs, kb)
        s_ctx = _dot_nt(qs, kc)
        outs = []
        for g in range(2):
            pb, pc, sk = [], [], []
            for h in (2 * g, 2 * g + 1):
                sb = s_band[h * w:(h + 1) * w] + bias
                sc = s_ctx[h * w:(h + 1) * w] - shift
                sink = sink_ref[layer, h] * LOG2_E - shift
                if online:
                    m = jnp.maximum(jnp.maximum(jnp.max(sb, axis=-1, keepdims=True),
                                                jnp.max(sc, axis=-1, keepdims=True)), sink)
                    sb, sc, sink = sb - m, sc - m, sink - m
                pb.append(jnp.exp2(sb).astype(BF16))
                pc.append(jnp.exp2(sc).astype(BF16))
                sk.append(jnp.exp2(sink) if online else jnp.full((w, 1), jnp.exp2(sink), F32))
            acc = _dot(jnp.concatenate(pb, axis=0), vb_sum[g]) + _dot(jnp.concatenate(pc, axis=0), vc_sum[g])
            den = pltpu.roll(acc, HEAD_DIM, 1) + jnp.concatenate(sk, axis=0)
            outs.append(acc / den)
        o_ref[pl.ds(r0, w), :] = jnp.concatenate(
            [jnp.where(lo, outs[0][0:w], outs[1][0:w]), jnp.where(lo, outs[0][w:2 * w], outs[1][w:2 * w])],
            axis=1).astype(o_ref.dtype)
        return 0

    lax.fori_loop(0, q_ref.shape[0] // w, body, 0, unroll=2 if online else 16)


def _wattn_kernel(bound_ref, *refs, layer):
    safe = bound_ref[layer] <= MAX_SHIFT_BOUND

    @pl.when(safe)
    def _():
        _wattn_body(bound_ref, *refs, layer, False)

    @pl.when(jnp.logical_not(safe))
    def _():
        _wattn_body(bound_ref, *refs, layer, True)


def _window_attention(layer, bound, sink, qkv_lat, qkv_ctx, batch, tq):
    seq = qkv_lat.shape[0] // batch
    lc = qkv_ctx.shape[0] // batch
    nq = seq // tq

    return pl.pallas_call(
        functools.partial(_wattn_kernel, layer=layer),
        out_shape=jax.ShapeDtypeStruct((batch * seq, 256), BF16),
        grid=(batch, nq),
        in_specs=[pl.BlockSpec(memory_space=pltpu.SMEM), pl.BlockSpec(memory_space=pltpu.SMEM),
                  pl.BlockSpec((tq, 256), lambda b, i: (b * nq + i, 0)),
                  pl.BlockSpec((seq, LANES), lambda b, i: (b, 2)),
                  pl.BlockSpec((seq, LANES), lambda b, i: (b, 3)),
                  pl.BlockSpec((lc, LANES), lambda b, i: (b, 2)),
                  pl.BlockSpec((lc, LANES), lambda b, i: (b, 3))],
        out_specs=pl.BlockSpec((tq, 256), lambda b, i: (b * nq + i, 0)),
        scratch_shapes=[pltpu.VMEM((3, WINDOW, 3 * WINDOW), F32)],
        compiler_params=_cparams(("parallel", "parallel"), 32),
        name="window_attention",
    )(bound, sink, qkv_lat, qkv_lat, qkv_lat, qkv_ctx, qkv_ctx)


def _cattn_kernel(sink_ref, c_ref, d_ref, oc_ref, od_ref, *, layer):
    t = c_ref.shape[0]
    for ref, out, has_sink in ((c_ref, oc_ref, False), (d_ref, od_ref, True)):
        qs = _stack_heads(ref[:, 0:256])
        s = _dot_nt(qs, ref[:, 256:384])
        m = jnp.max(s, axis=-1, keepdims=True)
        if has_sink:
            sink = _sink_rows(sink_ref, layer, t)
            m = jnp.maximum(m, sink)
        p = jnp.exp2(s - m)
        den = jnp.sum(p, axis=-1, keepdims=True)
        if has_sink:
            den = den + jnp.exp2(sink - m)
        acc = _dot(p.astype(BF16), ref[:, 384:512])
        out[...] = _unstack_heads(acc / den, t).astype(out.dtype)


def _context_attention(layer, sink, qkv_c, qkv_d, batch):
    lc = qkv_c.shape[0] // batch
    blk = lambda width: pl.BlockSpec((lc, width), lambda b: (b, 0))
    return pl.pallas_call(
        functools.partial(_cattn_kernel, layer=layer),
        out_shape=(jax.ShapeDtypeStruct((batch * lc, 256), BF16),) * 2,
        grid=(batch,),
        in_specs=[pl.BlockSpec(memory_space=pltpu.SMEM), blk(512), blk(512)],
        out_specs=(blk(256), blk(256)),
        compiler_params=_cparams(("parallel",), 32),
        name="context_attention",
    )(sink, qkv_c, qkv_d)


def _out_ffn_kernel(x_ref, ya_ref, yb_ref, yc_ref, yd_ref, mod_ref, wo_ref, g_ref, w1_ref, w2_ref, o_ref, *, hchunk):
    m = mod_ref[0]
    gw = GROUP_WIDTH
    mix = _dot(ya_ref[...], wo_ref[0, 0:gw, :])
    for i, y_ref in enumerate((yb_ref, yc_ref, yd_ref), start=1):
        mix = mix + _dot(y_ref[...], wo_ref[0, i * gw:(i + 1) * gw, :])
    x1 = x_ref[...] + m[2:3] * mix
    ms = jnp.mean(x1 * x1, axis=-1, keepdims=True)
    h = ((x1 * lax.rsqrt(ms + EPS) * g_ref[0]) * (1.0 + m[4:5]) + m[3:4]).astype(BF16)
    acc = None
    for c in range(w1_ref.shape[2] // hchunk):
        u = jnp.maximum(_dot(h, w1_ref[0, :, c * hchunk:(c + 1) * hchunk]), 0.0)
        t = _dot((u * u).astype(BF16), w2_ref[0, c * hchunk:(c + 1) * hchunk, :])
        acc = t if acc is None else acc + t
    o_ref[...] = x1 + m[5:6] * acc


def _out_ffn(layer, x2d, ys, mod, mod_row_of_tile, w_out, g, w1, w2, tm):
    rows, d = x2d.shape
    hidden = w1.shape[2]
    row_blk = lambda width: pl.BlockSpec((tm, width), lambda i: (i, 0))
    const = lambda *shape: pl.BlockSpec((1,) + shape, lambda i: (layer,) + (0,) * len(shape), pipeline_mode=pl.Buffered(1))
    return pl.pallas_call(
        functools.partial(_out_ffn_kernel, hchunk=1024),
        out_shape=jax.ShapeDtypeStruct((rows, d), F32),
        grid=(rows // tm,),
        in_specs=[row_blk(d), row_blk(256), row_blk(256), row_blk(256), row_blk(256),
                  pl.BlockSpec((1, N_MOD, d), lambda i: (mod_row_of_tile(i), 0, 0)),
                  const(d, d), pl.BlockSpec((1, 1, d), lambda i: (layer, 0, 0)), const(d, hidden), const(hidden, d)],
        out_specs=row_blk(d),
        compiler_params=_cparams(("parallel",), 56),
        name="out_ffn",
    )(x2d, *ys, mod, w_out, g, w1, w2)


def _permute_q_heads(cols):
    q = cols[..., 0:256].reshape(cols.shape[:-1] + (2, 2, HEAD_DIM))
    q = jnp.swapaxes(q, -3, -2).reshape(cols.shape[:-1] + (256,))
    return jnp.concatenate([q, cols[..., 256:512]], axis=-1)


def _relayout_w_in(w):
    ssd_in = GROUP_WIDTH + SSD_CONV_CH + 2 * SSD_HEADS
    b0, c0, d0 = ssd_in, ssd_in + 512, ssd_in + 1024
    dt = w[..., GROUP_WIDTH + SSD_CONV_CH:ssd_in]
    dt = jnp.pad(dt, [(0, 0)] * (w.ndim - 1) + [(0, LANES - 2 * SSD_HEADS)])
    return jnp.concatenate([w[..., 0:GROUP_WIDTH + SSD_CONV_CH], w[..., b0:c0], _permute_q_heads(w[..., c0:d0]),
                            _permute_q_heads(w[..., d0:d0 + 512]), dt], axis=-1).astype(BF16)


def _relayout_w_out(w):
    depth, _, d = w.shape
    attn = jnp.swapaxes(w[:, 512:1024].reshape(depth, 2, 2, 2, HEAD_DIM, d), 2, 3).reshape(depth, 512, d)
    return jnp.concatenate([w[:, 0:512], attn], axis=1).astype(BF16)


def _block_diag(w):
    eye = jnp.eye(LRU_BLOCKS, dtype=w.dtype)
    out = jnp.einsum('...kij,kn->...kinj', w, eye)
    return out.reshape(w.shape[:-3] + (LRU_WIDTH, LRU_WIDTH))


def _rope_tables(length):
    pos = np.arange(length)
    inv = ROPE_THETA ** (-np.arange(0, ROPE_AXIS_DIM, 2, dtype=np.float64) / ROPE_AXIS_DIM)
    row = (pos // GRID_W)[:, None] * inv
    col = (pos % GRID_W)[:, None] * inv
    ang = np.concatenate([row, row, col, col], axis=1)
    half = ROPE_AXIS_DIM // 2
    sign = np.tile(np.concatenate([-np.ones(half), np.ones(half)]), 2)
    reps = (1, LANES // HEAD_DIM)
    return (jnp.asarray(np.tile(np.cos(ang), reps), F32), jnp.asarray(np.tile(np.sin(ang) * sign, reps), F32))


def _head_selectors():
    sel = np.zeros((2, 2 * LANES, SSD_HEADS * LANES), np.float32)
    for d in range(2):
        for h in range(SSD_HEADS):
            for part in range(2):
                sel[d, part * LANES + SSD_HEADS * d + h, h * LANES:(h + 1) * LANES] = 1.0
    return jnp.asarray(sel, BF16)


def _head_mean_matrix():
    idx = np.arange(256) // HEAD_DIM
    return jnp.asarray((idx[:, None] == idx[None, :]).astype(np.float32) / HEAD_DIM, BF16)


def kernel(x, c, ctx, c_ctx, w_mod, b_mod, g_mix, w_in, ssd_conv_w, ssd_conv_b, ssd_a_log, ssd_dt_bias, ssd_d, ssd_norm_g, lru_conv_w, lru_conv_b, lru_lambda, lru_w_a, lru_b_a, lru_w_i, lru_b_i, gqa_q_norm, gqa_k_norm, swa_q_norm, swa_k_norm, swa_sink, w_out, g_ffn, w_ffn1, w_ffn2):
    batch, seq, d = x.shape
    lc = ctx.shape[1]
    depth = w_mod.shape[0]
    assert batch + 1 <= 8 and seq % 1024 == 0 and lc % 256 == 0 and seq % GRID_W == 0

    cvecs = jnp.concatenate([c, c_ctx[None, :], jnp.zeros((8 - batch - 1, d), F32)], axis=0)
    mod = _modulation(cvecs, w_mod, b_mod).reshape(depth * 8, N_MOD, d)

    rope = _rope_tables(seq)
    sel = _head_selectors()
    bd = _head_mean_matrix()
    conv_sel = _conv_selectors()
    w_in_r = _relayout_w_in(w_in)
    w_out_r = _relayout_w_out(w_out)
    w1, w2 = w_ffn1.astype(BF16), w_ffn2.astype(BF16)
    g1, g2 = g_mix.reshape(depth, 1, d), g_ffn.reshape(depth, 1, d)
    nrm = jnp.tile(jnp.stack([gqa_q_norm, gqa_k_norm, swa_q_norm, swa_k_norm], axis=1), (1, 1, 256 // HEAD_DIM))
    lane_pad = ((0, 0), (0, 0), (0, LANES - 2 * SSD_HEADS))
    head_params = jnp.concatenate([jnp.pad(ssd_dt_bias.reshape(depth, 1, -1), lane_pad),
                                   jnp.pad(ssd_a_log.reshape(depth, 1, -1), lane_pad),
                                   jnp.zeros((depth, 6, LANES), F32)], axis=1)
    ssd_cb = ssd_conv_b.reshape(depth, 1, -1)
    d_skip = jnp.repeat(ssd_d, HEAD_DIM, axis=-1).reshape(depth, 1, -1)
    ssd_ng = ssd_norm_g.reshape(depth, 1, -1)
    bd_a, bd_i = _block_diag(lru_w_a), _block_diag(lru_w_i)
    w_gates = (0.5 * jnp.concatenate([bd_a[:, 0], bd_i[:, 0], bd_a[:, 1], bd_i[:, 1]], axis=-1)).astype(BF16)
    b_gates = 0.5 * jnp.concatenate([lru_b_a[:, 0], lru_b_i[:, 0], lru_b_a[:, 1], lru_b_i[:, 1]], axis=-1).reshape(depth, 1, -1)
    lru_cb = lru_conv_b.reshape(depth, 1, -1)
    lam = lru_lambda.reshape(depth, 1, -1)
    bounds = _logit_bounds(gqa_q_norm, gqa_k_norm)
    swa_bounds = _logit_bounds(swa_q_norm, swa_k_norm)

    xl = x.reshape(batch * seq, d)
    xc = ctx.reshape(batch * lc, d)
    for l in range(depth):
        need_ctx = l < depth - 1
        lat_row = lambda rows: (lambda i, l=l: l * 8 + i // (seq // rows))
        ctx_row = lambda i, l=l: l * 8 + batch
        ssd_l, lru_l, qc_l, qd_l, dt_l = _in_projection(l, xl, mod, lat_row(IN_PROJ_ROWS), g1, w_in_r, nrm, bd, rope,
                                                        IN_PROJ_ROWS, seq // IN_PROJ_ROWS)
        ssd_c, lru_c, qc_c, qd_c, dt_c = _in_projection(l, xc, mod, ctx_row, g1, w_in_r, nrm, bd, None, lc, 1)
        ya_l, ya_c = _ssd_mixer(l, ssd_l, ssd_c, dt_l, dt_c, conv_sel, ssd_conv_w, ssd_cb, head_params, sel, d_skip,
                                ssd_ng, batch)
        yb_l, yb_c = _lru_mixer(l, lru_l, lru_c, conv_sel, lru_conv_w, lru_cb, w_gates, b_gates, lam, batch)
        yc_l = _global_attention(l, bounds, qc_l, qc_c, batch, GATTN_Q_ROWS, GATTN_KV_ROWS)
        yd_l = _window_attention(l, swa_bounds, swa_sink, qd_l, qd_c, batch, seq)
        xl = _out_ffn(l, xl, (ya_l, yb_l, yc_l, yd_l), mod, lat_row(FFN_ROWS), w_out_r, g2, w1, w2, FFN_ROWS)
        if need_ctx:
            yc_c, yd_c = _context_attention(l, swa_sink, qc_c, qd_c, batch)
            xc = _out_ffn(l, xc, (ya_c, yb_c, yc_c, yd_c), mod, ctx_row, w_out_r, g2, w1, w2, FFN_ROWS)
    return xl.reshape(batch, seq, d)
```

```python
import functools
import math

import numpy as np
import jax
import jax.numpy as jnp
from jax import lax
from jax.experimental import pallas as pl
from jax.experimental.pallas import tpu as pltpu

F32 = jnp.float32
BF16 = jnp.bfloat16

EPS = 1e-6
GRID_W = 64
N_MOD = 6
GROUP_WIDTH = 256
SSD_HEADS = 4
SSD_STATE = 128
SSD_CONV = 4
SSD_CHUNK = 128
SSD_CONV_CH = 768
LRU_WIDTH = 256
LRU_BLOCKS = 4
LRU_C = 8.0
HEAD_DIM = 64
Q_HEADS = 4
WINDOW = 128
ROPE_THETA = 10000.0
ROPE_AXIS_DIM = 32
LOG2_E = math.log2(math.e)
Q_SCALE = HEAD_DIM ** -0.5 * LOG2_E

LANES = 128
BF16_ROWS = 16

COL_SSD, COL_LRU, COL_C, COL_D, COL_DT, COL_END = 0, 1024, 1536, 2048, 2560, 2688

CONV_WIN = SSD_CHUNK + 2 * BF16_ROWS

IN_PROJ_ROWS = 1024
FFN_ROWS = 512
GATTN_Q_ROWS = 512
GATTN_KV_ROWS = 512
SSD_GROUP = 8


def _cparams(sem, vmem_mib):
    return pltpu.CompilerParams(dimension_semantics=sem, vmem_limit_bytes=vmem_mib * 1024 * 1024)


def _dot(a, b):
    return jnp.dot(a, b, preferred_element_type=F32)


def _dot_nt(a, b):
    return lax.dot_general(a, b, (((1,), (1,)), ((), ())), preferred_element_type=F32)


def _split_bf16(a, parts):
    out = []
    for _ in range(parts - 1):
        hi = a.astype(BF16)
        out.append(hi)
        a = a - hi.astype(F32)
    out.append(a.astype(BF16))
    return out


def _silu(x):
    h = 0.5 * x
    return h + h * jnp.tanh(h)


def _softplus(x):
    return jnp.maximum(x, 0.0) + jnp.log1p(jnp.exp(-jnp.abs(x)))


def _gelu_tanh(x):
    return 0.5 * x * (1.0 + jnp.tanh(math.sqrt(2.0 / math.pi) * (x + 0.044715 * (x * x * x))))


def _lane_iota(shape):
    return lax.broadcasted_iota(jnp.int32, shape, len(shape) - 1)


def _row_iota(shape):
    return lax.broadcasted_iota(jnp.int32, shape, len(shape) - 2)


def _mod_kernel(cv_ref, w_ref, b_ref, o_ref):
    cv = cv_ref[...]
    s = _silu(cv)
    w = w_ref[0]
    s_hi, s_lo = _split_bf16(s, 2)
    w_hi, w_lo = _split_bf16(w, 2)
    acc = _dot(s_hi, w_hi) + _dot(s_lo, w_hi) + _dot(s_hi, w_lo)
    o_ref[0] = acc + b_ref[0]


def _modulation(cvecs, w_mod, b_mod):
    depth, d, n = w_mod.shape
    tn = 1024
    return pl.pallas_call(
        _mod_kernel,
        out_shape=jax.ShapeDtypeStruct((depth, 8, n), F32),
        grid=(depth, n // tn),
        in_specs=[pl.BlockSpec((8, d), lambda l, j: (0, 0)),
                  pl.BlockSpec((1, d, tn), lambda l, j: (l, 0, j)),
                  pl.BlockSpec((1, 1, tn), lambda l, j: (l, 0, j))],
        out_specs=pl.BlockSpec((1, 8, tn), lambda l, j: (l, 0, j)),
        compiler_params=_cparams(("parallel", "parallel"), 32),
        name="modulation",
    )(cvecs, w_mod, b_mod.reshape(depth, 1, n))


def _head_norm_rope(p, g, bd, cos, sin, scale):
    ms = _dot((p * p).astype(BF16), bd)
    y = p * lax.rsqrt(ms + EPS) * g
    if cos is not None:
        w = y.shape[-1]
        first = (_lane_iota(y.shape) % ROPE_AXIS_DIM) < (ROPE_AXIS_DIM // 2)
        partner = jnp.where(first, pltpu.roll(y, w - ROPE_AXIS_DIM // 2, 1), pltpu.roll(y, ROPE_AXIS_DIM // 2, 1))
        y = y * cos + partner * sin
    if scale != 1.0:
        y = y * scale
    return y


def _inproj_kernel(*refs, use_rope):
    if use_rope:
        x_ref, mod_ref, g_ref, w_ref, nrm_ref, bd_ref, cos_ref, sin_ref, ssd_ref, lru_ref, c_ref, d_ref, dt_ref = refs
    else:
        x_ref, mod_ref, g_ref, w_ref, nrm_ref, bd_ref, ssd_ref, lru_ref, c_ref, d_ref, dt_ref = refs
    x = x_ref[...]
    ms = jnp.mean(x * x, axis=-1, keepdims=True)
    m = mod_ref[0]
    h = (x * lax.rsqrt(ms + EPS) * g_ref[0]) * (1.0 + m[1:2]) + m[0:1]
    hb = h.astype(BF16)
    bd = bd_ref[...]
    if use_rope:
        cos1, sin1 = cos_ref[...], sin_ref[...]
        cos2 = jnp.concatenate([cos1, cos1], axis=1)
        sin2 = jnp.concatenate([sin1, sin1], axis=1)
    else:
        cos1 = sin1 = cos2 = sin2 = None
    for o_ref, c0, row in ((c_ref, COL_C, 0), (d_ref, COL_D, 2)):
        p = _dot(hb, w_ref[0, :, c0:c0 + 512])
        q = _head_norm_rope(p[:, 0:256], nrm_ref[0, row:row + 1, :], bd, cos2, sin2, Q_SCALE)
        k = _head_norm_rope(p[:, 256:384], nrm_ref[0, row + 1:row + 2, 0:128], bd[0:128, 0:128], cos1, sin1, 1.0)
        o_ref[:, 0:256] = q.astype(BF16)
        o_ref[:, 256:384] = k.astype(BF16)
        o_ref[:, 384:512] = p[:, 384:512].astype(BF16)
    ssd_ref[...] = _dot(hb, w_ref[0, :, COL_SSD:COL_LRU]).astype(BF16)
    lru_ref[...] = _dot(hb, w_ref[0, :, COL_LRU:COL_C]).astype(BF16)
    dt_ref[...] = _dot(hb, w_ref[0, :, COL_DT:COL_END])


def _in_projection(layer, x2d, mod, mod_row_of_tile, g, w, nrm, bd, rope, tm, rope_tiles):
    rows, d = x2d.shape
    use_rope = rope is not None
    in_specs = [pl.BlockSpec((tm, d), lambda i: (i, 0)),
                pl.BlockSpec((1, N_MOD, d), lambda i: (mod_row_of_tile(i), 0, 0)),
                pl.BlockSpec((1, 1, d), lambda i: (layer, 0, 0)),
                pl.BlockSpec((1, d, COL_END), lambda i: (layer, 0, 0), pipeline_mode=pl.Buffered(1)),
                pl.BlockSpec((1, 4, 256), lambda i: (layer, 0, 0)),
                pl.BlockSpec((256, 256), lambda i: (0, 0))]
    args = [x2d, mod, g, w, nrm, bd]
    if use_rope:
        in_specs += [pl.BlockSpec((tm, LANES), lambda i: (i % rope_tiles, 0))] * 2
        args += list(rope)
    widths = (1024, 512, 512, 512, LANES)
    dtypes = (BF16, BF16, BF16, BF16, F32)
    return pl.pallas_call(
        functools.partial(_inproj_kernel, use_rope=use_rope),
        out_shape=tuple(jax.ShapeDtypeStruct((rows, wd), dt) for wd, dt in zip(widths, dtypes)),
        grid=(rows // tm,),
        in_specs=in_specs,
        out_specs=tuple(pl.BlockSpec((tm, wd), lambda i: (i, 0)) for wd in widths),
        compiler_params=_cparams(("parallel",), 56),
        name="in_projection",
    )(*args)


def _conv_selectors():
    row = np.arange(SSD_CONV * SSD_CHUNK)
    want = (row % SSD_CHUNK) + (row // SSD_CHUNK) - SSD_CONV // 2
    col = np.arange(CONV_WIN)
    sel = np.stack([(col[None, :] == want[:, None] + BF16_ROWS * case) for case in range(3)])
    return jnp.asarray(sel.astype(np.float32), BF16)


def _conv_chunk(seq_ref, col0, width, length, r0, sel_ref, w_ref, b_ref):
    ws = jnp.clip(r0 - BF16_ROWS, 0, length - CONV_WIN)
    ws = pl.multiple_of(ws, BF16_ROWS)
    xw = seq_ref[pl.ds(ws, CONV_WIN), col0:col0 + width]
    sh = _dot(sel_ref[(r0 - ws) // BF16_ROWS], xw)
    w = w_ref[0]
    acc = b_ref[0] + sh[0:SSD_CHUNK] * w[0:1]
    for k in range(1, SSD_CONV):
        acc = acc + sh[k * SSD_CHUNK:(k + 1) * SSD_CHUNK] * w[k:k + 1]
    return acc


def _ssd_kernel(pl_ref, pc_ref, dtl_ref, dtc_ref, csel_ref, cw_ref, cb_ref, hp_ref, sel_ref, dsk_ref, ng_ref,
                yl_ref, yc_ref, accl_ref, accc_ref, xbcl_ref, xbcc_ref):
    q = SSD_CHUNK
    lane8 = _lane_iota((1, LANES)) < 2 * SSD_HEADS
    dt_bias = hp_ref[0, 0:1, :]
    a_row = jnp.where(lane8, -jnp.exp(hp_ref[0, 1:2, :]) * LOG2_E, 0.0)
    ii = _row_iota((q, q))
    jj = _lane_iota((q, q))
    lo = _lane_iota((q, LANES)) < HEAD_DIM
    masks = (jj <= ii, jj >= ii)
    tris = [jnp.where(m, 1.0, 0.0).astype(BF16) for m in masks]
    tri_pairs = [jnp.concatenate([t, t], axis=1) for t in tris]
    zblk = jnp.zeros((SSD_STATE, LANES), BF16)

    def block_diag(a0, a1):
        return jnp.concatenate([jnp.concatenate([a0, zblk], axis=1), jnp.concatenate([zblk, a1], axis=1)], axis=0)

    def chunk(xbc_ref, r0, d, state, acum, dtb, acb):
        xs = xbc_ref[pl.ds(r0, q), 0:256].astype(F32)
        bm = xbc_ref[pl.ds(r0, q), 256:512].astype(F32)
        cmb = xbc_ref[pl.ds(r0, q), 512:768]
        mask = masks[d]
        act = acum.T
        edge = q - 1 if d == 0 else 0
        per_head = []
        for h in range(SSD_HEADS):
            ac = acb[:, h * LANES:(h + 1) * LANES]
            ar = act[SSD_HEADS * d + h:SSD_HEADS * d + h + 1, :]
            lmat = jnp.where(mask, jnp.exp2(ac - ar), 0.0)
            alast = ac[edge:edge + 1, :]
            dth = dtb[:, h * LANES:(h + 1) * LANES]
            per_head.append((lmat, dth, jnp.exp2(alast - ac) * dth, jnp.exp2(ac), jnp.exp2(alast)))
        bts = [bm[:, g * LANES:(g + 1) * LANES].T.astype(BF16) for g in range(2)]
        cb_all = _dot(cmb, block_diag(bts[0], bts[1]))
        yo_all = _dot(cmb, block_diag(state[0].astype(BF16), state[1].astype(BF16)))
        ys, new_state = [], []
        for g in range(2):
            (l0, dt0, we0, ei0, cd0), (l1, dt1, we1, ei1, cd1) = per_head[2 * g], per_head[2 * g + 1]
            xg = xs[:, g * LANES:(g + 1) * LANES]
            cb = cb_all[:, g * LANES:(g + 1) * LANES]
            xdt = xg * jnp.where(lo, dt0, dt1)
            scores = jnp.concatenate([(cb * l0).astype(BF16), (cb * l1).astype(BF16)], axis=1)
            xpair = jnp.concatenate([jnp.where(lo, xdt, 0.0), jnp.where(lo, 0.0, xdt)], axis=0).astype(BF16)
            y = _dot(scores, xpair) + yo_all[:, g * LANES:(g + 1) * LANES] * jnp.where(lo, ei0, ei1)
            s_new = _dot(bts[g], (xg * jnp.where(lo, we0, we1)).astype(BF16))
            new_state.append(state[g] * jnp.where(lo, cd0, cd1) + s_new)
            ys.append(y)
        return jnp.concatenate(ys, axis=1), xs, tuple(new_state)

    def sweep(seq_ref, xbc_ref, dt_ref, acc_ref, out_ref, d, state):
        length = seq_ref.shape[0]
        nc = length // q
        gsz = min(nc, SSD_GROUP)
        ngroups = nc // gsz

        def body(gi, st):
            g0 = pl.multiple_of((gi if d == 0 else ngroups - 1 - gi) * (gsz * q), gsz * q)
            if d == 0:
                for c in range(gsz):
                    conv = _conv_chunk(seq_ref, GROUP_WIDTH, SSD_CONV_CH, length, g0 + c * q, csel_ref, cw_ref, cb_ref)
                    xbc_ref[pl.ds(g0 + c * q, q), :] = _silu(conv).astype(BF16)
            dtv = _softplus(dt_ref[pl.ds(g0, gsz * q), :] + dt_bias)
            hi, lo_part = _split_bf16(dtv * a_row, 2)
            rhs = jnp.concatenate([jnp.concatenate([hi[c * q:(c + 1) * q], lo_part[c * q:(c + 1) * q]], axis=0)
                                   for c in range(gsz)], axis=1)
            acum = _dot(tri_pairs[d], rhs)
            stacked = jnp.concatenate([blk for c in range(gsz)
                                       for blk in (dtv[c * q:(c + 1) * q], acum[:, c * LANES:(c + 1) * LANES])], axis=0)
            ex = _dot(jnp.concatenate(_split_bf16(stacked, 2), axis=1), sel_ref[d])
            for i in range(gsz):
                c = i if d == 0 else gsz - 1 - i
                r0 = g0 + c * q
                y, xs, st = chunk(xbc_ref, r0, d, st, acum[:, c * LANES:(c + 1) * LANES],
                                  ex[2 * c * q:(2 * c + 1) * q], ex[(2 * c + 1) * q:(2 * c + 2) * q])
                if d == 0:
                    acc_ref[pl.ds(r0, q), :] = y
                else:
                    y = acc_ref[pl.ds(r0, q), :] + y + dsk_ref[0] * xs
                    z = seq_ref[pl.ds(r0, q), 0:GROUP_WIDTH].astype(F32)
                    t = y * _silu(z)
                    ms = jnp.mean(t * t, axis=-1, keepdims=True)
                    out_ref[pl.ds(r0, q), :] = (t * lax.rsqrt(ms + EPS) * ng_ref[0]).astype(out_ref.dtype)
            return st

        return lax.fori_loop(0, ngroups, body, state)

    zero = (jnp.zeros((SSD_STATE, LANES), F32), jnp.zeros((SSD_STATE, LANES), F32))
    for d in range(2):
        st = sweep(pc_ref, xbcc_ref, dtc_ref, accc_ref, yc_ref, d, zero)
        sweep(pl_ref, xbcl_ref, dtl_ref, accl_ref, yl_ref, d, st)


def _ssd_mixer(layer, p_lat, p_ctx, dt_lat, dt_ctx, conv_sel, conv_w, conv_b, head_params, sel, d_skip, norm_g, batch):
    seq = p_lat.shape[0] // batch
    lc = p_ctx.shape[0] // batch
    const = lambda *shape: pl.BlockSpec(shape, lambda b: (0,) * len(shape))
    of_layer = lambda *shape: pl.BlockSpec((1,) + shape, lambda b: (layer,) + (0,) * len(shape))
    return pl.pallas_call(
        _ssd_kernel,
        out_shape=(jax.ShapeDtypeStruct((batch * seq, GROUP_WIDTH), BF16),
                   jax.ShapeDtypeStruct((batch * lc, GROUP_WIDTH), BF16)),
        grid=(batch,),
        in_specs=[pl.BlockSpec((seq, 1024), lambda b: (b, 0)),
                  pl.BlockSpec((lc, 1024), lambda b: (b, 0)),
                  pl.BlockSpec((seq, LANES), lambda b: (b, 0)),
                  pl.BlockSpec((lc, LANES), lambda b: (b, 0)),
                  const(3, SSD_CONV * SSD_CHUNK, CONV_WIN),
                  of_layer(SSD_CONV, SSD_CONV_CH), of_layer(1, SSD_CONV_CH), of_layer(8, LANES),
                  const(2, 2 * LANES, SSD_HEADS * LANES), of_layer(1, GROUP_WIDTH), of_layer(1, GROUP_WIDTH)],
        out_specs=(pl.BlockSpec((seq, GROUP_WIDTH), lambda b: (b, 0)),
                   pl.BlockSpec((lc, GROUP_WIDTH), lambda b: (b, 0))),
        scratch_shapes=[pltpu.VMEM((seq, GROUP_WIDTH), F32), pltpu.VMEM((lc, GROUP_WIDTH), F32),
                        pltpu.VMEM((seq, SSD_CONV_CH), BF16), pltpu.VMEM((lc, SSD_CONV_CH), BF16)],
        compiler_params=_cparams(("parallel",), 48),
        name="ssd_mixer",
    )(p_lat, p_ctx, dt_lat, dt_ctx, conv_sel, conv_w, conv_b, head_params, sel, d_skip, norm_g)


LRU_SEGMENTS = 8
LRU_GROUP = 8


def _lru_kernel(pl_ref, pc_ref, csel_ref, cw_ref, cb_ref, wg_ref, bg_ref, lam_ref, ol_ref, oc_ref, a_scr, pad_scr, b_scr, cin_scr):
    q = SSD_CHUNK
    nseg = LRU_SEGMENTS
    nhc = (0.5 * LRU_C) * _softplus(-lam_ref[0])

    def run(seq_ref, out_ref, h0):
        length = seq_ref.shape[0]
        seglen = length // nseg
        sub = min(q, seglen)

        def seg_rows(t0):
            return pl.ds((t0 % seglen) * nseg + t0 // seglen, sub, stride=nseg)

        gsz = min(length // q, LRU_GROUP)

        def gates(gi, _):
            g0 = pl.multiple_of(gi * (gsz * q), gsz * q)
            xcs = [_conv_chunk(seq_ref, LRU_WIDTH, LRU_WIDTH, length, g0 + c * q, csel_ref, cw_ref, cb_ref)
                   for c in range(gsz)]
            xc_all = jnp.concatenate(xcs, axis=0)
            t_all = jnp.tanh(_dot(xc_all.astype(BF16), wg_ref[0]) + bg_ref[0])
            for ci in range(gsz):
                r0 = g0 + ci * q
                t = t_all[ci * q:(ci + 1) * q]
                hx = 0.5 * xcs[ci]
                for d in range(2):
                    c = nhc[:, d * 256:(d + 1) * 256]
                    nla = c * t[:, d * 512:d * 512 + 256] + c
                    a = jnp.exp2(nla * -LOG2_E)
                    y = jnp.tanh(nla) * (a * a + 1.0)
                    root = jnp.where(y > 0.0, y * lax.rsqrt(y), 0.0)
                    bv = root * (hx * t[:, d * 512 + 256:(d + 1) * 512] + hx)
                    for hf in range(2):
                        for j in range(q // sub):
                            rows = seg_rows(r0 + j * sub)
                            a_scr[2 * d + hf, rows, :] = a[j * sub:(j + 1) * sub, hf * LANES:(hf + 1) * LANES]
                            b_scr[2 * d + hf, rows, :] = bv[j * sub:(j + 1) * sub, hf * LANES:(hf + 1) * LANES]
            return 0

        lax.fori_loop(0, length // (gsz * q), gates, 0)

        def scan(k, carry):
            hs, ps = carry
            new_h, new_p = [], []
            for qd in range(4):
                steps = [4 * k + i if qd < 2 else seglen - 1 - (4 * k + i) for i in range(4)]
                idxs = [(qd, pl.ds(pl.multiple_of(st * nseg, nseg), nseg), slice(None)) for st in steps]
                a = [a_scr[ix] for ix in idxs]
                b = [b_scr[ix] for ix in idxs]
                h, p = hs[qd], ps[qd]
                a01, b01 = a[1] * a[0], a[1] * b[0] + b[1]
                a23, b23 = a[3] * a[2], a[3] * b[2] + b[3]
                a03, b03 = a23 * a01, a23 * b01 + b23
                h1 = a[0] * h + b[0]
                h2 = a01 * h + b01
                h3 = a[2] * h2 + b[2]
                h4 = a03 * h + b03
                p2 = p * a01
                for ix, hv, pv in zip(idxs, (h1, h2, h3, h4), (p * a[0], p2, p2 * a[2], p * a03)):
                    b_scr[ix] = hv
                    a_scr[ix] = pv
                new_h.append(h4)
                new_p.append(p * a03)
            return tuple(new_h), tuple(new_p)

        z8 = jnp.zeros((nseg, LANES), F32)
        o8 = jnp.ones((nseg, LANES), F32)
        hend, pend = lax.fori_loop(0, seglen // 4, scan, ((z8,) * 4, (o8,) * 4), unroll=1)

        h_out = []
        for qd in range(4):
            c = h0[qd]
            order = range(nseg) if qd < 2 else range(nseg - 1, -1, -1)
            for k in order:
                cin_scr[qd, k:k + 1, :] = c
                c = hend[qd][k:k + 1, :] + pend[qd][k:k + 1, :] * c
            h_out.append(c)

        cins = [cin_scr[qd] for qd in range(4)]

        def fixup(st, _):
            for qd in range(4):
                idx = (qd, pl.ds(pl.multiple_of(st * nseg, nseg), nseg), slice(None))
                b_scr[idx] = b_scr[idx] + a_scr[idx] * cins[qd]
            return 0

        lax.fori_loop(0, seglen, fixup, 0, unroll=8)

        def emit(c, _):
            r0 = pl.multiple_of(c * sub, sub)
            rows = seg_rows(r0)
            hq = [b_scr[qd, rows, :] for qd in range(4)]
            hsum = jnp.concatenate([hq[0] + hq[2], hq[1] + hq[3]], axis=1)
            gate = seq_ref[pl.ds(r0, sub), 0:LRU_WIDTH].astype(F32)
            out_ref[pl.ds(r0, sub), :] = (hsum * _gelu_tanh(gate)).astype(out_ref.dtype)
            return 0

        lax.fori_loop(0, length // sub, emit, 0)
        return h_out

    z1 = jnp.zeros((1, LANES), F32)
    states = run(pc_ref, oc_ref, [z1] * 4)
    run(pl_ref, ol_ref, states)


def _lru_mixer(layer, p_lat, p_ctx, conv_sel, conv_w, conv_b, w_gates, b_gates, lam, batch):
    seq = p_lat.shape[0] // batch
    lc = p_ctx.shape[0] // batch
    const = lambda *shape: pl.BlockSpec(shape, lambda b: (0,) * len(shape))
    of_layer = lambda *shape: pl.BlockSpec((1,) + shape, lambda b: (layer,) + (0,) * len(shape))
    return pl.pallas_call(
        _lru_kernel,
        out_shape=(jax.ShapeDtypeStruct((batch * seq, LRU_WIDTH), BF16),
                   jax.ShapeDtypeStruct((batch * lc, LRU_WIDTH), BF16)),
        grid=(batch,),
        in_specs=[pl.BlockSpec((seq, 512), lambda b: (b, 0)),
                  pl.BlockSpec((lc, 512), lambda b: (b, 0)),
                  const(3, SSD_CONV * SSD_CHUNK, CONV_WIN),
                  of_layer(SSD_CONV, LRU_WIDTH), of_layer(1, LRU_WIDTH), of_layer(LRU_WIDTH, 1024), of_layer(1, 1024),
                  of_layer(1, 512)],
        out_specs=(pl.BlockSpec((seq, LRU_WIDTH), lambda b: (b, 0)),
                   pl.BlockSpec((lc, LRU_WIDTH), lambda b: (b, 0))),
        scratch_shapes=[pltpu.VMEM((4, seq + 8, LANES), F32), pltpu.VMEM((16, LANES), F32), pltpu.VMEM((4, seq + 8, LANES), F32),
                        pltpu.VMEM((4, LRU_SEGMENTS, LANES), F32)],
        compiler_params=_cparams(("parallel",), 48),
        name="lru_mixer",
    )(p_lat, p_ctx, conv_sel, conv_w, conv_b, w_gates, b_gates, lam)


def _stack_heads(qblk):
    lo = _lane_iota((qblk.shape[0], LANES)) < HEAD_DIM
    zero = jnp.zeros((), qblk.dtype)
    s0, s1 = qblk[:, 0:LANES], qblk[:, LANES:2 * LANES]
    return jnp.concatenate([jnp.where(lo, s0, zero), jnp.where(lo, s1, zero),
                            jnp.where(lo, zero, s0), jnp.where(lo, zero, s1)], axis=0)


def _unstack_heads(acc, t):
    lo = _lane_iota((t, LANES)) < HEAD_DIM
    return jnp.concatenate([jnp.where(lo, acc[0:t], acc[2 * t:3 * t]),
                            jnp.where(lo, acc[t:2 * t], acc[3 * t:4 * t])], axis=1)


MAX_SHIFT_BOUND = 50.0


def _gattn_body(bound_ref, q_ref, kl_ref, vl_ref, kc_ref, vc_ref, o_ref, qs_scr, m_scr, acc_scr, layer, tk, online):
    tq = q_ref.shape[0]
    half = 2 * tq
    qs_scr[...] = _stack_heads(q_ref[...])
    acc_scr[...] = jnp.zeros(acc_scr.shape, F32)
    if online:
        m_scr[...] = jnp.full(m_scr.shape, -jnp.inf, F32)

    def step(k, v):
        lo = _lane_iota(v.shape) < HEAD_DIM
        one = jnp.ones((), v.dtype)
        s = _dot_nt(qs_scr[...], k)
        if online:
            m_old = m_scr[...]
            shift = jnp.maximum(m_old, jnp.max(s, axis=-1, keepdims=True))
            m_scr[...] = shift
            acc_scr[...] = acc_scr[...] * jnp.exp2(m_old - shift)
        else:
            shift = bound_ref[layer]
        p = jnp.exp2(s - shift).astype(BF16)
        acc_scr[0:half, :] += _dot(p[0:half], jnp.where(lo, v, one))
        acc_scr[half:, :] += _dot(p[half:], jnp.where(lo, one, v))

    def body(j, _):
        r0 = pl.multiple_of(j * tk, tk)
        step(kl_ref[pl.ds(r0, tk), :], vl_ref[pl.ds(r0, tk), :])
        return 0

    lax.fori_loop(0, kl_ref.shape[0] // tk, body, 0, unroll=not online)
    step(kc_ref[...], vc_ref[...])
    acc = acc_scr[...]
    o_ref[...] = _unstack_heads(acc / pltpu.roll(acc, HEAD_DIM, 1), tq).astype(o_ref.dtype)


def _gattn_kernel(bound_ref, *refs, layer, tk):
    safe = bound_ref[layer] <= MAX_SHIFT_BOUND

    @pl.when(safe)
    def _():
        _gattn_body(bound_ref, *refs, layer, tk, False)

    @pl.when(jnp.logical_not(safe))
    def _():
        _gattn_body(bound_ref, *refs, layer, tk, True)


def _logit_bounds(q_gain, k_gain):
    return HEAD_DIM * Q_SCALE * 1.02 * jnp.max(jnp.abs(q_gain), axis=-1) * jnp.max(jnp.abs(k_gain), axis=-1)


def _global_attention(layer, bound, qkv_lat, qkv_ctx, batch, tq, tk):
    seq = qkv_lat.shape[0] // batch
    lc = qkv_ctx.shape[0] // batch
    nq = seq // tq

    return pl.pallas_call(
        functools.partial(_gattn_kernel, layer=layer, tk=tk),
        out_shape=jax.ShapeDtypeStruct((batch * seq, 256), BF16),
        grid=(batch, nq),
        in_specs=[pl.BlockSpec(memory_space=pltpu.SMEM),
                  pl.BlockSpec((tq, 256), lambda b, i: (b * nq + i, 0)),
                  pl.BlockSpec((seq, LANES), lambda b, i: (b, 2)),
                  pl.BlockSpec((seq, LANES), lambda b, i: (b, 3)),
                  pl.BlockSpec((lc, LANES), lambda b, i: (b, 2)),
                  pl.BlockSpec((lc, LANES), lambda b, i: (b, 3))],
        out_specs=pl.BlockSpec((tq, 256), lambda b, i: (b * nq + i, 0)),
        scratch_shapes=[pltpu.VMEM((4 * tq, LANES), BF16), pltpu.VMEM((4 * tq, 1), F32),
                        pltpu.VMEM((4 * tq, LANES), F32)],
        compiler_params=_cparams(("parallel", "parallel"), 48),
        name="global_attention",
    )(bound, qkv_lat, qkv_lat, qkv_lat, qkv_ctx, qkv_ctx)


def _sink_rows(sink_ref, layer, t):
    head_of_row = _row_iota((4 * t, 1)) // t
    out = jnp.zeros((4 * t, 1), F32)
    for h in range(Q_HEADS):
        out = jnp.where(head_of_row == h, sink_ref[layer, h] * LOG2_E, out)
    return out


def _wattn_kernel(bound_ref, sink_ref, q_ref, kl_ref, vl_ref, kc_ref, vc_ref, o_ref, bias_scr, *, layer, online):
    w = WINDOW
    seq = kl_ref.shape[0]
    band = 3 * w
    shift = 0.0 if online else bound_ref[layer]
    dmat = _lane_iota((w, band)) - _row_iota((w, band))
    for case in range(3):
        bias_scr[case] = jnp.where(jnp.abs(dmat - case * w) <= w, -shift, -jnp.inf)
    one = jnp.ones((), BF16)
    kc, vc = kc_ref[...], vc_ref[...]
    lo_c = _lane_iota(vc.shape) < HEAD_DIM
    vc_sum = (jnp.where(lo_c, vc, one), jnp.where(lo_c, one, vc))
    lo_b = _lane_iota((band, LANES)) < HEAD_DIM
    lo = _lane_iota((w, LANES)) < HEAD_DIM

    def body(n, _):
        r0 = pl.multiple_of(n * w, w)
        gq = pl.program_id(1) * q_ref.shape[0] + r0
        ks = pl.multiple_of(jnp.clip(gq - w, 0, seq - band), w)
        bias = bias_scr[(gq - ks) // w]
        kb, vb = kl_ref[pl.ds(ks, band), :], vl_ref[pl.ds(ks, band), :]
        vb_sum = (jnp.where(lo_b, vb, one), jnp.where(lo_b, one, vb))
        qs = _stack_heads(q_ref[pl.ds(r0, w), :])
        s_band = _dot_nt(qs, kb)
        s_ctx = _dot_nt(qs, kc)
        outs = []
        for g in range(2):
            pb, pc, sk = [], [], []
            for h in (2 * g, 2 * g + 1):
                sb = s_band[h * w:(h + 1) * w] + bias
                sc = s_ctx[h * w:(h + 1) * w] - shift
                sink = sink_ref[layer, h] * LOG2_E - shift
                if online:
                    m = jnp.maximum(jnp.maximum(jnp.max(sb, axis=-1, keepdims=True),
                                                jnp.max(sc, axis=-1, keepdims=True)), sink)
                    sb, sc, sink = sb - m, sc - m, sink - m
                pb.append(jnp.exp2(sb).astype(BF16))
                pc.append(jnp.exp2(sc).astype(BF16))
                sk.append(jnp.exp2(sink) if online else jnp.full((w, 1), jnp.exp2(sink), F32))
            acc = _dot(jnp.concatenate(pb, axis=0), vb_sum[g]) + _dot(jnp.concatenate(pc, axis=0), vc_sum[g])
            den = pltpu.roll(acc, HEAD_DIM, 1) + jnp.concatenate(sk, axis=0)
            outs.append(acc / den)
        o_ref[pl.ds(r0, w), :] = jnp.concatenate(
            [jnp.where(lo, outs[0][0:w], outs[1][0:w]), jnp.where(lo, outs[0][w:2 * w], outs[1][w:2 * w])],
            axis=1).astype(o_ref.dtype)
        return 0

    lax.fori_loop(0, q_ref.shape[0] // w, body, 0, unroll=8)


def _window_attention(layer, bound, sink, qkv_lat, qkv_ctx, batch, tq):
    seq = qkv_lat.shape[0] // batch
    lc = qkv_ctx.shape[0] // batch
    nq = seq // tq

    def call(online):
        return pl.pallas_call(
            functools.partial(_wattn_kernel, layer=layer, online=online),
            out_shape=jax.ShapeDtypeStruct((batch * seq, 256), BF16),
            grid=(batch, nq),
            in_specs=[pl.BlockSpec(memory_space=pltpu.SMEM), pl.BlockSpec(memory_space=pltpu.SMEM),
                      pl.BlockSpec((tq, 256), lambda b, i: (b * nq + i, 0)),
                      pl.BlockSpec((seq, LANES), lambda b, i: (b, 2)),
                      pl.BlockSpec((seq, LANES), lambda b, i: (b, 3)),
                      pl.BlockSpec((lc, LANES), lambda b, i: (b, 2)),
                      pl.BlockSpec((lc, LANES), lambda b, i: (b, 3))],
            out_specs=pl.BlockSpec((tq, 256), lambda b, i: (b * nq + i, 0)),
            scratch_shapes=[pltpu.VMEM((3, WINDOW, 3 * WINDOW), F32)],
            compiler_params=_cparams(("parallel", "parallel"), 32),
            name="window_attention_online" if online else "window_attention",
        )(bound, sink, qkv_lat, qkv_lat, qkv_lat, qkv_ctx, qkv_ctx)

    return lax.cond(bound[layer] <= MAX_SHIFT_BOUND, lambda: call(False), lambda: call(True))


def _cattn_kernel(sink_ref, c_ref, d_ref, oc_ref, od_ref, *, layer):
    t = c_ref.shape[0]
    for ref, out, has_sink in ((c_ref, oc_ref, False), (d_ref, od_ref, True)):
        qs = _stack_heads(ref[:, 0:256])
        s = _dot_nt(qs, ref[:, 256:384])
        m = jnp.max(s, axis=-1, keepdims=True)
        if has_sink:
            sink = _sink_rows(sink_ref, layer, t)
            m = jnp.maximum(m, sink)
        p = jnp.exp2(s - m)
        den = jnp.sum(p, axis=-1, keepdims=True)
        if has_sink:
            den = den + jnp.exp2(sink - m)
        acc = _dot(p.astype(BF16), ref[:, 384:512])
        out[...] = _unstack_heads(acc / den, t).astype(out.dtype)


def _context_attention(layer, sink, qkv_c, qkv_d, batch):
    lc = qkv_c.shape[0] // batch
    blk = lambda width: pl.BlockSpec((lc, width), lambda b: (b, 0))
    return pl.pallas_call(
        functools.partial(_cattn_kernel, layer=layer),
        out_shape=(jax.ShapeDtypeStruct((batch * lc, 256), BF16),) * 2,
        grid=(batch,),
        in_specs=[pl.BlockSpec(memory_space=pltpu.SMEM), blk(512), blk(512)],
        out_specs=(blk(256), blk(256)),
        compiler_params=_cparams(("parallel",), 32),
        name="context_attention",
    )(sink, qkv_c, qkv_d)


def _out_ffn_kernel(x_ref, ya_ref, yb_ref, yc_ref, yd_ref, mod_ref, wo_ref, g_ref, w1_ref, w2_ref, o_ref, *, hchunk):
    m = mod_ref[0]
    gw = GROUP_WIDTH
    mix = _dot(ya_ref[...], wo_ref[0, 0:gw, :])
    for i, y_ref in enumerate((yb_ref, yc_ref, yd_ref), start=1):
        mix = mix + _dot(y_ref[...], wo_ref[0, i * gw:(i + 1) * gw, :])
    x1 = x_ref[...] + m[2:3] * mix
    ms = jnp.mean(x1 * x1, axis=-1, keepdims=True)
    h = ((x1 * lax.rsqrt(ms + EPS) * g_ref[0]) * (1.0 + m[4:5]) + m[3:4]).astype(BF16)
    acc = None
    for c in range(w1_ref.shape[2] // hchunk):
        u = jnp.maximum(_dot(h, w1_ref[0, :, c * hchunk:(c + 1) * hchunk]), 0.0)
        t = _dot((u * u).astype(BF16), w2_ref[0, c * hchunk:(c + 1) * hchunk, :])
        acc = t if acc is None else acc + t
    o_ref[...] = x1 + m[5:6] * acc


def _out_ffn(layer, x2d, ys, mod, mod_row_of_tile, w_out, g, w1, w2, tm):
    rows, d = x2d.shape
    hidden = w1.shape[2]
    row_blk = lambda width: pl.BlockSpec((tm, width), lambda i: (i, 0))
    const = lambda *shape: pl.BlockSpec((1,) + shape, lambda i: (layer,) + (0,) * len(shape), pipeline_mode=pl.Buffered(1))
    return pl.pallas_call(
        functools.partial(_out_ffn_kernel, hchunk=1024),
        out_shape=jax.ShapeDtypeStruct((rows, d), F32),
        grid=(rows // tm,),
        in_specs=[row_blk(d), row_blk(256), row_blk(256), row_blk(256), row_blk(256),
                  pl.BlockSpec((1, N_MOD, d), lambda i: (mod_row_of_tile(i), 0, 0)),
                  const(d, d), pl.BlockSpec((1, 1, d), lambda i: (layer, 0, 0)), const(d, hidden), const(hidden, d)],
        out_specs=row_blk(d),
        compiler_params=_cparams(("parallel",), 56),
        name="out_ffn",
    )(x2d, *ys, mod, w_out, g, w1, w2)


def _permute_q_heads(cols):
    q = cols[..., 0:256].reshape(cols.shape[:-1] + (2, 2, HEAD_DIM))
    q = jnp.swapaxes(q, -3, -2).reshape(cols.shape[:-1] + (256,))
    return jnp.concatenate([q, cols[..., 256:512]], axis=-1)


def _relayout_w_in(w):
    ssd_in = GROUP_WIDTH + SSD_CONV_CH + 2 * SSD_HEADS
    b0, c0, d0 = ssd_in, ssd_in + 512, ssd_in + 1024
    dt = w[..., GROUP_WIDTH + SSD_CONV_CH:ssd_in]
    dt = jnp.pad(dt, [(0, 0)] * (w.ndim - 1) + [(0, LANES - 2 * SSD_HEADS)])
    return jnp.concatenate([w[..., 0:GROUP_WIDTH + SSD_CONV_CH], w[..., b0:c0], _permute_q_heads(w[..., c0:d0]),
                            _permute_q_heads(w[..., d0:d0 + 512]), dt], axis=-1).astype(BF16)


def _relayout_w_out(w):
    depth, _, d = w.shape
    attn = jnp.swapaxes(w[:, 512:1024].reshape(depth, 2, 2, 2, HEAD_DIM, d), 2, 3).reshape(depth, 512, d)
    return jnp.concatenate([w[:, 0:512], attn], axis=1).astype(BF16)


def _block_diag(w):
    eye = jnp.eye(LRU_BLOCKS, dtype=w.dtype)
    out = jnp.einsum('...kij,kn->...kinj', w, eye)
    return out.reshape(w.shape[:-3] + (LRU_WIDTH, LRU_WIDTH))


def _rope_tables(length):
    pos = np.arange(length)
    inv = ROPE_THETA ** (-np.arange(0, ROPE_AXIS_DIM, 2, dtype=np.float64) / ROPE_AXIS_DIM)
    row = (pos // GRID_W)[:, None] * inv
    col = (pos % GRID_W)[:, None] * inv
    ang = np.concatenate([row, row, col, col], axis=1)
    half = ROPE_AXIS_DIM // 2
    sign = np.tile(np.concatenate([-np.ones(half), np.ones(half)]), 2)
    reps = (1, LANES // HEAD_DIM)
    return (jnp.asarray(np.tile(np.cos(ang), reps), F32), jnp.asarray(np.tile(np.sin(ang) * sign, reps), F32))


def _head_selectors():
    sel = np.zeros((2, 2 * LANES, SSD_HEADS * LANES), np.float32)
    for d in range(2):
        for h in range(SSD_HEADS):
            for part in range(2):
                sel[d, part * LANES + SSD_HEADS * d + h, h * LANES:(h + 1) * LANES] = 1.0
    return jnp.asarray(sel, BF16)


def _head_mean_matrix():
    idx = np.arange(256) // HEAD_DIM
    return jnp.asarray((idx[:, None] == idx[None, :]).astype(np.float32) / HEAD_DIM, BF16)


def kernel(x, c, ctx, c_ctx, w_mod, b_mod, g_mix, w_in, ssd_conv_w, ssd_conv_b, ssd_a_log, ssd_dt_bias, ssd_d, ssd_norm_g, lru_conv_w, lru_conv_b, lru_lambda, lru_w_a, lru_b_a, lru_w_i, lru_b_i, gqa_q_norm, gqa_k_norm, swa_q_norm, swa_k_norm, swa_sink, w_out, g_ffn, w_ffn1, w_ffn2):
    batch, seq, d = x.shape
    lc = ctx.shape[1]
    depth = w_mod.shape[0]
    assert batch + 1 <= 8 and seq % 1024 == 0 and lc % 256 == 0 and seq % GRID_W == 0

    cvecs = jnp.concatenate([c, c_ctx[None, :], jnp.zeros((8 - batch - 1, d), F32)], axis=0)
    mod = _modulation(cvecs, w_mod, b_mod).reshape(depth * 8, N_MOD, d)

    rope = _rope_tables(seq)
    sel = _head_selectors()
    bd = _head_mean_matrix()
    conv_sel = _conv_selectors()
    w_in_r = _relayout_w_in(w_in)
    w_out_r = _relayout_w_out(w_out)
    w1, w2 = w_ffn1.astype(BF16), w_ffn2.astype(BF16)
    g1, g2 = g_mix.reshape(depth, 1, d), g_ffn.reshape(depth, 1, d)
    nrm = jnp.tile(jnp.stack([gqa_q_norm, gqa_k_norm, swa_q_norm, swa_k_norm], axis=1), (1, 1, 256 // HEAD_DIM))
    lane_pad = ((0, 0), (0, 0), (0, LANES - 2 * SSD_HEADS))
    head_params = jnp.concatenate([jnp.pad(ssd_dt_bias.reshape(depth, 1, -1), lane_pad),
                                   jnp.pad(ssd_a_log.reshape(depth, 1, -1), lane_pad),
                                   jnp.zeros((depth, 6, LANES), F32)], axis=1)
    ssd_cb = ssd_conv_b.reshape(depth, 1, -1)
    d_skip = jnp.repeat(ssd_d, HEAD_DIM, axis=-1).reshape(depth, 1, -1)
    ssd_ng = ssd_norm_g.reshape(depth, 1, -1)
    bd_a, bd_i = _block_diag(lru_w_a), _block_diag(lru_w_i)
    w_gates = (0.5 * jnp.concatenate([bd_a[:, 0], bd_i[:, 0], bd_a[:, 1], bd_i[:, 1]], axis=-1)).astype(BF16)
    b_gates = 0.5 * jnp.concatenate([lru_b_a[:, 0], lru_b_i[:, 0], lru_b_a[:, 1], lru_b_i[:, 1]], axis=-1).reshape(depth, 1, -1)
    lru_cb = lru_conv_b.reshape(depth, 1, -1)
    lam = lru_lambda.reshape(depth, 1, -1)
    bounds = _logit_bounds(gqa_q_norm, gqa_k_norm)
    swa_bounds = _logit_bounds(swa_q_norm, swa_k_norm)

    xl = x.reshape(batch * seq, d)
    xc = ctx.reshape(batch * lc, d)
    for l in range(depth):
        need_ctx = l < depth - 1
        lat_row = lambda rows: (lambda i, l=l: l * 8 + i // (seq // rows))
        ctx_row = lambda i, l=l: l * 8 + batch
        ssd_l, lru_l, qc_l, qd_l, dt_l = _in_projection(l, xl, mod, lat_row(IN_PROJ_ROWS), g1, w_in_r, nrm, bd, rope,
                                                        IN_PROJ_ROWS, seq // IN_PROJ_ROWS)
        ssd_c, lru_c, qc_c, qd_c, dt_c = _in_projection(l, xc, mod, ctx_row, g1, w_in_r, nrm, bd, None, lc, 1)
        ya_l, ya_c = _ssd_mixer(l, ssd_l, ssd_c, dt_l, dt_c, conv_sel, ssd_conv_w, ssd_cb, head_params, sel, d_skip,
                                ssd_ng, batch)
        yb_l, yb_c = _lru_mixer(l, lru_l, lru_c, conv_sel, lru_conv_w, lru_cb, w_gates, b_gates, lam, batch)
        yc_l = _global_attention(l, bounds, qc_l, qc_c, batch, GATTN_Q_ROWS, GATTN_KV_ROWS)
        yd_l = _window_attention(l, swa_bounds, swa_sink, qd_l, qd_c, batch, seq)
        xl = _out_ffn(l, xl, (ya_l, yb_l, yc_l, yd_l), mod, lat_row(FFN_ROWS), w_out_r, g2, w1, w2, FFN_ROWS)
        if need_ctx:
            yc_c, yd_c = _context_attention(l, swa_sink, qc_c, qd_c, batch)
            xc = _out_ffn(l, xc, (ya_c, yb_c, yc_c, yd_c), mod, ctx_row, w_out_r, g2, w1, w2, FFN_ROWS)
    return xl.reshape(batch, seq, d)
```

```python
import functools
import math

import numpy as np
import jax
import jax.numpy as jnp
from jax import lax
from jax.experimental import pallas as pl
from jax.experimental.pallas import tpu as pltpu

F32 = jnp.float32
BF16 = jnp.bfloat16

EPS = 1e-6
GRID_W = 64
N_MOD = 6
GROUP_WIDTH = 256
SSD_HEADS = 4
SSD_STATE = 128
SSD_CONV = 4
SSD_CHUNK = 128
SSD_CONV_CH = 768
LRU_WIDTH = 256
LRU_BLOCKS = 4
LRU_C = 8.0
HEAD_DIM = 64
Q_HEADS = 4
WINDOW = 128
ROPE_THETA = 10000.0
ROPE_AXIS_DIM = 32
LOG2_E = math.log2(math.e)
Q_SCALE = HEAD_DIM ** -0.5 * LOG2_E

LANES = 128
BF16_ROWS = 16

COL_SSD, COL_LRU, COL_C, COL_D, COL_DT, COL_END = 0, 1024, 1536, 2048, 2560, 2688

CONV_WIN = SSD_CHUNK + 2 * BF16_ROWS

IN_PROJ_ROWS = 1024
FFN_ROWS = 1024
FFN_CTX_ROWS = 512
GATTN_Q_ROWS = 512
GATTN_KV_ROWS = 512
SSD_GROUP = 8


def _cparams(sem, vmem_mib):
    return pltpu.CompilerParams(dimension_semantics=sem, vmem_limit_bytes=vmem_mib * 1024 * 1024)


def _dot(a, b):
    return jnp.dot(a, b, preferred_element_type=F32)


def _dot_nt(a, b):
    return lax.dot_general(a, b, (((1,), (1,)), ((), ())), preferred_element_type=F32)


def _split_bf16(a, parts):
    out = []
    for _ in range(parts - 1):
        hi = a.astype(BF16)
        out.append(hi)
        a = a - hi.astype(F32)
    out.append(a.astype(BF16))
    return out


def _silu(x):
    h = 0.5 * x
    return h + h * jnp.tanh(h)


def _softplus(x):
    return jnp.maximum(x, 0.0) + jnp.log1p(jnp.exp(-jnp.abs(x)))


def _gelu_tanh(x):
    return 0.5 * x * (1.0 + jnp.tanh(math.sqrt(2.0 / math.pi) * (x + 0.044715 * (x * x * x))))


def _lane_iota(shape):
    return lax.broadcasted_iota(jnp.int32, shape, len(shape) - 1)


def _row_iota(shape):
    return lax.broadcasted_iota(jnp.int32, shape, len(shape) - 2)


def _mod_kernel(cv_ref, w_ref, b_ref, o_ref):
    cv = cv_ref[...]
    s = _silu(cv)
    w = w_ref[0]
    s_hi, s_lo = _split_bf16(s, 2)
    w_hi, w_lo = _split_bf16(w, 2)
    acc = _dot(s_hi, w_hi) + _dot(s_lo, w_hi) + _dot(s_hi, w_lo)
    o_ref[0] = acc + b_ref[0]


def _modulation(cvecs, w_mod, b_mod):
    depth, d, n = w_mod.shape
    tn = 1024
    return pl.pallas_call(
        _mod_kernel,
        out_shape=jax.ShapeDtypeStruct((depth, 8, n), F32),
        grid=(depth, n // tn),
        in_specs=[pl.BlockSpec((8, d), lambda l, j: (0, 0)),
                  pl.BlockSpec((1, d, tn), lambda l, j: (l, 0, j)),
                  pl.BlockSpec((1, 1, tn), lambda l, j: (l, 0, j))],
        out_specs=pl.BlockSpec((1, 8, tn), lambda l, j: (l, 0, j)),
        compiler_params=_cparams(("parallel", "parallel"), 32),
        name="modulation",
    )(cvecs, w_mod, b_mod.reshape(depth, 1, n))


def _head_norm_rope(p, g, bd, cos, sin, scale):
    ms = _dot((p * p).astype(BF16), bd)
    y = p * lax.rsqrt(ms + EPS) * g
    if cos is not None:
        w = y.shape[-1]
        first = (_lane_iota(y.shape) % ROPE_AXIS_DIM) < (ROPE_AXIS_DIM // 2)
        partner = jnp.where(first, pltpu.roll(y, w - ROPE_AXIS_DIM // 2, 1), pltpu.roll(y, ROPE_AXIS_DIM // 2, 1))
        y = y * cos + partner * sin
    if scale != 1.0:
        y = y * scale
    return y


def _inproj_kernel(*refs, use_rope):
    if use_rope:
        x_ref, mod_ref, g_ref, w_ref, nrm_ref, bd_ref, cos_ref, sin_ref, ssd_ref, lru_ref, c_ref, d_ref, dt_ref = refs
    else:
        x_ref, mod_ref, g_ref, w_ref, nrm_ref, bd_ref, ssd_ref, lru_ref, c_ref, d_ref, dt_ref = refs
    x = x_ref[...]
    ms = jnp.mean(x * x, axis=-1, keepdims=True)
    m = mod_ref[0]
    h = (x * lax.rsqrt(ms + EPS) * g_ref[0]) * (1.0 + m[1:2]) + m[0:1]
    hb = h.astype(BF16)
    bd = bd_ref[...]
    if use_rope:
        cos1, sin1 = cos_ref[...], sin_ref[...]
        cos2 = jnp.concatenate([cos1, cos1], axis=1)
        sin2 = jnp.concatenate([sin1, sin1], axis=1)
    else:
        cos1 = sin1 = cos2 = sin2 = None
    for o_ref, c0, row in ((c_ref, COL_C, 0), (d_ref, COL_D, 2)):
        p = _dot(hb, w_ref[0, :, c0:c0 + 512])
        q = _head_norm_rope(p[:, 0:256], nrm_ref[0, row:row + 1, :], bd, cos2, sin2, Q_SCALE)
        k = _head_norm_rope(p[:, 256:384], nrm_ref[0, row + 1:row + 2, 0:128], bd[0:128, 0:128], cos1, sin1, 1.0)
        o_ref[:, 0:256] = q.astype(BF16)
        o_ref[:, 256:384] = k.astype(BF16)
        o_ref[:, 384:512] = p[:, 384:512].astype(BF16)
    ssd_ref[...] = _dot(hb, w_ref[0, :, COL_SSD:COL_LRU]).astype(BF16)
    lru_ref[...] = _dot(hb, w_ref[0, :, COL_LRU:COL_C]).astype(BF16)
    dt_ref[...] = _dot(hb, w_ref[0, :, COL_DT:COL_END])


def _in_projection(layer, x2d, mod, mod_row_of_tile, g, w, nrm, bd, rope, tm, rope_tiles):
    rows, d = x2d.shape
    use_rope = rope is not None
    in_specs = [pl.BlockSpec((tm, d), lambda i: (i, 0)),
                pl.BlockSpec((1, N_MOD, d), lambda i: (mod_row_of_tile(i), 0, 0)),
                pl.BlockSpec((1, 1, d), lambda i: (layer, 0, 0)),
                pl.BlockSpec((1, d, COL_END), lambda i: (layer, 0, 0), pipeline_mode=pl.Buffered(1)),
                pl.BlockSpec((1, 4, 256), lambda i: (layer, 0, 0)),
                pl.BlockSpec((256, 256), lambda i: (0, 0))]
    args = [x2d, mod, g, w, nrm, bd]
    if use_rope:
        in_specs += [pl.BlockSpec((tm, LANES), lambda i: (i % rope_tiles, 0))] * 2
        args += list(rope)
    widths = (1024, 512, 512, 512, LANES)
    dtypes = (BF16, BF16, BF16, BF16, F32)
    return pl.pallas_call(
        functools.partial(_inproj_kernel, use_rope=use_rope),
        out_shape=tuple(jax.ShapeDtypeStruct((rows, wd), dt) for wd, dt in zip(widths, dtypes)),
        grid=(rows // tm,),
        in_specs=in_specs,
        out_specs=tuple(pl.BlockSpec((tm, wd), lambda i: (i, 0)) for wd in widths),
        compiler_params=_cparams(("parallel",), 56),
        name="in_projection",
    )(*args)


def _conv_selectors():
    row = np.arange(SSD_CONV * SSD_CHUNK)
    want = (row % SSD_CHUNK) + (row // SSD_CHUNK) - SSD_CONV // 2
    col = np.arange(CONV_WIN)
    sel = np.stack([(col[None, :] == want[:, None] + BF16_ROWS * case) for case in range(3)])
    return jnp.asarray(sel.astype(np.float32), BF16)


def _conv_chunk(seq_ref, col0, width, length, r0, sel_ref, w_ref, b_ref):
    ws = jnp.clip(r0 - BF16_ROWS, 0, length - CONV_WIN)
    ws = pl.multiple_of(ws, BF16_ROWS)
    xw = seq_ref[pl.ds(ws, CONV_WIN), col0:col0 + width]
    sh = _dot(sel_ref[(r0 - ws) // BF16_ROWS], xw)
    w = w_ref[0]
    acc = b_ref[0] + sh[0:SSD_CHUNK] * w[0:1]
    for k in range(1, SSD_CONV):
        acc = acc + sh[k * SSD_CHUNK:(k + 1) * SSD_CHUNK] * w[k:k + 1]
    return acc


def _ssd_kernel(pl_ref, pc_ref, dtl_ref, dtc_ref, csel_ref, cw_ref, cb_ref, hp_ref, sel_ref, dsk_ref, ng_ref,
                yl_ref, yc_ref, accl_ref, accc_ref, xbcl_ref, xbcc_ref):
    q = SSD_CHUNK
    lane8 = _lane_iota((1, LANES)) < 2 * SSD_HEADS
    dt_bias = hp_ref[0, 0:1, :]
    a_row = jnp.where(lane8, -jnp.exp(hp_ref[0, 1:2, :]) * LOG2_E, 0.0)
    ii = _row_iota((q, q))
    jj = _lane_iota((q, q))
    lo = _lane_iota((q, LANES)) < HEAD_DIM
    masks = (jj <= ii, jj >= ii)
    tris = [jnp.where(m, 1.0, 0.0).astype(BF16) for m in masks]
    tri_pairs = [jnp.concatenate([t, t], axis=1) for t in tris]
    zblk = jnp.zeros((SSD_STATE, LANES), BF16)

    def block_diag(a0, a1):
        return jnp.concatenate([jnp.concatenate([a0, zblk], axis=1), jnp.concatenate([zblk, a1], axis=1)], axis=0)

    def chunk(xbc_ref, r0, d, state, acum, dtb, acb):
        xs = xbc_ref[pl.ds(r0, q), 0:256].astype(F32)
        bm = xbc_ref[pl.ds(r0, q), 256:512].astype(F32)
        cmb = xbc_ref[pl.ds(r0, q), 512:768]
        mask = masks[d]
        act = acum.T
        edge = q - 1 if d == 0 else 0
        per_head = []
        for h in range(SSD_HEADS):
            ac = acb[:, h * LANES:(h + 1) * LANES]
            ar = act[SSD_HEADS * d + h:SSD_HEADS * d + h + 1, :]
            lmat = jnp.where(mask, jnp.exp2(ac - ar), 0.0)
            alast = ac[edge:edge + 1, :]
            dth = dtb[:, h * LANES:(h + 1) * LANES]
            per_head.append((lmat, dth, jnp.exp2(alast - ac) * dth, jnp.exp2(ac), jnp.exp2(alast)))
        bts = [bm[:, g * LANES:(g + 1) * LANES].T.astype(BF16) for g in range(2)]
        cb_all = _dot(cmb, block_diag(bts[0], bts[1]))
        yo_all = _dot(cmb, block_diag(state[0].astype(BF16), state[1].astype(BF16)))
        ys, new_state = [], []
        for g in range(2):
            (l0, dt0, we0, ei0, cd0), (l1, dt1, we1, ei1, cd1) = per_head[2 * g], per_head[2 * g + 1]
            xg = xs[:, g * LANES:(g + 1) * LANES]
            cb = cb_all[:, g * LANES:(g + 1) * LANES]
            xdt = xg * jnp.where(lo, dt0, dt1)
            scores = jnp.concatenate([(cb * l0).astype(BF16), (cb * l1).astype(BF16)], axis=1)
            xpair = jnp.concatenate([jnp.where(lo, xdt, 0.0), jnp.where(lo, 0.0, xdt)], axis=0).astype(BF16)
            y = _dot(scores, xpair) + yo_all[:, g * LANES:(g + 1) * LANES] * jnp.where(lo, ei0, ei1)
            s_new = _dot(bts[g], (xg * jnp.where(lo, we0, we1)).astype(BF16))
            new_state.append(state[g] * jnp.where(lo, cd0, cd1) + s_new)
            ys.append(y)
        return jnp.concatenate(ys, axis=1), xs, tuple(new_state)

    def sweep(seq_ref, xbc_ref, dt_ref, acc_ref, out_ref, d, state):
        length = seq_ref.shape[0]
        nc = length // q
        gsz = min(nc, SSD_GROUP)
        ngroups = nc // gsz

        def body(gi, st):
            g0 = pl.multiple_of((gi if d == 0 else ngroups - 1 - gi) * (gsz * q), gsz * q)
            if d == 0:
                for c in range(gsz):
                    conv = _conv_chunk(seq_ref, GROUP_WIDTH, SSD_CONV_CH, length, g0 + c * q, csel_ref, cw_ref, cb_ref)
                    xbc_ref[pl.ds(g0 + c * q, q), :] = _silu(conv).astype(BF16)
            dtv = _softplus(dt_ref[pl.ds(g0, gsz * q), :] + dt_bias)
            hi, lo_part = _split_bf16(dtv * a_row, 2)
            rhs = jnp.concatenate([jnp.concatenate([hi[c * q:(c + 1) * q], lo_part[c * q:(c + 1) * q]], axis=0)
                                   for c in range(gsz)], axis=1)
            acum = _dot(tri_pairs[d], rhs)
            stacked = jnp.concatenate([blk for c in range(gsz)
                                       for blk in (dtv[c * q:(c + 1) * q], acum[:, c * LANES:(c + 1) * LANES])], axis=0)
            ex = _dot(jnp.concatenate(_split_bf16(stacked, 2), axis=1), sel_ref[d])
            for i in range(gsz):
                c = i if d == 0 else gsz - 1 - i
                r0 = g0 + c * q
                y, xs, st = chunk(xbc_ref, r0, d, st, acum[:, c * LANES:(c + 1) * LANES],
                                  ex[2 * c * q:(2 * c + 1) * q], ex[(2 * c + 1) * q:(2 * c + 2) * q])
                if d == 0:
                    acc_ref[pl.ds(r0, q), :] = y
                else:
                    y = acc_ref[pl.ds(r0, q), :] + y + dsk_ref[0] * xs
                    z = seq_ref[pl.ds(r0, q), 0:GROUP_WIDTH].astype(F32)
                    t = y * _silu(z)
                    ms = jnp.mean(t * t, axis=-1, keepdims=True)
                    out_ref[pl.ds(r0, q), :] = (t * lax.rsqrt(ms + EPS) * ng_ref[0]).astype(out_ref.dtype)
            return st

        return lax.fori_loop(0, ngroups, body, state)

    zero = (jnp.zeros((SSD_STATE, LANES), F32), jnp.zeros((SSD_STATE, LANES), F32))
    for d in range(2):
        st = sweep(pc_ref, xbcc_ref, dtc_ref, accc_ref, yc_ref, d, zero)
        sweep(pl_ref, xbcl_ref, dtl_ref, accl_ref, yl_ref, d, st)


def _ssd_mixer(layer, p_lat, p_ctx, dt_lat, dt_ctx, conv_sel, conv_w, conv_b, head_params, sel, d_skip, norm_g, batch):
    seq = p_lat.shape[0] // batch
    lc = p_ctx.shape[0] // batch
    const = lambda *shape: pl.BlockSpec(shape, lambda b: (0,) * len(shape))
    of_layer = lambda *shape: pl.BlockSpec((1,) + shape, lambda b: (layer,) + (0,) * len(shape))
    return pl.pallas_call(
        _ssd_kernel,
        out_shape=(jax.ShapeDtypeStruct((batch * seq, GROUP_WIDTH), BF16),
                   jax.ShapeDtypeStruct((batch * lc, GROUP_WIDTH), BF16)),
        grid=(batch,),
        in_specs=[pl.BlockSpec((seq, 1024), lambda b: (b, 0)),
                  pl.BlockSpec((lc, 1024), lambda b: (b, 0)),
                  pl.BlockSpec((seq, LANES), lambda b: (b, 0)),
                  pl.BlockSpec((lc, LANES), lambda b: (b, 0)),
                  const(3, SSD_CONV * SSD_CHUNK, CONV_WIN),
                  of_layer(SSD_CONV, SSD_CONV_CH), of_layer(1, SSD_CONV_CH), of_layer(8, LANES),
                  const(2, 2 * LANES, SSD_HEADS * LANES), of_layer(1, GROUP_WIDTH), of_layer(1, GROUP_WIDTH)],
        out_specs=(pl.BlockSpec((seq, GROUP_WIDTH), lambda b: (b, 0)),
                   pl.BlockSpec((lc, GROUP_WIDTH), lambda b: (b, 0))),
        scratch_shapes=[pltpu.VMEM((seq, GROUP_WIDTH), F32), pltpu.VMEM((lc, GROUP_WIDTH), F32),
                        pltpu.VMEM((seq, SSD_CONV_CH), BF16), pltpu.VMEM((lc, SSD_CONV_CH), BF16)],
        compiler_params=_cparams(("parallel",), 48),
        name="ssd_mixer",
    )(p_lat, p_ctx, dt_lat, dt_ctx, conv_sel, conv_w, conv_b, head_params, sel, d_skip, norm_g)


LRU_SEGMENTS = 8
LRU_GROUP = 8


def _lru_kernel(pl_ref, pc_ref, csel_ref, cw_ref, cb_ref, wg_ref, bg_ref, lam_ref, ol_ref, oc_ref, a_scr, b_scr, cin_scr):
    q = SSD_CHUNK
    nseg = LRU_SEGMENTS
    nhc = (0.5 * LRU_C) * _softplus(-lam_ref[0])

    def run(seq_ref, out_ref, h0):
        length = seq_ref.shape[0]
        seglen = length // nseg
        sub = min(q, seglen)

        def seg_rows(t0):
            return pl.ds((t0 % seglen) * nseg + t0 // seglen, sub, stride=nseg)

        gsz = min(length // q, LRU_GROUP)

        def gates(gi, _):
            g0 = pl.multiple_of(gi * (gsz * q), gsz * q)
            xcs = [_conv_chunk(seq_ref, LRU_WIDTH, LRU_WIDTH, length, g0 + c * q, csel_ref, cw_ref, cb_ref)
                   for c in range(gsz)]
            xc_all = jnp.concatenate(xcs, axis=0)
            t_all = jnp.tanh(_dot(xc_all.astype(BF16), wg_ref[0]) + bg_ref[0])
            for ci in range(gsz):
                r0 = g0 + ci * q
                t = t_all[ci * q:(ci + 1) * q]
                hx = 0.5 * xcs[ci]
                for d in range(2):
                    c = nhc[:, d * 256:(d + 1) * 256]
                    nla = c * t[:, d * 512:d * 512 + 256] + c
                    a = jnp.exp2(nla * -LOG2_E)
                    y = jnp.tanh(nla) * (a * a + 1.0)
                    root = jnp.where(y > 0.0, y * lax.rsqrt(y), 0.0)
                    bv = root * (hx * t[:, d * 512 + 256:(d + 1) * 512] + hx)
                    for hf in range(2):
                        for j in range(q // sub):
                            rows = seg_rows(r0 + j * sub)
                            a_scr[2 * d + hf, rows, :] = a[j * sub:(j + 1) * sub, hf * LANES:(hf + 1) * LANES]
                            b_scr[2 * d + hf, rows, :] = bv[j * sub:(j + 1) * sub, hf * LANES:(hf + 1) * LANES]
            return 0

        lax.fori_loop(0, length // (gsz * q), gates, 0)

        def scan(k, carry):
            hs, ps = carry
            new_h, new_p = [], []
            for qd in range(4):
                steps = [4 * k + i if qd < 2 else seglen - 1 - (4 * k + i) for i in range(4)]
                idxs = [(qd, pl.ds(pl.multiple_of(st * nseg, nseg), nseg), slice(None)) for st in steps]
                a = [a_scr[ix] for ix in idxs]
                b = [b_scr[ix] for ix in idxs]
                h, p = hs[qd], ps[qd]
                a01, b01 = a[1] * a[0], a[1] * b[0] + b[1]
                a23, b23 = a[3] * a[2], a[3] * b[2] + b[3]
                a03, b03 = a23 * a01, a23 * b01 + b23
                h1 = a[0] * h + b[0]
                h2 = a01 * h + b01
                h3 = a[2] * h2 + b[2]
                h4 = a03 * h + b03
                p2 = p * a01
                for ix, hv, pv in zip(idxs, (h1, h2, h3, h4), (p * a[0], p2, p2 * a[2], p * a03)):
                    b_scr[ix] = hv
                    a_scr[ix] = pv
                new_h.append(h4)
                new_p.append(p * a03)
            return tuple(new_h), tuple(new_p)

        z8 = jnp.zeros((nseg, LANES), F32)
        o8 = jnp.ones((nseg, LANES), F32)
        hend, pend = lax.fori_loop(0, seglen // 4, scan, ((z8,) * 4, (o8,) * 4), unroll=1)

        h_out = []
        for qd in range(4):
            c = h0[qd]
            order = range(nseg) if qd < 2 else range(nseg - 1, -1, -1)
            for k in order:
                cin_scr[qd, k:k + 1, :] = c
                c = hend[qd][k:k + 1, :] + pend[qd][k:k + 1, :] * c
            h_out.append(c)

        cins = [cin_scr[qd] for qd in range(4)]

        def fixup(st, _):
            for qd in range(4):
                idx = (qd, pl.ds(pl.multiple_of(st * nseg, nseg), nseg), slice(None))
                b_scr[idx] = b_scr[idx] + a_scr[idx] * cins[qd]
            return 0

        lax.fori_loop(0, seglen, fixup, 0, unroll=8)

        def emit(c, _):
            r0 = pl.multiple_of(c * sub, sub)
            rows = seg_rows(r0)
            hq = [b_scr[qd, rows, :] for qd in range(4)]
            hsum = jnp.concatenate([hq[0] + hq[2], hq[1] + hq[3]], axis=1)
            gate = seq_ref[pl.ds(r0, sub), 0:LRU_WIDTH].astype(F32)
            out_ref[pl.ds(r0, sub), :] = (hsum * _gelu_tanh(gate)).astype(out_ref.dtype)
            return 0

        lax.fori_loop(0, length // sub, emit, 0)
        return h_out

    z1 = jnp.zeros((1, LANES), F32)
    states = run(pc_ref, oc_ref, [z1] * 4)
    run(pl_ref, ol_ref, states)


def _lru_mixer(layer, p_lat, p_ctx, conv_sel, conv_w, conv_b, w_gates, b_gates, lam, batch):
    seq = p_lat.shape[0] // batch
    lc = p_ctx.shape[0] // batch
    const = lambda *shape: pl.BlockSpec(shape, lambda b: (0,) * len(shape))
    of_layer = lambda *shape: pl.BlockSpec((1,) + shape, lambda b: (layer,) + (0,) * len(shape))
    return pl.pallas_call(
        _lru_kernel,
        out_shape=(jax.ShapeDtypeStruct((batch * seq, LRU_WIDTH), BF16),
                   jax.ShapeDtypeStruct((batch * lc, LRU_WIDTH), BF16)),
        grid=(batch,),
        in_specs=[pl.BlockSpec((seq, 512), lambda b: (b, 0)),
                  pl.BlockSpec((lc, 512), lambda b: (b, 0)),
                  const(3, SSD_CONV * SSD_CHUNK, CONV_WIN),
                  of_layer(SSD_CONV, LRU_WIDTH), of_layer(1, LRU_WIDTH), of_layer(LRU_WIDTH, 1024), of_layer(1, 1024),
                  of_layer(1, 512)],
        out_specs=(pl.BlockSpec((seq, LRU_WIDTH), lambda b: (b, 0)),
                   pl.BlockSpec((lc, LRU_WIDTH), lambda b: (b, 0))),
        scratch_shapes=[pltpu.VMEM((4, seq, LANES), F32), pltpu.VMEM((4, seq, LANES), F32),
                        pltpu.VMEM((4, LRU_SEGMENTS, LANES), F32)],
        compiler_params=_cparams(("parallel",), 48),
        name="lru_mixer",
    )(p_lat, p_ctx, conv_sel, conv_w, conv_b, w_gates, b_gates, lam)


def _stack_heads(qblk):
    lo = _lane_iota((qblk.shape[0], LANES)) < HEAD_DIM
    zero = jnp.zeros((), qblk.dtype)
    s0, s1 = qblk[:, 0:LANES], qblk[:, LANES:2 * LANES]
    return jnp.concatenate([jnp.where(lo, s0, zero), jnp.where(lo, s1, zero),
                            jnp.where(lo, zero, s0), jnp.where(lo, zero, s1)], axis=0)


def _unstack_heads(acc, t):
    lo = _lane_iota((t, LANES)) < HEAD_DIM
    return jnp.concatenate([jnp.where(lo, acc[0:t], acc[2 * t:3 * t]),
                            jnp.where(lo, acc[t:2 * t], acc[3 * t:4 * t])], axis=1)


MAX_SHIFT_BOUND = 50.0


def _gattn_body(bound_ref, q_ref, kl_ref, vl_ref, kc_ref, vc_ref, o_ref, qs_scr, m_scr, acc_scr, layer, tk, online):
    tq = q_ref.shape[0]
    half = 2 * tq
    qs_scr[...] = _stack_heads(q_ref[...])
    acc_scr[...] = jnp.zeros(acc_scr.shape, F32)
    if online:
        m_scr[...] = jnp.full(m_scr.shape, -jnp.inf, F32)

    def step(k, v):
        lo = _lane_iota(v.shape) < HEAD_DIM
        one = jnp.ones((), v.dtype)
        s = _dot_nt(qs_scr[...], k)
        if online:
            m_old = m_scr[...]
            shift = jnp.maximum(m_old, jnp.max(s, axis=-1, keepdims=True))
            m_scr[...] = shift
            acc_scr[...] = acc_scr[...] * jnp.exp2(m_old - shift)
        else:
            shift = bound_ref[layer]
        p = jnp.exp2(s - shift).astype(BF16)
        acc_scr[0:half, :] += _dot(p[0:half], jnp.where(lo, v, one))
        acc_scr[half:, :] += _dot(p[half:], jnp.where(lo, one, v))

    def body(j, _):
        r0 = pl.multiple_of(j * tk, tk)
        step(kl_ref[pl.ds(r0, tk), :], vl_ref[pl.ds(r0, tk), :])
        return 0

    lax.fori_loop(0, kl_ref.shape[0] // tk, body, 0, unroll=not online)
    step(kc_ref[...], vc_ref[...])
    acc = acc_scr[...]
    o_ref[...] = _unstack_heads(acc / pltpu.roll(acc, HEAD_DIM, 1), tq).astype(o_ref.dtype)


def _gattn_kernel(bound_ref, *refs, layer, tk):
    safe = bound_ref[layer] <= MAX_SHIFT_BOUND

    @pl.when(safe)
    def _():
        _gattn_body(bound_ref, *refs, layer, tk, False)

    @pl.when(jnp.logical_not(safe))
    def _():
        _gattn_body(bound_ref, *refs, layer, tk, True)


def _logit_bounds(q_gain, k_gain):
    return HEAD_DIM * Q_SCALE * 1.02 * jnp.max(jnp.abs(q_gain), axis=-1) * jnp.max(jnp.abs(k_gain), axis=-1)


def _global_attention(layer, bound, qkv_lat, qkv_ctx, batch, tq, tk):
    seq = qkv_lat.shape[0] // batch
    lc = qkv_ctx.shape[0] // batch
    nq = seq // tq

    return pl.pallas_call(
        functools.partial(_gattn_kernel, layer=layer, tk=tk),
        out_shape=jax.ShapeDtypeStruct((batch * seq, 256), BF16),
        grid=(batch, nq),
        in_specs=[pl.BlockSpec(memory_space=pltpu.SMEM),
                  pl.BlockSpec((tq, 256), lambda b, i: (b * nq + i, 0)),
                  pl.BlockSpec((seq, LANES), lambda b, i: (b, 2)),
                  pl.BlockSpec((seq, LANES), lambda b, i: (b, 3)),
                  pl.BlockSpec((lc, LANES), lambda b, i: (b, 2)),
                  pl.BlockSpec((lc, LANES), lambda b, i: (b, 3))],
        out_specs=pl.BlockSpec((tq, 256), lambda b, i: (b * nq + i, 0)),
        scratch_shapes=[pltpu.VMEM((4 * tq, LANES), BF16), pltpu.VMEM((4 * tq, 1), F32),
                        pltpu.VMEM((4 * tq, LANES), F32)],
        compiler_params=_cparams(("parallel", "parallel"), 48),
        name="global_attention",
    )(bound, qkv_lat, qkv_lat, qkv_lat, qkv_ctx, qkv_ctx)


def _sink_rows(sink_ref, layer, t):
    head_of_row = _row_iota((4 * t, 1)) // t
    out = jnp.zeros((4 * t, 1), F32)
    for h in range(Q_HEADS):
        out = jnp.where(head_of_row == h, sink_ref[layer, h] * LOG2_E, out)
    return out


def _wattn_kernel(bound_ref, sink_ref, q_ref, kl_ref, vl_ref, kc_ref, vc_ref, o_ref, bias_scr, *, layer, online):
    w = WINDOW
    seq = kl_ref.shape[0]
    band = 3 * w
    shift = 0.0 if online else bound_ref[layer]
    dmat = _lane_iota((w, band)) - _row_iota((w, band))
    for case in range(3):
        bias_scr[case] = jnp.where(jnp.abs(dmat - case * w) <= w, -shift, -jnp.inf)
    one = jnp.ones((), BF16)
    kc, vc = kc_ref[...], vc_ref[...]
    lo_c = _lane_iota(vc.shape) < HEAD_DIM
    vc_sum = (jnp.where(lo_c, vc, one), jnp.where(lo_c, one, vc))
    lo_b = _lane_iota((band, LANES)) < HEAD_DIM
    lo = _lane_iota((w, LANES)) < HEAD_DIM

    def body(n, _):
        r0 = pl.multiple_of(n * w, w)
        gq = pl.program_id(1) * q_ref.shape[0] + r0
        ks = pl.multiple_of(jnp.clip(gq - w, 0, seq - band), w)
        bias = bias_scr[(gq - ks) // w]
        kb, vb = kl_ref[pl.ds(ks, band), :], vl_ref[pl.ds(ks, band), :]
        vb_sum = (jnp.where(lo_b, vb, one), jnp.where(lo_b, one, vb))
        qs = _stack_heads(q_ref[pl.ds(r0, w), :])
        s_band = _dot_nt(qs, kb)
        s_ctx = _dot_nt(qs, kc)
        outs = []
        for g in range(2):
            pb, pc, sk = [], [], []
            for h in (2 * g, 2 * g + 1):
                sb = s_band[h * w:(h + 1) * w] + bias
                sc = s_ctx[h * w:(h + 1) * w] - shift
                sink = sink_ref[layer, h] * LOG2_E - shift
                if online:
                    m = jnp.maximum(jnp.maximum(jnp.max(sb, axis=-1, keepdims=True),
                                                jnp.max(sc, axis=-1, keepdims=True)), sink)
                    sb, sc, sink = sb - m, sc - m, sink - m
                pb.append(jnp.exp2(sb).astype(BF16))
                pc.append(jnp.exp2(sc).astype(BF16))
                sk.append(jnp.exp2(sink) if online else jnp.full((w, 1), jnp.exp2(sink), F32))
            acc = _dot(jnp.concatenate(pb, axis=0), vb_sum[g]) + _dot(jnp.concatenate(pc, axis=0), vc_sum[g])
            den = pltpu.roll(acc, HEAD_DIM, 1) + jnp.concatenate(sk, axis=0)
            outs.append(acc / den)
        o_ref[pl.ds(r0, w), :] = jnp.concatenate(
            [jnp.where(lo, outs[0][0:w], outs[1][0:w]), jnp.where(lo, outs[0][w:2 * w], outs[1][w:2 * w])],
            axis=1).astype(o_ref.dtype)
        return 0

    lax.fori_loop(0, q_ref.shape[0] // w, body, 0, unroll=8)


def _window_attention(layer, bound, sink, qkv_lat, qkv_ctx, batch, tq):
    seq = qkv_lat.shape[0] // batch
    lc = qkv_ctx.shape[0] // batch
    nq = seq // tq

    def call(online):
        return pl.pallas_call(
            functools.partial(_wattn_kernel, layer=layer, online=online),
            out_shape=jax.ShapeDtypeStruct((batch * seq, 256), BF16),
            grid=(batch, nq),
            in_specs=[pl.BlockSpec(memory_space=pltpu.SMEM), pl.BlockSpec(memory_space=pltpu.SMEM),
                      pl.BlockSpec((tq, 256), lambda b, i: (b * nq + i, 0)),
                      pl.BlockSpec((seq, LANES), lambda b, i: (b, 2)),
                      pl.BlockSpec((seq, LANES), lambda b, i: (b, 3)),
                      pl.BlockSpec((lc, LANES), lambda b, i: (b, 2)),
                      pl.BlockSpec((lc, LANES), lambda b, i: (b, 3))],
            out_specs=pl.BlockSpec((tq, 256), lambda b, i: (b * nq + i, 0)),
            scratch_shapes=[pltpu.VMEM((3, WINDOW, 3 * WINDOW), F32)],
            compiler_params=_cparams(("parallel", "parallel"), 32),
            name="window_attention_online" if online else "window_attention",
        )(bound, sink, qkv_lat, qkv_lat, qkv_lat, qkv_ctx, qkv_ctx)

    return lax.cond(bound[layer] <= MAX_SHIFT_BOUND, lambda: call(False), lambda: call(True))


def _cattn_kernel(sink_ref, c_ref, d_ref, oc_ref, od_ref, *, layer):
    t = c_ref.shape[0]
    for ref, out, has_sink in ((c_ref, oc_ref, False), (d_ref, od_ref, True)):
        qs = _stack_heads(ref[:, 0:256])
        s = _dot_nt(qs, ref[:, 256:384])
        m = jnp.max(s, axis=-1, keepdims=True)
        if has_sink:
            sink = _sink_rows(sink_ref, layer, t)
            m = jnp.maximum(m, sink)
        p = jnp.exp2(s - m)
        den = jnp.sum(p, axis=-1, keepdims=True)
        if has_sink:
            den = den + jnp.exp2(sink - m)
        acc = _dot(p.astype(BF16), ref[:, 384:512])
        out[...] = _unstack_heads(acc / den, t).astype(out.dtype)


def _context_attention(layer, sink, qkv_c, qkv_d, batch):
    lc = qkv_c.shape[0] // batch
    blk = lambda width: pl.BlockSpec((lc, width), lambda b: (b, 0))
    return pl.pallas_call(
        functools.partial(_cattn_kernel, layer=layer),
        out_shape=(jax.ShapeDtypeStruct((batch * lc, 256), BF16),) * 2,
        grid=(batch,),
        in_specs=[pl.BlockSpec(memory_space=pltpu.SMEM), blk(512), blk(512)],
        out_specs=(blk(256), blk(256)),
        compiler_params=_cparams(("parallel",), 32),
        name="context_attention",
    )(sink, qkv_c, qkv_d)


def _out_ffn_kernel(x_ref, ya_ref, yb_ref, yc_ref, yd_ref, mod_ref, wo_ref, g_ref, w1_ref, w2_ref, o_ref, *, hchunk):
    m = mod_ref[0]
    gw = GROUP_WIDTH
    mix = _dot(ya_ref[...], wo_ref[0, 0:gw, :])
    for i, y_ref in enumerate((yb_ref, yc_ref, yd_ref), start=1):
        mix = mix + _dot(y_ref[...], wo_ref[0, i * gw:(i + 1) * gw, :])
    x1 = x_ref[...] + m[2:3] * mix
    ms = jnp.mean(x1 * x1, axis=-1, keepdims=True)
    h = ((x1 * lax.rsqrt(ms + EPS) * g_ref[0]) * (1.0 + m[4:5]) + m[3:4]).astype(BF16)
    acc = None
    for c in range(w1_ref.shape[2] // hchunk):
        u = jnp.maximum(_dot(h, w1_ref[0, :, c * hchunk:(c + 1) * hchunk]), 0.0)
        t = _dot((u * u).astype(BF16), w2_ref[0, c * hchunk:(c + 1) * hchunk, :])
        acc = t if acc is None else acc + t
    o_ref[...] = x1 + m[5:6] * acc


def _out_ffn(layer, x2d, ys, mod, mod_row_of_tile, w_out, g, w1, w2, tm):
    rows, d = x2d.shape
    hidden = w1.shape[2]
    row_blk = lambda width: pl.BlockSpec((tm, width), lambda i: (i, 0))
    const = lambda *shape: pl.BlockSpec((1,) + shape, lambda i: (layer,) + (0,) * len(shape), pipeline_mode=pl.Buffered(1))
    return pl.pallas_call(
        functools.partial(_out_ffn_kernel, hchunk=1024),
        out_shape=jax.ShapeDtypeStruct((rows, d), F32),
        grid=(rows // tm,),
        in_specs=[row_blk(d), row_blk(256), row_blk(256), row_blk(256), row_blk(256),
                  pl.BlockSpec((1, N_MOD, d), lambda i: (mod_row_of_tile(i), 0, 0)),
                  const(d, d), pl.BlockSpec((1, 1, d), lambda i: (layer, 0, 0)), const(d, hidden), const(hidden, d)],
        out_specs=row_blk(d),
        compiler_params=_cparams(("parallel",), 56),
        name="out_ffn",
    )(x2d, *ys, mod, w_out, g, w1, w2)


def _permute_q_heads(cols):
    q = cols[..., 0:256].reshape(cols.shape[:-1] + (2, 2, HEAD_DIM))
    q = jnp.swapaxes(q, -3, -2).reshape(cols.shape[:-1] + (256,))
    return jnp.concatenate([q, cols[..., 256:512]], axis=-1)


def _relayout_w_in(w):
    ssd_in = GROUP_WIDTH + SSD_CONV_CH + 2 * SSD_HEADS
    b0, c0, d0 = ssd_in, ssd_in + 512, ssd_in + 1024
    dt = w[..., GROUP_WIDTH + SSD_CONV_CH:ssd_in]
    dt = jnp.pad(dt, [(0, 0)] * (w.ndim - 1) + [(0, LANES - 2 * SSD_HEADS)])
    return jnp.concatenate([w[..., 0:GROUP_WIDTH + SSD_CONV_CH], w[..., b0:c0], _permute_q_heads(w[..., c0:d0]),
                            _permute_q_heads(w[..., d0:d0 + 512]), dt], axis=-1).astype(BF16)


def _relayout_w_out(w):
    depth, _, d = w.shape
    attn = jnp.swapaxes(w[:, 512:1024].reshape(depth, 2, 2, 2, HEAD_DIM, d), 2, 3).reshape(depth, 512, d)
    return jnp.concatenate([w[:, 0:512], attn], axis=1).astype(BF16)


def _block_diag(w):
    eye = jnp.eye(LRU_BLOCKS, dtype=w.dtype)
    out = jnp.einsum('...kij,kn->...kinj', w, eye)
    return out.reshape(w.shape[:-3] + (LRU_WIDTH, LRU_WIDTH))


def _rope_tables(length):
    pos = np.arange(length)
    inv = ROPE_THETA ** (-np.arange(0, ROPE_AXIS_DIM, 2, dtype=np.float64) / ROPE_AXIS_DIM)
    row = (pos // GRID_W)[:, None] * inv
    col = (pos % GRID_W)[:, None] * inv
    ang = np.concatenate([row, row, col, col], axis=1)
    half = ROPE_AXIS_DIM // 2
    sign = np.tile(np.concatenate([-np.ones(half), np.ones(half)]), 2)
    reps = (1, LANES // HEAD_DIM)
    return (jnp.asarray(np.tile(np.cos(ang), reps), F32), jnp.asarray(np.tile(np.sin(ang) * sign, reps), F32))


def _head_selectors():
    sel = np.zeros((2, 2 * LANES, SSD_HEADS * LANES), np.float32)
    for d in range(2):
        for h in range(SSD_HEADS):
            for part in range(2):
                sel[d, part * LANES + SSD_HEADS * d + h, h * LANES:(h + 1) * LANES] = 1.0
    return jnp.asarray(sel, BF16)


def _head_mean_matrix():
    idx = np.arange(256) // HEAD_DIM
    return jnp.asarray((idx[:, None] == idx[None, :]).astype(np.float32) / HEAD_DIM, BF16)


def kernel(x, c, ctx, c_ctx, w_mod, b_mod, g_mix, w_in, ssd_conv_w, ssd_conv_b, ssd_a_log, ssd_dt_bias, ssd_d, ssd_norm_g, lru_conv_w, lru_conv_b, lru_lambda, lru_w_a, lru_b_a, lru_w_i, lru_b_i, gqa_q_norm, gqa_k_norm, swa_q_norm, swa_k_norm, swa_sink, w_out, g_ffn, w_ffn1, w_ffn2):
    batch, seq, d = x.shape
    lc = ctx.shape[1]
    depth = w_mod.shape[0]
    assert batch + 1 <= 8 and seq % 1024 == 0 and lc % 256 == 0 and seq % GRID_W == 0

    cvecs = jnp.concatenate([c, c_ctx[None, :], jnp.zeros((8 - batch - 1, d), F32)], axis=0)
    mod = _modulation(cvecs, w_mod, b_mod).reshape(depth * 8, N_MOD, d)

    rope = _rope_tables(seq)
    sel = _head_selectors()
    bd = _head_mean_matrix()
    conv_sel = _conv_selectors()
    w_in_r = _relayout_w_in(w_in)
    w_out_r = _relayout_w_out(w_out)
    w1, w2 = w_ffn1.astype(BF16), w_ffn2.astype(BF16)
    g1, g2 = g_mix.reshape(depth, 1, d), g_ffn.reshape(depth, 1, d)
    nrm = jnp.tile(jnp.stack([gqa_q_norm, gqa_k_norm, swa_q_norm, swa_k_norm], axis=1), (1, 1, 256 // HEAD_DIM))
    lane_pad = ((0, 0), (0, 0), (0, LANES - 2 * SSD_HEADS))
    head_params = jnp.concatenate([jnp.pad(ssd_dt_bias.reshape(depth, 1, -1), lane_pad),
                                   jnp.pad(ssd_a_log.reshape(depth, 1, -1), lane_pad),
                                   jnp.zeros((depth, 6, LANES), F32)], axis=1)
    ssd_cb = ssd_conv_b.reshape(depth, 1, -1)
    d_skip = jnp.repeat(ssd_d, HEAD_DIM, axis=-1).reshape(depth, 1, -1)
    ssd_ng = ssd_norm_g.reshape(depth, 1, -1)
    bd_a, bd_i = _block_diag(lru_w_a), _block_diag(lru_w_i)
    w_gates = (0.5 * jnp.concatenate([bd_a[:, 0], bd_i[:, 0], bd_a[:, 1], bd_i[:, 1]], axis=-1)).astype(BF16)
    b_gates = 0.5 * jnp.concatenate([lru_b_a[:, 0], lru_b_i[:, 0], lru_b_a[:, 1], lru_b_i[:, 1]], axis=-1).reshape(depth, 1, -1)
    lru_cb = lru_conv_b.reshape(depth, 1, -1)
    lam = lru_lambda.reshape(depth, 1, -1)
    bounds = _logit_bounds(gqa_q_norm, gqa_k_norm)
    swa_bounds = _logit_bounds(swa_q_norm, swa_k_norm)

    xl = x.reshape(batch * seq, d)
    xc = ctx.reshape(batch * lc, d)
    for l in range(depth):
        need_ctx = l < depth - 1
        lat_row = lambda rows: (lambda i, l=l: l * 8 + i // (seq // rows))
        ctx_row = lambda i, l=l: l * 8 + batch
        ssd_l, lru_l, qc_l, qd_l, dt_l = _in_projection(l, xl, mod, lat_row(IN_PROJ_ROWS), g1, w_in_r, nrm, bd, rope,
                                                        IN_PROJ_ROWS, seq // IN_PROJ_ROWS)
        ssd_c, lru_c, qc_c, qd_c, dt_c = _in_projection(l, xc, mod, ctx_row, g1, w_in_r, nrm, bd, None, lc, 1)
        ya_l, ya_c = _ssd_mixer(l, ssd_l, ssd_c, dt_l, dt_c, conv_sel, ssd_conv_w, ssd_cb, head_params, sel, d_skip,
                                ssd_ng, batch)
        yb_l, yb_c = _lru_mixer(l, lru_l, lru_c, conv_sel, lru_conv_w, lru_cb, w_gates, b_gates, lam, batch)
        yc_l = _global_attention(l, bounds, qc_l, qc_c, batch, GATTN_Q_ROWS, GATTN_KV_ROWS)
        yd_l = _window_attention(l, swa_bounds, swa_sink, qd_l, qd_c, batch, seq)
        xl = _out_ffn(l, xl, (ya_l, yb_l, yc_l, yd_l), mod, lat_row(FFN_ROWS), w_out_r, g2, w1, w2, FFN_ROWS)
        if need_ctx:
            yc_c, yd_c = _context_attention(l, swa_sink, qc_c, qd_c, batch)
            xc = _out_ffn(l, xc, (ya_c, yb_c, yc_c, yd_c), mod, ctx_row, w_out_r, g2, w1, w2, FFN_CTX_ROWS)
    return xl.reshape(batch, seq, d)
```

```python
import functools
import math

import numpy as np
import jax
import jax.numpy as jnp
from jax import lax
from jax.experimental import pallas as pl
from jax.experimental.pallas import tpu as pltpu

F32 = jnp.float32
BF16 = jnp.bfloat16

EPS = 1e-6
GRID_W = 64
N_MOD = 6
GROUP_WIDTH = 256
SSD_HEADS = 4
SSD_STATE = 128
SSD_CONV = 4
SSD_CHUNK = 128
SSD_CONV_CH = 768
LRU_WIDTH = 256
LRU_BLOCKS = 4
LRU_C = 8.0
HEAD_DIM = 64
Q_HEADS = 4
WINDOW = 128
ROPE_THETA = 10000.0
ROPE_AXIS_DIM = 32
LOG2_E = math.log2(math.e)
Q_SCALE = HEAD_DIM ** -0.5 * LOG2_E

LANES = 128
BF16_ROWS = 16

COL_SSD, COL_LRU, COL_C, COL_D, COL_DT, COL_END = 0, 1024, 1536, 2048, 2560, 2688

CONV_WIN = SSD_CHUNK + 2 * BF16_ROWS

IN_PROJ_ROWS = 1024
FFN_ROWS = 1024
FFN_CTX_ROWS = 512
GATTN_Q_ROWS = 512
GATTN_KV_ROWS = 512
SSD_GROUP = 8


def _cparams(sem, vmem_mib):
    return pltpu.CompilerParams(dimension_semantics=sem, vmem_limit_bytes=vmem_mib * 1024 * 1024)


def _dot(a, b):
    return jnp.dot(a, b, preferred_element_type=F32)


def _dot_nt(a, b):
    return lax.dot_general(a, b, (((1,), (1,)), ((), ())), preferred_element_type=F32)


def _split_bf16(a, parts):
    out = []
    for _ in range(parts - 1):
        hi = a.astype(BF16)
        out.append(hi)
        a = a - hi.astype(F32)
    out.append(a.astype(BF16))
    return out


def _silu(x):
    h = 0.5 * x
    return h + h * jnp.tanh(h)


def _softplus(x):
    return jnp.maximum(x, 0.0) + jnp.log1p(jnp.exp(-jnp.abs(x)))


def _gelu_tanh(x):
    return 0.5 * x * (1.0 + jnp.tanh(math.sqrt(2.0 / math.pi) * (x + 0.044715 * (x * x * x))))


def _lane_iota(shape):
    return lax.broadcasted_iota(jnp.int32, shape, len(shape) - 1)


def _row_iota(shape):
    return lax.broadcasted_iota(jnp.int32, shape, len(shape) - 2)


def _mod_kernel(cv_ref, w_ref, b_ref, o_ref):
    cv = cv_ref[...]
    s = _silu(cv)
    w = w_ref[0]
    s_hi, s_lo = _split_bf16(s, 2)
    w_hi, w_lo = _split_bf16(w, 2)
    acc = _dot(s_hi, w_hi) + _dot(s_lo, w_hi) + _dot(s_hi, w_lo)
    o_ref[0] = acc + b_ref[0]


def _modulation(cvecs, w_mod, b_mod):
    depth, d, n = w_mod.shape
    tn = 1024
    return pl.pallas_call(
        _mod_kernel,
        out_shape=jax.ShapeDtypeStruct((depth, 8, n), F32),
        grid=(depth, n // tn),
        in_specs=[pl.BlockSpec((8, d), lambda l, j: (0, 0)),
                  pl.BlockSpec((1, d, tn), lambda l, j: (l, 0, j)),
                  pl.BlockSpec((1, 1, tn), lambda l, j: (l, 0, j))],
        out_specs=pl.BlockSpec((1, 8, tn), lambda l, j: (l, 0, j)),
        compiler_params=_cparams(("parallel", "parallel"), 32),
        name="modulation",
    )(cvecs, w_mod, b_mod.reshape(depth, 1, n))


def _head_norm_rope(p, g, bd, cos, sin, scale):
    ms = _dot((p * p).astype(BF16), bd)
    y = p * lax.rsqrt(ms + EPS) * g
    if cos is not None:
        w = y.shape[-1]
        first = (_lane_iota(y.shape) % ROPE_AXIS_DIM) < (ROPE_AXIS_DIM // 2)
        partner = jnp.where(first, pltpu.roll(y, w - ROPE_AXIS_DIM // 2, 1), pltpu.roll(y, ROPE_AXIS_DIM // 2, 1))
        y = y * cos + partner * sin
    if scale != 1.0:
        y = y * scale
    return y


def _inproj_kernel(*refs, use_rope):
    if use_rope:
        x_ref, mod_ref, g_ref, w_ref, nrm_ref, bd_ref, cos_ref, sin_ref, ssd_ref, lru_ref, c_ref, d_ref, dt_ref = refs
    else:
        x_ref, mod_ref, g_ref, w_ref, nrm_ref, bd_ref, ssd_ref, lru_ref, c_ref, d_ref, dt_ref = refs
    x = x_ref[...]
    ms = jnp.mean(x * x, axis=-1, keepdims=True)
    m = mod_ref[0]
    h = (x * lax.rsqrt(ms + EPS) * g_ref[0]) * (1.0 + m[1:2]) + m[0:1]
    hb = h.astype(BF16)
    bd = bd_ref[...]
    if use_rope:
        cos1, sin1 = cos_ref[...], sin_ref[...]
        cos2 = jnp.concatenate([cos1, cos1], axis=1)
        sin2 = jnp.concatenate([sin1, sin1], axis=1)
    else:
        cos1 = sin1 = cos2 = sin2 = None
    for o_ref, c0, row in ((c_ref, COL_C, 0), (d_ref, COL_D, 2)):
        p = _dot(hb, w_ref[0, :, c0:c0 + 512])
        q = _head_norm_rope(p[:, 0:256], nrm_ref[0, row:row + 1, :], bd, cos2, sin2, Q_SCALE)
        k = _head_norm_rope(p[:, 256:384], nrm_ref[0, row + 1:row + 2, 0:128], bd[0:128, 0:128], cos1, sin1, 1.0)
        o_ref[:, 0:256] = q.astype(BF16)
        o_ref[:, 256:384] = k.astype(BF16)
        o_ref[:, 384:512] = p[:, 384:512].astype(BF16)
    ssd_ref[...] = _dot(hb, w_ref[0, :, COL_SSD:COL_LRU]).astype(BF16)
    lru_ref[...] = _dot(hb, w_ref[0, :, COL_LRU:COL_C]).astype(BF16)
    dt_ref[...] = _dot(hb, w_ref[0, :, COL_DT:COL_END])


def _in_projection(layer, x2d, mod, mod_row_of_tile, g, w, nrm, bd, rope, tm, rope_tiles):
    rows, d = x2d.shape
    use_rope = rope is not None
    in_specs = [pl.BlockSpec((tm, d), lambda i: (i, 0)),
                pl.BlockSpec((1, N_MOD, d), lambda i: (mod_row_of_tile(i), 0, 0)),
                pl.BlockSpec((1, 1, d), lambda i: (layer, 0, 0)),
                pl.BlockSpec((1, d, COL_END), lambda i: (layer, 0, 0), pipeline_mode=pl.Buffered(1)),
                pl.BlockSpec((1, 4, 256), lambda i: (layer, 0, 0)),
                pl.BlockSpec((256, 256), lambda i: (0, 0))]
    args = [x2d, mod, g, w, nrm, bd]
    if use_rope:
        in_specs += [pl.BlockSpec((tm, LANES), lambda i: (i % rope_tiles, 0))] * 2
        args += list(rope)
    widths = (1024, 512, 512, 512, LANES)
    dtypes = (BF16, BF16, BF16, BF16, F32)
    return pl.pallas_call(
        functools.partial(_inproj_kernel, use_rope=use_rope),
        out_shape=tuple(jax.ShapeDtypeStruct((rows, wd), dt) for wd, dt in zip(widths, dtypes)),
        grid=(rows // tm,),
        in_specs=in_specs,
        out_specs=tuple(pl.BlockSpec((tm, wd), lambda i: (i, 0)) for wd in widths),
        compiler_params=_cparams(("parallel",), 56),
        name="in_projection",
    )(*args)


def _conv_selectors():
    row = np.arange(SSD_CONV * SSD_CHUNK)
    want = (row % SSD_CHUNK) + (row // SSD_CHUNK) - SSD_CONV // 2
    col = np.arange(CONV_WIN)
    sel = np.stack([(col[None, :] == want[:, None] + BF16_ROWS * case) for case in range(3)])
    return jnp.asarray(sel.astype(np.float32), BF16)


def _conv_chunk(seq_ref, col0, width, length, r0, sel_ref, w_ref, b_ref):
    ws = jnp.clip(r0 - BF16_ROWS, 0, length - CONV_WIN)
    ws = pl.multiple_of(ws, BF16_ROWS)
    xw = seq_ref[pl.ds(ws, CONV_WIN), col0:col0 + width]
    sh = _dot(sel_ref[(r0 - ws) // BF16_ROWS], xw)
    w = w_ref[0]
    acc = b_ref[0] + sh[0:SSD_CHUNK] * w[0:1]
    for k in range(1, SSD_CONV):
        acc = acc + sh[k * SSD_CHUNK:(k + 1) * SSD_CHUNK] * w[k:k + 1]
    return acc


def _ssd_kernel(pl_ref, pc_ref, dtl_ref, dtc_ref, csel_ref, cw_ref, cb_ref, hp_ref, sel_ref, dsk_ref, ng_ref,
                yl_ref, yc_ref, accl_ref, accc_ref, xbcl_ref, xbcc_ref):
    q = SSD_CHUNK
    lane8 = _lane_iota((1, LANES)) < 2 * SSD_HEADS
    dt_bias = hp_ref[0, 0:1, :]
    a_row = jnp.where(lane8, -jnp.exp(hp_ref[0, 1:2, :]) * LOG2_E, 0.0)
    ii = _row_iota((q, q))
    jj = _lane_iota((q, q))
    lo = _lane_iota((q, LANES)) < HEAD_DIM
    masks = (jj <= ii, jj >= ii)
    tris = [jnp.where(m, 1.0, 0.0).astype(BF16) for m in masks]
    tri_pairs = [jnp.concatenate([t, t], axis=1) for t in tris]
    zblk = jnp.zeros((SSD_STATE, LANES), BF16)

    def block_diag(a0, a1):
        return jnp.concatenate([jnp.concatenate([a0, zblk], axis=1), jnp.concatenate([zblk, a1], axis=1)], axis=0)

    def chunk(xbc_ref, r0, d, state, acum, dtb, acb):
        xs = xbc_ref[pl.ds(r0, q), 0:256].astype(F32)
        bm = xbc_ref[pl.ds(r0, q), 256:512].astype(F32)
        cmb = xbc_ref[pl.ds(r0, q), 512:768]
        mask = masks[d]
        act = acum.T
        edge = q - 1 if d == 0 else 0
        per_head = []
        for h in range(SSD_HEADS):
            ac = acb[:, h * LANES:(h + 1) * LANES]
            ar = act[SSD_HEADS * d + h:SSD_HEADS * d + h + 1, :]
            lmat = jnp.where(mask, jnp.exp2(ac - ar), 0.0)
            alast = ac[edge:edge + 1, :]
            dth = dtb[:, h * LANES:(h + 1) * LANES]
            per_head.append((lmat, dth, jnp.exp2(alast - ac) * dth, jnp.exp2(ac), jnp.exp2(alast)))
        bts = [bm[:, g * LANES:(g + 1) * LANES].T.astype(BF16) for g in range(2)]
        cb_all = _dot(cmb, block_diag(bts[0], bts[1]))
        yo_all = _dot(cmb, block_diag(state[0].astype(BF16), state[1].astype(BF16)))
        ys, new_state = [], []
        for g in range(2):
            (l0, dt0, we0, ei0, cd0), (l1, dt1, we1, ei1, cd1) = per_head[2 * g], per_head[2 * g + 1]
            xg = xs[:, g * LANES:(g + 1) * LANES]
            cb = cb_all[:, g * LANES:(g + 1) * LANES]
            xdt = xg * jnp.where(lo, dt0, dt1)
            scores = jnp.concatenate([(cb * l0).astype(BF16), (cb * l1).astype(BF16)], axis=1)
            xpair = jnp.concatenate([jnp.where(lo, xdt, 0.0), jnp.where(lo, 0.0, xdt)], axis=0).astype(BF16)
            y = _dot(scores, xpair) + yo_all[:, g * LANES:(g + 1) * LANES] * jnp.where(lo, ei0, ei1)
            s_new = _dot(bts[g], (xg * jnp.where(lo, we0, we1)).astype(BF16))
            new_state.append(state[g] * jnp.where(lo, cd0, cd1) + s_new)
            ys.append(y)
        return jnp.concatenate(ys, axis=1), xs, tuple(new_state)

    def sweep(seq_ref, xbc_ref, dt_ref, acc_ref, out_ref, d, state):
        length = seq_ref.shape[0]
        nc = length // q
        gsz = min(nc, SSD_GROUP)
        ngroups = nc // gsz

        def body(gi, st):
            g0 = pl.multiple_of((gi if d == 0 else ngroups - 1 - gi) * (gsz * q), gsz * q)
            if d == 0:
                for c in range(gsz):
                    conv = _conv_chunk(seq_ref, GROUP_WIDTH, SSD_CONV_CH, length, g0 + c * q, csel_ref, cw_ref, cb_ref)
                    xbc_ref[pl.ds(g0 + c * q, q), :] = _silu(conv).astype(BF16)
            dtv = _softplus(dt_ref[pl.ds(g0, gsz * q), :] + dt_bias)
            hi, lo_part = _split_bf16(dtv * a_row, 2)
            rhs = jnp.concatenate([jnp.concatenate([hi[c * q:(c + 1) * q], lo_part[c * q:(c + 1) * q]], axis=0)
                                   for c in range(gsz)], axis=1)
            acum = _dot(tri_pairs[d], rhs)
            stacked = jnp.concatenate([blk for c in range(gsz)
                                       for blk in (dtv[c * q:(c + 1) * q], acum[:, c * LANES:(c + 1) * LANES])], axis=0)
            ex = _dot(jnp.concatenate(_split_bf16(stacked, 2), axis=1), sel_ref[d])
            for i in range(gsz):
                c = i if d == 0 else gsz - 1 - i
                r0 = g0 + c * q
                y, xs, st = chunk(xbc_ref, r0, d, st, acum[:, c * LANES:(c + 1) * LANES],
                                  ex[2 * c * q:(2 * c + 1) * q], ex[(2 * c + 1) * q:(2 * c + 2) * q])
                if d == 0:
                    acc_ref[pl.ds(r0, q), :] = y
                else:
                    y = acc_ref[pl.ds(r0, q), :] + y + dsk_ref[0] * xs
                    z = seq_ref[pl.ds(r0, q), 0:GROUP_WIDTH].astype(F32)
                    t = y * _silu(z)
                    ms = jnp.mean(t * t, axis=-1, keepdims=True)
                    out_ref[pl.ds(r0, q), :] = (t * lax.rsqrt(ms + EPS) * ng_ref[0]).astype(out_ref.dtype)
            return st

        return lax.fori_loop(0, ngroups, body, state)

    zero = (jnp.zeros((SSD_STATE, LANES), F32), jnp.zeros((SSD_STATE, LANES), F32))
    for d in range(2):
        st = sweep(pc_ref, xbcc_ref, dtc_ref, accc_ref, yc_ref, d, zero)
        sweep(pl_ref, xbcl_ref, dtl_ref, accl_ref, yl_ref, d, st)


def _ssd_mixer(layer, p_lat, p_ctx, dt_lat, dt_ctx, conv_sel, conv_w, conv_b, head_params, sel, d_skip, norm_g, batch):
    seq = p_lat.shape[0] // batch
    lc = p_ctx.shape[0] // batch
    const = lambda *shape: pl.BlockSpec(shape, lambda b: (0,) * len(shape))
    of_layer = lambda *shape: pl.BlockSpec((1,) + shape, lambda b: (layer,) + (0,) * len(shape))
    return pl.pallas_call(
        _ssd_kernel,
        out_shape=(jax.ShapeDtypeStruct((batch * seq, GROUP_WIDTH), BF16),
                   jax.ShapeDtypeStruct((batch * lc, GROUP_WIDTH), BF16)),
        grid=(batch,),
        in_specs=[pl.BlockSpec((seq, 1024), lambda b: (b, 0)),
                  pl.BlockSpec((lc, 1024), lambda b: (b, 0)),
                  pl.BlockSpec((seq, LANES), lambda b: (b, 0)),
                  pl.BlockSpec((lc, LANES), lambda b: (b, 0)),
                  const(3, SSD_CONV * SSD_CHUNK, CONV_WIN),
                  of_layer(SSD_CONV, SSD_CONV_CH), of_layer(1, SSD_CONV_CH), of_layer(8, LANES),
                  const(2, 2 * LANES, SSD_HEADS * LANES), of_layer(1, GROUP_WIDTH), of_layer(1, GROUP_WIDTH)],
        out_specs=(pl.BlockSpec((seq, GROUP_WIDTH), lambda b: (b, 0)),
                   pl.BlockSpec((lc, GROUP_WIDTH), lambda b: (b, 0))),
        scratch_shapes=[pltpu.VMEM((seq, GROUP_WIDTH), F32), pltpu.VMEM((lc, GROUP_WIDTH), F32),
                        pltpu.VMEM((seq, SSD_CONV_CH), BF16), pltpu.VMEM((lc, SSD_CONV_CH), BF16)],
        compiler_params=_cparams(("parallel",), 48),
        name="ssd_mixer",
    )(p_lat, p_ctx, dt_lat, dt_ctx, conv_sel, conv_w, conv_b, head_params, sel, d_skip, norm_g)


LRU_SEGMENTS = 8
LRU_GROUP = 16


def _lru_kernel(pl_ref, pc_ref, csel_ref, cw_ref, cb_ref, wg_ref, bg_ref, lam_ref, ol_ref, oc_ref, a_scr, b_scr, cin_scr):
    q = SSD_CHUNK
    nseg = LRU_SEGMENTS
    nhc = (0.5 * LRU_C) * _softplus(-lam_ref[0])

    def run(seq_ref, out_ref, h0):
        length = seq_ref.shape[0]
        seglen = length // nseg
        sub = min(q, seglen)

        def seg_rows(t0):
            return pl.ds((t0 % seglen) * nseg + t0 // seglen, sub, stride=nseg)

        gsz = min(length // q, LRU_GROUP)

        def gates(gi, _):
            g0 = pl.multiple_of(gi * (gsz * q), gsz * q)
            xcs = [_conv_chunk(seq_ref, LRU_WIDTH, LRU_WIDTH, length, g0 + c * q, csel_ref, cw_ref, cb_ref)
                   for c in range(gsz)]
            xc_all = jnp.concatenate(xcs, axis=0)
            t_all = jnp.tanh(_dot(xc_all.astype(BF16), wg_ref[0]) + bg_ref[0])
            for ci in range(gsz):
                r0 = g0 + ci * q
                t = t_all[ci * q:(ci + 1) * q]
                hx = 0.5 * xcs[ci]
                for d in range(2):
                    c = nhc[:, d * 256:(d + 1) * 256]
                    nla = c * t[:, d * 512:d * 512 + 256] + c
                    a = jnp.exp2(nla * -LOG2_E)
                    y = jnp.tanh(nla) * (a * a + 1.0)
                    root = jnp.where(y > 0.0, y * lax.rsqrt(y), 0.0)
                    bv = root * (hx * t[:, d * 512 + 256:(d + 1) * 512] + hx)
                    for hf in range(2):
                        for j in range(q // sub):
                            rows = seg_rows(r0 + j * sub)
                            a_scr[2 * d + hf, rows, :] = a[j * sub:(j + 1) * sub, hf * LANES:(hf + 1) * LANES]
                            b_scr[2 * d + hf, rows, :] = bv[j * sub:(j + 1) * sub, hf * LANES:(hf + 1) * LANES]
            return 0

        lax.fori_loop(0, length // (gsz * q), gates, 0)

        def scan(k, carry):
            hs, ps = carry
            new_h, new_p = [], []
            for qd in range(4):
                steps = [4 * k + i if qd < 2 else seglen - 1 - (4 * k + i) for i in range(4)]
                idxs = [(qd, pl.ds(pl.multiple_of(st * nseg, nseg), nseg), slice(None)) for st in steps]
                a = [a_scr[ix] for ix in idxs]
                b = [b_scr[ix] for ix in idxs]
                h, p = hs[qd], ps[qd]
                a01, b01 = a[1] * a[0], a[1] * b[0] + b[1]
                a23, b23 = a[3] * a[2], a[3] * b[2] + b[3]
                a03, b03 = a23 * a01, a23 * b01 + b23
                h1 = a[0] * h + b[0]
                h2 = a01 * h + b01
                h3 = a[2] * h2 + b[2]
                h4 = a03 * h + b03
                p2 = p * a01
                for ix, hv, pv in zip(idxs, (h1, h2, h3, h4), (p * a[0], p2, p2 * a[2], p * a03)):
                    b_scr[ix] = hv
                    a_scr[ix] = pv
                new_h.append(h4)
                new_p.append(p * a03)
            return tuple(new_h), tuple(new_p)

        z8 = jnp.zeros((nseg, LANES), F32)
        o8 = jnp.ones((nseg, LANES), F32)
        hend, pend = lax.fori_loop(0, seglen // 4, scan, ((z8,) * 4, (o8,) * 4), unroll=1)

        h_out = []
        for qd in range(4):
            c = h0[qd]
            order = range(nseg) if qd < 2 else range(nseg - 1, -1, -1)
            for k in order:
                cin_scr[qd, k:k + 1, :] = c
                c = hend[qd][k:k + 1, :] + pend[qd][k:k + 1, :] * c
            h_out.append(c)

        cins = [cin_scr[qd] for qd in range(4)]

        def fixup(st, _):
            for qd in range(4):
                idx = (qd, pl.ds(pl.multiple_of(st * nseg, nseg), nseg), slice(None))
                b_scr[idx] = b_scr[idx] + a_scr[idx] * cins[qd]
            return 0

        lax.fori_loop(0, seglen, fixup, 0, unroll=8)

        def emit(c, _):
            r0 = pl.multiple_of(c * sub, sub)
            rows = seg_rows(r0)
            hq = [b_scr[qd, rows, :] for qd in range(4)]
            hsum = jnp.concatenate([hq[0] + hq[2], hq[1] + hq[3]], axis=1)
            gate = seq_ref[pl.ds(r0, sub), 0:LRU_WIDTH].astype(F32)
            out_ref[pl.ds(r0, sub), :] = (hsum * _gelu_tanh(gate)).astype(out_ref.dtype)
            return 0

        lax.fori_loop(0, length // sub, emit, 0)
        return h_out

    z1 = jnp.zeros((1, LANES), F32)
    states = run(pc_ref, oc_ref, [z1] * 4)
    run(pl_ref, ol_ref, states)


def _lru_mixer(layer, p_lat, p_ctx, conv_sel, conv_w, conv_b, w_gates, b_gates, lam, batch):
    seq = p_lat.shape[0] // batch
    lc = p_ctx.shape[0] // batch
    const = lambda *shape: pl.BlockSpec(shape, lambda b: (0,) * len(shape))
    of_layer = lambda *shape: pl.BlockSpec((1,) + shape, lambda b: (layer,) + (0,) * len(shape))
    return pl.pallas_call(
        _lru_kernel,
        out_shape=(jax.ShapeDtypeStruct((batch * seq, LRU_WIDTH), BF16),
                   jax.ShapeDtypeStruct((batch * lc, LRU_WIDTH), BF16)),
        grid=(batch,),
        in_specs=[pl.BlockSpec((seq, 512), lambda b: (b, 0)),
                  pl.BlockSpec((lc, 512), lambda b: (b, 0)),
                  const(3, SSD_CONV * SSD_CHUNK, CONV_WIN),
                  of_layer(SSD_CONV, LRU_WIDTH), of_layer(1, LRU_WIDTH), of_layer(LRU_WIDTH, 1024), of_layer(1, 1024),
                  of_layer(1, 512)],
        out_specs=(pl.BlockSpec((seq, LRU_WIDTH), lambda b: (b, 0)),
                   pl.BlockSpec((lc, LRU_WIDTH), lambda b: (b, 0))),
        scratch_shapes=[pltpu.VMEM((4, seq, LANES), F32), pltpu.VMEM((4, seq, LANES), F32),
                        pltpu.VMEM((4, LRU_SEGMENTS, LANES), F32)],
        compiler_params=_cparams(("parallel",), 48),
        name="lru_mixer",
    )(p_lat, p_ctx, conv_sel, conv_w, conv_b, w_gates, b_gates, lam)


def _stack_heads(qblk):
    lo = _lane_iota((qblk.shape[0], LANES)) < HEAD_DIM
    zero = jnp.zeros((), qblk.dtype)
    s0, s1 = qblk[:, 0:LANES], qblk[:, LANES:2 * LANES]
    return jnp.concatenate([jnp.where(lo, s0, zero), jnp.where(lo, s1, zero),
                            jnp.where(lo, zero, s0), jnp.where(lo, zero, s1)], axis=0)


def _unstack_heads(acc, t):
    lo = _lane_iota((t, LANES)) < HEAD_DIM
    return jnp.concatenate([jnp.where(lo, acc[0:t], acc[2 * t:3 * t]),
                            jnp.where(lo, acc[t:2 * t], acc[3 * t:4 * t])], axis=1)


MAX_SHIFT_BOUND = 50.0


def _gattn_body(bound_ref, q_ref, kl_ref, vl_ref, kc_ref, vc_ref, o_ref, qs_scr, m_scr, acc_scr, layer, tk, online):
    tq = q_ref.shape[0]
    half = 2 * tq
    qs_scr[...] = _stack_heads(q_ref[...])
    acc_scr[...] = jnp.zeros(acc_scr.shape, F32)
    if online:
        m_scr[...] = jnp.full(m_scr.shape, -jnp.inf, F32)

    def step(k, v):
        lo = _lane_iota(v.shape) < HEAD_DIM
        one = jnp.ones((), v.dtype)
        s = _dot_nt(qs_scr[...], k)
        if online:
            m_old = m_scr[...]
            shift = jnp.maximum(m_old, jnp.max(s, axis=-1, keepdims=True))
            m_scr[...] = shift
            acc_scr[...] = acc_scr[...] * jnp.exp2(m_old - shift)
        else:
            shift = bound_ref[layer]
        p = jnp.exp2(s - shift).astype(BF16)
        acc_scr[0:half, :] += _dot(p[0:half], jnp.where(lo, v, one))
        acc_scr[half:, :] += _dot(p[half:], jnp.where(lo, one, v))

    def body(j, _):
        r0 = pl.multiple_of(j * tk, tk)
        step(kl_ref[pl.ds(r0, tk), :], vl_ref[pl.ds(r0, tk), :])
        return 0

    lax.fori_loop(0, kl_ref.shape[0] // tk, body, 0, unroll=not online)
    step(kc_ref[...], vc_ref[...])
    acc = acc_scr[...]
    o_ref[...] = _unstack_heads(acc / pltpu.roll(acc, HEAD_DIM, 1), tq).astype(o_ref.dtype)


def _gattn_kernel(bound_ref, *refs, layer, tk):
    safe = bound_ref[layer] <= MAX_SHIFT_BOUND

    @pl.when(safe)
    def _():
        _gattn_body(bound_ref, *refs, layer, tk, False)

    @pl.when(jnp.logical_not(safe))
    def _():
        _gattn_body(bound_ref, *refs, layer, tk, True)


def _logit_bounds(q_gain, k_gain):
    return HEAD_DIM * Q_SCALE * 1.02 * jnp.max(jnp.abs(q_gain), axis=-1) * jnp.max(jnp.abs(k_gain), axis=-1)


def _global_attention(layer, bound, qkv_lat, qkv_ctx, batch, tq, tk):
    seq = qkv_lat.shape[0] // batch
    lc = qkv_ctx.shape[0] // batch
    nq = seq // tq

    return pl.pallas_call(
        functools.partial(_gattn_kernel, layer=layer, tk=tk),
        out_shape=jax.ShapeDtypeStruct((batch * seq, 256), BF16),
        grid=(batch, nq),
        in_specs=[pl.BlockSpec(memory_space=pltpu.SMEM),
                  pl.BlockSpec((tq, 256), lambda b, i: (b * nq + i, 0)),
                  pl.BlockSpec((seq, LANES), lambda b, i: (b, 2)),
                  pl.BlockSpec((seq, LANES), lambda b, i: (b, 3)),
                  pl.BlockSpec((lc, LANES), lambda b, i: (b, 2)),
                  pl.BlockSpec((lc, LANES), lambda b, i: (b, 3))],
        out_specs=pl.BlockSpec((tq, 256), lambda b, i: (b * nq + i, 0)),
        scratch_shapes=[pltpu.VMEM((4 * tq, LANES), BF16), pltpu.VMEM((4 * tq, 1), F32),
                        pltpu.VMEM((4 * tq, LANES), F32)],
        compiler_params=_cparams(("parallel", "parallel"), 48),
        name="global_attention",
    )(bound, qkv_lat, qkv_lat, qkv_lat, qkv_ctx, qkv_ctx)


def _sink_rows(sink_ref, layer, t):
    head_of_row = _row_iota((4 * t, 1)) // t
    out = jnp.zeros((4 * t, 1), F32)
    for h in range(Q_HEADS):
        out = jnp.where(head_of_row == h, sink_ref[layer, h] * LOG2_E, out)
    return out


def _wattn_kernel(bound_ref, sink_ref, q_ref, kl_ref, vl_ref, kc_ref, vc_ref, o_ref, bias_scr, *, layer, online):
    w = WINDOW
    seq = kl_ref.shape[0]
    band = 3 * w
    shift = 0.0 if online else bound_ref[layer]
    dmat = _lane_iota((w, band)) - _row_iota((w, band))
    for case in range(3):
        bias_scr[case] = jnp.where(jnp.abs(dmat - case * w) <= w, -shift, -jnp.inf)
    one = jnp.ones((), BF16)
    kc, vc = kc_ref[...], vc_ref[...]
    lo_c = _lane_iota(vc.shape) < HEAD_DIM
    vc_sum = (jnp.where(lo_c, vc, one), jnp.where(lo_c, one, vc))
    lo_b = _lane_iota((band, LANES)) < HEAD_DIM
    lo = _lane_iota((w, LANES)) < HEAD_DIM

    def body(n, _):
        r0 = pl.multiple_of(n * w, w)
        gq = pl.program_id(1) * q_ref.shape[0] + r0
        ks = pl.multiple_of(jnp.clip(gq - w, 0, seq - band), w)
        bias = bias_scr[(gq - ks) // w]
        kb, vb = kl_ref[pl.ds(ks, band), :], vl_ref[pl.ds(ks, band), :]
        vb_sum = (jnp.where(lo_b, vb, one), jnp.where(lo_b, one, vb))
        qs = _stack_heads(q_ref[pl.ds(r0, w), :])
        s_band = _dot_nt(qs, kb)
        s_ctx = _dot_nt(qs, kc)
        outs = []
        for g in range(2):
            pb, pc, sk = [], [], []
            for h in (2 * g, 2 * g + 1):
                sb = s_band[h * w:(h + 1) * w] + bias
                sc = s_ctx[h * w:(h + 1) * w] - shift
                sink = sink_ref[layer, h] * LOG2_E - shift
                if online:
                    m = jnp.maximum(jnp.maximum(jnp.max(sb, axis=-1, keepdims=True),
                                                jnp.max(sc, axis=-1, keepdims=True)), sink)
                    sb, sc, sink = sb - m, sc - m, sink - m
                pb.append(jnp.exp2(sb).astype(BF16))
                pc.append(jnp.exp2(sc).astype(BF16))
                sk.append(jnp.exp2(sink) if online else jnp.full((w, 1), jnp.exp2(sink), F32))
            acc = _dot(jnp.concatenate(pb, axis=0), vb_sum[g]) + _dot(jnp.concatenate(pc, axis=0), vc_sum[g])
            den = pltpu.roll(acc, HEAD_DIM, 1) + jnp.concatenate(sk, axis=0)
            outs.append(acc / den)
        o_ref[pl.ds(r0, w), :] = jnp.concatenate(
            [jnp.where(lo, outs[0][0:w], outs[1][0:w]), jnp.where(lo, outs[0][w:2 * w], outs[1][w:2 * w])],
            axis=1).astype(o_ref.dtype)
        return 0

    lax.fori_loop(0, q_ref.shape[0] // w, body, 0, unroll=8)


def _window_attention(layer, bound, sink, qkv_lat, qkv_ctx, batch, tq):
    seq = qkv_lat.shape[0] // batch
    lc = qkv_ctx.shape[0] // batch
    nq = seq // tq

    def call(online):
        return pl.pallas_call(
            functools.partial(_wattn_kernel, layer=layer, online=online),
            out_shape=jax.ShapeDtypeStruct((batch * seq, 256), BF16),
            grid=(batch, nq),
            in_specs=[pl.BlockSpec(memory_space=pltpu.SMEM), pl.BlockSpec(memory_space=pltpu.SMEM),
                      pl.BlockSpec((tq, 256), lambda b, i: (b * nq + i, 0)),
                      pl.BlockSpec((seq, LANES), lambda b, i: (b, 2)),
                      pl.BlockSpec((seq, LANES), lambda b, i: (b, 3)),
                      pl.BlockSpec((lc, LANES), lambda b, i: (b, 2)),
                      pl.BlockSpec((lc, LANES), lambda b, i: (b, 3))],
            out_specs=pl.BlockSpec((tq, 256), lambda b, i: (b * nq + i, 0)),
            scratch_shapes=[pltpu.VMEM((3, WINDOW, 3 * WINDOW), F32)],
            compiler_params=_cparams(("parallel", "parallel"), 32),
            name="window_attention_online" if online else "window_attention",
        )(bound, sink, qkv_lat, qkv_lat, qkv_lat, qkv_ctx, qkv_ctx)

    return lax.cond(bound[layer] <= MAX_SHIFT_BOUND, lambda: call(False), lambda: call(True))


def _cattn_kernel(sink_ref, c_ref, d_ref, oc_ref, od_ref, *, layer):
    t = c_ref.shape[0]
    for ref, out, has_sink in ((c_ref, oc_ref, False), (d_ref, od_ref, True)):
        qs = _stack_heads(ref[:, 0:256])
        s = _dot_nt(qs, ref[:, 256:384])
        m = jnp.max(s, axis=-1, keepdims=True)
        if has_sink:
            sink = _sink_rows(sink_ref, layer, t)
            m = jnp.maximum(m, sink)
        p = jnp.exp2(s - m)
        den = jnp.sum(p, axis=-1, keepdims=True)
        if has_sink:
            den = den + jnp.exp2(sink - m)
        acc = _dot(p.astype(BF16), ref[:, 384:512])
        out[...] = _unstack_heads(acc / den, t).astype(out.dtype)


def _context_attention(layer, sink, qkv_c, qkv_d, batch):
    lc = qkv_c.shape[0] // batch
    blk = lambda width: pl.BlockSpec((lc, width), lambda b: (b, 0))
    return pl.pallas_call(
        functools.partial(_cattn_kernel, layer=layer),
        out_shape=(jax.ShapeDtypeStruct((batch * lc, 256), BF16),) * 2,
        grid=(batch,),
        in_specs=[pl.BlockSpec(memory_space=pltpu.SMEM), blk(512), blk(512)],
        out_specs=(blk(256), blk(256)),
        compiler_params=_cparams(("parallel",), 32),
        name="context_attention",
    )(sink, qkv_c, qkv_d)


def _out_ffn_kernel(x_ref, ya_ref, yb_ref, yc_ref, yd_ref, mod_ref, wo_ref, g_ref, w1_ref, w2_ref, o_ref, *, hchunk):
    m = mod_ref[0]
    gw = GROUP_WIDTH
    mix = _dot(ya_ref[...], wo_ref[0, 0:gw, :])
    for i, y_ref in enumerate((yb_ref, yc_ref, yd_ref), start=1):
        mix = mix + _dot(y_ref[...], wo_ref[0, i * gw:(i + 1) * gw, :])
    x1 = x_ref[...] + m[2:3] * mix
    ms = jnp.mean(x1 * x1, axis=-1, keepdims=True)
    h = ((x1 * lax.rsqrt(ms + EPS) * g_ref[0]) * (1.0 + m[4:5]) + m[3:4]).astype(BF16)
    acc = None
    for c in range(w1_ref.shape[2] // hchunk):
        u = jnp.maximum(_dot(h, w1_ref[0, :, c * hchunk:(c + 1) * hchunk]), 0.0)
        t = _dot((u * u).astype(BF16), w2_ref[0, c * hchunk:(c + 1) * hchunk, :])
        acc = t if acc is None else acc + t
    o_ref[...] = x1 + m[5:6] * acc


def _out_ffn(layer, x2d, ys, mod, mod_row_of_tile, w_out, g, w1, w2, tm):
    rows, d = x2d.shape
    hidden = w1.shape[2]
    row_blk = lambda width: pl.BlockSpec((tm, width), lambda i: (i, 0))
    const = lambda *shape: pl.BlockSpec((1,) + shape, lambda i: (layer,) + (0,) * len(shape), pipeline_mode=pl.Buffered(1))
    return pl.pallas_call(
        functools.partial(_out_ffn_kernel, hchunk=1024),
        out_shape=jax.ShapeDtypeStruct((rows, d), F32),
        grid=(rows // tm,),
        in_specs=[row_blk(d), row_blk(256), row_blk(256), row_blk(256), row_blk(256),
                  pl.BlockSpec((1, N_MOD, d), lambda i: (mod_row_of_tile(i), 0, 0)),
                  const(d, d), pl.BlockSpec((1, 1, d), lambda i: (layer, 0, 0)), const(d, hidden), const(hidden, d)],
        out_specs=row_blk(d),
        compiler_params=_cparams(("parallel",), 56),
        name="out_ffn",
    )(x2d, *ys, mod, w_out, g, w1, w2)


def _permute_q_heads(cols):
    q = cols[..., 0:256].reshape(cols.shape[:-1] + (2, 2, HEAD_DIM))
    q = jnp.swapaxes(q, -3, -2).reshape(cols.shape[:-1] + (256,))
    return jnp.concatenate([q, cols[..., 256:512]], axis=-1)


def _relayout_w_in(w):
    ssd_in = GROUP_WIDTH + SSD_CONV_CH + 2 * SSD_HEADS
    b0, c0, d0 = ssd_in, ssd_in + 512, ssd_in + 1024
    dt = w[..., GROUP_WIDTH + SSD_CONV_CH:ssd_in]
    dt = jnp.pad(dt, [(0, 0)] * (w.ndim - 1) + [(0, LANES - 2 * SSD_HEADS)])
    return jnp.concatenate([w[..., 0:GROUP_WIDTH + SSD_CONV_CH], w[..., b0:c0], _permute_q_heads(w[..., c0:d0]),
                            _permute_q_heads(w[..., d0:d0 + 512]), dt], axis=-1).astype(BF16)


def _relayout_w_out(w):
    depth, _, d = w.shape
    attn = jnp.swapaxes(w[:, 512:1024].reshape(depth, 2, 2, 2, HEAD_DIM, d), 2, 3).reshape(depth, 512, d)
    return jnp.concatenate([w[:, 0:512], attn], axis=1).astype(BF16)


def _block_diag(w):
    eye = jnp.eye(LRU_BLOCKS, dtype=w.dtype)
    out = jnp.einsum('...kij,kn->...kinj', w, eye)
    return out.reshape(w.shape[:-3] + (LRU_WIDTH, LRU_WIDTH))


def _rope_tables(length):
    pos = np.arange(length)
    inv = ROPE_THETA ** (-np.arange(0, ROPE_AXIS_DIM, 2, dtype=np.float64) / ROPE_AXIS_DIM)
    row = (pos // GRID_W)[:, None] * inv
    col = (pos % GRID_W)[:, None] * inv
    ang = np.concatenate([row, row, col, col], axis=1)
    half = ROPE_AXIS_DIM // 2
    sign = np.tile(np.concatenate([-np.ones(half), np.ones(half)]), 2)
    reps = (1, LANES // HEAD_DIM)
    return (jnp.asarray(np.tile(np.cos(ang), reps), F32), jnp.asarray(np.tile(np.sin(ang) * sign, reps), F32))


def _head_selectors():
    sel = np.zeros((2, 2 * LANES, SSD_HEADS * LANES), np.float32)
    for d in range(2):
        for h in range(SSD_HEADS):
            for part in range(2):
                sel[d, part * LANES + SSD_HEADS * d + h, h * LANES:(h + 1) * LANES] = 1.0
    return jnp.asarray(sel, BF16)


def _head_mean_matrix():
    idx = np.arange(256) // HEAD_DIM
    return jnp.asarray((idx[:, None] == idx[None, :]).astype(np.float32) / HEAD_DIM, BF16)


def kernel(x, c, ctx, c_ctx, w_mod, b_mod, g_mix, w_in, ssd_conv_w, ssd_conv_b, ssd_a_log, ssd_dt_bias, ssd_d, ssd_norm_g, lru_conv_w, lru_conv_b, lru_lambda, lru_w_a, lru_b_a, lru_w_i, lru_b_i, gqa_q_norm, gqa_k_norm, swa_q_norm, swa_k_norm, swa_sink, w_out, g_ffn, w_ffn1, w_ffn2):
    batch, seq, d = x.shape
    lc = ctx.shape[1]
    depth = w_mod.shape[0]
    assert batch + 1 <= 8 and seq % 1024 == 0 and lc % 256 == 0 and seq % GRID_W == 0

    cvecs = jnp.concatenate([c, c_ctx[None, :], jnp.zeros((8 - batch - 1, d), F32)], axis=0)
    mod = _modulation(cvecs, w_mod, b_mod).reshape(depth * 8, N_MOD, d)

    rope = _rope_tables(seq)
    sel = _head_selectors()
    bd = _head_mean_matrix()
    conv_sel = _conv_selectors()
    w_in_r = _relayout_w_in(w_in)
    w_out_r = _relayout_w_out(w_out)
    w1, w2 = w_ffn1.astype(BF16), w_ffn2.astype(BF16)
    g1, g2 = g_mix.reshape(depth, 1, d), g_ffn.reshape(depth, 1, d)
    nrm = jnp.tile(jnp.stack([gqa_q_norm, gqa_k_norm, swa_q_norm, swa_k_norm], axis=1), (1, 1, 256 // HEAD_DIM))
    lane_pad = ((0, 0), (0, 0), (0, LANES - 2 * SSD_HEADS))
    head_params = jnp.concatenate([jnp.pad(ssd_dt_bias.reshape(depth, 1, -1), lane_pad),
                                   jnp.pad(ssd_a_log.reshape(depth, 1, -1), lane_pad),
                                   jnp.zeros((depth, 6, LANES), F32)], axis=1)
    ssd_cb = ssd_conv_b.reshape(depth, 1, -1)
    d_skip = jnp.repeat(ssd_d, HEAD_DIM, axis=-1).reshape(depth, 1, -1)
    ssd_ng = ssd_norm_g.reshape(depth, 1, -1)
    bd_a, bd_i = _block_diag(lru_w_a), _block_diag(lru_w_i)
    w_gates = (0.5 * jnp.concatenate([bd_a[:, 0], bd_i[:, 0], bd_a[:, 1], bd_i[:, 1]], axis=-1)).astype(BF16)
    b_gates = 0.5 * jnp.concatenate([lru_b_a[:, 0], lru_b_i[:, 0], lru_b_a[:, 1], lru_b_i[:, 1]], axis=-1).reshape(depth, 1, -1)
    lru_cb = lru_conv_b.reshape(depth, 1, -1)
    lam = lru_lambda.reshape(depth, 1, -1)
    bounds = _logit_bounds(gqa_q_norm, gqa_k_norm)
    swa_bounds = _logit_bounds(swa_q_norm, swa_k_norm)

    xl = x.reshape(batch * seq, d)
    xc = ctx.reshape(batch * lc, d)
    for l in range(depth):
        need_ctx = l < depth - 1
        lat_row = lambda rows: (lambda i, l=l: l * 8 + i // (seq // rows))
        ctx_row = lambda i, l=l: l * 8 + batch
        ssd_l, lru_l, qc_l, qd_l, dt_l = _in_projection(l, xl, mod, lat_row(IN_PROJ_ROWS), g1, w_in_r, nrm, bd, rope,
                                                        IN_PROJ_ROWS, seq // IN_PROJ_ROWS)
        ssd_c, lru_c, qc_c, qd_c, dt_c = _in_projection(l, xc, mod, ctx_row, g1, w_in_r, nrm, bd, None, lc, 1)
        ya_l, ya_c = _ssd_mixer(l, ssd_l, ssd_c, dt_l, dt_c, conv_sel, ssd_conv_w, ssd_cb, head_params, sel, d_skip,
                                ssd_ng, batch)
        yb_l, yb_c = _lru_mixer(l, lru_l, lru_c, conv_sel, lru_conv_w, lru_cb, w_gates, b_gates, lam, batch)
        yc_l = _global_attention(l, bounds, qc_l, qc_c, batch, GATTN_Q_ROWS, GATTN_KV_ROWS)
        yd_l = _window_attention(l, swa_bounds, swa_sink, qd_l, qd_c, batch, seq)
        xl = _out_ffn(l, xl, (ya_l, yb_l, yc_l, yd_l), mod, lat_row(FFN_ROWS), w_out_r, g2, w1, w2, FFN_ROWS)
        if need_ctx:
            yc_c, yd_c = _context_attention(l, swa_sink, qc_c, qd_c, batch)
            xc = _out_ffn(l, xc, (ya_c, yb_c, yc_c, yd_c), mod, ctx_row, w_out_r, g2, w1, w2, FFN_CTX_ROWS)
    return xl.reshape(batch, seq, d)
```
